```python
import jax, jax.numpy as jnp
from jax import lax
import numpy as np

D_MODEL = 1024
BATCH = 16
SEQ = 256
DEPTH = 2
DEC_BATCH = 4
DEC_SEQ = 1024
PAST_LEN = 512

GRID_W = 64
NORM_EPS = 1e-6
RW_HEADS = 8
RW_HEAD = 64
RW_WIDTH = RW_HEADS * RW_HEAD
RW_DECAY_RANK = 64
RW_A_RANK = 64
RW_GN_EPS = 64e-5
GLA_HEADS = 4
GLA_DK = 64
GLA_DV = 128
GLA_KW = GLA_HEADS * GLA_DK
GLA_VW = GLA_HEADS * GLA_DV
GLA_GATE_RANK = 16
GLA_LOGIT_NORM = 16.0
GLA_CHUNK = 64
MLA_HEADS = 8
MLA_NOPE = 64
MLA_ROPE = 32
MLA_V = 64
MLA_Q_RANK = 256
MLA_KV_RANK = 128
MLA_VW = MLA_HEADS * MLA_V
ROPE_THETA = 10000.0
ATTN_Q_BLOCK = 128
N_BRANCH = 3
IN_SIZES = (RW_WIDTH, RW_WIDTH, RW_WIDTH, RW_DECAY_RANK, RW_A_RANK, RW_WIDTH,
            GLA_KW, GLA_KW, GLA_VW, GLA_GATE_RANK, GLA_VW,
            MLA_Q_RANK, MLA_KV_RANK, MLA_ROPE, MLA_VW,
            N_BRANCH * D_MODEL)
IN_WIDTH = sum(IN_SIZES)
IN_SPLITS = tuple(int(v) for v in np.cumsum(IN_SIZES)[:-1])

kernel_name = "hybrid_rwkv7_gla_mla_diffusion_step"

F32 = jnp.float32


def rmsnorm(x, g):
    xf = x.astype(F32)
    y = xf * lax.rsqrt(jnp.mean(xf * xf, axis=-1, keepdims=True) + NORM_EPS)
    return y.astype(x.dtype) * g


def adaln_project(x, cvec, norm_g, w_mod, b_mod, w_in):
    mod = (jax.nn.silu(cvec) @ w_mod + b_mod)[:, None, :]
    shift, scale, gate = jnp.split(mod, 3, axis=-1)
    h = rmsnorm(x, norm_g) * (1 + scale) + shift
    return h @ w_in, gate


def rwkv7_scan(r, decay, k, v, kk, a, s0):
    xs = tuple(jnp.moveaxis(t.astype(F32), 1, 0) for t in (r, decay, k, v, kk, a))

    def step(s, inp):
        r_t, w_t, k_t, v_t, kk_t, a_t = inp
        sa = jnp.einsum('bhij,bhj->bhi', s, -kk_t)
        s = (s * w_t[:, :, None, :] + sa[..., None] * (kk_t * a_t)[:, :, None, :]
             + v_t[..., None] * k_t[:, :, None, :])
        return s, jnp.einsum('bhij,bhj->bhi', s, r_t)

    s_fin, out = lax.scan(step, s0.astype(F32), xs)
    return jnp.moveaxis(out, 0, 1), s_fin


def rwkv7_branch(pr, pk, pv, pwd, pad, s0, w0, w2, a0, a2, k_k, k_a, r_k, ln_g, ln_b):
    B, T, _ = pr.shape
    heads = lambda t: t.reshape(B, T, RW_HEADS, RW_HEAD)
    r, k, v = heads(pr), heads(pk), heads(pv)
    kk = (k * k_k.reshape(RW_HEADS, RW_HEAD)).astype(F32)
    kk = kk / jnp.maximum(jnp.sqrt(jnp.sum(kk * kk, -1, keepdims=True)), 1e-12)
    tw = jnp.tanh(pwd)
    outs, states = [], []
    for d in range(2):
        w_log = -jax.nn.softplus(-(w0[d] + tw @ w2[d]).astype(F32)) - 0.5
        decay = heads(jnp.exp(-jnp.exp(w_log)))
        a = heads(jax.nn.sigmoid(a0[d] + pad @ a2[d]))
        kd = k * (1 + (a - 1) * k_a.reshape(RW_HEADS, RW_HEAD))
        seqs = (r, decay, kd, v, kk, a)
        if d == 1:
            seqs = tuple(jnp.flip(t, 1) for t in seqs)
        o, s = rwkv7_scan(*seqs, s0[:, d])
        if d == 1:
            o = jnp.flip(o, 1)
        outs.append(o)
        states.append(s)
    o = outs[0] + outs[1]
    mu = jnp.mean(o, -1, keepdims=True)
    var = jnp.mean(jnp.square(o - mu), -1, keepdims=True)
    o = (o - mu) * lax.rsqrt(var + RW_GN_EPS)
    o = o * ln_g.reshape(RW_HEADS, RW_HEAD) + ln_b.reshape(RW_HEADS, RW_HEAD)
    bonus = jnp.sum(r * k * r_k, -1, keepdims=True) * v
    o = o.astype(pr.dtype) + bonus
    return o.reshape(B, T, RW_WIDTH), jnp.stack(states, 1).astype(pr.dtype)


def gla_chunk_scan(q, k, v, g, s0):
    B, T, H, _ = q.shape
    nc = T // GLA_CHUNK
    chunks = lambda t: t.astype(F32).reshape(B, nc, GLA_CHUNK, H, t.shape[-1]).transpose(1, 0, 3, 2, 4)
    lower = jnp.tril(jnp.ones((GLA_CHUNK, GLA_CHUNK), bool))[:, :, None]

    def step(s, inp):
        qc, kc, vc, gc = inp
        b = jnp.cumsum(gc, axis=2)
        b_last = b[:, :, -1:, :]
        inter = jnp.einsum('bhcd,bhde->bhce', qc * jnp.exp(b), s)
        diff = jnp.where(lower, b[:, :, :, None, :] - b[:, :, None, :, :], -jnp.inf)
        att = jnp.einsum('bhtd,bhsd,bhtsd->bhts', qc, kc, jnp.exp(diff))
        out = inter + att @ vc
        s = (s * jnp.exp(b_last).swapaxes(-1, -2)
             + jnp.einsum('bhsd,bhse->bhde', kc * jnp.exp(b_last - b), vc))
        return s, out

    s_fin, out = lax.scan(step, s0.astype(F32), (chunks(q), chunks(k), chunks(v), chunks(g)))
    out = out.transpose(1, 0, 3, 2, 4).reshape(B, T, H, v.shape[-1])
    return out, s_fin


def gla_branch(pq, pk, pv, pad, s0, a2, ab, norm_g):
    B, T, _ = pq.shape
    q = pq.reshape(B, T, GLA_HEADS, GLA_DK) * (GLA_DK ** -0.5)
    k = pk.reshape(B, T, GLA_HEADS, GLA_DK)
    v = pv.reshape(B, T, GLA_HEADS, GLA_DV)
    outs, states = [], []
    for d in range(2):
        g = jax.nn.log_sigmoid((pad @ a2[d] + ab[d]).astype(F32)) / GLA_LOGIT_NORM
        seqs = (q, k, v, g.reshape(B, T, GLA_HEADS, GLA_DK))
        if d == 1:
            seqs = tuple(jnp.flip(t, 1) for t in seqs)
        o, s = gla_chunk_scan(*seqs, s0[:, d])
        if d == 1:
            o = jnp.flip(o, 1)
        outs.append(o)
        states.append(s)
    o = rmsnorm(outs[0] + outs[1], norm_g).astype(pq.dtype)
    return o.reshape(B, T, GLA_VW), jnp.stack(states, 1).astype(pq.dtype)


def rope_tables(T):
    n_rows = T // GRID_W
    rows = jnp.repeat(jnp.arange(n_rows), GRID_W)
    cols = jnp.tile(jnp.arange(GRID_W), n_rows)
    n_freq = MLA_ROPE // 4
    inv = ROPE_THETA ** (-jnp.arange(n_freq, dtype=F32) / n_freq)
    ang = jnp.stack([rows[:, None] * inv, cols[:, None] * inv], axis=1)
    return jnp.cos(ang), jnp.sin(ang)


def apply_rope_2d(x, cos, sin):
    xs = x.reshape(x.shape[:-1] + (2, 2, MLA_ROPE // 4))
    x1, x2 = xs[..., 0, :], xs[..., 1, :]
    out = jnp.stack([x1 * cos - x2 * sin, x2 * cos + x1 * sin], axis=-2)
    return out.reshape(x.shape).astype(x.dtype)


def mla_keys(ckv, wkv_up):
    B, S, _ = ckv.shape
    kv = (ckv @ wkv_up).reshape(B, S, MLA_HEADS, MLA_NOPE + MLA_V)
    return kv[..., :MLA_NOPE], kv[..., MLA_NOPE:]


def mla_attend(q_nope, segs):
    B, T = q_nope.shape[:2]
    nb = T // ATTN_Q_BLOCK
    blk = lambda t: jnp.moveaxis(t.reshape((B, nb, ATTN_Q_BLOCK) + t.shape[2:]), 1, 0)
    scale = (MLA_NOPE + MLA_ROPE) ** -0.5
    lengths = [s[1].shape[1] for s in segs]
    cuts = [int(c) for c in np.cumsum(lengths)[:-1]]

    def one(qs):
        qn, qps = qs
        logits = [jnp.einsum('bqhd,bkhd->bhqk', qn, kn) + jnp.einsum('bqhr,bkr->bhqk', qp, kp)
                  for qp, (_, kn, kp, _) in zip(qps, segs)]
        p = jax.nn.softmax(jnp.concatenate(logits, -1).astype(F32) * scale, axis=-1).astype(q_nope.dtype)
        pieces = jnp.split(p, cuts, axis=-1)
        out = jnp.einsum('bhqk,bkhd->bqhd', pieces[0], segs[0][3])
        for pp, seg in zip(pieces[1:], segs[1:]):
            out = out + jnp.einsum('bhqk,bkhd->bqhd', pp, seg[3])
        return out

    out = lax.map(one, (blk(q_nope), tuple(blk(s[0]) for s in segs)))
    return jnp.moveaxis(out, 0, 1).reshape(B, T, MLA_VW)


def layer_forward(x, cvec, lp, ctx):
    B, T, _ = x.shape
    proj, gate = adaln_project(x, cvec, lp['norm_g'], lp['w_mod'], lp['b_mod'], lp['w_in'])
    (rw_r, rw_k, rw_v, rw_wd, rw_ad, rw_g, gl_q, gl_k, gl_v, gl_ad, gl_g,
     ml_qd, ml_kvd, ml_kpe, ml_g, merge) = jnp.split(proj, IN_SPLITS, axis=-1)
    if ctx is None:
        s_rw0 = jnp.zeros((B, 2, RW_HEADS, RW_HEAD, RW_HEAD), x.dtype)
        s_gla0 = jnp.zeros((B, 2, GLA_HEADS, GLA_DK, GLA_DV), x.dtype)
    else:
        s_rw0, s_gla0 = ctx[2], ctx[3]
    o_a, st_rw = rwkv7_branch(rw_r, rw_k, rw_v, rw_wd, rw_ad, s_rw0, lp['rw_w0'], lp['rw_w2'], lp['rw_a0'],
                              lp['rw_a2'], lp['rw_k_k'], lp['rw_k_a'], lp['rw_r_k'], lp['rw_ln_g'], lp['rw_ln_b'])
    o_b, st_gla = gla_branch(gl_q, gl_k, gl_v, gl_ad, s_gla0, lp['gla_a2'], lp['gla_ab'], lp['gla_norm_g'])
    q = (rmsnorm(ml_qd, lp['mla_qn_g']) @ lp['mla_wq_up']).reshape(B, T, MLA_HEADS, MLA_NOPE + MLA_ROPE)
    q_nope, q_pe = q[..., :MLA_NOPE], q[..., MLA_NOPE:]
    ckv = rmsnorm(ml_kvd, lp['mla_kvn_g'])
    k_nope, v = mla_keys(ckv, lp['mla_wkv_up'])
    if ctx is None:
        o_c = mla_attend(q_nope, ((q_pe, k_nope, ml_kpe, v),))
    else:
        cos, sin = rope_tables(T)
        q_rot = apply_rope_2d(q_pe, cos[:, None], sin[:, None])
        k_rot = apply_rope_2d(ml_kpe, cos, sin)
        ck_nope, cv = mla_keys(ctx[0], lp['mla_wkv_up'])
        o_c = mla_attend(q_nope, ((q_rot, k_nope, k_rot, v), (q_pe, ck_nope, ctx[1], cv)))
    y_a = (o_a * jax.nn.silu(rw_g)) @ lp['rw_out']
    y_b = (o_b * jax.nn.silu(gl_g)) @ lp['gla_out']
    y_c = (o_c * jax.nn.silu(ml_g)) @ lp['mla_out']
    g_a, g_b, g_c = jnp.split(jax.nn.sigmoid(merge), N_BRANCH, axis=-1)
    y = (g_a * y_a + g_b * y_b + g_c * y_c) @ lp['w_out']
    return x + gate * y, (ckv, ml_kpe, st_rw, st_gla)


def setup_inputs(seed: int = 0) -> dict:
    key = jax.random.key(seed)
    ks = iter(jax.random.split(key, 40))
    nrm = lambda shape, s=1.0: jax.random.normal(next(ks), shape, F32) * s
    D = D_MODEL
    H_QK = MLA_HEADS * (MLA_NOPE + MLA_ROPE)
    H_KV = MLA_HEADS * (MLA_NOPE + MLA_V)
    return {
        "x_prompt": nrm((BATCH, SEQ, D)),
        "x_sample": nrm((DEC_BATCH, DEC_SEQ, D)),
        "c": nrm((DEC_BATCH, D)),
        "cache_ckv": nrm((DEC_BATCH, DEPTH, PAST_LEN, MLA_KV_RANK)),
        "cache_kpe": nrm((DEC_BATCH, DEPTH, PAST_LEN, MLA_ROPE)),
        "state_rwkv": nrm((DEC_BATCH, DEPTH, 2, RW_HEADS, RW_HEAD, RW_HEAD)),
        "state_gla": nrm((DEC_BATCH, DEPTH, 2, GLA_HEADS, GLA_DK, GLA_DV)),
        "c_ctx": nrm((D,)),
        "norm_g": 1.0 + nrm((DEPTH, D), 0.05),
        "w_mod": nrm((DEPTH, D, 3 * D), 0.5 * D ** -0.5),
        "b_mod": nrm((DEPTH, 3 * D), 0.02),
        "w_in": nrm((DEPTH, D, IN_WIDTH), D ** -0.5),
        "rw_w0": -2.0 + nrm((DEPTH, 2, RW_WIDTH), 1.0),
        "rw_w2": nrm((DEPTH, 2, RW_DECAY_RANK, RW_WIDTH), 0.5 * RW_DECAY_RANK ** -0.5),
        "rw_a0": nrm((DEPTH, 2, RW_WIDTH), 0.5),
        "rw_a2": nrm((DEPTH, 2, RW_A_RANK, RW_WIDTH), 0.5 * RW_A_RANK ** -0.5),
        "rw_k_k": 0.85 + nrm((DEPTH, RW_WIDTH), 0.1),
        "rw_k_a": 1.0 + nrm((DEPTH, RW_WIDTH), 0.1),
        "rw_r_k": nrm((DEPTH, RW_HEADS, RW_HEAD), 0.1),
        "rw_ln_g": 1.0 + nrm((DEPTH, RW_WIDTH), 0.05),
        "rw_ln_b": nrm((DEPTH, RW_WIDTH), 0.02),
        "rw_out": nrm((DEPTH, RW_WIDTH, D), RW_WIDTH ** -0.5),
        "gla_a2": nrm((DEPTH, 2, GLA_GATE_RANK, GLA_KW), GLA_GATE_RANK ** -0.5),
        "gla_ab": nrm((DEPTH, 2, GLA_KW), 0.5),
        "gla_norm_g": 1.0 + nrm((DEPTH, GLA_DV), 0.05),
        "gla_out": nrm((DEPTH, GLA_VW, D), GLA_VW ** -0.5),
        "mla_qn_g": 1.0 + nrm((DEPTH, MLA_Q_RANK), 0.05),
        "mla_wq_up": nrm((DEPTH, MLA_Q_RANK, H_QK), MLA_Q_RANK ** -0.5),
        "mla_kvn_g": 1.0 + nrm((DEPTH, MLA_KV_RANK), 0.05),
        "mla_wkv_up": nrm((DEPTH, MLA_KV_RANK, H_KV), MLA_KV_RANK ** -0.5),
        "mla_out": nrm((DEPTH, MLA_VW, D), MLA_VW ** -0.5),
        "w_out": nrm((DEPTH, D, D), D ** -0.5),
        "final_g": 1.0 + nrm((D,), 0.05),
    }


def reference(x_prompt, x_sample, c, cache_ckv, cache_kpe, state_rwkv, state_gla, c_ctx,
              norm_g, w_mod, b_mod, w_in, rw_w0, rw_w2, rw_a0, rw_a2, rw_k_k, rw_k_a, rw_r_k,
              rw_ln_g, rw_ln_b, rw_out, gla_a2, gla_ab, gla_norm_g, gla_out,
              mla_qn_g, mla_wq_up, mla_kvn_g, mla_wkv_up, mla_out, w_out, final_g):
    hp, hs = x_prompt, x_sample
    ckv_l, kpe_l, rw_l, gla_l = [], [], [], []
    for l in range(DEPTH):
        lp = dict(norm_g=norm_g[l], w_mod=w_mod[l], b_mod=b_mod[l], w_in=w_in[l],
                  rw_w0=rw_w0[l], rw_w2=rw_w2[l], rw_a0=rw_a0[l], rw_a2=rw_a2[l], rw_k_k=rw_k_k[l],
                  rw_k_a=rw_k_a[l], rw_r_k=rw_r_k[l], rw_ln_g=rw_ln_g[l], rw_ln_b=rw_ln_b[l], rw_out=rw_out[l],
                  gla_a2=gla_a2[l], gla_ab=gla_ab[l], gla_norm_g=gla_norm_g[l], gla_out=gla_out[l],
                  mla_qn_g=mla_qn_g[l], mla_wq_up=mla_wq_up[l], mla_kvn_g=mla_kvn_g[l],
                  mla_wkv_up=mla_wkv_up[l], mla_out=mla_out[l], w_out=w_out[l])
        hp, (ckv, kpe, s_rw, s_gla) = layer_forward(hp, c_ctx[None, :], lp, None)
        ckv_l.append(ckv)
        kpe_l.append(kpe)
        rw_l.append(s_rw)
        gla_l.append(s_gla)
        ctx = (cache_ckv[:, l], cache_kpe[:, l], state_rwkv[:, l], state_gla[:, l])
        hs, _ = layer_forward(hs, c, lp, ctx)
    y_prompt = rmsnorm(hp, final_g)
    y_sample = rmsnorm(hs, final_g)
    new_ckv = jnp.stack(ckv_l, axis=1)
    new_kpe = jnp.stack(kpe_l, axis=1)
    new_rwkv = jnp.stack(rw_l, axis=1)
    new_gla = jnp.stack(gla_l, axis=1)
    return (y_prompt, y_sample, new_ckv, new_kpe, new_rwkv, new_gla)
```

```python
import functools

import numpy as np
import jax
import jax.numpy as jnp
from jax import lax
from jax.experimental import pallas as pl
from jax.experimental.pallas import tpu as pltpu

F32 = jnp.float32
BF16 = jnp.bfloat16
HI = lax.Precision.HIGHEST

D_MODEL = 1024
DEPTH = 2
GRID_W = 64
NORM_EPS = 1e-6
RW_HEADS = 8
RW_HEAD = 64
RW_WIDTH = RW_HEADS * RW_HEAD
RW_RANK = 64
RW_GN_EPS = 64e-5
GLA_HEADS = 4
GLA_DK = 64
GLA_DV = 128
GLA_KW = GLA_HEADS * GLA_DK
GLA_VW = GLA_HEADS * GLA_DV
GLA_GATE_RANK = 16
GLA_LOGIT_NORM = 16.0
MLA_HEADS = 8
MLA_NOPE = 64
MLA_ROPE = 32
MLA_V = 64
MLA_Q_RANK = 256
MLA_KV_RANK = 128
MLA_VW = MLA_HEADS * MLA_V
ROPE_THETA = 10000.0
N_BRANCH = 3

_IN_SIZES = (RW_WIDTH, RW_WIDTH, RW_WIDTH, RW_RANK, RW_RANK, RW_WIDTH,
             GLA_KW, GLA_KW, GLA_VW, GLA_GATE_RANK, GLA_VW,
             MLA_Q_RANK, MLA_KV_RANK, MLA_ROPE, MLA_VW, N_BRANCH * D_MODEL)
_IN_OFF = tuple(int(v) for v in np.concatenate([[0], np.cumsum(_IN_SIZES)]))

LANE = 128
COL_RW_R = 0
COL_RW_K = 512
COL_RW_V = 1024
COL_RW_WA = 1536
COL_GL_AD = 1664
COL_ML_QD = 1792
COL_GL_Q = 2048
COL_GL_K = 2304
COL_GL_V = 2560
COL_ML_KVD = 3072
COL_KPE = 3200
COL_KPE_SW = 3328
COL_GATES = 3584
COL_MERGE = 5120
PACK_W = 8192
KPE_LANE = MLA_NOPE

CHUNK = 64
VMEM_LIMIT = 48 * 1024 * 1024


def _dot(a, b, prec=None):
    return jnp.dot(a, b, preferred_element_type=F32, precision=prec)


def _dot_nt(a, b, prec=None):
    return lax.dot_general(a, b, (((1,), (1,)), ((), ())), preferred_element_type=F32, precision=prec)


def _dot_tn(a, b, prec=None):
    return lax.dot_general(a, b, (((0,), (0,)), ((), ())), preferred_element_type=F32, precision=prec)


def _softplus(z):
    return jnp.maximum(z, 0.0) + jnp.log1p(jnp.exp(-jnp.abs(z)))


def _rms(x, eps=NORM_EPS):
    return x * lax.rsqrt(jnp.mean(x * x, axis=-1, keepdims=True) + eps)


def _mod_kernel(c_ref, w_ref, b_ref, o_ref):
    c = c_ref[...]
    o_ref[...] = _dot(c * jax.nn.sigmoid(c), w_ref[...], HI) + b_ref[...]


def _mod_call(cvec8, w_mod, b_mod, layer):
    tn = 1024
    return pl.pallas_call(
        _mod_kernel,
        grid=(3 * D_MODEL // tn,),
        in_specs=[pl.BlockSpec((8, D_MODEL), lambda j: (0, 0)),
                  pl.BlockSpec((None, D_MODEL, tn), lambda j: (layer, 0, j)),
                  pl.BlockSpec((None, 1, tn), lambda j: (layer, 0, j))],
        out_specs=pl.BlockSpec((8, tn), lambda j: (0, j)),
        out_shape=jax.ShapeDtypeStruct((8, 3 * D_MODEL), F32),
        compiler_params=pltpu.CompilerParams(vmem_limit_bytes=VMEM_LIMIT),
        name="mod",
    )(cvec8, w_mod, b_mod.reshape(DEPTH, 1, 3 * D_MODEL))


def _inproj_kernel(x_ref, mod_ref, g_ref, w_ref, o_ref):
    x = x_ref[...]
    m = mod_ref[...]
    shift = m[:, 0:D_MODEL]
    scale = m[:, D_MODEL:2 * D_MODEL]
    h = _rms(x) * g_ref[...] * (1.0 + scale) + shift
    o_ref[...] = _dot(h.astype(BF16), w_ref[...])


def _inproj_call(x2d, mod3, norm_g, w_packed, rows_per_mod, mod_base):
    rows = x2d.shape[0]
    tm, tn = 512, 2048
    tiles_per_mod = rows_per_mod // tm
    return pl.pallas_call(
        _inproj_kernel,
        grid=(PACK_W // tn, rows // tm),
        in_specs=[pl.BlockSpec((tm, D_MODEL), lambda j, i: (i, 0)),
                  pl.BlockSpec((None, 1, 3 * D_MODEL), lambda j, i: (mod_base + i // tiles_per_mod, 0, 0)),
                  pl.BlockSpec((1, D_MODEL), lambda j, i: (0, 0)),
                  pl.BlockSpec((D_MODEL, tn), lambda j, i: (0, j))],
        out_specs=pl.BlockSpec((tm, tn), lambda j, i: (i, j)),
        out_shape=jax.ShapeDtypeStruct((rows, PACK_W), F32),
        compiler_params=pltpu.CompilerParams(vmem_limit_bytes=VMEM_LIMIT),
        name="inproj",
    )(x2d, mod3, norm_g, w_packed)


def _order_masks(c, reverse):
    row = lax.broadcasted_iota(jnp.int32, (c, c), 0)
    col = lax.broadcasted_iota(jnp.int32, (c, c), 1)
    if reverse:
        return col > row, col >= row
    return col < row, col <= row


def _solve_unit_triangular(a, rhs, n):
    u = rhs + _dot(a, rhs, HI)
    p = a
    covered = 2
    while covered < n:
        p = _dot(p, p, HI)
        u = u + _dot(p, u, HI)
        covered *= 2
    return u


def _rwkv_kernel(*refs, seq, has_s0):
    if has_s0:
        r_ref, k_ref, v_ref, wa_ref, s0_ref = refs[:5]
        rest = refs[5:]
    else:
        r_ref, k_ref, v_ref, wa_ref = refs[:4]
        s0_ref = None
        rest = refs[4:]
    (w0_ref, w2_ref, a0_ref, a2_ref, kk_ref, ka_ref, rk_ref, lng_ref, lnb_ref, hsum_ref,
     o_ref, sout_ref, acc_ref, st_ref) = rest
    c = CHUNK
    nc = seq // c
    hsum = hsum_ref[...]

    def chunk_step(d, ci):
        rows = pl.ds(pl.multiple_of(ci * c, c), c)
        strict, incl = _order_masks(c, d == 1)
        r = r_ref[rows, :]
        k = k_ref[rows, :]
        v = v_ref[rows, :]
        wa = wa_ref[rows, :]
        wd = wa[:, 0:RW_RANK]
        ad = wa[:, RW_RANK:2 * RW_RANK]
        kk = k * kk_ref[...]
        kk = kk / jnp.maximum(jnp.sqrt(_dot(kk * kk, hsum, HI)), 1e-12)
        w_log = -_softplus(-(w0_ref[d:d + 1, :] + _dot(jnp.tanh(wd), w2_ref[d], HI))) - 0.5
        lw = -jnp.exp(w_log)
        a = jax.nn.sigmoid(a0_ref[d:d + 1, :] + _dot(ad, a2_ref[d], HI))
        kd = k * (1.0 + (a - 1.0) * ka_ref[...])
        bv = kk * a
        gam = _dot(jnp.where(incl, 1.0, 0.0), lw, HI)
        last = 0 if d == 1 else c - 1
        gtot = gam[last:last + 1, :]
        e_in = jnp.exp(gam)
        e_out = jnp.exp(-gam)
        e_end = jnp.exp(gtot - gam)
        rt = r * e_in
        at = -kk * jnp.exp(gam - lw)
        kt = kd * e_out
        bt = bv * e_out
        ke = kd * e_end
        be = bv * e_end
        dtot = jnp.exp(gtot)
        for h in range(RW_HEADS):
            sl = slice(h * RW_HEAD, (h + 1) * RW_HEAD)
            s_h = st_ref[h]
            a_ab = jnp.where(strict, _dot_nt(at[:, sl], bt[:, sl], HI), 0.0)
            a_ak = jnp.where(strict, _dot_nt(at[:, sl], kt[:, sl], HI), 0.0)
            a_rb = jnp.where(incl, _dot_nt(rt[:, sl], bt[:, sl], HI), 0.0)
            a_rk = jnp.where(incl, _dot_nt(rt[:, sl], kt[:, sl], HI), 0.0)
            v_h = v[:, sl]
            rhs = _dot(a_ak, v_h, HI) + _dot_nt(at[:, sl], s_h, HI)
            u = _solve_unit_triangular(a_ab, rhs, c)
            o_h = _dot_nt(rt[:, sl], s_h, HI) + _dot(a_rb, u, HI) + _dot(a_rk, v_h, HI)
            st_ref[h] = s_h * dtot[:, sl] + _dot_tn(u, be[:, sl], HI) + _dot_tn(v_h, ke[:, sl], HI)
            if d == 0:
                acc_ref[rows, sl] = o_h
            else:
                acc_ref[rows, sl] = acc_ref[rows, sl] + o_h
        if d == 1:
            o = acc_ref[rows, :]
            mu = _dot(o, hsum, HI) * (1.0 / RW_HEAD)
            dev = o - mu
            var = _dot(dev * dev, hsum, HI) * (1.0 / RW_HEAD)
            o = dev * lax.rsqrt(var + RW_GN_EPS) * lng_ref[...] + lnb_ref[...]
            bonus = _dot(r * k * rk_ref[...], hsum, HI) * v
            o_ref[rows, :] = o + bonus

    for d in range(2):
        if has_s0:
            st_ref[...] = s0_ref[d]
        else:
            st_ref[...] = jnp.zeros_like(st_ref)

        def body(i, carry, d=d):
            chunk_step(d, i if d == 0 else nc - 1 - i)
            return carry

        lax.fori_loop(0, nc, body, 0)
        sout_ref[d] = st_ref[...]


def _rwkv_call(proj, s0, layer, batch, seq, wts):
    has_s0 = s0 is not None
    blk = lambda w, cb: pl.BlockSpec((seq, w), lambda b: (b, cb))
    in_specs = [blk(512, COL_RW_R // 512), blk(512, COL_RW_K // 512), blk(512, COL_RW_V // 512),
                blk(LANE, COL_RW_WA // LANE)]
    args = [proj, proj, proj, proj]
    if has_s0:
        in_specs.append(pl.BlockSpec((None, None, 2, RW_HEADS, RW_HEAD, RW_HEAD),
                                     lambda b: (b, layer, 0, 0, 0, 0)))
        args.append(s0)
    for w in wts:
        in_specs.append(pl.BlockSpec(w.shape, lambda b, n=w.ndim: (0,) * n))
        args.append(w)
    return pl.pallas_call(
        functools.partial(_rwkv_kernel, seq=seq, has_s0=has_s0),
        grid=(batch,),
        in_specs=in_specs,
        out_specs=[pl.BlockSpec((seq, RW_WIDTH), lambda b: (b, 0)),
                   pl.BlockSpec((None, 2, RW_HEADS, RW_HEAD, RW_HEAD), lambda b: (b, 0, 0, 0, 0))],
        out_shape=[jax.ShapeDtypeStruct((batch * seq, RW_WIDTH), F32),
                   jax.ShapeDtypeStruct((batch, 2, RW_HEADS, RW_HEAD, RW_HEAD), F32)],
        scratch_shapes=[pltpu.VMEM((seq, RW_WIDTH), F32),
                        pltpu.VMEM((RW_HEADS, RW_HEAD, RW_HEAD), F32)],
        compiler_params=pltpu.CompilerParams(vmem_limit_bytes=VMEM_LIMIT),
        name="rwkv",
    )(*args)


GLA_LEVELS = (32, 16, 8, 4, 2, 1)


def _gla_constants():
    c = CHUNK
    nl = len(GLA_LEVELS)
    mexp = np.zeros((2, 2 * nl * c, c), np.float32)
    bmask = np.zeros((2, nl + 1, c, c), np.float32)
    lincl = np.zeros((2, c, c), np.float32)
    for d in range(2):
        pos = np.arange(c) if d == 0 else c - 1 - np.arange(c)
        pt = pos[:, None]
        pj = pos[None, :]
        lincl[d] = (pj <= pt)
        for li, m in enumerate(GLA_LEVELS):
            mid = (pos // (2 * m)) * (2 * m) + m
            second = pos >= mid
            mq = (pj >= mid[:, None]) & (pj <= pt) & second[:, None]
            mk = (pj > pt) & (pj <= mid[:, None] - 1) & (~second)[:, None]
            mexp[d, (2 * li) * c:(2 * li + 1) * c] = mq
            mexp[d, (2 * li + 1) * c:(2 * li + 2) * c] = mk
            same = (pos[:, None] // (2 * m)) == (pos[None, :] // (2 * m))
            bmask[d, li] = same & second[:, None] & (~second)[None, :]
        bmask[d, nl] = np.eye(c)
    return mexp, bmask, lincl


def _gla_kernel(*refs, seq, has_s0):
    if has_s0:
        q_ref, k_ref, v_ref, ad_ref, s0_ref = refs[:5]
        rest = refs[5:]
    else:
        q_ref, k_ref, v_ref, ad_ref = refs[:4]
        s0_ref = None
        rest = refs[4:]
    (a2_ref, ab_ref, ng_ref, mexp_ref, bmask_ref, lincl_ref,
     o_ref, sout_ref, acc_ref, st_ref) = rest
    c = CHUNK
    nc = seq // c
    nl = len(GLA_LEVELS)

    def chunk_step(d, ci):
        rows = pl.ds(pl.multiple_of(ci * c, c), c)
        q = q_ref[rows, :] * (GLA_DK ** -0.5)
        k = k_ref[rows, :]
        v = v_ref[rows, :]
        x = _dot(ad_ref[rows, :], a2_ref[d], HI) + ab_ref[d:d + 1, :]
        g = (jnp.minimum(x, 0.0) - jnp.log1p(jnp.exp(-jnp.abs(x)))) * (1.0 / GLA_LOGIT_NORM)
        ex = jnp.exp(_dot(mexp_ref[d], g, HI))
        b = _dot(lincl_ref[d], g, HI)
        last = 0 if d == 1 else c - 1
        blast = b[last:last + 1, :]
        qb = q * jnp.exp(b)
        kdec = k * jnp.exp(blast - b)
        dtot = jnp.exp(blast)
        qs = [q * ex[(2 * li) * c:(2 * li + 1) * c] for li in range(nl)] + [q]
        ks = [k * ex[(2 * li + 1) * c:(2 * li + 2) * c] for li in range(nl)] + [k]
        for h in range(GLA_HEADS):
            sk = slice(h * GLA_DK, (h + 1) * GLA_DK)
            sv = slice(h * GLA_DV, (h + 1) * GLA_DV)
            att = jnp.zeros((c, c), F32)
            for li in range(nl + 1):
                att = att + bmask_ref[d, li] * _dot_nt(qs[li][:, sk], ks[li][:, sk], HI)
            st = st_ref[h]
            v_h = v[:, sv]
            o_h = _dot_nt(qb[:, sk], st, HI) + _dot(att, v_h, HI)
            st_ref[h] = st * dtot[:, sk] + _dot_tn(v_h, kdec[:, sk], HI)
            if d == 0:
                acc_ref[rows, sv] = o_h
            else:
                o_h = acc_ref[rows, sv] + o_h
                o_ref[rows, sv] = _rms(o_h) * ng_ref[...]

    for d in range(2):
        for h in range(GLA_HEADS):
            if has_s0:
                st_ref[h] = s0_ref[d, h].T
            else:
                st_ref[h] = jnp.zeros((GLA_DV, GLA_DK), F32)

        def body(i, carry, d=d):
            chunk_step(d, i if d == 0 else nc - 1 - i)
            return carry

        lax.fori_loop(0, nc, body, 0)
        for h in range(GLA_HEADS):
            sout_ref[d, h] = st_ref[h].T


def _gla_call(proj, s0, layer, batch, seq, wts):
    has_s0 = s0 is not None
    blk = lambda w, cb: pl.BlockSpec((seq, w), lambda b: (b, cb))
    in_specs = [blk(GLA_KW, COL_GL_Q // GLA_KW), blk(GLA_KW, COL_GL_K // GLA_KW),
                blk(GLA_VW, COL_GL_V // GLA_VW), blk(LANE, COL_GL_AD // LANE)]
    args = [proj, proj, proj, proj]
    if has_s0:
        in_specs.append(pl.BlockSpec((None, None, 2, GLA_HEADS, GLA_DK, GLA_DV),
                                     lambda b: (b, layer, 0, 0, 0, 0)))
        args.append(s0)
    for w in wts:
        in_specs.append(pl.BlockSpec(w.shape, lambda b, n=w.ndim: (0,) * n))
        args.append(w)
    return pl.pallas_call(
        functools.partial(_gla_kernel, seq=seq, has_s0=has_s0),
        grid=(batch,),
        in_specs=in_specs,
        out_specs=[pl.BlockSpec((seq, GLA_VW), lambda b: (b, 0)),
                   pl.BlockSpec((None, 2, GLA_HEADS, GLA_DK, GLA_DV), lambda b: (b, 0, 0, 0, 0))],
        out_shape=[jax.ShapeDtypeStruct((batch * seq, GLA_VW), F32),
                   jax.ShapeDtypeStruct((batch, 2, GLA_HEADS, GLA_DK, GLA_DV), F32)],
        scratch_shapes=[pltpu.VMEM((seq, GLA_VW), F32),
                        pltpu.VMEM((GLA_HEADS, GLA_DV, GLA_DK), F32)],
        compiler_params=pltpu.CompilerParams(vmem_limit_bytes=VMEM_LIMIT),
        name="gla",
    )(*args)


def _mla_kernel(*refs, seq, past, tq):
    has_ctx = past > 0
    if has_ctx:
        (qd_ref, kvd_ref, kpe_ref, kpesw_ref, cckv_ref, ckpe_ref,
         qng_ref, wq_ref, wqsw_ref, kvng_ref, wk_ref, wv_ref,
         cosq_ref, sinq_ref, cosk_ref, sink_ref, epos_ref,
         o_ref, ckv_ref, kcat_ref, vv_ref) = refs
    else:
        (qd_ref, kvd_ref, kpe_ref,
         qng_ref, wq_ref, kvng_ref, wk_ref, wv_ref,
         o_ref, ckv_ref, kcat_ref, vv_ref) = refs

    @pl.when(pl.program_id(1) == 0)
    def _():
        ckv = _rms(kvd_ref[...]) * kvng_ref[...]
        ckv_ref[...] = ckv
        if has_ctx:
            kpos = kpe_ref[...] * cosk_ref[...] + kpesw_ref[...] * sink_ref[...]
        else:
            kpos = kpe_ref[...]
        segs = [(0, seq, ckv, kpos)]
        if has_ctx:
            cpos = _dot(ckpe_ref[...].astype(BF16), epos_ref[...])
            segs.append((seq, past, cckv_ref[...], cpos))
        for start, n, lat, pos in segs:
            lat16 = lat.astype(BF16)
            kc = _dot(lat16, wk_ref[...])
            for h in range(MLA_HEADS):
                kcat_ref[h, start:start + n, :] = (kc[:, h * LANE:(h + 1) * LANE] + pos).astype(BF16)
            vv_ref[start:start + n, :] = _dot(lat16, wv_ref[...]).astype(BF16)

    qlat = (_rms(qd_ref[...]) * qng_ref[...]).astype(BF16)
    qc = _dot(qlat, wq_ref[...])
    if has_ctx:
        qsw = _dot(qlat, wqsw_ref[...])
    for h in range(MLA_HEADS):
        hl = slice(h * LANE, (h + 1) * LANE)
        q_raw = qc[:, hl]
        if has_ctx:
            q_self = (q_raw * cosq_ref[...] + qsw[:, hl] * sinq_ref[...]).astype(BF16)
        else:
            q_self = q_raw.astype(BF16)
        s1 = _dot_nt(q_self, kcat_ref[h, 0:seq, :])
        m = jnp.max(s1, axis=-1, keepdims=True)
        if has_ctx:
            s2 = _dot_nt(q_raw.astype(BF16), kcat_ref[h, seq:seq + past, :])
            m = jnp.maximum(m, jnp.max(s2, axis=-1, keepdims=True))
        p1 = jnp.exp(s1 - m)
        l = jnp.sum(p1, axis=-1, keepdims=True)
        vl = slice(h * MLA_V, (h + 1) * MLA_V)
        o_h = _dot(p1.astype(BF16), vv_ref[0:seq, vl])
        if has_ctx:
            p2 = jnp.exp(s2 - m)
            l = l + jnp.sum(p2, axis=-1, keepdims=True)
            o_h = o_h + _dot(p2.astype(BF16), vv_ref[seq:seq + past, vl])
        o_ref[:, vl] = o_h / l


def _mla_call(proj, cache_ckv, cache_kpe, layer, batch, seq, wts, tables):
    has_ctx = cache_ckv is not None
    past = cache_ckv.shape[2] if has_ctx else 0
    tq = 256
    nq = seq // tq
    full = lambda cb: pl.BlockSpec((seq, LANE), lambda b, i: (b, cb))
    in_specs = [pl.BlockSpec((tq, MLA_Q_RANK), lambda b, i: (b * nq + i, COL_ML_QD // MLA_Q_RANK)),
                full(COL_ML_KVD // LANE), full(COL_KPE // LANE)]
    args = [proj, proj, proj]
    if has_ctx:
        in_specs += [full(COL_KPE_SW // LANE),
                     pl.BlockSpec((None, None, past, MLA_KV_RANK), lambda b, i: (b, layer, 0, 0)),
                     pl.BlockSpec((None, None, past, MLA_ROPE), lambda b, i: (b, layer, 0, 0))]
        args += [proj, cache_ckv, cache_kpe]
    qn_g, wq_cat, wq_sw, kvn_g, wk_pad, wv = wts
    const = lambda w: pl.BlockSpec(w.shape, lambda b, i, n=w.ndim: (0,) * n)
    if has_ctx:
        cosq, sinq, cosk, sink, epos = tables
        wlist = [qn_g, wq_cat, wq_sw, kvn_g, wk_pad, wv]
        in_specs += [const(w) for w in wlist]
        in_specs += [pl.BlockSpec((tq, LANE), lambda b, i: (i, 0)), pl.BlockSpec((tq, LANE), lambda b, i: (i, 0)),
                     const(cosk), const(sink), const(epos)]
        args += wlist + [cosq, sinq, cosk, sink, epos]
    else:
        wlist = [qn_g, wq_cat, kvn_g, wk_pad, wv]
        in_specs += [const(w) for w in wlist]
        args += wlist
    return pl.pallas_call(
        functools.partial(_mla_kernel, seq=seq, past=past, tq=tq),
        grid=(batch, nq),
        in_specs=in_specs,
        out_specs=[pl.BlockSpec((tq, MLA_VW), lambda b, i: (b * nq + i, 0)),
                   pl.BlockSpec((seq, MLA_KV_RANK), lambda b, i: (b, 0))],
        out_shape=[jax.ShapeDtypeStruct((batch * seq, MLA_VW), F32),
                   jax.ShapeDtypeStruct((batch * seq, MLA_KV_RANK), F32)],
        scratch_shapes=[pltpu.VMEM((MLA_HEADS, seq + past, LANE), BF16),
                        pltpu.VMEM((seq + past, MLA_VW), BF16)],
        compiler_params=pltpu.CompilerParams(vmem_limit_bytes=VMEM_LIMIT,
                                             dimension_semantics=("arbitrary", "arbitrary")),
        name="mla",
    )(*args)


def _outproj_kernel(x_ref, oa_ref, ob_ref, oc_ref, ga_ref, gb_ref, gc_ref, ma_ref, mb_ref, mc_ref,
                    mod_ref, wa_ref, wb_ref, wc_ref, wo_ref, fg_ref, o_ref, *, final):
    def branch(o_r, g_r, w_r):
        g = g_r[...]
        return _dot((o_r[...] * (g * jax.nn.sigmoid(g))).astype(BF16), w_r[...])

    y = (jax.nn.sigmoid(ma_ref[...]) * branch(oa_ref, ga_ref, wa_ref)
         + jax.nn.sigmoid(mb_ref[...]) * branch(ob_ref, gb_ref, wb_ref)
         + jax.nn.sigmoid(mc_ref[...]) * branch(oc_ref, gc_ref, wc_ref))
    y = _dot(y.astype(BF16), wo_ref[...])
    gate = mod_ref[...][:, 2 * D_MODEL:3 * D_MODEL]
    hn = x_ref[...] + gate * y
    if final:
        hn = _rms(hn) * fg_ref[...]
    o_ref[...] = hn


def _outproj_call(x2d, o_a, o_b, o_c, proj, mod3, wts, final_g, rows_per_mod, mod_base, final):
    rows = x2d.shape[0]
    tm = 256
    tiles_per_mod = rows_per_mod // tm
    row = lambda w, cb=0: pl.BlockSpec((tm, w), lambda i: (i, cb))
    const = lambda w: pl.BlockSpec(w.shape, lambda i, n=w.ndim: (0,) * n)
    in_specs = [row(D_MODEL), row(512), row(512), row(512),
                row(512, COL_GATES // 512), row(512, COL_GATES // 512 + 1), row(512, COL_GATES // 512 + 2),
                row(D_MODEL, COL_MERGE // D_MODEL), row(D_MODEL, COL_MERGE // D_MODEL + 1),
                row(D_MODEL, COL_MERGE // D_MODEL + 2),
                pl.BlockSpec((None, 1, 3 * D_MODEL), lambda i: (mod_base + i // tiles_per_mod, 0, 0))]
    in_specs += [const(w) for w in wts] + [const(final_g)]
    return pl.pallas_call(
        functools.partial(_outproj_kernel, final=final),
        grid=(rows // tm,),
        in_specs=in_specs,
        out_specs=row(D_MODEL),
        out_shape=jax.ShapeDtypeStruct((rows, D_MODEL), F32),
        compiler_params=pltpu.CompilerParams(vmem_limit_bytes=VMEM_LIMIT),
        name="outproj",
    )(x2d, o_a, o_b, o_c, proj, proj, proj, proj, proj, proj, mod3, *wts, final_g)


def _pack_w_in(w):
    seg = lambda i: w[:, _IN_OFF[i]:_IN_OFF[i + 1]]
    z = lambda n: jnp.zeros((D_MODEL, n), w.dtype)
    kpe = seg(13)
    q = MLA_ROPE // 4
    kpe_sw = jnp.concatenate([kpe[:, q:2 * q], kpe[:, 0:q], kpe[:, 3 * q:4 * q], kpe[:, 2 * q:3 * q]], axis=1)
    tail = LANE - KPE_LANE - MLA_ROPE
    cols = [w[:, _IN_OFF[0]:_IN_OFF[5]],
            seg(9), z(LANE - GLA_GATE_RANK),
            seg(11),
            seg(6), seg(7), seg(8),
            seg(12),
            z(KPE_LANE), kpe, z(tail),
            z(KPE_LANE), kpe_sw, z(tail),
            z(COL_GATES - COL_KPE_SW - LANE),
            seg(5), seg(10), seg(14),
            seg(15)]
    return jnp.concatenate(cols, axis=1).astype(BF16)


def _rope_lane_tables(seq):
    n_freq = MLA_ROPE // 4
    t = np.arange(seq)
    inv = ROPE_THETA ** (-np.arange(n_freq, dtype=np.float64) / n_freq)
    ang = np.stack([(t // GRID_W)[:, None] * inv, (t % GRID_W)[:, None] * inv], axis=1)
    cos = np.repeat(np.cos(ang)[:, :, None, :], 2, axis=2).reshape(seq, MLA_ROPE)
    sin = np.stack([-np.sin(ang), np.sin(ang)], axis=2).reshape(seq, MLA_ROPE)
    return cos, sin


def _mla_tables(seq):
    cos, sin = _rope_lane_tables(seq)
    cosq = np.zeros((seq, LANE), np.float32)
    sinq = np.zeros((seq, LANE), np.float32)
    cosq[:, :KPE_LANE] = 1.0
    cosq[:, KPE_LANE:KPE_LANE + MLA_ROPE] = cos
    sinq[:, KPE_LANE:KPE_LANE + MLA_ROPE] = sin
    cosk = np.zeros((seq, LANE), np.float32)
    cosk[:, KPE_LANE:KPE_LANE + MLA_ROPE] = cos
    epos = np.zeros((MLA_ROPE, LANE), np.float32)
    epos[np.arange(MLA_ROPE), KPE_LANE + np.arange(MLA_ROPE)] = 1.0
    return (jnp.asarray(cosq), jnp.asarray(sinq), jnp.asarray(cosk), jnp.asarray(sinq),
            jnp.asarray(epos, dtype=BF16))


def _pack_mla_weights(qn_g, wq_up, kvn_g, wkv_up):
    scale = (MLA_NOPE + MLA_ROPE) ** -0.5
    wq = wq_up.reshape(MLA_Q_RANK, MLA_HEADS, MLA_NOPE + MLA_ROPE) * scale
    nope, rope = wq[..., :MLA_NOPE], wq[..., MLA_NOPE:]
    q = MLA_ROPE // 4
    rope_sw = jnp.concatenate([rope[..., q:2 * q], rope[..., 0:q], rope[..., 3 * q:4 * q], rope[..., 2 * q:3 * q]], -1)
    tail = jnp.zeros((MLA_Q_RANK, MLA_HEADS, LANE - KPE_LANE - MLA_ROPE), wq.dtype)
    wq_cat = jnp.concatenate([nope, rope, tail], -1).reshape(MLA_Q_RANK, MLA_HEADS * LANE).astype(BF16)
    wq_sw = jnp.concatenate([jnp.zeros_like(nope), rope_sw, tail], -1).reshape(MLA_Q_RANK, MLA_HEADS * LANE).astype(BF16)
    wkv = wkv_up.reshape(MLA_KV_RANK, MLA_HEADS, MLA_NOPE + MLA_V)
    wk = jnp.concatenate([wkv[..., :MLA_NOPE], jnp.zeros((MLA_KV_RANK, MLA_HEADS, LANE - MLA_NOPE), wkv.dtype)], -1)
    wk_pad = wk.reshape(MLA_KV_RANK, MLA_HEADS * LANE).astype(BF16)
    wv = wkv[..., MLA_NOPE:].reshape(MLA_KV_RANK, MLA_VW).astype(BF16)
    return (qn_g.reshape(1, -1), wq_cat, wq_sw, kvn_g.reshape(1, -1), wk_pad, wv)


def _head_sum_matrix():
    lane = np.arange(RW_WIDTH)
    return jnp.asarray((lane[:, None] // RW_HEAD == lane[None, :] // RW_HEAD).astype(np.float32))


def _trunk(x_prompt, x_sample, c, cache_ckv, cache_kpe, state_rwkv, state_gla, c_ctx,
           norm_g, w_mod, b_mod, w_in, rw_w0, rw_w2, rw_a0, rw_a2, rw_k_k, rw_k_a, rw_r_k,
           rw_ln_g, rw_ln_b, rw_out, gla_a2, gla_ab, gla_norm_g, gla_out,
           mla_qn_g, mla_wq_up, mla_kvn_g, mla_wkv_up, mla_out, w_out, final_g):
    bp, tp, _ = x_prompt.shape
    bs, ts, _ = x_sample.shape
    hp = x_prompt.reshape(bp * tp, D_MODEL)
    hs = x_sample.reshape(bs * ts, D_MODEL)
    cvec8 = jnp.concatenate([c, c_ctx[None, :], jnp.zeros((8 - bs - 1, D_MODEL), F32)], axis=0)
    ctx_row = bs
    hsum = _head_sum_matrix()
    mexp, bmask, lincl = (jnp.asarray(a) for a in _gla_constants())
    tables = _mla_tables(ts)
    fg = final_g.reshape(1, D_MODEL)
    ckv_l, kpe_l, rw_l, gla_l = [], [], [], []
    for l in range(DEPTH):
        mod3 = _mod_call(cvec8, w_mod, b_mod, l).reshape(8, 1, 3 * D_MODEL)
        w_packed = _pack_w_in(w_in[l])
        ng = norm_g[l].reshape(1, D_MODEL)
        row = lambda a: a.reshape(1, -1)
        rw_wts = (rw_w0[l], rw_w2[l], rw_a0[l], rw_a2[l], row(rw_k_k[l]), row(rw_k_a[l]), row(rw_r_k[l]),
                  row(rw_ln_g[l]), row(rw_ln_b[l]), hsum)
        a2p = jnp.concatenate([gla_a2[l], jnp.zeros((2, LANE - GLA_GATE_RANK, GLA_KW), F32)], axis=1)
        gla_wts = (a2p, gla_ab[l], row(gla_norm_g[l]), mexp, bmask, lincl)
        mla_wts = _pack_mla_weights(mla_qn_g[l], mla_wq_up[l], mla_kvn_g[l], mla_wkv_up[l])
        out_wts = (rw_out[l].astype(BF16), gla_out[l].astype(BF16), mla_out[l].astype(BF16), w_out[l].astype(BF16))
        final = l == DEPTH - 1

        proj = _inproj_call(hp, mod3, ng, w_packed, bp * tp, ctx_row)
        o_a, s_rw = _rwkv_call(proj, None, l, bp, tp, rw_wts)
        o_b, s_gla = _gla_call(proj, None, l, bp, tp, gla_wts)
        o_c, ckv = _mla_call(proj, None, None, l, bp, tp, mla_wts, None)
        ckv_l.append(ckv.reshape(bp, tp, MLA_KV_RANK))
        kpe_l.append(proj[:, COL_KPE + KPE_LANE:COL_KPE + KPE_LANE + MLA_ROPE].reshape(bp, tp, MLA_ROPE))
        rw_l.append(s_rw)
        gla_l.append(s_gla)
        hp = _outproj_call(hp, o_a, o_b, o_c, proj, mod3, out_wts, fg, bp * tp, ctx_row, final)

        proj = _inproj_call(hs, mod3, ng, w_packed, ts, 0)
        o_a, _ = _rwkv_call(proj, state_rwkv, l, bs, ts, rw_wts)
        o_b, _ = _gla_call(proj, state_gla, l, bs, ts, gla_wts)
        o_c, _ = _mla_call(proj, cache_ckv, cache_kpe, l, bs, ts, mla_wts, tables)
        hs = _outproj_call(hs, o_a, o_b, o_c, proj, mod3, out_wts, fg, ts, 0, final)

    return (hp.reshape(bp, tp, D_MODEL), hs.reshape(bs, ts, D_MODEL),
            jnp.stack(ckv_l, axis=1), jnp.stack(kpe_l, axis=1),
            jnp.stack(rw_l, axis=1), jnp.stack(gla_l, axis=1))


_trunk_jit = jax.jit(_trunk)


def kernel(x_prompt, x_sample, c, cache_ckv, cache_kpe, state_rwkv, state_gla, c_ctx, norm_g, w_mod, b_mod, w_in, rw_w0, rw_w2, rw_a0, rw_a2, rw_k_k, rw_k_a, rw_r_k, rw_ln_g, rw_ln_b, rw_out, gla_a2, gla_ab, gla_norm_g, gla_out, mla_qn_g, mla_wq_up, mla_kvn_g, mla_wkv_up, mla_out, w_out, final_g):
    return _trunk_jit(x_prompt, x_sample, c, cache_ckv, cache_kpe, state_rwkv, state_gla, c_ctx, norm_g, w_mod, b_mod, w_in, rw_w0, rw_w2, rw_a0, rw_a2, rw_k_k, rw_k_a, rw_r_k, rw_ln_g, rw_ln_b, rw_out, gla_a2, gla_ab, gla_norm_g, gla_out, mla_qn_g, mla_wq_up, mla_kvn_g, mla_wkv_up, mla_out, w_out, final_g)
```

```python
import functools

import numpy as np
import jax
import jax.numpy as jnp
from jax import lax
from jax.experimental import pallas as pl
from jax.experimental.pallas import tpu as pltpu

F32 = jnp.float32
BF16 = jnp.bfloat16
HI = lax.Precision.HIGHEST

D_MODEL = 1024
DEPTH = 2
GRID_W = 64
NORM_EPS = 1e-6
RW_HEADS = 8
RW_HEAD = 64
RW_WIDTH = RW_HEADS * RW_HEAD
RW_RANK = 64
RW_GN_EPS = 64e-5
GLA_HEADS = 4
GLA_DK = 64
GLA_DV = 128
GLA_KW = GLA_HEADS * GLA_DK
GLA_VW = GLA_HEADS * GLA_DV
GLA_GATE_RANK = 16
GLA_LOGIT_NORM = 16.0
MLA_HEADS = 8
MLA_NOPE = 64
MLA_ROPE = 32
MLA_V = 64
MLA_Q_RANK = 256
MLA_KV_RANK = 128
MLA_VW = MLA_HEADS * MLA_V
ROPE_THETA = 10000.0
N_BRANCH = 3

_IN_SIZES = (RW_WIDTH, RW_WIDTH, RW_WIDTH, RW_RANK, RW_RANK, RW_WIDTH,
             GLA_KW, GLA_KW, GLA_VW, GLA_GATE_RANK, GLA_VW,
             MLA_Q_RANK, MLA_KV_RANK, MLA_ROPE, MLA_VW, N_BRANCH * D_MODEL)
_IN_OFF = tuple(int(v) for v in np.concatenate([[0], np.cumsum(_IN_SIZES)]))

LANE = 128
COL_RW_R = 0
COL_RW_K = 512
COL_RW_V = 1024
COL_RW_WA = 1536
COL_GL_AD = 1664
COL_ML_QD = 1792
COL_GL_Q = 2048
COL_GL_K = 2304
COL_GL_V = 2560
COL_ML_KVD = 3072
COL_KPE = 3200
COL_KPE_SW = 3328
COL_GATES = 3584
COL_MERGE = 5120
PACK_W = 8192
KPE_LANE = MLA_NOPE

CHUNK = 64
VMEM_LIMIT = 48 * 1024 * 1024


def _dot(a, b, prec=None):
    return jnp.dot(a, b, preferred_element_type=F32, precision=prec)


def _dot_nt(a, b, prec=None):
    return lax.dot_general(a, b, (((1,), (1,)), ((), ())), preferred_element_type=F32, precision=prec)


def _dot_tn(a, b, prec=None):
    return lax.dot_general(a, b, (((0,), (0,)), ((), ())), preferred_element_type=F32, precision=prec)


def _split(x):
    hi = x.astype(BF16)
    return hi, (x - hi.astype(F32)).astype(BF16)


def _dot_split(a, b):
    ah, al = _split(a)
    bh, bl = _split(b)
    return _dot(ah, bh) + _dot(al, bh) + _dot(ah, bl)


def _dot_sel(sel16, x):
    xh, xl = _split(x)
    return _dot(sel16, xh) + _dot(sel16, xl)


def _softplus(z):
    return jnp.maximum(z, 0.0) + jnp.log1p(jnp.exp(-jnp.abs(z)))


def _rms(x, eps=NORM_EPS):
    return x * lax.rsqrt(jnp.mean(x * x, axis=-1, keepdims=True) + eps)


def _mod_kernel(c_ref, w_ref, b_ref, o_ref):
    c = c_ref[...]
    o_ref[...] = _dot(c * jax.nn.sigmoid(c), w_ref[...], HI) + b_ref[...]


def _mod_call(cvec8, w_mod, b_mod, layer):
    tn = 1024
    return pl.pallas_call(
        _mod_kernel,
        grid=(3 * D_MODEL // tn,),
        in_specs=[pl.BlockSpec((8, D_MODEL), lambda j: (0, 0)),
                  pl.BlockSpec((None, D_MODEL, tn), lambda j: (layer, 0, j)),
                  pl.BlockSpec((None, 1, tn), lambda j: (layer, 0, j))],
        out_specs=pl.BlockSpec((8, tn), lambda j: (0, j)),
        out_shape=jax.ShapeDtypeStruct((8, 3 * D_MODEL), F32),
        compiler_params=pltpu.CompilerParams(vmem_limit_bytes=VMEM_LIMIT),
        name="mod",
    )(cvec8, w_mod, b_mod.reshape(DEPTH, 1, 3 * D_MODEL))


def _inproj_kernel(x_ref, mod_ref, g_ref, w_ref, o_ref):
    x = x_ref[...]
    m = mod_ref[...]
    shift = m[:, 0:D_MODEL]
    scale = m[:, D_MODEL:2 * D_MODEL]
    h = _rms(x) * g_ref[...] * (1.0 + scale) + shift
    o_ref[...] = _dot(h.astype(BF16), w_ref[...])


def _inproj_call(x2d, mod3, norm_g, w_packed, rows_per_mod, mod_base):
    rows = x2d.shape[0]
    tm, tn = 512, 2048
    tiles_per_mod = rows_per_mod // tm
    return pl.pallas_call(
        _inproj_kernel,
        grid=(PACK_W // tn, rows // tm),
        in_specs=[pl.BlockSpec((tm, D_MODEL), lambda j, i: (i, 0)),
                  pl.BlockSpec((None, 1, 3 * D_MODEL), lambda j, i: (mod_base + i // tiles_per_mod, 0, 0)),
                  pl.BlockSpec((1, D_MODEL), lambda j, i: (0, 0)),
                  pl.BlockSpec((D_MODEL, tn), lambda j, i: (0, j))],
        out_specs=pl.BlockSpec((tm, tn), lambda j, i: (i, j)),
        out_shape=jax.ShapeDtypeStruct((rows, PACK_W), F32),
        compiler_params=pltpu.CompilerParams(vmem_limit_bytes=VMEM_LIMIT),
        name="inproj",
    )(x2d, mod3, norm_g, w_packed)


RW_CHUNK = 64


def _rwkv_pair_mask(c, reverse):
    row = lax.broadcasted_iota(jnp.int32, (2 * c, 2 * c), 0)
    col = lax.broadcasted_iota(jnp.int32, (2 * c, 2 * c), 1)
    t = jnp.where(row >= c, row - c, row)
    s = jnp.where(col >= c, col - c, col)
    earlier = (s > t) if reverse else (s < t)
    return earlier | ((row >= c) & (s == t))


def _rwkv_kernel(*refs, seq, has_s0):
    if has_s0:
        r_ref, k_ref, v_ref, wa_ref, s0_ref = refs[:5]
        rest = refs[5:]
    else:
        r_ref, k_ref, v_ref, wa_ref = refs[:4]
        s0_ref = None
        rest = refs[4:]
    (w0_ref, w2_ref, a0_ref, a2_ref, kk_ref, ka_ref, rk_ref, lng_ref, lnb_ref, hsum_ref,
     o_ref, sout_ref, acc_ref, kn_ref, st_ref) = rest
    c = RW_CHUNK
    nc = seq // c
    hsum = hsum_ref[...]

    def chunk_step(d, ci, incl16, pair_mask):
        rows = pl.ds(pl.multiple_of(ci * c, c), c)
        r = r_ref[rows, :]
        k = k_ref[rows, :]
        v = v_ref[rows, :]
        wa = wa_ref[rows, :]
        wd = wa[:, 0:RW_RANK]
        ad = wa[:, RW_RANK:2 * RW_RANK]
        if d == 0:
            kk0 = k * kk_ref[...]
            red = _dot(jnp.concatenate([kk0 * kk0, r * k * rk_ref[...]], axis=0).astype(BF16), hsum)
            kk = kk0 / jnp.maximum(jnp.sqrt(red[0:c]), 1e-12)
            kn_ref[rows, :] = kk
            o_ref[rows, :] = red[c:2 * c] * v
        else:
            kk = kn_ref[rows, :]
        w_log = -_softplus(-(w0_ref[d:d + 1, :] + _dot_split(jnp.tanh(wd), w2_ref[d]))) - 0.5
        lw = -jnp.exp(w_log)
        a = jax.nn.sigmoid(a0_ref[d:d + 1, :] + _dot_split(ad, a2_ref[d]))
        kd = k * (1.0 + (a - 1.0) * ka_ref[...])
        bv = kk * a
        gam = _dot_sel(incl16, lw)
        last = 0 if d == 1 else c - 1
        gtot = gam[last:last + 1, :]
        gref = gam[c // 2:c // 2 + 1, :]
        e_in = jnp.exp(gam - gref)
        e_out = jnp.exp(gref - gam)
        e_end = jnp.exp(gtot - gam)
        lhs = jnp.concatenate([-kk * (e_in * jnp.exp(-lw)), r * e_in], axis=0).astype(BF16)
        rhs = jnp.concatenate([bv * e_out, kd * e_out], axis=0).astype(BF16)
        end = jnp.concatenate([bv * e_end, kd * e_end], axis=0).astype(BF16)
        v16 = v.astype(BF16)
        dtot = jnp.exp(gtot)
        eref = jnp.exp(gref)
        zeros16 = jnp.zeros((c, RW_HEAD), BF16)
        heads = range(RW_HEADS)
        sls = [slice(h * RW_HEAD, (h + 1) * RW_HEAD) for h in heads]
        s0 = [st_ref[h] for h in heads]
        lhs_h = [lhs[:, sl] for sl in sls]
        v_h = [v16[:, sl] for sl in sls]
        aa = [jnp.where(pair_mask, _dot_nt(lhs_h[h], rhs[:, sls[h]]), 0.0) for h in heads]
        sp = [_dot_nt(lhs_h[h], (s0[h] * eref[:, sls[h]]).astype(BF16)) for h in heads]
        aa16 = [x.astype(BF16) for x in aa]
        u = [_dot(aa16[h][0:c], jnp.concatenate([zeros16, v_h[h]], axis=0)) + sp[h][0:c] for h in heads]
        p16 = [aa16[h][0:c, 0:c] for h in heads]
        u = [u[h] + _dot(p16[h], u[h].astype(BF16)) for h in heads]
        covered = 2
        while covered < c:
            p16 = [_dot(p, p).astype(BF16) for p in p16]
            u = [u[h] + _dot(p16[h], u[h].astype(BF16)) for h in heads]
            covered *= 2
        uv = [jnp.concatenate([u[h].astype(BF16), v_h[h]], axis=0) for h in heads]
        for h in heads:
            o_h = sp[h][c:2 * c] + _dot(aa16[h][c:2 * c], uv[h])
            st_ref[h] = s0[h] * dtot[:, sls[h]] + _dot_tn(uv[h], end[:, sls[h]])
            if d == 0:
                acc_ref[rows, sls[h]] = o_h
            else:
                acc_ref[rows, sls[h]] = acc_ref[rows, sls[h]] + o_h
        if d == 1:
            o = acc_ref[rows, :]
            mu = _dot(o.astype(BF16), hsum) * (1.0 / RW_HEAD)
            dev = o - mu
            var = _dot((dev * dev).astype(BF16), hsum) * (1.0 / RW_HEAD)
            o = dev * lax.rsqrt(var + RW_GN_EPS) * lng_ref[...] + lnb_ref[...]
            o_ref[rows, :] = o_ref[rows, :] + o

    for d in range(2):
        if has_s0:
            st_ref[...] = s0_ref[d]
        else:
            st_ref[...] = jnp.zeros_like(st_ref)
        row = lax.broadcasted_iota(jnp.int32, (c, c), 0)
        col = lax.broadcasted_iota(jnp.int32, (c, c), 1)
        incl16 = jnp.where((col >= row) if d == 1 else (col <= row), 1.0, 0.0).astype(BF16)
        pair_mask = _rwkv_pair_mask(c, d == 1)

        def body(i, carry, d=d, incl16=incl16, pair_mask=pair_mask):
            chunk_step(d, i if d == 0 else nc - 1 - i, incl16, pair_mask)
            return carry

        lax.fori_loop(0, nc, body, 0)
        sout_ref[d] = st_ref[...]


def _rwkv_call(proj, s0, layer, batch, seq, wts):
    has_s0 = s0 is not None
    blk = lambda w, cb: pl.BlockSpec((seq, w), lambda b: (b, cb))
    in_specs = [blk(512, COL_RW_R // 512), blk(512, COL_RW_K // 512), blk(512, COL_RW_V // 512),
                blk(LANE, COL_RW_WA // LANE)]
    args = [proj, proj, proj, proj]
    if has_s0:
        in_specs.append(pl.BlockSpec((None, None, 2, RW_HEADS, RW_HEAD, RW_HEAD),
                                     lambda b: (b, layer, 0, 0, 0, 0)))
        args.append(s0)
    for w in wts:
        in_specs.append(pl.BlockSpec(w.shape, lambda b, n=w.ndim: (0,) * n))
        args.append(w)
    return pl.pallas_call(
        functools.partial(_rwkv_kernel, seq=seq, has_s0=has_s0),
        grid=(batch,),
        in_specs=in_specs,
        out_specs=[pl.BlockSpec((seq, RW_WIDTH), lambda b: (b, 0)),
                   pl.BlockSpec((None, 2, RW_HEADS, RW_HEAD, RW_HEAD), lambda b: (b, 0, 0, 0, 0))],
        out_shape=[jax.ShapeDtypeStruct((batch * seq, RW_WIDTH), F32),
                   jax.ShapeDtypeStruct((batch, 2, RW_HEADS, RW_HEAD, RW_HEAD), F32)],
        scratch_shapes=[pltpu.VMEM((seq, RW_WIDTH), F32),
                        pltpu.VMEM((seq, RW_WIDTH), F32),
                        pltpu.VMEM((RW_HEADS, RW_HEAD, RW_HEAD), F32)],
        compiler_params=pltpu.CompilerParams(vmem_limit_bytes=VMEM_LIMIT),
        name="rwkv",
    )(*args)


GLA_LEVELS = (32, 16, 8, 4, 2, 1)


def _gla_constants():
    c = CHUNK
    nl = len(GLA_LEVELS)
    mexp = np.zeros((2, (2 * nl + 1) * c, c), np.float32)
    bmask = np.zeros((2, nl + 1, c, c), np.float32)
    for d in range(2):
        pos = np.arange(c) if d == 0 else c - 1 - np.arange(c)
        pt = pos[:, None]
        pj = pos[None, :]
        mexp[d, 2 * nl * c:] = (pj <= pt)
        for li, m in enumerate(GLA_LEVELS):
            mid = (pos // (2 * m)) * (2 * m) + m
            second = pos >= mid
            mq = (pj >= mid[:, None]) & (pj <= pt) & second[:, None]
            mk = (pj > pt) & (pj <= mid[:, None] - 1) & (~second)[:, None]
            mexp[d, (2 * li) * c:(2 * li + 1) * c] = mq
            mexp[d, (2 * li + 1) * c:(2 * li + 2) * c] = mk
            same = (pos[:, None] // (2 * m)) == (pos[None, :] // (2 * m))
            bmask[d, li] = same & second[:, None] & (~second)[None, :]
        bmask[d, nl] = np.eye(c)
    return mexp, bmask


def _gla_kernel(*refs, seq, has_s0):
    if has_s0:
        q_ref, k_ref, v_ref, ad_ref, s0_ref = refs[:5]
        rest = refs[5:]
    else:
        q_ref, k_ref, v_ref, ad_ref = refs[:4]
        s0_ref = None
        rest = refs[4:]
    (a2_ref, ab_ref, ng_ref, mexp_ref, bmask_ref,
     o_ref, sout_ref, acc_ref, st_ref) = rest
    c = CHUNK
    nc = seq // c
    nl = len(GLA_LEVELS)

    def chunk_step(d, ci):
        rows = pl.ds(pl.multiple_of(ci * c, c), c)
        q = q_ref[rows, :] * (GLA_DK ** -0.5)
        k = k_ref[rows, :]
        v = v_ref[rows, :]
        x = _dot_split(ad_ref[rows, :], a2_ref[d]) + ab_ref[d:d + 1, :]
        g = (jnp.minimum(x, 0.0) - jnp.log1p(jnp.exp(-jnp.abs(x)))) * (1.0 / GLA_LOGIT_NORM)
        sums = _dot_sel(mexp_ref[d], g)
        ex = jnp.exp(sums[0:2 * nl * c])
        b = sums[2 * nl * c:(2 * nl + 1) * c]
        last = 0 if d == 1 else c - 1
        blast = b[last:last + 1, :]
        qb = (q * jnp.exp(b)).astype(BF16)
        kdec = (k * jnp.exp(blast - b)).astype(BF16)
        dtot = jnp.exp(blast)
        qs = [(q * ex[(2 * li) * c:(2 * li + 1) * c]).astype(BF16) for li in range(nl)] + [q.astype(BF16)]
        ks = [(k * ex[(2 * li + 1) * c:(2 * li + 2) * c]).astype(BF16) for li in range(nl)] + [k.astype(BF16)]
        v16 = v.astype(BF16)
        for h in range(GLA_HEADS):
            sk = slice(h * GLA_DK, (h + 1) * GLA_DK)
            sv = slice(h * GLA_DV, (h + 1) * GLA_DV)
            att = jnp.zeros((c, c), F32)
            for li in range(nl + 1):
                att = att + bmask_ref[d, li] * _dot_nt(qs[li][:, sk], ks[li][:, sk])
            st = st_ref[h]
            v_h = v16[:, sv]
            o_h = _dot_nt(qb[:, sk], st.astype(BF16)) + _dot(att.astype(BF16), v_h)
            st_ref[h] = st * dtot[:, sk] + _dot_tn(v_h, kdec[:, sk])
            if d == 0:
                acc_ref[rows, sv] = o_h
            else:
                o_h = acc_ref[rows, sv] + o_h
                o_ref[rows, sv] = _rms(o_h) * ng_ref[...]

    for d in range(2):
        for h in range(GLA_HEADS):
            if has_s0:
                st_ref[h] = s0_ref[d, h].T
            else:
                st_ref[h] = jnp.zeros((GLA_DV, GLA_DK), F32)

        def body(i, carry, d=d):
            chunk_step(d, i if d == 0 else nc - 1 - i)
            return carry

        lax.fori_loop(0, nc, body, 0)
        for h in range(GLA_HEADS):
            sout_ref[d, h] = st_ref[h].T


def _gla_call(proj, s0, layer, batch, seq, wts):
    has_s0 = s0 is not None
    blk = lambda w, cb: pl.BlockSpec((seq, w), lambda b: (b, cb))
    in_specs = [blk(GLA_KW, COL_GL_Q // GLA_KW), blk(GLA_KW, COL_GL_K // GLA_KW),
                blk(GLA_VW, COL_GL_V // GLA_VW), blk(LANE, COL_GL_AD // LANE)]
    args = [proj, proj, proj, proj]
    if has_s0:
        in_specs.append(pl.BlockSpec((None, None, 2, GLA_HEADS, GLA_DK, GLA_DV),
                                     lambda b: (b, layer, 0, 0, 0, 0)))
        args.append(s0)
    for w in wts:
        in_specs.append(pl.BlockSpec(w.shape, lambda b, n=w.ndim: (0,) * n))
        args.append(w)
    return pl.pallas_call(
        functools.partial(_gla_kernel, seq=seq, has_s0=has_s0),
        grid=(batch,),
        in_specs=in_specs,
        out_specs=[pl.BlockSpec((seq, GLA_VW), lambda b: (b, 0)),
                   pl.BlockSpec((None, 2, GLA_HEADS, GLA_DK, GLA_DV), lambda b: (b, 0, 0, 0, 0))],
        out_shape=[jax.ShapeDtypeStruct((batch * seq, GLA_VW), F32),
                   jax.ShapeDtypeStruct((batch, 2, GLA_HEADS, GLA_DK, GLA_DV), F32)],
        scratch_shapes=[pltpu.VMEM((seq, GLA_VW), F32),
                        pltpu.VMEM((GLA_HEADS, GLA_DV, GLA_DK), F32)],
        compiler_params=pltpu.CompilerParams(vmem_limit_bytes=VMEM_LIMIT),
        name="gla",
    )(*args)


def _mla_kernel(*refs, seq, past, tq):
    has_ctx = past > 0
    if has_ctx:
        (qd_ref, kvd_ref, kpe_ref, kpesw_ref, cckv_ref, ckpe_ref,
         qng_ref, wq_ref, wqsw_ref, kvng_ref, wk_ref, wv_ref,
         cosq_ref, sinq_ref, cosk_ref, sink_ref, epos_ref,
         o_ref, ckv_ref, kcat_ref, vv_ref) = refs
    else:
        (qd_ref, kvd_ref, kpe_ref,
         qng_ref, wq_ref, kvng_ref, wk_ref, wv_ref,
         o_ref, ckv_ref, kcat_ref, vv_ref) = refs

    @pl.when(pl.program_id(1) == 0)
    def _():
        ckv = _rms(kvd_ref[...]) * kvng_ref[...]
        ckv_ref[...] = ckv
        if has_ctx:
            kpos = kpe_ref[...] * cosk_ref[...] + kpesw_ref[...] * sink_ref[...]
        else:
            kpos = kpe_ref[...]
        segs = [(0, seq, ckv, kpos)]
        if has_ctx:
            cpos = _dot(ckpe_ref[...].astype(BF16), epos_ref[...])
            segs.append((seq, past, cckv_ref[...], cpos))
        for start, n, lat, pos in segs:
            lat16 = lat.astype(BF16)
            kc = _dot(lat16, wk_ref[...])
            for h in range(MLA_HEADS):
                kcat_ref[h, start:start + n, :] = (kc[:, h * LANE:(h + 1) * LANE] + pos).astype(BF16)
            vv_ref[start:start + n, :] = _dot(lat16, wv_ref[...]).astype(BF16)

    qlat = (_rms(qd_ref[...]) * qng_ref[...]).astype(BF16)
    qc = _dot(qlat, wq_ref[...])
    if has_ctx:
        qsw = _dot(qlat, wqsw_ref[...])
    for h in range(MLA_HEADS):
        hl = slice(h * LANE, (h + 1) * LANE)
        q_raw = qc[:, hl]
        if has_ctx:
            q_self = (q_raw * cosq_ref[...] + qsw[:, hl] * sinq_ref[...]).astype(BF16)
        else:
            q_self = q_raw.astype(BF16)
        s1 = _dot_nt(q_self, kcat_ref[h, 0:seq, :])
        m = jnp.max(s1, axis=-1, keepdims=True)
        if has_ctx:
            s2 = _dot_nt(q_raw.astype(BF16), kcat_ref[h, seq:seq + past, :])
            m = jnp.maximum(m, jnp.max(s2, axis=-1, keepdims=True))
        p1 = jnp.exp(s1 - m)
        l = jnp.sum(p1, axis=-1, keepdims=True)
        vl = slice(h * MLA_V, (h + 1) * MLA_V)
        o_h = _dot(p1.astype(BF16), vv_ref[0:seq, vl])
        if has_ctx:
            p2 = jnp.exp(s2 - m)
            l = l + jnp.sum(p2, axis=-1, keepdims=True)
            o_h = o_h + _dot(p2.astype(BF16), vv_ref[seq:seq + past, vl])
        o_ref[:, vl] = o_h / l


def _mla_call(proj, cache_ckv, cache_kpe, layer, batch, seq, wts, tables):
    has_ctx = cache_ckv is not None
    past = cache_ckv.shape[2] if has_ctx else 0
    tq = 256
    nq = seq // tq
    full = lambda cb: pl.BlockSpec((seq, LANE), lambda b, i: (b, cb))
    in_specs = [pl.BlockSpec((tq, MLA_Q_RANK), lambda b, i: (b * nq + i, COL_ML_QD // MLA_Q_RANK)),
                full(COL_ML_KVD // LANE), full(COL_KPE // LANE)]
    args = [proj, proj, proj]
    if has_ctx:
        in_specs += [full(COL_KPE_SW // LANE),
                     pl.BlockSpec((None, None, past, MLA_KV_RANK), lambda b, i: (b, layer, 0, 0)),
                     pl.BlockSpec((None, None, past, MLA_ROPE), lambda b, i: (b, layer, 0, 0))]
        args += [proj, cache_ckv, cache_kpe]
    qn_g, wq_cat, wq_sw, kvn_g, wk_pad, wv = wts
    const = lambda w: pl.BlockSpec(w.shape, lambda b, i, n=w.ndim: (0,) * n)
    if has_ctx:
        cosq, sinq, cosk, sink, epos = tables
        wlist = [qn_g, wq_cat, wq_sw, kvn_g, wk_pad, wv]
        in_specs += [const(w) for w in wlist]
        in_specs += [pl.BlockSpec((tq, LANE), lambda b, i: (i, 0)), pl.BlockSpec((tq, LANE), lambda b, i: (i, 0)),
                     const(cosk), const(sink), const(epos)]
        args += wlist + [cosq, sinq, cosk, sink, epos]
    else:
        wlist = [qn_g, wq_cat, kvn_g, wk_pad, wv]
        in_specs += [const(w) for w in wlist]
        args += wlist
    return pl.pallas_call(
        functools.partial(_mla_kernel, seq=seq, past=past, tq=tq),
        grid=(batch, nq),
        in_specs=in_specs,
        out_specs=[pl.BlockSpec((tq, MLA_VW), lambda b, i: (b * nq + i, 0)),
                   pl.BlockSpec((seq, MLA_KV_RANK), lambda b, i: (b, 0))],
        out_shape=[jax.ShapeDtypeStruct((batch * seq, MLA_VW), F32),
                   jax.ShapeDtypeStruct((batch * seq, MLA_KV_RANK), F32)],
        scratch_shapes=[pltpu.VMEM((MLA_HEADS, seq + past, LANE), BF16),
                        pltpu.VMEM((seq + past, MLA_VW), BF16)],
        compiler_params=pltpu.CompilerParams(vmem_limit_bytes=VMEM_LIMIT,
                                             dimension_semantics=("arbitrary", "arbitrary")),
        name="mla",
    )(*args)


def _outproj_kernel(x_ref, oa_ref, ob_ref, oc_ref, ga_ref, gb_ref, gc_ref, ma_ref, mb_ref, mc_ref,
                    mod_ref, wa_ref, wb_ref, wc_ref, wo_ref, fg_ref, o_ref, *, final):
    def branch(o_r, g_r, w_r):
        g = g_r[...]
        return _dot((o_r[...] * (g * jax.nn.sigmoid(g))).astype(BF16), w_r[...])

    y = (jax.nn.sigmoid(ma_ref[...]) * branch(oa_ref, ga_ref, wa_ref)
         + jax.nn.sigmoid(mb_ref[...]) * branch(ob_ref, gb_ref, wb_ref)
         + jax.nn.sigmoid(mc_ref[...]) * branch(oc_ref, gc_ref, wc_ref))
    y = _dot(y.astype(BF16), wo_ref[...])
    gate = mod_ref[...][:, 2 * D_MODEL:3 * D_MODEL]
    hn = x_ref[...] + gate * y
    if final:
        hn = _rms(hn) * fg_ref[...]
    o_ref[...] = hn


def _outproj_call(x2d, o_a, o_b, o_c, proj, mod3, wts, final_g, rows_per_mod, mod_base, final):
    rows = x2d.shape[0]
    tm = 256
    tiles_per_mod = rows_per_mod // tm
    row = lambda w, cb=0: pl.BlockSpec((tm, w), lambda i: (i, cb))
    const = lambda w: pl.BlockSpec(w.shape, lambda i, n=w.ndim: (0,) * n)
    in_specs = [row(D_MODEL), row(512), row(512), row(512),
                row(512, COL_GATES // 512), row(512, COL_GATES // 512 + 1), row(512, COL_GATES // 512 + 2),
                row(D_MODEL, COL_MERGE // D_MODEL), row(D_MODEL, COL_MERGE // D_MODEL + 1),
                row(D_MODEL, COL_MERGE // D_MODEL + 2),
                pl.BlockSpec((None, 1, 3 * D_MODEL), lambda i: (mod_base + i // tiles_per_mod, 0, 0))]
    in_specs += [const(w) for w in wts] + [const(final_g)]
    return pl.pallas_call(
        functools.partial(_outproj_kernel, final=final),
        grid=(rows // tm,),
        in_specs=in_specs,
        out_specs=row(D_MODEL),
        out_shape=jax.ShapeDtypeStruct((rows, D_MODEL), F32),
        compiler_params=pltpu.CompilerParams(vmem_limit_bytes=VMEM_LIMIT),
        name="outproj",
    )(x2d, o_a, o_b, o_c, proj, proj, proj, proj, proj, proj, mod3, *wts, final_g)


def _pack_w_in(w):
    seg = lambda i: w[:, _IN_OFF[i]:_IN_OFF[i + 1]]
    z = lambda n: jnp.zeros((D_MODEL, n), w.dtype)
    kpe = seg(13)
    q = MLA_ROPE // 4
    kpe_sw = jnp.concatenate([kpe[:, q:2 * q], kpe[:, 0:q], kpe[:, 3 * q:4 * q], kpe[:, 2 * q:3 * q]], axis=1)
    tail = LANE - KPE_LANE - MLA_ROPE
    cols = [w[:, _IN_OFF[0]:_IN_OFF[5]],
            seg(9), z(LANE - GLA_GATE_RANK),
            seg(11),
            seg(6), seg(7), seg(8),
            seg(12),
            z(KPE_LANE), kpe, z(tail),
            z(KPE_LANE), kpe_sw, z(tail),
            z(COL_GATES - COL_KPE_SW - LANE),
            seg(5), seg(10), seg(14),
            seg(15)]
    return jnp.concatenate(cols, axis=1).astype(BF16)


def _rope_lane_tables(seq):
    n_freq = MLA_ROPE // 4
    t = np.arange(seq)
    inv = ROPE_THETA ** (-np.arange(n_freq, dtype=np.float64) / n_freq)
    ang = np.stack([(t // GRID_W)[:, None] * inv, (t % GRID_W)[:, None] * inv], axis=1)
    cos = np.repeat(np.cos(ang)[:, :, None, :], 2, axis=2).reshape(seq, MLA_ROPE)
    sin = np.stack([-np.sin(ang), np.sin(ang)], axis=2).reshape(seq, MLA_ROPE)
    return cos, sin


def _mla_tables(seq):
    cos, sin = _rope_lane_tables(seq)
    cosq = np.zeros((seq, LANE), np.float32)
    sinq = np.zeros((seq, LANE), np.float32)
    cosq[:, :KPE_LANE] = 1.0
    cosq[:, KPE_LANE:KPE_LANE + MLA_ROPE] = cos
    sinq[:, KPE_LANE:KPE_LANE + MLA_ROPE] = sin
    cosk = np.zeros((seq, LANE), np.float32)
    cosk[:, KPE_LANE:KPE_LANE + MLA_ROPE] = cos
    epos = np.zeros((MLA_ROPE, LANE), np.float32)
    epos[np.arange(MLA_ROPE), KPE_LANE + np.arange(MLA_ROPE)] = 1.0
    return (jnp.asarray(cosq), jnp.asarray(sinq), jnp.asarray(cosk), jnp.asarray(sinq),
            jnp.asarray(epos, dtype=BF16))


def _pack_mla_weights(qn_g, wq_up, kvn_g, wkv_up):
    scale = (MLA_NOPE + MLA_ROPE) ** -0.5
    wq = wq_up.reshape(MLA_Q_RANK, MLA_HEADS, MLA_NOPE + MLA_ROPE) * scale
    nope, rope = wq[..., :MLA_NOPE], wq[..., MLA_NOPE:]
    q = MLA_ROPE // 4
    rope_sw = jnp.concatenate([rope[..., q:2 * q], rope[..., 0:q], rope[..., 3 * q:4 * q], rope[..., 2 * q:3 * q]], -1)
    tail = jnp.zeros((MLA_Q_RANK, MLA_HEADS, LANE - KPE_LANE - MLA_ROPE), wq.dtype)
    wq_cat = jnp.concatenate([nope, rope, tail], -1).reshape(MLA_Q_RANK, MLA_HEADS * LANE).astype(BF16)
    wq_sw = jnp.concatenate([jnp.zeros_like(nope), rope_sw, tail], -1).reshape(MLA_Q_RANK, MLA_HEADS * LANE).astype(BF16)
    wkv = wkv_up.reshape(MLA_KV_RANK, MLA_HEADS, MLA_NOPE + MLA_V)
    wk = jnp.concatenate([wkv[..., :MLA_NOPE], jnp.zeros((MLA_KV_RANK, MLA_HEADS, LANE - MLA_NOPE), wkv.dtype)], -1)
    wk_pad = wk.reshape(MLA_KV_RANK, MLA_HEADS * LANE).astype(BF16)
    wv = wkv[..., MLA_NOPE:].reshape(MLA_KV_RANK, MLA_VW).astype(BF16)
    return (qn_g.reshape(1, -1), wq_cat, wq_sw, kvn_g.reshape(1, -1), wk_pad, wv)


def _head_sum_matrix():
    lane = np.arange(RW_WIDTH)
    return jnp.asarray((lane[:, None] // RW_HEAD == lane[None, :] // RW_HEAD).astype(np.float32), dtype=BF16)


def _trunk(x_prompt, x_sample, c, cache_ckv, cache_kpe, state_rwkv, state_gla, c_ctx,
           norm_g, w_mod, b_mod, w_in, rw_w0, rw_w2, rw_a0, rw_a2, rw_k_k, rw_k_a, rw_r_k,
           rw_ln_g, rw_ln_b, rw_out, gla_a2, gla_ab, gla_norm_g, gla_out,
           mla_qn_g, mla_wq_up, mla_kvn_g, mla_wkv_up, mla_out, w_out, final_g):
    bp, tp, _ = x_prompt.shape
    bs, ts, _ = x_sample.shape
    hp = x_prompt.reshape(bp * tp, D_MODEL)
    hs = x_sample.reshape(bs * ts, D_MODEL)
    cvec8 = jnp.concatenate([c, c_ctx[None, :], jnp.zeros((8 - bs - 1, D_MODEL), F32)], axis=0)
    ctx_row = bs
    hsum = _head_sum_matrix()
    mexp, bmask = _gla_constants()
    mexp = jnp.asarray(mexp, dtype=BF16)
    bmask = jnp.asarray(bmask)
    tables = _mla_tables(ts)
    fg = final_g.reshape(1, D_MODEL)
    ckv_l, kpe_l, rw_l, gla_l = [], [], [], []
    for l in range(DEPTH):
        mod3 = _mod_call(cvec8, w_mod, b_mod, l).reshape(8, 1, 3 * D_MODEL)
        w_packed = _pack_w_in(w_in[l])
        ng = norm_g[l].reshape(1, D_MODEL)
        row = lambda a: a.reshape(1, -1)
        rw_wts = (rw_w0[l], rw_w2[l], rw_a0[l], rw_a2[l], row(rw_k_k[l]), row(rw_k_a[l]), row(rw_r_k[l]),
                  row(rw_ln_g[l]), row(rw_ln_b[l]), hsum)
        a2p = jnp.concatenate([gla_a2[l], jnp.zeros((2, LANE - GLA_GATE_RANK, GLA_KW), F32)], axis=1)
        gla_wts = (a2p, gla_ab[l], row(gla_norm_g[l]), mexp, bmask)
        mla_wts = _pack_mla_weights(mla_qn_g[l], mla_wq_up[l], mla_kvn_g[l], mla_wkv_up[l])
        out_wts = (rw_out[l].astype(BF16), gla_out[l].astype(BF16), mla_out[l].astype(BF16), w_out[l].astype(BF16))
        final = l == DEPTH - 1

        proj = _inproj_call(hp, mod3, ng, w_packed, bp * tp, ctx_row)
        o_a, s_rw = _rwkv_call(proj, None, l, bp, tp, rw_wts)
        o_b, s_gla = _gla_call(proj, None, l, bp, tp, gla_wts)
        o_c, ckv = _mla_call(proj, None, None, l, bp, tp, mla_wts, None)
        ckv_l.append(ckv.reshape(bp, tp, MLA_KV_RANK))
        kpe_l.append(proj[:, COL_KPE + KPE_LANE:COL_KPE + KPE_LANE + MLA_ROPE].reshape(bp, tp, MLA_ROPE))
        rw_l.append(s_rw)
        gla_l.append(s_gla)
        hp = _outproj_call(hp, o_a, o_b, o_c, proj, mod3, out_wts, fg, bp * tp, ctx_row, final)

        proj = _inproj_call(hs, mod3, ng, w_packed, ts, 0)
        o_a, _ = _rwkv_call(proj, state_rwkv, l, bs, ts, rw_wts)
        o_b, _ = _gla_call(proj, state_gla, l, bs, ts, gla_wts)
        o_c, _ = _mla_call(proj, cache_ckv, cache_kpe, l, bs, ts, mla_wts, tables)
        hs = _outproj_call(hs, o_a, o_b, o_c, proj, mod3, out_wts, fg, ts, 0, final)

    return (hp.reshape(bp, tp, D_MODEL), hs.reshape(bs, ts, D_MODEL),
            jnp.stack(ckv_l, axis=1), jnp.stack(kpe_l, axis=1),
            jnp.stack(rw_l, axis=1), jnp.stack(gla_l, axis=1))


_trunk_jit = jax.jit(_trunk)


def kernel(x_prompt, x_sample, c, cache_ckv, cache_kpe, state_rwkv, state_gla, c_ctx, norm_g, w_mod, b_mod, w_in, rw_w0, rw_w2, rw_a0, rw_a2, rw_k_k, rw_k_a, rw_r_k, rw_ln_g, rw_ln_b, rw_out, gla_a2, gla_ab, gla_norm_g, gla_out, mla_qn_g, mla_wq_up, mla_kvn_g, mla_wkv_up, mla_out, w_out, final_g):
    return _trunk_jit(x_prompt, x_sample, c, cache_ckv, cache_kpe, state_rwkv, state_gla, c_ctx, norm_g, w_mod, b_mod, w_in, rw_w0, rw_w2, rw_a0, rw_a2, rw_k_k, rw_k_a, rw_r_k, rw_ln_g, rw_ln_b, rw_out, gla_a2, gla_ab, gla_norm_g, gla_out, mla_qn_g, mla_wq_up, mla_kvn_g, mla_wkv_up, mla_out, w_out, final_g)
```

```python
import functools

import numpy as np
import jax
import jax.numpy as jnp
from jax import lax
from jax.experimental import pallas as pl
from jax.experimental.pallas import tpu as pltpu

F32 = jnp.float32
BF16 = jnp.bfloat16
HI = lax.Precision.HIGHEST

D_MODEL = 1024
DEPTH = 2
GRID_W = 64
NORM_EPS = 1e-6
RW_HEADS = 8
RW_HEAD = 64
RW_WIDTH = RW_HEADS * RW_HEAD
RW_RANK = 64
RW_GN_EPS = 64e-5
GLA_HEADS = 4
GLA_DK = 64
GLA_DV = 128
GLA_KW = GLA_HEADS * GLA_DK
GLA_VW = GLA_HEADS * GLA_DV
GLA_GATE_RANK = 16
GLA_LOGIT_NORM = 16.0
MLA_HEADS = 8
MLA_NOPE = 64
MLA_ROPE = 32
MLA_V = 64
MLA_Q_RANK = 256
MLA_KV_RANK = 128
MLA_VW = MLA_HEADS * MLA_V
ROPE_THETA = 10000.0
N_BRANCH = 3

_IN_SIZES = (RW_WIDTH, RW_WIDTH, RW_WIDTH, RW_RANK, RW_RANK, RW_WIDTH,
             GLA_KW, GLA_KW, GLA_VW, GLA_GATE_RANK, GLA_VW,
             MLA_Q_RANK, MLA_KV_RANK, MLA_ROPE, MLA_VW, N_BRANCH * D_MODEL)
_IN_OFF = tuple(int(v) for v in np.concatenate([[0], np.cumsum(_IN_SIZES)]))

LANE = 128
COL_RW_R = 0
COL_RW_K = 512
COL_RW_V = 1024
COL_RW_WA = 1536
COL_GL_AD = 1664
COL_ML_QD = 1792
COL_GL_Q = 2048
COL_GL_K = 2304
COL_GL_V = 2560
COL_ML_KVD = 3072
COL_KPE = 3200
COL_KPE_SW = 3328
COL_GATES = 3584
COL_MERGE = 5120
PACK_W = 8192
KPE_LANE = MLA_NOPE

CHUNK = 64
VMEM_LIMIT = 48 * 1024 * 1024


def _dot(a, b, prec=None):
    return jnp.dot(a, b, preferred_element_type=F32, precision=prec)


def _dot_nt(a, b, prec=None):
    return lax.dot_general(a, b, (((1,), (1,)), ((), ())), preferred_element_type=F32, precision=prec)


def _dot_tn(a, b, prec=None):
    return lax.dot_general(a, b, (((0,), (0,)), ((), ())), preferred_element_type=F32, precision=prec)


def _split(x):
    hi = x.astype(BF16)
    return hi, (x - hi.astype(F32)).astype(BF16)


def _dot_split(a, b):
    ah, al = _split(a)
    bh, bl = _split(b)
    return _dot(ah, bh) + _dot(al, bh) + _dot(ah, bl)


def _dot_sel(sel16, x):
    xh, xl = _split(x)
    return _dot(sel16, xh) + _dot(sel16, xl)


def _softplus(z):
    return jnp.maximum(z, 0.0) + jnp.log1p(jnp.exp(-jnp.abs(z)))


def _rms(x, eps=NORM_EPS):
    return x * lax.rsqrt(jnp.mean(x * x, axis=-1, keepdims=True) + eps)


def _mod_kernel(c_ref, w_ref, b_ref, o_ref):
    c = c_ref[...]
    o_ref[...] = _dot(c * jax.nn.sigmoid(c), w_ref[...], HI) + b_ref[...]


def _mod_call(cvec8, w_mod, b_mod, layer):
    tn = 1024
    return pl.pallas_call(
        _mod_kernel,
        grid=(3 * D_MODEL // tn,),
        in_specs=[pl.BlockSpec((8, D_MODEL), lambda j: (0, 0)),
                  pl.BlockSpec((None, D_MODEL, tn), lambda j: (layer, 0, j)),
                  pl.BlockSpec((None, 1, tn), lambda j: (layer, 0, j))],
        out_specs=pl.BlockSpec((8, tn), lambda j: (0, j)),
        out_shape=jax.ShapeDtypeStruct((8, 3 * D_MODEL), F32),
        compiler_params=pltpu.CompilerParams(vmem_limit_bytes=VMEM_LIMIT),
        name="mod",
    )(cvec8, w_mod, b_mod.reshape(DEPTH, 1, 3 * D_MODEL))


def _inproj_kernel(x_ref, mod_ref, g_ref, w_ref, o_ref):
    x = x_ref[...]
    m = mod_ref[...]
    shift = m[:, 0:D_MODEL]
    scale = m[:, D_MODEL:2 * D_MODEL]
    h = _rms(x) * g_ref[...] * (1.0 + scale) + shift
    o_ref[...] = _dot(h.astype(BF16), w_ref[...])


def _inproj_call(x2d, mod3, norm_g, w_packed, rows_per_mod, mod_base):
    rows = x2d.shape[0]
    tm, tn = 512, 2048
    tiles_per_mod = rows_per_mod // tm
    return pl.pallas_call(
        _inproj_kernel,
        grid=(PACK_W // tn, rows // tm),
        in_specs=[pl.BlockSpec((tm, D_MODEL), lambda j, i: (i, 0)),
                  pl.BlockSpec((None, 1, 3 * D_MODEL), lambda j, i: (mod_base + i // tiles_per_mod, 0, 0)),
                  pl.BlockSpec((1, D_MODEL), lambda j, i: (0, 0)),
                  pl.BlockSpec((D_MODEL, tn), lambda j, i: (0, j))],
        out_specs=pl.BlockSpec((tm, tn), lambda j, i: (i, j)),
        out_shape=jax.ShapeDtypeStruct((rows, PACK_W), F32),
        compiler_params=pltpu.CompilerParams(vmem_limit_bytes=VMEM_LIMIT),
        name="inproj",
    )(x2d, mod3, norm_g, w_packed)


RW_CHUNK = 64


def _rwkv_pair_mask(c, reverse):
    row = lax.broadcasted_iota(jnp.int32, (2 * c, 2 * c), 0)
    col = lax.broadcasted_iota(jnp.int32, (2 * c, 2 * c), 1)
    t = jnp.where(row >= c, row - c, row)
    s = jnp.where(col >= c, col - c, col)
    earlier = (s > t) if reverse else (s < t)
    return earlier | ((row >= c) & (s == t))


def _rwkv_kernel(*refs, seq, has_s0):
    if has_s0:
        r_ref, k_ref, v_ref, wa_ref, s0_ref = refs[:5]
        rest = refs[5:]
    else:
        r_ref, k_ref, v_ref, wa_ref = refs[:4]
        s0_ref = None
        rest = refs[4:]
    (w0_ref, w2_ref, a0_ref, a2_ref, kk_ref, ka_ref, rk_ref, lng_ref, lnb_ref, hsum_ref,
     o_ref, sout_ref, acc_ref, kn_ref, st_ref) = rest
    c = RW_CHUNK
    nc = seq // c
    hsum = hsum_ref[...]

    def pre_step(i, carry):
        rows = pl.ds(pl.multiple_of(i * c, c), c)
        r = r_ref[rows, :]
        k = k_ref[rows, :]
        kk0 = k * kk_ref[...]
        red = _dot(jnp.concatenate([kk0 * kk0, r * k * rk_ref[...]], axis=0).astype(BF16), hsum)
        kn_ref[rows, :] = kk0 / jnp.maximum(jnp.sqrt(red[0:c]), 1e-12)
        o_ref[rows, :] = red[c:2 * c] * v_ref[rows, :]
        return carry

    def post_step(i, carry):
        rows = pl.ds(pl.multiple_of(i * c, c), c)
        o = acc_ref[0, rows, :] + acc_ref[1, rows, :]
        mu = _dot(o.astype(BF16), hsum) * (1.0 / RW_HEAD)
        dev = o - mu
        var = _dot((dev * dev).astype(BF16), hsum) * (1.0 / RW_HEAD)
        o_ref[rows, :] = o_ref[rows, :] + (dev * lax.rsqrt(var + RW_GN_EPS) * lng_ref[...] + lnb_ref[...])
        return carry

    def chunk_operands(d, ci, incl16):
        rows = pl.ds(pl.multiple_of(ci * c, c), c)
        r = r_ref[rows, :]
        k = k_ref[rows, :]
        v = v_ref[rows, :]
        wa = wa_ref[rows, :]
        wd = wa[:, 0:RW_RANK]
        ad = wa[:, RW_RANK:2 * RW_RANK]
        kk = kn_ref[rows, :]
        w_log = -_softplus(-(w0_ref[d:d + 1, :] + _dot_split(jnp.tanh(wd), w2_ref[d]))) - 0.5
        lw = -jnp.exp(w_log)
        a = jax.nn.sigmoid(a0_ref[d:d + 1, :] + _dot_split(ad, a2_ref[d]))
        kd = k * (1.0 + (a - 1.0) * ka_ref[...])
        bv = kk * a
        gam = _dot_sel(incl16, lw)
        last = 0 if d == 1 else c - 1
        gtot = gam[last:last + 1, :]
        gref = gam[c // 2:c // 2 + 1, :]
        e_in = jnp.exp(gam - gref)
        e_out = jnp.exp(gref - gam)
        e_end = jnp.exp(gtot - gam)
        lhs = jnp.concatenate([-kk * (e_in * jnp.exp(-lw)), r * e_in], axis=0).astype(BF16)
        rhs = jnp.concatenate([bv * e_out, kd * e_out], axis=0).astype(BF16)
        end = jnp.concatenate([bv * e_end, kd * e_end], axis=0).astype(BF16)
        return dict(rows=rows, lhs=lhs, rhs=rhs, end=end, v16=v.astype(BF16),
                    dtot=jnp.exp(gtot), eref=jnp.exp(gref))

    row = lax.broadcasted_iota(jnp.int32, (c, c), 0)
    col = lax.broadcasted_iota(jnp.int32, (c, c), 1)
    incl16 = [jnp.where(col <= row, 1.0, 0.0).astype(BF16), jnp.where(col >= row, 1.0, 0.0).astype(BF16)]
    pair_mask = [_rwkv_pair_mask(c, False), _rwkv_pair_mask(c, True)]
    zeros16 = jnp.zeros((c, RW_HEAD), BF16)
    sls = [slice(h * RW_HEAD, (h + 1) * RW_HEAD) for h in range(RW_HEADS)]
    probs = [(d, h) for d in range(2) for h in range(RW_HEADS)]

    def main_step(i, carry):
        ops = [chunk_operands(0, i, incl16[0]), chunk_operands(1, nc - 1 - i, incl16[1])]
        s0 = [st_ref[d, h] for d, h in probs]
        lhs_h = [ops[d]["lhs"][:, sls[h]] for d, h in probs]
        v_h = [ops[d]["v16"][:, sls[h]] for d, h in probs]
        n = range(len(probs))
        aa = [jnp.where(pair_mask[d], _dot_nt(lhs_h[j], ops[d]["rhs"][:, sls[h]]), 0.0)
              for j, (d, h) in enumerate(probs)]
        sp = [_dot_nt(lhs_h[j], (s0[j] * ops[d]["eref"][:, sls[h]]).astype(BF16))
              for j, (d, h) in enumerate(probs)]
        aa16 = [x.astype(BF16) for x in aa]
        u = [_dot(aa16[j][0:c], jnp.concatenate([zeros16, v_h[j]], axis=0)) + sp[j][0:c] for j in n]
        p16 = [aa16[j][0:c, 0:c] for j in n]
        u = [u[j] + _dot(p16[j], u[j].astype(BF16)) for j in n]
        covered = 2
        while covered < c:
            p16 = [_dot(p, p).astype(BF16) for p in p16]
            u = [u[j] + _dot(p16[j], u[j].astype(BF16)) for j in n]
            covered *= 2
        uv = [jnp.concatenate([u[j].astype(BF16), v_h[j]], axis=0) for j in n]
        for j, (d, h) in enumerate(probs):
            acc_ref[d, ops[d]["rows"], sls[h]] = sp[j][c:2 * c] + _dot(aa16[j][c:2 * c], uv[j])
            st_ref[d, h] = s0[j] * ops[d]["dtot"][:, sls[h]] + _dot_tn(uv[j], ops[d]["end"][:, sls[h]])
        return carry

    if has_s0:
        st_ref[...] = s0_ref[...]
    else:
        st_ref[...] = jnp.zeros_like(st_ref)
    lax.fori_loop(0, nc, pre_step, 0)
    lax.fori_loop(0, nc, main_step, 0)
    sout_ref[...] = st_ref[...]
    lax.fori_loop(0, nc, post_step, 0)


def _rwkv_call(proj, s0, layer, batch, seq, wts):
    has_s0 = s0 is not None
    blk = lambda w, cb: pl.BlockSpec((seq, w), lambda b: (b, cb))
    in_specs = [blk(512, COL_RW_R // 512), blk(512, COL_RW_K // 512), blk(512, COL_RW_V // 512),
                blk(LANE, COL_RW_WA // LANE)]
    args = [proj, proj, proj, proj]
    if has_s0:
        in_specs.append(pl.BlockSpec((None, None, 2, RW_HEADS, RW_HEAD, RW_HEAD),
                                     lambda b: (b, layer, 0, 0, 0, 0)))
        args.append(s0)
    for w in wts:
        in_specs.append(pl.BlockSpec(w.shape, lambda b, n=w.ndim: (0,) * n))
        args.append(w)
    return pl.pallas_call(
        functools.partial(_rwkv_kernel, seq=seq, has_s0=has_s0),
        grid=(batch,),
        in_specs=in_specs,
        out_specs=[pl.BlockSpec((seq, RW_WIDTH), lambda b: (b, 0)),
                   pl.BlockSpec((None, 2, RW_HEADS, RW_HEAD, RW_HEAD), lambda b: (b, 0, 0, 0, 0))],
        out_shape=[jax.ShapeDtypeStruct((batch * seq, RW_WIDTH), F32),
                   jax.ShapeDtypeStruct((batch, 2, RW_HEADS, RW_HEAD, RW_HEAD), F32)],
        scratch_shapes=[pltpu.VMEM((2, seq, RW_WIDTH), F32),
                        pltpu.VMEM((seq, RW_WIDTH), F32),
                        pltpu.VMEM((2, RW_HEADS, RW_HEAD, RW_HEAD), F32)],
        compiler_params=pltpu.CompilerParams(vmem_limit_bytes=VMEM_LIMIT),
        name="rwkv",
    )(*args)


GLA_LEVELS = (32, 16, 8, 4, 2, 1)


def _gla_constants():
    c = CHUNK
    nl = len(GLA_LEVELS)
    mexp = np.zeros((2, (2 * nl + 1) * c, c), np.float32)
    bmask = np.zeros((2, nl + 1, c, c), np.float32)
    for d in range(2):
        pos = np.arange(c) if d == 0 else c - 1 - np.arange(c)
        pt = pos[:, None]
        pj = pos[None, :]
        mexp[d, 2 * nl * c:] = (pj <= pt)
        for li, m in enumerate(GLA_LEVELS):
            mid = (pos // (2 * m)) * (2 * m) + m
            second = pos >= mid
            mq = (pj >= mid[:, None]) & (pj <= pt) & second[:, None]
            mk = (pj > pt) & (pj <= mid[:, None] - 1) & (~second)[:, None]
            mexp[d, (2 * li) * c:(2 * li + 1) * c] = mq
            mexp[d, (2 * li + 1) * c:(2 * li + 2) * c] = mk
            same = (pos[:, None] // (2 * m)) == (pos[None, :] // (2 * m))
            bmask[d, li] = same & second[:, None] & (~second)[None, :]
        bmask[d, nl] = np.eye(c)
    return mexp, bmask


def _gla_kernel(*refs, seq, has_s0):
    if has_s0:
        q_ref, k_ref, v_ref, ad_ref, s0_ref = refs[:5]
        rest = refs[5:]
    else:
        q_ref, k_ref, v_ref, ad_ref = refs[:4]
        s0_ref = None
        rest = refs[4:]
    (a2_ref, ab_ref, ng_ref, mexp_ref, bmask_ref,
     o_ref, sout_ref, acc_ref, st_ref) = rest
    c = CHUNK
    nc = seq // c
    nl = len(GLA_LEVELS)

    def chunk_operands(d, ci):
        rows = pl.ds(pl.multiple_of(ci * c, c), c)
        q = q_ref[rows, :] * (GLA_DK ** -0.5)
        k = k_ref[rows, :]
        v = v_ref[rows, :]
        x = _dot_split(ad_ref[rows, :], a2_ref[d]) + ab_ref[d:d + 1, :]
        g = (jnp.minimum(x, 0.0) - jnp.log1p(jnp.exp(-jnp.abs(x)))) * (1.0 / GLA_LOGIT_NORM)
        sums = _dot_sel(mexp_ref[d], g)
        ex = jnp.exp(sums[0:2 * nl * c])
        b = sums[2 * nl * c:(2 * nl + 1) * c]
        last = 0 if d == 1 else c - 1
        blast = b[last:last + 1, :]
        qb = (q * jnp.exp(b)).astype(BF16)
        kdec = (k * jnp.exp(blast - b)).astype(BF16)
        dtot = jnp.exp(blast)
        qs = [(q * ex[(2 * li) * c:(2 * li + 1) * c]).astype(BF16) for li in range(nl)] + [q.astype(BF16)]
        ks = [(k * ex[(2 * li + 1) * c:(2 * li + 2) * c]).astype(BF16) for li in range(nl)] + [k.astype(BF16)]
        return dict(rows=rows, qs=qs, ks=ks, qb=qb, kdec=kdec, dtot=dtot, v16=v.astype(BF16))

    sks = [slice(h * GLA_DK, (h + 1) * GLA_DK) for h in range(GLA_HEADS)]
    svs = [slice(h * GLA_DV, (h + 1) * GLA_DV) for h in range(GLA_HEADS)]
    probs = [(d, h) for d in range(2) for h in range(GLA_HEADS)]

    def main_step(i, carry):
        ops = [chunk_operands(0, i), chunk_operands(1, nc - 1 - i)]
        st = [st_ref[d, h] for d, h in probs]
        lvl = [[_dot_nt(ops[d]["qs"][li][:, sks[h]], ops[d]["ks"][li][:, sks[h]]) for d, h in probs]
               for li in range(nl + 1)]
        att = [sum(bmask_ref[d, li] * lvl[li][j] for li in range(nl + 1)).astype(BF16)
               for j, (d, h) in enumerate(probs)]
        inter = [_dot_nt(ops[d]["qb"][:, sks[h]], st[j].astype(BF16)) for j, (d, h) in enumerate(probs)]
        for j, (d, h) in enumerate(probs):
            v_h = ops[d]["v16"][:, svs[h]]
            acc_ref[d, ops[d]["rows"], svs[h]] = inter[j] + _dot(att[j], v_h)
            st_ref[d, h] = st[j] * ops[d]["dtot"][:, sks[h]] + _dot_tn(v_h, ops[d]["kdec"][:, sks[h]])
        return carry

    def post_step(i, carry):
        rows = pl.ds(pl.multiple_of(i * c, c), c)
        for sv in svs:
            o_ref[rows, sv] = _rms(acc_ref[0, rows, sv] + acc_ref[1, rows, sv]) * ng_ref[...]
        return carry

    for d, h in probs:
        if has_s0:
            st_ref[d, h] = s0_ref[d, h].T
        else:
            st_ref[d, h] = jnp.zeros((GLA_DV, GLA_DK), F32)
    lax.fori_loop(0, nc, main_step, 0)
    for d, h in probs:
        sout_ref[d, h] = st_ref[d, h].T
    lax.fori_loop(0, nc, post_step, 0)


def _gla_call(proj, s0, layer, batch, seq, wts):
    has_s0 = s0 is not None
    blk = lambda w, cb: pl.BlockSpec((seq, w), lambda b: (b, cb))
    in_specs = [blk(GLA_KW, COL_GL_Q // GLA_KW), blk(GLA_KW, COL_GL_K // GLA_KW),
                blk(GLA_VW, COL_GL_V // GLA_VW), blk(LANE, COL_GL_AD // LANE)]
    args = [proj, proj, proj, proj]
    if has_s0:
        in_specs.append(pl.BlockSpec((None, None, 2, GLA_HEADS, GLA_DK, GLA_DV),
                                     lambda b: (b, layer, 0, 0, 0, 0)))
        args.append(s0)
    for w in wts:
        in_specs.append(pl.BlockSpec(w.shape, lambda b, n=w.ndim: (0,) * n))
        args.append(w)
    return pl.pallas_call(
        functools.partial(_gla_kernel, seq=seq, has_s0=has_s0),
        grid=(batch,),
        in_specs=in_specs,
        out_specs=[pl.BlockSpec((seq, GLA_VW), lambda b: (b, 0)),
                   pl.BlockSpec((None, 2, GLA_HEADS, GLA_DK, GLA_DV), lambda b: (b, 0, 0, 0, 0))],
        out_shape=[jax.ShapeDtypeStruct((batch * seq, GLA_VW), F32),
                   jax.ShapeDtypeStruct((batch, 2, GLA_HEADS, GLA_DK, GLA_DV), F32)],
        scratch_shapes=[pltpu.VMEM((2, seq, GLA_VW), F32),
                        pltpu.VMEM((2, GLA_HEADS, GLA_DV, GLA_DK), F32)],
        compiler_params=pltpu.CompilerParams(vmem_limit_bytes=VMEM_LIMIT),
        name="gla",
    )(*args)


def _mla_kernel(*refs, seq, past, tq):
    has_ctx = past > 0
    if has_ctx:
        (qd_ref, kvd_ref, kpe_ref, kpesw_ref, cckv_ref, ckpe_ref,
         qng_ref, wq_ref, wqsw_ref, kvng_ref, wk_ref, wv_ref,
         cosq_ref, sinq_ref, cosk_ref, sink_ref, epos_ref,
         o_ref, ckv_ref, kcat_ref, vv_ref) = refs
    else:
        (qd_ref, kvd_ref, kpe_ref,
         qng_ref, wq_ref, kvng_ref, wk_ref, wv_ref,
         o_ref, ckv_ref, kcat_ref, vv_ref) = refs

    @pl.when(pl.program_id(1) == 0)
    def _():
        ckv = _rms(kvd_ref[...]) * kvng_ref[...]
        ckv_ref[...] = ckv
        if has_ctx:
            kpos = kpe_ref[...] * cosk_ref[...] + kpesw_ref[...] * sink_ref[...]
        else:
            kpos = kpe_ref[...]
        segs = [(0, seq, ckv, kpos)]
        if has_ctx:
            cpos = _dot(ckpe_ref[...].astype(BF16), epos_ref[...])
            segs.append((seq, past, cckv_ref[...], cpos))
        for start, n, lat, pos in segs:
            lat16 = lat.astype(BF16)
            kc = _dot(lat16, wk_ref[...])
            for h in range(MLA_HEADS):
                kcat_ref[h, start:start + n, :] = (kc[:, h * LANE:(h + 1) * LANE] + pos).astype(BF16)
            vv_ref[start:start + n, :] = _dot(lat16, wv_ref[...]).astype(BF16)

    qlat = (_rms(qd_ref[...]) * qng_ref[...]).astype(BF16)
    qc = _dot(qlat, wq_ref[...])
    if has_ctx:
        qsw = _dot(qlat, wqsw_ref[...])
    for h in range(MLA_HEADS):
        hl = slice(h * LANE, (h + 1) * LANE)
        q_raw = qc[:, hl]
        if has_ctx:
            q_self = (q_raw * cosq_ref[...] + qsw[:, hl] * sinq_ref[...]).astype(BF16)
        else:
            q_self = q_raw.astype(BF16)
        s1 = _dot_nt(q_self, kcat_ref[h, 0:seq, :])
        m = jnp.max(s1, axis=-1, keepdims=True)
        if has_ctx:
            s2 = _dot_nt(q_raw.astype(BF16), kcat_ref[h, seq:seq + past, :])
            m = jnp.maximum(m, jnp.max(s2, axis=-1, keepdims=True))
        p1 = jnp.exp(s1 - m)
        l = jnp.sum(p1, axis=-1, keepdims=True)
        vl = slice(h * MLA_V, (h + 1) * MLA_V)
        o_h = _dot(p1.astype(BF16), vv_ref[0:seq, vl])
        if has_ctx:
            p2 = jnp.exp(s2 - m)
            l = l + jnp.sum(p2, axis=-1, keepdims=True)
            o_h = o_h + _dot(p2.astype(BF16), vv_ref[seq:seq + past, vl])
        o_ref[:, vl] = o_h / l


def _mla_call(proj, cache_ckv, cache_kpe, layer, batch, seq, wts, tables):
    has_ctx = cache_ckv is not None
    past = cache_ckv.shape[2] if has_ctx else 0
    tq = 256
    nq = seq // tq
    full = lambda cb: pl.BlockSpec((seq, LANE), lambda b, i: (b, cb))
    in_specs = [pl.BlockSpec((tq, MLA_Q_RANK), lambda b, i: (b * nq + i, COL_ML_QD // MLA_Q_RANK)),
                full(COL_ML_KVD // LANE), full(COL_KPE // LANE)]
    args = [proj, proj, proj]
    if has_ctx:
        in_specs += [full(COL_KPE_SW // LANE),
                     pl.BlockSpec((None, None, past, MLA_KV_RANK), lambda b, i: (b, layer, 0, 0)),
                     pl.BlockSpec((None, None, past, MLA_ROPE), lambda b, i: (b, layer, 0, 0))]
        args += [proj, cache_ckv, cache_kpe]
    qn_g, wq_cat, wq_sw, kvn_g, wk_pad, wv = wts
    const = lambda w: pl.BlockSpec(w.shape, lambda b, i, n=w.ndim: (0,) * n)
    if has_ctx:
        cosq, sinq, cosk, sink, epos = tables
        wlist = [qn_g, wq_cat, wq_sw, kvn_g, wk_pad, wv]
        in_specs += [const(w) for w in wlist]
        in_specs += [pl.BlockSpec((tq, LANE), lambda b, i: (i, 0)), pl.BlockSpec((tq, LANE), lambda b, i: (i, 0)),
                     const(cosk), const(sink), const(epos)]
        args += wlist + [cosq, sinq, cosk, sink, epos]
    else:
        wlist = [qn_g, wq_cat, kvn_g, wk_pad, wv]
        in_specs += [const(w) for w in wlist]
        args += wlist
    return pl.pallas_call(
        functools.partial(_mla_kernel, seq=seq, past=past, tq=tq),
        grid=(batch, nq),
        in_specs=in_specs,
        out_specs=[pl.BlockSpec((tq, MLA_VW), lambda b, i: (b * nq + i, 0)),
                   pl.BlockSpec((seq, MLA_KV_RANK), lambda b, i: (b, 0))],
        out_shape=[jax.ShapeDtypeStruct((batch * seq, MLA_VW), F32),
                   jax.ShapeDtypeStruct((batch * seq, MLA_KV_RANK), F32)],
        scratch_shapes=[pltpu.VMEM((MLA_HEADS, seq + past, LANE), BF16),
                        pltpu.VMEM((seq + past, MLA_VW), BF16)],
        compiler_params=pltpu.CompilerParams(vmem_limit_bytes=VMEM_LIMIT,
                                             dimension_semantics=("arbitrary", "arbitrary")),
        name="mla",
    )(*args)


def _outproj_kernel(x_ref, oa_ref, ob_ref, oc_ref, ga_ref, gb_ref, gc_ref, ma_ref, mb_ref, mc_ref,
                    mod_ref, wa_ref, wb_ref, wc_ref, wo_ref, fg_ref, o_ref, *, final):
    def branch(o_r, g_r, w_r):
        g = g_r[...]
        return _dot((o_r[...] * (g * jax.nn.sigmoid(g))).astype(BF16), w_r[...])

    y = (jax.nn.sigmoid(ma_ref[...]) * branch(oa_ref, ga_ref, wa_ref)
         + jax.nn.sigmoid(mb_ref[...]) * branch(ob_ref, gb_ref, wb_ref)
         + jax.nn.sigmoid(mc_ref[...]) * branch(oc_ref, gc_ref, wc_ref))
    y = _dot(y.astype(BF16), wo_ref[...])
    gate = mod_ref[...][:, 2 * D_MODEL:3 * D_MODEL]
    hn = x_ref[...] + gate * y
    if final:
        hn = _rms(hn) * fg_ref[...]
    o_ref[...] = hn


def _outproj_call(x2d, o_a, o_b, o_c, proj, mod3, wts, final_g, rows_per_mod, mod_base, final):
    rows = x2d.shape[0]
    tm = 256
    tiles_per_mod = rows_per_mod // tm
    row = lambda w, cb=0: pl.BlockSpec((tm, w), lambda i: (i, cb))
    const = lambda w: pl.BlockSpec(w.shape, lambda i, n=w.ndim: (0,) * n)
    in_specs = [row(D_MODEL), row(512), row(512), row(512),
                row(512, COL_GATES // 512), row(512, COL_GATES // 512 + 1), row(512, COL_GATES // 512 + 2),
                row(D_MODEL, COL_MERGE // D_MODEL), row(D_MODEL, COL_MERGE // D_MODEL + 1),
                row(D_MODEL, COL_MERGE // D_MODEL + 2),
                pl.BlockSpec((None, 1, 3 * D_MODEL), lambda i: (mod_base + i // tiles_per_mod, 0, 0))]
    in_specs += [const(w) for w in wts] + [const(final_g)]
    return pl.pallas_call(
        functools.partial(_outproj_kernel, final=final),
        grid=(rows // tm,),
        in_specs=in_specs,
        out_specs=row(D_MODEL),
        out_shape=jax.ShapeDtypeStruct((rows, D_MODEL), F32),
        compiler_params=pltpu.CompilerParams(vmem_limit_bytes=VMEM_LIMIT),
        name="outproj",
    )(x2d, o_a, o_b, o_c, proj, proj, proj, proj, proj, proj, mod3, *wts, final_g)


def _pack_w_in(w):
    seg = lambda i: w[:, _IN_OFF[i]:_IN_OFF[i + 1]]
    z = lambda n: jnp.zeros((D_MODEL, n), w.dtype)
    kpe = seg(13)
    q = MLA_ROPE // 4
    kpe_sw = jnp.concatenate([kpe[:, q:2 * q], kpe[:, 0:q], kpe[:, 3 * q:4 * q], kpe[:, 2 * q:3 * q]], axis=1)
    tail = LANE - KPE_LANE - MLA_ROPE
    cols = [w[:, _IN_OFF[0]:_IN_OFF[5]],
            seg(9), z(LANE - GLA_GATE_RANK),
            seg(11),
            seg(6), seg(7), seg(8),
            seg(12),
            z(KPE_LANE), kpe, z(tail),
            z(KPE_LANE), kpe_sw, z(tail),
            z(COL_GATES - COL_KPE_SW - LANE),
            seg(5), seg(10), seg(14),
            seg(15)]
    return jnp.concatenate(cols, axis=1).astype(BF16)


def _rope_lane_tables(seq):
    n_freq = MLA_ROPE // 4
    t = np.arange(seq)
    inv = ROPE_THETA ** (-np.arange(n_freq, dtype=np.float64) / n_freq)
    ang = np.stack([(t // GRID_W)[:, None] * inv, (t % GRID_W)[:, None] * inv], axis=1)
    cos = np.repeat(np.cos(ang)[:, :, None, :], 2, axis=2).reshape(seq, MLA_ROPE)
    sin = np.stack([-np.sin(ang), np.sin(ang)], axis=2).reshape(seq, MLA_ROPE)
    return cos, sin


def _mla_tables(seq):
    cos, sin = _rope_lane_tables(seq)
    cosq = np.zeros((seq, LANE), np.float32)
    sinq = np.zeros((seq, LANE), np.float32)
    cosq[:, :KPE_LANE] = 1.0
    cosq[:, KPE_LANE:KPE_LANE + MLA_ROPE] = cos
    sinq[:, KPE_LANE:KPE_LANE + MLA_ROPE] = sin
    cosk = np.zeros((seq, LANE), np.float32)
    cosk[:, KPE_LANE:KPE_LANE + MLA_ROPE] = cos
    epos = np.zeros((MLA_ROPE, LANE), np.float32)
    epos[np.arange(MLA_ROPE), KPE_LANE + np.arange(MLA_ROPE)] = 1.0
    return (jnp.asarray(cosq), jnp.asarray(sinq), jnp.asarray(cosk), jnp.asarray(sinq),
            jnp.asarray(epos, dtype=BF16))


def _pack_mla_weights(qn_g, wq_up, kvn_g, wkv_up):
    scale = (MLA_NOPE + MLA_ROPE) ** -0.5
    wq = wq_up.reshape(MLA_Q_RANK, MLA_HEADS, MLA_NOPE + MLA_ROPE) * scale
    nope, rope = wq[..., :MLA_NOPE], wq[..., MLA_NOPE:]
    q = MLA_ROPE // 4
    rope_sw = jnp.concatenate([rope[..., q:2 * q], rope[..., 0:q], rope[..., 3 * q:4 * q], rope[..., 2 * q:3 * q]], -1)
    tail = jnp.zeros((MLA_Q_RANK, MLA_HEADS, LANE - KPE_LANE - MLA_ROPE), wq.dtype)
    wq_cat = jnp.concatenate([nope, rope, tail], -1).reshape(MLA_Q_RANK, MLA_HEADS * LANE).astype(BF16)
    wq_sw = jnp.concatenate([jnp.zeros_like(nope), rope_sw, tail], -1).reshape(MLA_Q_RANK, MLA_HEADS * LANE).astype(BF16)
    wkv = wkv_up.reshape(MLA_KV_RANK, MLA_HEADS, MLA_NOPE + MLA_V)
    wk = jnp.concatenate([wkv[..., :MLA_NOPE], jnp.zeros((MLA_KV_RANK, MLA_HEADS, LANE - MLA_NOPE), wkv.dtype)], -1)
    wk_pad = wk.reshape(MLA_KV_RANK, MLA_HEADS * LANE).astype(BF16)
    wv = wkv[..., MLA_NOPE:].reshape(MLA_KV_RANK, MLA_VW).astype(BF16)
    return (qn_g.reshape(1, -1), wq_cat, wq_sw, kvn_g.reshape(1, -1), wk_pad, wv)


def _head_sum_matrix():
    lane = np.arange(RW_WIDTH)
    return jnp.asarray((lane[:, None] // RW_HEAD == lane[None, :] // RW_HEAD).astype(np.float32), dtype=BF16)


def _trunk(x_prompt, x_sample, c, cache_ckv, cache_kpe, state_rwkv, state_gla, c_ctx,
           norm_g, w_mod, b_mod, w_in, rw_w0, rw_w2, rw_a0, rw_a2, rw_k_k, rw_k_a, rw_r_k,
           rw_ln_g, rw_ln_b, rw_out, gla_a2, gla_ab, gla_norm_g, gla_out,
           mla_qn_g, mla_wq_up, mla_kvn_g, mla_wkv_up, mla_out, w_out, final_g):
    bp, tp, _ = x_prompt.shape
    bs, ts, _ = x_sample.shape
    hp = x_prompt.reshape(bp * tp, D_MODEL)
    hs = x_sample.reshape(bs * ts, D_MODEL)
    cvec8 = jnp.concatenate([c, c_ctx[None, :], jnp.zeros((8 - bs - 1, D_MODEL), F32)], axis=0)
    ctx_row = bs
    hsum = _head_sum_matrix()
    mexp, bmask = _gla_constants()
    mexp = jnp.asarray(mexp, dtype=BF16)
    bmask = jnp.asarray(bmask)
    tables = _mla_tables(ts)
    fg = final_g.reshape(1, D_MODEL)
    ckv_l, kpe_l, rw_l, gla_l = [], [], [], []
    for l in range(DEPTH):
        mod3 = _mod_call(cvec8, w_mod, b_mod, l).reshape(8, 1, 3 * D_MODEL)
        w_packed = _pack_w_in(w_in[l])
        ng = norm_g[l].reshape(1, D_MODEL)
        row = lambda a: a.reshape(1, -1)
        rw_wts = (rw_w0[l], rw_w2[l], rw_a0[l], rw_a2[l], row(rw_k_k[l]), row(rw_k_a[l]), row(rw_r_k[l]),
                  row(rw_ln_g[l]), row(rw_ln_b[l]), hsum)
        a2p = jnp.concatenate([gla_a2[l], jnp.zeros((2, LANE - GLA_GATE_RANK, GLA_KW), F32)], axis=1)
        gla_wts = (a2p, gla_ab[l], row(gla_norm_g[l]), mexp, bmask)
        mla_wts = _pack_mla_weights(mla_qn_g[l], mla_wq_up[l], mla_kvn_g[l], mla_wkv_up[l])
        out_wts = (rw_out[l].astype(BF16), gla_out[l].astype(BF16), mla_out[l].astype(BF16), w_out[l].astype(BF16))
        final = l == DEPTH - 1

        proj = _inproj_call(hp, mod3, ng, w_packed, bp * tp, ctx_row)
        o_a, s_rw = _rwkv_call(proj, None, l, bp, tp, rw_wts)
        o_b, s_gla = _gla_call(proj, None, l, bp, tp, gla_wts)
        o_c, ckv = _mla_call(proj, None, None, l, bp, tp, mla_wts, None)
        ckv_l.append(ckv.reshape(bp, tp, MLA_KV_RANK))
        kpe_l.append(proj[:, COL_KPE + KPE_LANE:COL_KPE + KPE_LANE + MLA_ROPE].reshape(bp, tp, MLA_ROPE))
        rw_l.append(s_rw)
        gla_l.append(s_gla)
        hp = _outproj_call(hp, o_a, o_b, o_c, proj, mod3, out_wts, fg, bp * tp, ctx_row, final)

        proj = _inproj_call(hs, mod3, ng, w_packed, ts, 0)
        o_a, _ = _rwkv_call(proj, state_rwkv, l, bs, ts, rw_wts)
        o_b, _ = _gla_call(proj, state_gla, l, bs, ts, gla_wts)
        o_c, _ = _mla_call(proj, cache_ckv, cache_kpe, l, bs, ts, mla_wts, tables)
        hs = _outproj_call(hs, o_a, o_b, o_c, proj, mod3, out_wts, fg, ts, 0, final)

    return (hp.reshape(bp, tp, D_MODEL), hs.reshape(bs, ts, D_MODEL),
            jnp.stack(ckv_l, axis=1), jnp.stack(kpe_l, axis=1),
            jnp.stack(rw_l, axis=1), jnp.stack(gla_l, axis=1))


_trunk_jit = jax.jit(_trunk)


def kernel(x_prompt, x_sample, c, cache_ckv, cache_kpe, state_rwkv, state_gla, c_ctx, norm_g, w_mod, b_mod, w_in, rw_w0, rw_w2, rw_a0, rw_a2, rw_k_k, rw_k_a, rw_r_k, rw_ln_g, rw_ln_b, rw_out, gla_a2, gla_ab, gla_norm_g, gla_out, mla_qn_g, mla_wq_up, mla_kvn_g, mla_wkv_up, mla_out, w_out, final_g):
    return _trunk_jit(x_prompt, x_sample, c, cache_ckv, cache_kpe, state_rwkv, state_gla, c_ctx, norm_g, w_mod, b_mod, w_in, rw_w0, rw_w2, rw_a0, rw_a2, rw_k_k, rw_k_a, rw_r_k, rw_ln_g, rw_ln_b, rw_out, gla_a2, gla_ab, gla_norm_g, gla_out, mla_qn_g, mla_wq_up, mla_kvn_g, mla_wkv_up, mla_out, w_out, final_g)
```

```python
import functools

import numpy as np
import jax
import jax.numpy as jnp
from jax import lax
from jax.experimental import pallas as pl
from jax.experimental.pallas import tpu as pltpu

F32 = jnp.float32
BF16 = jnp.bfloat16
HI = lax.Precision.HIGHEST

D_MODEL = 1024
DEPTH = 2
GRID_W = 64
NORM_EPS = 1e-6
RW_HEADS = 8
RW_HEAD = 64
RW_WIDTH = RW_HEADS * RW_HEAD
RW_RANK = 64
RW_GN_EPS = 64e-5
GLA_HEADS = 4
GLA_DK = 64
GLA_DV = 128
GLA_KW = GLA_HEADS * GLA_DK
GLA_VW = GLA_HEADS * GLA_DV
GLA_GATE_RANK = 16
GLA_LOGIT_NORM = 16.0
MLA_HEADS = 8
MLA_NOPE = 64
MLA_ROPE = 32
MLA_V = 64
MLA_Q_RANK = 256
MLA_KV_RANK = 128
MLA_VW = MLA_HEADS * MLA_V
ROPE_THETA = 10000.0
N_BRANCH = 3

_IN_SIZES = (RW_WIDTH, RW_WIDTH, RW_WIDTH, RW_RANK, RW_RANK, RW_WIDTH,
             GLA_KW, GLA_KW, GLA_VW, GLA_GATE_RANK, GLA_VW,
             MLA_Q_RANK, MLA_KV_RANK, MLA_ROPE, MLA_VW, N_BRANCH * D_MODEL)
_IN_OFF = tuple(int(v) for v in np.concatenate([[0], np.cumsum(_IN_SIZES)]))

LANE = 128
COL_RW_R = 0
COL_RW_K = 512
COL_RW_V = 1024
COL_RW_WA = 1536
COL_GL_AD = 1664
COL_ML_QD = 1792
COL_GL_Q = 2048
COL_GL_K = 2304
COL_GL_V = 2560
COL_ML_KVD = 3072
COL_KPE = 3200
COL_KPE_SW = 3328
COL_GATES = 3584
COL_MERGE = 5120
PACK_W = 8192
KPE_LANE = MLA_NOPE

CHUNK = 64
VMEM_LIMIT = 48 * 1024 * 1024


def _dot(a, b, prec=None):
    return jnp.dot(a, b, preferred_element_type=F32, precision=prec)


def _dot_nt(a, b, prec=None):
    return lax.dot_general(a, b, (((1,), (1,)), ((), ())), preferred_element_type=F32, precision=prec)


def _dot_tn(a, b, prec=None):
    return lax.dot_general(a, b, (((0,), (0,)), ((), ())), preferred_element_type=F32, precision=prec)


def _split(x):
    hi = x.astype(BF16)
    return hi, (x - hi.astype(F32)).astype(BF16)


def _dot_split(a, b):
    ah, al = _split(a)
    bh, bl = _split(b)
    return _dot(ah, bh) + _dot(al, bh) + _dot(ah, bl)


def _dot_sel(sel16, x):
    xh, xl = _split(x)
    return _dot(sel16, xh) + _dot(sel16, xl)


def _softplus(z):
    return jnp.maximum(z, 0.0) + jnp.log1p(jnp.exp(-jnp.abs(z)))


def _rms(x, eps=NORM_EPS):
    return x * lax.rsqrt(jnp.mean(x * x, axis=-1, keepdims=True) + eps)


def _mod_kernel(c_ref, w_ref, b_ref, o_ref):
    c = c_ref[...]
    o_ref[...] = _dot_split(c * jax.nn.sigmoid(c), w_ref[...]) + b_ref[...]


def _mod_call(cvec8, w_mod, b_mod, layer):
    tn = 1024
    return pl.pallas_call(
        _mod_kernel,
        grid=(3 * D_MODEL // tn,),
        in_specs=[pl.BlockSpec((8, D_MODEL), lambda j: (0, 0)),
                  pl.BlockSpec((None, D_MODEL, tn), lambda j: (layer, 0, j)),
                  pl.BlockSpec((None, 1, tn), lambda j: (layer, 0, j))],
        out_specs=pl.BlockSpec((8, tn), lambda j: (0, j)),
        out_shape=jax.ShapeDtypeStruct((8, 3 * D_MODEL), F32),
        compiler_params=pltpu.CompilerParams(vmem_limit_bytes=VMEM_LIMIT),
        name="mod",
    )(cvec8, w_mod, b_mod.reshape(DEPTH, 1, 3 * D_MODEL))


def _inproj_kernel(x_ref, mod_ref, g_ref, w_ref, o_ref):
    x = x_ref[...]
    m = mod_ref[...]
    shift = m[:, 0:D_MODEL]
    scale = m[:, D_MODEL:2 * D_MODEL]
    h = _rms(x) * g_ref[...] * (1.0 + scale) + shift
    o_ref[...] = _dot(h.astype(BF16), w_ref[...])


def _inproj_call(x2d, mod3, norm_g, w_packed, rows_per_mod, mod_base):
    rows = x2d.shape[0]
    tm, tn = 512, 2048
    tiles_per_mod = rows_per_mod // tm
    return pl.pallas_call(
        _inproj_kernel,
        grid=(PACK_W // tn, rows // tm),
        in_specs=[pl.BlockSpec((tm, D_MODEL), lambda j, i: (i, 0)),
                  pl.BlockSpec((None, 1, 3 * D_MODEL), lambda j, i: (mod_base + i // tiles_per_mod, 0, 0)),
                  pl.BlockSpec((1, D_MODEL), lambda j, i: (0, 0)),
                  pl.BlockSpec((D_MODEL, tn), lambda j, i: (0, j))],
        out_specs=pl.BlockSpec((tm, tn), lambda j, i: (i, j)),
        out_shape=jax.ShapeDtypeStruct((rows, PACK_W), F32),
        compiler_params=pltpu.CompilerParams(vmem_limit_bytes=VMEM_LIMIT),
        name="inproj",
    )(x2d, mod3, norm_g, w_packed)


RW_CHUNK = 64


RW_PAIRS = RW_HEADS // 2
RW_BLOCK_ROWS = 256


def _rwkv_time_mask(c, reverse):
    row = lax.broadcasted_iota(jnp.int32, (2 * c, 4 * c), 0)
    col = lax.broadcasted_iota(jnp.int32, (2 * c, 4 * c), 1)
    t = jnp.where(row >= c, row - c, row)
    s = col & (c - 1)
    earlier = (s > t) if reverse else (s < t)
    return earlier | ((row >= c) & (s == t))


def _rwkv_kernel(*refs, seq, has_s0):
    if has_s0:
        r_ref, k_ref, v_ref, wa_ref, s0_ref = refs[:5]
        rest = refs[5:]
    else:
        r_ref, k_ref, v_ref, wa_ref = refs[:4]
        s0_ref = None
        rest = refs[4:]
    (wcat_ref, wa0_ref, kk_ref, ka_ref, rk_ref, lng_ref, lnb_ref, hsum_ref,
     o_ref, sout_ref, acc_ref, kn_ref, st_ref) = rest
    c = RW_CHUNK
    nc = seq // c
    blk = min(RW_BLOCK_ROWS, seq)
    hsum = hsum_ref[...]

    def pre_step(i, carry):
        rows = pl.ds(pl.multiple_of(i * blk, blk), blk)
        r = r_ref[rows, :]
        k = k_ref[rows, :]
        kk0 = k * kk_ref[...]
        red = _dot(jnp.concatenate([kk0 * kk0, r * k * rk_ref[...]], axis=0).astype(BF16), hsum)
        kn_ref[rows, :] = kk0 / jnp.maximum(jnp.sqrt(red[0:blk]), 1e-12)
        o_ref[rows, :] = red[blk:2 * blk] * v_ref[rows, :]
        return carry

    def post_step(i, carry):
        rows = pl.ds(pl.multiple_of(i * blk, blk), blk)
        o = acc_ref[0, rows, :] + acc_ref[1, rows, :]
        mu = _dot(o.astype(BF16), hsum) * (1.0 / RW_HEAD)
        dev = o - mu
        var = _dot((dev * dev).astype(BF16), hsum) * (1.0 / RW_HEAD)
        o_ref[rows, :] = o_ref[rows, :] + (dev * lax.rsqrt(var + RW_GN_EPS) * lng_ref[...] + lnb_ref[...])
        return carry

    even = lax.broadcasted_iota(jnp.int32, (1, LANE), 1) < RW_HEAD
    row = lax.broadcasted_iota(jnp.int32, (c, c), 0)
    col = lax.broadcasted_iota(jnp.int32, (c, c), 1)
    incl16 = [jnp.where(col <= row, 1.0, 0.0).astype(BF16), jnp.where(col >= row, 1.0, 0.0).astype(BF16)]
    tmask = [_rwkv_time_mask(c, False), _rwkv_time_mask(c, True)]
    r2 = lax.broadcasted_iota(jnp.int32, (LANE, LANE), 0)
    c2 = lax.broadcasted_iota(jnp.int32, (LANE, LANE), 1)
    same_head = (r2 < RW_HEAD) == (c2 < RW_HEAD)
    z16 = jnp.zeros((c, LANE), BF16)
    lanes = [slice(p * LANE, (p + 1) * LANE) for p in range(RW_PAIRS)]
    probs = [(d, p) for d in range(2) for p in range(RW_PAIRS)]

    def chunk_operands(d, ci):
        rows = pl.ds(pl.multiple_of(ci * c, c), c)
        r = r_ref[rows, :]
        k = k_ref[rows, :]
        wa = wa_ref[rows, :]
        kk = kn_ref[rows, :]
        lora = jnp.where(even, jnp.tanh(wa), wa)
        pre = _dot(lora.astype(BF16), wcat_ref[d]) + wa0_ref[d:d + 1, :]
        w_log = -_softplus(-pre[:, 0:RW_WIDTH]) - 0.5
        lw = -jnp.exp(w_log)
        a = jax.nn.sigmoid(pre[:, RW_WIDTH:2 * RW_WIDTH])
        kd = k * (1.0 + (a - 1.0) * ka_ref[...])
        bv = kk * a
        gam = _dot_sel(incl16[d], lw)
        last = 0 if d == 1 else c - 1
        gtot = gam[last:last + 1, :]
        gref = gam[c // 2:c // 2 + 1, :]
        e_in = jnp.exp(gam - gref)
        e_out = jnp.exp(gref - gam)
        e_end = jnp.exp(gtot - gam)
        lhs = jnp.concatenate([-kk * (e_in * jnp.exp(-lw)), r * e_in], axis=0).astype(BF16)
        return dict(rows=rows, lhs=lhs, b16=(bv * e_out).astype(BF16), k16=(kd * e_out).astype(BF16),
                    be16=(bv * e_end).astype(BF16), ke16=(kd * e_end).astype(BF16),
                    v16=v_ref[rows, :].astype(BF16), dtot=jnp.exp(gtot), eref=jnp.exp(gref))

    def main_step(i, carry):
        ops = [chunk_operands(0, i), chunk_operands(1, nc - 1 - i)]
        n = range(len(probs))
        pick = lambda name: [ops[d][name][:, lanes[p]] for d, p in probs]
        lhs, b, k, be, ke, v = (pick(s) for s in ("lhs", "b16", "k16", "be16", "ke16", "v16"))
        st = [st_ref[d, p] for d, p in probs]
        first = lambda x: jnp.where(even, x, jnp.zeros_like(x))
        second = lambda x: jnp.where(even, jnp.zeros_like(x), x)
        v_e = [first(x) for x in v]
        v_o = [second(x) for x in v]
        aa = [jnp.where(tmask[d], _dot_nt(lhs[j], jnp.concatenate(
            [first(k[j]), first(b[j]), second(b[j]), second(k[j])], axis=0)), 0.0) for j, (d, p) in enumerate(probs)]
        sp = [_dot_nt(lhs[j], (st[j] * ops[d]["eref"][:, lanes[p]]).astype(BF16))
              for j, (d, p) in enumerate(probs)]
        aa16 = [x.astype(BF16) for x in aa]
        rhs_u = [_dot(aa16[j][0:c], jnp.concatenate([v_e[j], z16, z16, v_o[j]], axis=0)) + sp[j][0:c] for j in n]
        x_e = [jnp.where(even, rhs_u[j], aa[j][0:c, 0:LANE]) for j in n]
        x_o = [jnp.where(even, aa[j][0:c, LANE:2 * LANE], rhs_u[j]) for j in n]
        covered = 1
        while covered < c:
            e16 = [x.astype(BF16) for x in x_e]
            o16 = [x.astype(BF16) for x in x_o]
            x_e = [_dot(e16[j], jnp.concatenate([z16, e16[j]], axis=0)) + first(x_e[j]) for j in n]
            x_o = [_dot(o16[j], jnp.concatenate([o16[j], z16], axis=0)) + second(x_o[j]) for j in n]
            covered *= 2
        u16 = [jnp.where(even, x_e[j], x_o[j]).astype(BF16) for j in n]
        for j, (d, p) in enumerate(probs):
            uv_rows = jnp.concatenate([v_e[j], first(u16[j]), second(u16[j]), v_o[j]], axis=0)
            acc_ref[d, ops[d]["rows"], lanes[p]] = sp[j][c:2 * c] + _dot(aa16[j][c:2 * c], uv_rows)
            upd = _dot_tn(jnp.concatenate([u16[j], v[j]], axis=0), jnp.concatenate([be[j], ke[j]], axis=0))
            st_ref[d, p] = st[j] * ops[d]["dtot"][:, lanes[p]] + jnp.where(same_head, upd, 0.0)
        return carry

    st_ref[...] = jnp.zeros_like(st_ref)
    if has_s0:
        for d in range(2):
            for h in range(RW_HEADS):
                off = (h % 2) * RW_HEAD
                st_ref[d, h // 2, off:off + RW_HEAD, off:off + RW_HEAD] = s0_ref[d, h]
    lax.fori_loop(0, seq // blk, pre_step, 0)
    lax.fori_loop(0, nc, main_step, 0)
    for d in range(2):
        for h in range(RW_HEADS):
            off = (h % 2) * RW_HEAD
            sout_ref[d, h] = st_ref[d, h // 2, off:off + RW_HEAD, off:off + RW_HEAD]
    lax.fori_loop(0, seq // blk, post_step, 0)


def _rwkv_call(proj, s0, layer, batch, seq, wts):
    has_s0 = s0 is not None
    blk = lambda w, cb: pl.BlockSpec((seq, w), lambda b: (b, cb))
    in_specs = [blk(512, COL_RW_R // 512), blk(512, COL_RW_K // 512), blk(512, COL_RW_V // 512),
                blk(LANE, COL_RW_WA // LANE)]
    args = [proj, proj, proj, proj]
    if has_s0:
        in_specs.append(pl.BlockSpec((None, None, 2, RW_HEADS, RW_HEAD, RW_HEAD),
                                     lambda b: (b, layer, 0, 0, 0, 0)))
        args.append(s0)
    for w in wts:
        in_specs.append(pl.BlockSpec(w.shape, lambda b, n=w.ndim: (0,) * n))
        args.append(w)
    return pl.pallas_call(
        functools.partial(_rwkv_kernel, seq=seq, has_s0=has_s0),
        grid=(batch,),
        in_specs=in_specs,
        out_specs=[pl.BlockSpec((seq, RW_WIDTH), lambda b: (b, 0)),
                   pl.BlockSpec((None, 2, RW_HEADS, RW_HEAD, RW_HEAD), lambda b: (b, 0, 0, 0, 0))],
        out_shape=[jax.ShapeDtypeStruct((batch * seq, RW_WIDTH), F32),
                   jax.ShapeDtypeStruct((batch, 2, RW_HEADS, RW_HEAD, RW_HEAD), F32)],
        scratch_shapes=[pltpu.VMEM((2, seq, RW_WIDTH), F32),
                        pltpu.VMEM((seq, RW_WIDTH), F32),
                        pltpu.VMEM((2, RW_PAIRS, LANE, LANE), F32)],
        compiler_params=pltpu.CompilerParams(vmem_limit_bytes=VMEM_LIMIT),
        name="rwkv",
    )(*args)


GLA_LEVELS = (32, 16, 8, 4, 2, 1)


def _gla_constants():
    c = CHUNK
    nl = len(GLA_LEVELS)
    mexp = np.zeros((2, (nl + 1) * c, c), np.float32)
    bmask = np.zeros((2, nl + 1, c, c), np.float32)
    for d in range(2):
        pos = np.arange(c) if d == 0 else c - 1 - np.arange(c)
        pt = pos[:, None]
        pj = pos[None, :]
        mexp[d, nl * c:] = (pj <= pt)
        for li, m in enumerate(GLA_LEVELS):
            mid = (pos // (2 * m)) * (2 * m) + m
            second = pos >= mid
            mq = (pj >= mid[:, None]) & (pj <= pt) & second[:, None]
            mk = (pj > pt) & (pj <= mid[:, None] - 1) & (~second)[:, None]
            mexp[d, li * c:(li + 1) * c] = mq | mk
            same = (pos[:, None] // (2 * m)) == (pos[None, :] // (2 * m))
            bmask[d, li] = same & second[:, None] & (~second)[None, :]
        bmask[d, nl] = np.eye(c)
    return mexp, np.concatenate([bmask, bmask], axis=-1)


def _gla_kernel(*refs, seq, has_s0):
    if has_s0:
        q_ref, k_ref, v_ref, ad_ref, s0_ref = refs[:5]
        rest = refs[5:]
    else:
        q_ref, k_ref, v_ref, ad_ref = refs[:4]
        s0_ref = None
        rest = refs[4:]
    (a2_ref, ab_ref, ng_ref, mexp_ref, bmask_ref,
     o_ref, sout_ref, acc_ref, st_ref) = rest
    c = CHUNK
    nc = seq // c
    nl = len(GLA_LEVELS)

    def chunk_operands(d, ci):
        rows = pl.ds(pl.multiple_of(ci * c, c), c)
        q = q_ref[rows, :] * (GLA_DK ** -0.5)
        k = k_ref[rows, :]
        v = v_ref[rows, :]
        x = _dot(ad_ref[rows, :].astype(BF16), a2_ref[d]) + ab_ref[d:d + 1, :]
        g = (jnp.minimum(x, 0.0) - jnp.log1p(jnp.exp(-jnp.abs(x)))) * (1.0 / GLA_LOGIT_NORM)
        g_hi, g_lo = _split(g)
        sums = _dot(mexp_ref[d], g_hi)
        ex = jnp.exp(sums[0:nl * c])
        b = sums[nl * c:(nl + 1) * c] + _dot(mexp_ref[d, nl * c:(nl + 1) * c, :], g_lo)
        last = 0 if d == 1 else c - 1
        blast = b[last:last + 1, :]
        qb = (q * jnp.exp(b)).astype(BF16)
        kdec = (k * jnp.exp(blast - b)).astype(BF16)
        dtot = jnp.exp(blast)
        qs = [(q * ex[li * c:(li + 1) * c]).astype(BF16) for li in range(nl)] + [q.astype(BF16)]
        ks = [(k * ex[li * c:(li + 1) * c]).astype(BF16) for li in range(nl)] + [k.astype(BF16)]
        return dict(rows=rows, qs=qs, ks=ks, qb=qb, kdec=kdec, dtot=dtot, v16=v.astype(BF16))

    even = lax.broadcasted_iota(jnp.int32, (1, LANE), 1) < GLA_DK
    r2 = lax.broadcasted_iota(jnp.int32, (2 * GLA_DV, LANE), 0)
    c2 = lax.broadcasted_iota(jnp.int32, (2 * GLA_DV, LANE), 1)
    same_head = (r2 < GLA_DV) == (c2 < GLA_DK)
    zv = jnp.zeros((c, GLA_DV), BF16)
    npair = GLA_HEADS // 2
    kls = [slice(p * LANE, (p + 1) * LANE) for p in range(npair)]
    vls = [slice(p * 2 * GLA_DV, (p + 1) * 2 * GLA_DV) for p in range(npair)]
    probs = [(d, p) for d in range(2) for p in range(npair)]
    first = lambda x: jnp.where(even, x, jnp.zeros_like(x))
    second = lambda x: jnp.where(even, jnp.zeros_like(x), x)

    def main_step(i, carry):
        ops = [chunk_operands(0, i), chunk_operands(1, nc - 1 - i)]
        st = [st_ref[d, p] for d, p in probs]
        lvl = [[_dot_nt(ops[d]["qs"][li][:, kls[p]],
                        jnp.concatenate([first(ops[d]["ks"][li][:, kls[p]]), second(ops[d]["ks"][li][:, kls[p]])], axis=0))
                for d, p in probs] for li in range(nl + 1)]
        att = [sum(bmask_ref[d, li] * lvl[li][j] for li in range(nl + 1)).astype(BF16)
               for j, (d, p) in enumerate(probs)]
        inter = [_dot_nt(ops[d]["qb"][:, kls[p]], st[j].astype(BF16)) for j, (d, p) in enumerate(probs)]
        for j, (d, p) in enumerate(probs):
            v_p = ops[d]["v16"][:, vls[p]]
            v_bd = jnp.concatenate([jnp.concatenate([v_p[:, 0:GLA_DV], zv], axis=1),
                                    jnp.concatenate([zv, v_p[:, GLA_DV:2 * GLA_DV]], axis=1)], axis=0)
            acc_ref[d, ops[d]["rows"], vls[p]] = inter[j] + _dot(att[j], v_bd)
            upd = _dot_tn(v_p, ops[d]["kdec"][:, kls[p]])
            st_ref[d, p] = st[j] * ops[d]["dtot"][:, kls[p]] + jnp.where(same_head, upd, 0.0)
        return carry

    blk = min(RW_BLOCK_ROWS, seq)

    def post_step(i, carry):
        rows = pl.ds(pl.multiple_of(i * blk, blk), blk)
        for h in range(GLA_HEADS):
            sv = slice(h * GLA_DV, (h + 1) * GLA_DV)
            o_ref[rows, sv] = _rms(acc_ref[0, rows, sv] + acc_ref[1, rows, sv]) * ng_ref[...]
        return carry

    st_ref[...] = jnp.zeros_like(st_ref)
    if has_s0:
        for d in range(2):
            for h in range(GLA_HEADS):
                ro, co = (h % 2) * GLA_DV, (h % 2) * GLA_DK
                st_ref[d, h // 2, ro:ro + GLA_DV, co:co + GLA_DK] = s0_ref[d, h].T
    lax.fori_loop(0, nc, main_step, 0)
    for d in range(2):
        for h in range(GLA_HEADS):
            ro, co = (h % 2) * GLA_DV, (h % 2) * GLA_DK
            sout_ref[d, h] = st_ref[d, h // 2, ro:ro + GLA_DV, co:co + GLA_DK].T
    lax.fori_loop(0, seq // blk, post_step, 0)


def _gla_call(proj, s0, layer, batch, seq, wts):
    has_s0 = s0 is not None
    blk = lambda w, cb: pl.BlockSpec((seq, w), lambda b: (b, cb))
    in_specs = [blk(GLA_KW, COL_GL_Q // GLA_KW), blk(GLA_KW, COL_GL_K // GLA_KW),
                blk(GLA_VW, COL_GL_V // GLA_VW), blk(LANE, COL_GL_AD // LANE)]
    args = [proj, proj, proj, proj]
    if has_s0:
        in_specs.append(pl.BlockSpec((None, None, 2, GLA_HEADS, GLA_DK, GLA_DV),
                                     lambda b: (b, layer, 0, 0, 0, 0)))
        args.append(s0)
    for w in wts:
        in_specs.append(pl.BlockSpec(w.shape, lambda b, n=w.ndim: (0,) * n))
        args.append(w)
    return pl.pallas_call(
        functools.partial(_gla_kernel, seq=seq, has_s0=has_s0),
        grid=(batch,),
        in_specs=in_specs,
        out_specs=[pl.BlockSpec((seq, GLA_VW), lambda b: (b, 0)),
                   pl.BlockSpec((None, 2, GLA_HEADS, GLA_DK, GLA_DV), lambda b: (b, 0, 0, 0, 0))],
        out_shape=[jax.ShapeDtypeStruct((batch * seq, GLA_VW), F32),
                   jax.ShapeDtypeStruct((batch, 2, GLA_HEADS, GLA_DK, GLA_DV), F32)],
        scratch_shapes=[pltpu.VMEM((2, seq, GLA_VW), F32),
                        pltpu.VMEM((2, GLA_HEADS // 2, 2 * GLA_DV, LANE), F32)],
        compiler_params=pltpu.CompilerParams(vmem_limit_bytes=VMEM_LIMIT),
        name="gla",
    )(*args)


def _mla_kernel(*refs, seq, past, tq):
    has_ctx = past > 0
    if has_ctx:
        (qd_ref, kvd_ref, kpe_ref, kpesw_ref, cckv_ref, ckpe_ref,
         qng_ref, wq_ref, wqsw_ref, kvng_ref, wk_ref, wv_ref,
         cosq_ref, sinq_ref, cosk_ref, sink_ref, epos_ref,
         o_ref, ckv_ref, kcat_ref, vv_ref) = refs
    else:
        (qd_ref, kvd_ref, kpe_ref,
         qng_ref, wq_ref, kvng_ref, wk_ref, wv_ref,
         o_ref, ckv_ref, kcat_ref, vv_ref) = refs

    @pl.when(pl.program_id(1) == 0)
    def _():
        ckv = _rms(kvd_ref[...]) * kvng_ref[...]
        ckv_ref[...] = ckv
        if has_ctx:
            kpos = kpe_ref[...] * cosk_ref[...] + kpesw_ref[...] * sink_ref[...]
        else:
            kpos = kpe_ref[...]
        segs = [(0, seq, ckv, kpos)]
        if has_ctx:
            cpos = _dot(ckpe_ref[...].astype(BF16), epos_ref[...])
            segs.append((seq, past, cckv_ref[...], cpos))
        for start, n, lat, pos in segs:
            lat16 = lat.astype(BF16)
            kc = _dot(lat16, wk_ref[...])
            for h in range(MLA_HEADS):
                kcat_ref[h, start:start + n, :] = (kc[:, h * LANE:(h + 1) * LANE] + pos).astype(BF16)
            vv_ref[start:start + n, :] = _dot(lat16, wv_ref[...]).astype(BF16)

    qlat = (_rms(qd_ref[...]) * qng_ref[...]).astype(BF16)
    qc = _dot(qlat, wq_ref[...])
    if has_ctx:
        qsw = _dot(qlat, wqsw_ref[...])
    for h in range(MLA_HEADS):
        hl = slice(h * LANE, (h + 1) * LANE)
        q_raw = qc[:, hl]
        if has_ctx:
            q_self = (q_raw * cosq_ref[...] + qsw[:, hl] * sinq_ref[...]).astype(BF16)
        else:
            q_self = q_raw.astype(BF16)
        s1 = _dot_nt(q_self, kcat_ref[h, 0:seq, :])
        m = jnp.max(s1, axis=-1, keepdims=True)
        if has_ctx:
            s2 = _dot_nt(q_raw.astype(BF16), kcat_ref[h, seq:seq + past, :])
            m = jnp.maximum(m, jnp.max(s2, axis=-1, keepdims=True))
        p1 = jnp.exp(s1 - m)
        l = jnp.sum(p1, axis=-1, keepdims=True)
        vl = slice(h * MLA_V, (h + 1) * MLA_V)
        o_h = _dot(p1.astype(BF16), vv_ref[0:seq, vl])
        if has_ctx:
            p2 = jnp.exp(s2 - m)
            l = l + jnp.sum(p2, axis=-1, keepdims=True)
            o_h = o_h + _dot(p2.astype(BF16), vv_ref[seq:seq + past, vl])
        o_ref[:, vl] = o_h / l


def _mla_call(proj, cache_ckv, cache_kpe, layer, batch, seq, wts, tables):
    has_ctx = cache_ckv is not None
    past = cache_ckv.shape[2] if has_ctx else 0
    tq = 256
    nq = seq // tq
    full = lambda cb: pl.BlockSpec((seq, LANE), lambda b, i: (b, cb))
    in_specs = [pl.BlockSpec((tq, MLA_Q_RANK), lambda b, i: (b * nq + i, COL_ML_QD // MLA_Q_RANK)),
                full(COL_ML_KVD // LANE), full(COL_KPE // LANE)]
    args = [proj, proj, proj]
    if has_ctx:
        in_specs += [full(COL_KPE_SW // LANE),
                     pl.BlockSpec((None, None, past, MLA_KV_RANK), lambda b, i: (b, layer, 0, 0)),
                     pl.BlockSpec((None, None, past, MLA_ROPE), lambda b, i: (b, layer, 0, 0))]
        args += [proj, cache_ckv, cache_kpe]
    qn_g, wq_cat, wq_sw, kvn_g, wk_pad, wv = wts
    const = lambda w: pl.BlockSpec(w.shape, lambda b, i, n=w.ndim: (0,) * n)
    if has_ctx:
        cosq, sinq, cosk, sink, epos = tables
        wlist = [qn_g, wq_cat, wq_sw, kvn_g, wk_pad, wv]
        in_specs += [const(w) for w in wlist]
        in_specs += [pl.BlockSpec((tq, LANE), lambda b, i: (i, 0)), pl.BlockSpec((tq, LANE), lambda b, i: (i, 0)),
                     const(cosk), const(sink), const(epos)]
        args += wlist + [cosq, sinq, cosk, sink, epos]
    else:
        wlist = [qn_g, wq_cat, kvn_g, wk_pad, wv]
        in_specs += [const(w) for w in wlist]
        args += wlist
    return pl.pallas_call(
        functools.partial(_mla_kernel, seq=seq, past=past, tq=tq),
        grid=(batch, nq),
        in_specs=in_specs,
        out_specs=[pl.BlockSpec((tq, MLA_VW), lambda b, i: (b * nq + i, 0)),
                   pl.BlockSpec((seq, MLA_KV_RANK), lambda b, i: (b, 0))],
        out_shape=[jax.ShapeDtypeStruct((batch * seq, MLA_VW), F32),
                   jax.ShapeDtypeStruct((batch * seq, MLA_KV_RANK), F32)],
        scratch_shapes=[pltpu.VMEM((MLA_HEADS, seq + past, LANE), BF16),
                        pltpu.VMEM((seq + past, MLA_VW), BF16)],
        compiler_params=pltpu.CompilerParams(vmem_limit_bytes=VMEM_LIMIT,
                                             dimension_semantics=("arbitrary", "arbitrary")),
        name="mla",
    )(*args)


def _outproj_kernel(x_ref, oa_ref, ob_ref, oc_ref, ga_ref, gb_ref, gc_ref, ma_ref, mb_ref, mc_ref,
                    mod_ref, wa_ref, wb_ref, wc_ref, wo_ref, fg_ref, o_ref, *, final):
    def branch(o_r, g_r, w_r):
        g = g_r[...]
        return _dot((o_r[...] * (g * jax.nn.sigmoid(g))).astype(BF16), w_r[...])

    y = (jax.nn.sigmoid(ma_ref[...]) * branch(oa_ref, ga_ref, wa_ref)
         + jax.nn.sigmoid(mb_ref[...]) * branch(ob_ref, gb_ref, wb_ref)
         + jax.nn.sigmoid(mc_ref[...]) * branch(oc_ref, gc_ref, wc_ref))
    y = _dot(y.astype(BF16), wo_ref[...])
    gate = mod_ref[...][:, 2 * D_MODEL:3 * D_MODEL]
    hn = x_ref[...] + gate * y
    if final:
        hn = _rms(hn) * fg_ref[...]
    o_ref[...] = hn


def _outproj_call(x2d, o_a, o_b, o_c, proj, mod3, wts, final_g, rows_per_mod, mod_base, final):
    rows = x2d.shape[0]
    tm = 256
    tiles_per_mod = rows_per_mod // tm
    row = lambda w, cb=0: pl.BlockSpec((tm, w), lambda i: (i, cb))
    const = lambda w: pl.BlockSpec(w.shape, lambda i, n=w.ndim: (0,) * n)
    in_specs = [row(D_MODEL), row(512), row(512), row(512),
                row(512, COL_GATES // 512), row(512, COL_GATES // 512 + 1), row(512, COL_GATES // 512 + 2),
                row(D_MODEL, COL_MERGE // D_MODEL), row(D_MODEL, COL_MERGE // D_MODEL + 1),
                row(D_MODEL, COL_MERGE // D_MODEL + 2),
                pl.BlockSpec((None, 1, 3 * D_MODEL), lambda i: (mod_base + i // tiles_per_mod, 0, 0))]
    in_specs += [const(w) for w in wts] + [const(final_g)]
    return pl.pallas_call(
        functools.partial(_outproj_kernel, final=final),
        grid=(rows // tm,),
        in_specs=in_specs,
        out_specs=row(D_MODEL),
        out_shape=jax.ShapeDtypeStruct((rows, D_MODEL), F32),
        compiler_params=pltpu.CompilerParams(vmem_limit_bytes=VMEM_LIMIT),
        name="outproj",
    )(x2d, o_a, o_b, o_c, proj, proj, proj, proj, proj, proj, mod3, *wts, final_g)


def _pack_w_in(w):
    seg = lambda i: w[:, _IN_OFF[i]:_IN_OFF[i + 1]]
    z = lambda n: jnp.zeros((D_MODEL, n), w.dtype)
    kpe = seg(13)
    q = MLA_ROPE // 4
    kpe_sw = jnp.concatenate([kpe[:, q:2 * q], kpe[:, 0:q], kpe[:, 3 * q:4 * q], kpe[:, 2 * q:3 * q]], axis=1)
    tail = LANE - KPE_LANE - MLA_ROPE
    cols = [w[:, _IN_OFF[0]:_IN_OFF[5]],
            seg(9), z(LANE - GLA_GATE_RANK),
            seg(11),
            seg(6), seg(7), seg(8),
            seg(12),
            z(KPE_LANE), kpe, z(tail),
            z(KPE_LANE), kpe_sw, z(tail),
            z(COL_GATES - COL_KPE_SW - LANE),
            seg(5), seg(10), seg(14),
            seg(15)]
    return jnp.concatenate(cols, axis=1).astype(BF16)


def _rope_lane_tables(seq):
    n_freq = MLA_ROPE // 4
    t = np.arange(seq)
    inv = ROPE_THETA ** (-np.arange(n_freq, dtype=np.float64) / n_freq)
    ang = np.stack([(t // GRID_W)[:, None] * inv, (t % GRID_W)[:, None] * inv], axis=1)
    cos = np.repeat(np.cos(ang)[:, :, None, :], 2, axis=2).reshape(seq, MLA_ROPE)
    sin = np.stack([-np.sin(ang), np.sin(ang)], axis=2).reshape(seq, MLA_ROPE)
    return cos, sin


def _mla_tables(seq):
    cos, sin = _rope_lane_tables(seq)
    cosq = np.zeros((seq, LANE), np.float32)
    sinq = np.zeros((seq, LANE), np.float32)
    cosq[:, :KPE_LANE] = 1.0
    cosq[:, KPE_LANE:KPE_LANE + MLA_ROPE] = cos
    sinq[:, KPE_LANE:KPE_LANE + MLA_ROPE] = sin
    cosk = np.zeros((seq, LANE), np.float32)
    cosk[:, KPE_LANE:KPE_LANE + MLA_ROPE] = cos
    epos = np.zeros((MLA_ROPE, LANE), np.float32)
    epos[np.arange(MLA_ROPE), KPE_LANE + np.arange(MLA_ROPE)] = 1.0
    return (jnp.asarray(cosq), jnp.asarray(sinq), jnp.asarray(cosk), jnp.asarray(sinq),
            jnp.asarray(epos, dtype=BF16))


def _pack_mla_weights(qn_g, wq_up, kvn_g, wkv_up):
    scale = (MLA_NOPE + MLA_ROPE) ** -0.5
    wq = wq_up.reshape(MLA_Q_RANK, MLA_HEADS, MLA_NOPE + MLA_ROPE) * scale
    nope, rope = wq[..., :MLA_NOPE], wq[..., MLA_NOPE:]
    q = MLA_ROPE // 4
    rope_sw = jnp.concatenate([rope[..., q:2 * q], rope[..., 0:q], rope[..., 3 * q:4 * q], rope[..., 2 * q:3 * q]], -1)
    tail = jnp.zeros((MLA_Q_RANK, MLA_HEADS, LANE - KPE_LANE - MLA_ROPE), wq.dtype)
    wq_cat = jnp.concatenate([nope, rope, tail], -1).reshape(MLA_Q_RANK, MLA_HEADS * LANE).astype(BF16)
    wq_sw = jnp.concatenate([jnp.zeros_like(nope), rope_sw, tail], -1).reshape(MLA_Q_RANK, MLA_HEADS * LANE).astype(BF16)
    wkv = wkv_up.reshape(MLA_KV_RANK, MLA_HEADS, MLA_NOPE + MLA_V)
    wk = jnp.concatenate([wkv[..., :MLA_NOPE], jnp.zeros((MLA_KV_RANK, MLA_HEADS, LANE - MLA_NOPE), wkv.dtype)], -1)
    wk_pad = wk.reshape(MLA_KV_RANK, MLA_HEADS * LANE).astype(BF16)
    wv = wkv[..., MLA_NOPE:].reshape(MLA_KV_RANK, MLA_VW).astype(BF16)
    return (qn_g.reshape(1, -1), wq_cat, wq_sw, kvn_g.reshape(1, -1), wk_pad, wv)


def _head_sum_matrix():
    lane = np.arange(RW_WIDTH)
    return jnp.asarray((lane[:, None] // RW_HEAD == lane[None, :] // RW_HEAD).astype(np.float32), dtype=BF16)


def _trunk(x_prompt, x_sample, c, cache_ckv, cache_kpe, state_rwkv, state_gla, c_ctx,
           norm_g, w_mod, b_mod, w_in, rw_w0, rw_w2, rw_a0, rw_a2, rw_k_k, rw_k_a, rw_r_k,
           rw_ln_g, rw_ln_b, rw_out, gla_a2, gla_ab, gla_norm_g, gla_out,
           mla_qn_g, mla_wq_up, mla_kvn_g, mla_wkv_up, mla_out, w_out, final_g):
    bp, tp, _ = x_prompt.shape
    bs, ts, _ = x_sample.shape
    hp = x_prompt.reshape(bp * tp, D_MODEL)
    hs = x_sample.reshape(bs * ts, D_MODEL)
    cvec8 = jnp.concatenate([c, c_ctx[None, :], jnp.zeros((8 - bs - 1, D_MODEL), F32)], axis=0)
    ctx_row = bs
    hsum = _head_sum_matrix()
    mexp, bmask = _gla_constants()
    mexp = jnp.asarray(mexp, dtype=BF16)
    bmask = jnp.asarray(bmask)
    tables = _mla_tables(ts)
    fg = final_g.reshape(1, D_MODEL)
    ckv_l, kpe_l, rw_l, gla_l = [], [], [], []
    for l in range(DEPTH):
        mod3 = _mod_call(cvec8, w_mod, b_mod, l).reshape(8, 1, 3 * D_MODEL)
        w_packed = _pack_w_in(w_in[l])
        ng = norm_g[l].reshape(1, D_MODEL)
        row = lambda a: a.reshape(1, -1)
        zr = jnp.zeros((2, RW_RANK, RW_WIDTH), F32)
        wcat = jnp.concatenate([jnp.concatenate([rw_w2[l], zr], axis=2),
                                jnp.concatenate([zr, rw_a2[l]], axis=2)], axis=1).astype(BF16)
        rw_wts = (wcat, jnp.concatenate([rw_w0[l], rw_a0[l]], axis=1), row(rw_k_k[l]), row(rw_k_a[l]),
                  row(rw_r_k[l]), row(rw_ln_g[l]), row(rw_ln_b[l]), hsum)
        a2p = jnp.concatenate([gla_a2[l], jnp.zeros((2, LANE - GLA_GATE_RANK, GLA_KW), F32)], axis=1).astype(BF16)
        gla_wts = (a2p, gla_ab[l], row(gla_norm_g[l]), mexp, bmask)
        mla_wts = _pack_mla_weights(mla_qn_g[l], mla_wq_up[l], mla_kvn_g[l], mla_wkv_up[l])
        out_wts = (rw_out[l].astype(BF16), gla_out[l].astype(BF16), mla_out[l].astype(BF16), w_out[l].astype(BF16))
        final = l == DEPTH - 1

        proj = _inproj_call(hp, mod3, ng, w_packed, bp * tp, ctx_row)
        o_a, s_rw = _rwkv_call(proj, None, l, bp, tp, rw_wts)
        o_b, s_gla = _gla_call(proj, None, l, bp, tp, gla_wts)
        o_c, ckv = _mla_call(proj, None, None, l, bp, tp, mla_wts, None)
        ckv_l.append(ckv.reshape(bp, tp, MLA_KV_RANK))
        kpe_l.append(proj[:, COL_KPE + KPE_LANE:COL_KPE + KPE_LANE + MLA_ROPE].reshape(bp, tp, MLA_ROPE))
        rw_l.append(s_rw)
        gla_l.append(s_gla)
        hp = _outproj_call(hp, o_a, o_b, o_c, proj, mod3, out_wts, fg, bp * tp, ctx_row, final)

        proj = _inproj_call(hs, mod3, ng, w_packed, ts, 0)
        o_a, _ = _rwkv_call(proj, state_rwkv, l, bs, ts, rw_wts)
        o_b, _ = _gla_call(proj, state_gla, l, bs, ts, gla_wts)
        o_c, _ = _mla_call(proj, cache_ckv, cache_kpe, l, bs, ts, mla_wts, tables)
        hs = _outproj_call(hs, o_a, o_b, o_c, proj, mod3, out_wts, fg, ts, 0, final)

    return (hp.reshape(bp, tp, D_MODEL), hs.reshape(bs, ts, D_MODEL),
            jnp.stack(ckv_l, axis=1), jnp.stack(kpe_l, axis=1),
            jnp.stack(rw_l, axis=1), jnp.stack(gla_l, axis=1))


_trunk_jit = jax.jit(_trunk)


def kernel(x_prompt, x_sample, c, cache_ckv, cache_kpe, state_rwkv, state_gla, c_ctx, norm_g, w_mod, b_mod, w_in, rw_w0, rw_w2, rw_a0, rw_a2, rw_k_k, rw_k_a, rw_r_k, rw_ln_g, rw_ln_b, rw_out, gla_a2, gla_ab, gla_norm_g, gla_out, mla_qn_g, mla_wq_up, mla_kvn_g, mla_wkv_up, mla_out, w_out, final_g):
    return _trunk_jit(x_prompt, x_sample, c, cache_ckv, cache_kpe, state_rwkv, state_gla, c_ctx, norm_g, w_mod, b_mod, w_in, rw_w0, rw_w2, rw_a0, rw_a2, rw_k_k, rw_k_a, rw_r_k, rw_ln_g, rw_ln_b, rw_out, gla_a2, gla_ab, gla_norm_g, gla_out, mla_qn_g, mla_wq_up, mla_kvn_g, mla_wkv_up, mla_out, w_out, final_g)
```

```python
import functools

import numpy as np
import jax
import jax.numpy as jnp
from jax import lax
from jax.experimental import pallas as pl
from jax.experimental.pallas import tpu as pltpu

F32 = jnp.float32
BF16 = jnp.bfloat16

D_MODEL = 1024
DEPTH = 2
GRID_W = 64
NORM_EPS = 1e-6
RW_HEADS = 8
RW_HEAD = 64
RW_WIDTH = RW_HEADS * RW_HEAD
RW_RANK = 64
RW_GN_EPS = 64e-5
GLA_HEADS = 4
GLA_DK = 64
GLA_DV = 128
GLA_KW = GLA_HEADS * GLA_DK
GLA_VW = GLA_HEADS * GLA_DV
GLA_GATE_RANK = 16
GLA_LOGIT_NORM = 16.0
MLA_HEADS = 8
MLA_NOPE = 64
MLA_ROPE = 32
MLA_V = 64
MLA_Q_RANK = 256
MLA_KV_RANK = 128
MLA_VW = MLA_HEADS * MLA_V
ROPE_THETA = 10000.0
N_BRANCH = 3

_IN_SIZES = (RW_WIDTH, RW_WIDTH, RW_WIDTH, RW_RANK, RW_RANK, RW_WIDTH,
             GLA_KW, GLA_KW, GLA_VW, GLA_GATE_RANK, GLA_VW,
             MLA_Q_RANK, MLA_KV_RANK, MLA_ROPE, MLA_VW, N_BRANCH * D_MODEL)
_IN_OFF = tuple(int(v) for v in np.concatenate([[0], np.cumsum(_IN_SIZES)]))

LANE = 128
COL_RW_R = 0
COL_RW_K = 512
COL_RW_V = 1024
COL_RW_WA = 1536
COL_GL_AD = 1664
COL_ML_QD = 1792
COL_GL_Q = 2048
COL_GL_K = 2304
COL_GL_V = 2560
COL_ML_KVD = 3072
COL_KPE = 3200
COL_KPE_SW = 3328
PACK_A = 3584
COL_MERGE = 0
COL_GATES = 3072
PACK_B = 4608
KPE_LANE = MLA_NOPE

CHUNK = 64
INPROJ_TM = 1024
INPROJ_TNA, INPROJ_TNB = 896, 768
OUTPROJ_TM = 512
VMEM_LIMIT = 48 * 1024 * 1024


def _dot(a, b, prec=None):
    return jnp.dot(a, b, preferred_element_type=F32, precision=prec)


def _dot_nt(a, b, prec=None):
    return lax.dot_general(a, b, (((1,), (1,)), ((), ())), preferred_element_type=F32, precision=prec)


def _dot_tn(a, b, prec=None):
    return lax.dot_general(a, b, (((0,), (0,)), ((), ())), preferred_element_type=F32, precision=prec)


def _split(x):
    hi = x.astype(BF16)
    return hi, (x - hi.astype(F32)).astype(BF16)


def _dot_split(a, b):
    ah, al = _split(a)
    bh, bl = _split(b)
    return _dot(ah, bh) + _dot(al, bh) + _dot(ah, bl)


def _dot_sel(sel16, x):
    xh, xl = _split(x)
    return _dot(sel16, xh) + _dot(sel16, xl)


def _softplus(z):
    return jnp.maximum(z, 0.0) + jnp.log1p(jnp.exp(-jnp.abs(z)))


def _sigmoid(x):
    return 0.5 * jnp.tanh(0.5 * x) + 0.5


def _rms(x, eps=NORM_EPS):
    return x * lax.rsqrt(jnp.mean(x * x, axis=-1, keepdims=True) + eps)


def _mod_kernel(c_ref, w_ref, b_ref, o_ref):
    c = c_ref[...]
    o_ref[...] = _dot_split(c * _sigmoid(c), w_ref[...]) + b_ref[...]


def _mod_call(cvec8, w_mod, b_mod, layer):
    tn = 1024
    return pl.pallas_call(
        _mod_kernel,
        grid=(3 * D_MODEL // tn,),
        in_specs=[pl.BlockSpec((8, D_MODEL), lambda j: (0, 0)),
                  pl.BlockSpec((None, D_MODEL, tn), lambda j: (layer, 0, j)),
                  pl.BlockSpec((None, 1, tn), lambda j: (layer, 0, j))],
        out_specs=pl.BlockSpec((8, tn), lambda j: (0, j)),
        out_shape=jax.ShapeDtypeStruct((8, 3 * D_MODEL), F32),
        compiler_params=pltpu.CompilerParams(vmem_limit_bytes=VMEM_LIMIT),
        name="mod",
    )(cvec8, w_mod, b_mod.reshape(DEPTH, 1, 3 * D_MODEL))


def _inproj_kernel(x_ref, mod_ref, g_ref, wa_ref, wb_ref, oa_ref, ob_ref, h_ref, *, na):
    j = pl.program_id(1)

    @pl.when(j == 0)
    def _():
        m = mod_ref[...]
        shift = m[:, 0:D_MODEL]
        scale = m[:, D_MODEL:2 * D_MODEL]
        h_ref[...] = (_rms(x_ref[...]) * g_ref[...] * (1.0 + scale) + shift).astype(BF16)

    @pl.when(j < na)
    def _():
        oa_ref[...] = _dot(h_ref[...], wa_ref[...])

    @pl.when(j >= na)
    def _():
        ob_ref[...] = _dot(h_ref[...], wb_ref[...]).astype(BF16)


def _inproj_call(x2d, mod3, norm_g, w_a, w_b, rows_per_mod, mod_base):
    rows = x2d.shape[0]
    tm, tna, tnb = INPROJ_TM, INPROJ_TNA, INPROJ_TNB
    na, nb = PACK_A // tna, PACK_B // tnb
    tiles_per_mod = rows_per_mod // tm
    col_a = lambda i, j: (0, jnp.minimum(j, na - 1))
    col_b = lambda i, j: (0, jnp.maximum(j - na, 0))
    return pl.pallas_call(
        functools.partial(_inproj_kernel, na=na),
        grid=(rows // tm, na + nb),
        in_specs=[pl.BlockSpec((tm, D_MODEL), lambda i, j: (i, 0)),
                  pl.BlockSpec((None, 1, 3 * D_MODEL), lambda i, j: (mod_base + i // tiles_per_mod, 0, 0)),
                  pl.BlockSpec((1, D_MODEL), lambda i, j: (0, 0)),
                  pl.BlockSpec((D_MODEL, tna), col_a),
                  pl.BlockSpec((D_MODEL, tnb), col_b)],
        out_specs=[pl.BlockSpec((tm, tna), lambda i, j: (i, jnp.minimum(j, na - 1))),
                   pl.BlockSpec((tm, tnb), lambda i, j: (i, jnp.maximum(j - na, 0)))],
        out_shape=[jax.ShapeDtypeStruct((rows, PACK_A), F32),
                   jax.ShapeDtypeStruct((rows, PACK_B), BF16)],
        scratch_shapes=[pltpu.VMEM((tm, D_MODEL), BF16)],
        compiler_params=pltpu.CompilerParams(vmem_limit_bytes=VMEM_LIMIT,
                                             dimension_semantics=("arbitrary", "arbitrary")),
        name="inproj",
    )(x2d, mod3, norm_g, w_a, w_b)


RW_CHUNK = 64
RW_PAIRS = RW_HEADS // 2
RW_BLOCK_ROWS = 256


def _rwkv_time_mask(c, reverse):
    row = lax.broadcasted_iota(jnp.int32, (2 * c, 4 * c), 0)
    col = lax.broadcasted_iota(jnp.int32, (2 * c, 4 * c), 1)
    t = jnp.where(row >= c, row - c, row)
    s = col & (c - 1)
    earlier = (s > t) if reverse else (s < t)
    return earlier | ((row >= c) & (s == t))


def _rwkv_kernel(*refs, seq, has_s0):
    if has_s0:
        r_ref, k_ref, v_ref, wa_ref, s0_ref = refs[:5]
        rest = refs[5:]
    else:
        r_ref, k_ref, v_ref, wa_ref = refs[:4]
        s0_ref = None
        rest = refs[4:]
    (wcat_ref, wa0_ref, kk_ref, ka_ref, rk_ref, lng_ref, lnb_ref, hsum_ref,
     o_ref, sout_ref, acc_ref, kn_ref, st_ref) = rest
    c = RW_CHUNK
    nc = seq // c
    blk = min(RW_BLOCK_ROWS, seq)
    hsum = hsum_ref[...]

    def pre_step(i, carry):
        rows = pl.ds(pl.multiple_of(i * blk, blk), blk)
        kk0 = k_ref[rows, :] * kk_ref[...]
        kn_ref[rows, :] = kk0 / jnp.maximum(jnp.sqrt(_dot((kk0 * kk0).astype(BF16), hsum)), 1e-12)
        return carry

    def post_step(i, carry):
        rows = pl.ds(pl.multiple_of(i * blk, blk), blk)
        o = acc_ref[0, rows, :] + acc_ref[1, rows, :]
        rk = r_ref[rows, :] * k_ref[rows, :] * rk_ref[...]
        red = _dot(jnp.concatenate([o, rk], axis=0).astype(BF16), hsum)
        dev = o - red[0:blk] * (1.0 / RW_HEAD)
        var = _dot((dev * dev).astype(BF16), hsum) * (1.0 / RW_HEAD)
        o = dev * lax.rsqrt(var + RW_GN_EPS) * lng_ref[...] + lnb_ref[...]
        o_ref[rows, :] = (o + red[blk:2 * blk] * v_ref[rows, :]).astype(o_ref.dtype)
        return carry

    even = lax.broadcasted_iota(jnp.int32, (1, LANE), 1) < RW_HEAD
    row = lax.broadcasted_iota(jnp.int32, (c, c), 0)
    col = lax.broadcasted_iota(jnp.int32, (c, c), 1)
    incl16 = [jnp.where(col <= row, 1.0, 0.0).astype(BF16), jnp.where(col >= row, 1.0, 0.0).astype(BF16)]
    tmask = [_rwkv_time_mask(c, False), _rwkv_time_mask(c, True)]
    r2 = lax.broadcasted_iota(jnp.int32, (LANE, LANE), 0)
    c2 = lax.broadcasted_iota(jnp.int32, (LANE, LANE), 1)
    same_head = (r2 < RW_HEAD) == (c2 < RW_HEAD)
    z16 = jnp.zeros((c, LANE), BF16)
    lanes = [slice(p * LANE, (p + 1) * LANE) for p in range(RW_PAIRS)]
    probs = [(d, p) for d in range(2) for p in range(RW_PAIRS)]

    def chunk_operands(d, ci):
        rows = pl.ds(pl.multiple_of(ci * c, c), c)
        r = r_ref[rows, :]
        k = k_ref[rows, :]
        wa = wa_ref[rows, :]
        kk = kn_ref[rows, :]
        lora = jnp.where(even, jnp.tanh(wa), wa)
        pre = _dot(lora.astype(BF16), wcat_ref[d]) + wa0_ref[d:d + 1, :]
        w_log = -_softplus(-pre[:, 0:RW_WIDTH]) - 0.5
        lw = -jnp.exp(w_log)
        a = _sigmoid(pre[:, RW_WIDTH:2 * RW_WIDTH])
        kd = k * (1.0 + (a - 1.0) * ka_ref[...])
        bv = kk * a
        gam = _dot_sel(incl16[d], lw)
        last = 0 if d == 1 else c - 1
        gtot = gam[last:last + 1, :]
        gref = gam[c // 2:c // 2 + 1, :]
        e_in = jnp.exp(gam - gref)
        e_out = jnp.exp(gref - gam)
        e_end = jnp.exp(gtot - gam)
        lhs = jnp.concatenate([-kk * (e_in * jnp.exp(-lw)), r * e_in], axis=0).astype(BF16)
        return dict(rows=rows, lhs=lhs, b16=(bv * e_out).astype(BF16), k16=(kd * e_out).astype(BF16),
                    be16=(bv * e_end).astype(BF16), ke16=(kd * e_end).astype(BF16),
                    v16=v_ref[rows, :].astype(BF16), dtot=jnp.exp(gtot), eref=jnp.exp(gref))

    def main_step(i, carry):
        ops = [chunk_operands(0, i), chunk_operands(1, nc - 1 - i)]
        n = range(len(probs))
        pick = lambda name: [ops[d][name][:, lanes[p]] for d, p in probs]
        lhs, b, k, be, ke, v = (pick(s) for s in ("lhs", "b16", "k16", "be16", "ke16", "v16"))
        st = [st_ref[d, p] for d, p in probs]
        first = lambda x: jnp.where(even, x, jnp.zeros_like(x))
        second = lambda x: jnp.where(even, jnp.zeros_like(x), x)
        v_e = [first(x) for x in v]
        v_o = [second(x) for x in v]
        aa = [jnp.where(tmask[d], _dot_nt(lhs[j], jnp.concatenate(
            [first(k[j]), first(b[j]), second(b[j]), second(k[j])], axis=0)), 0.0) for j, (d, p) in enumerate(probs)]
        sp = [_dot_nt(lhs[j], (st[j] * ops[d]["eref"][:, lanes[p]]).astype(BF16))
              for j, (d, p) in enumerate(probs)]
        aa16 = [x.astype(BF16) for x in aa]
        rhs_u = [_dot(aa16[j][0:c], jnp.concatenate([v_e[j], z16, z16, v_o[j]], axis=0)) + sp[j][0:c] for j in n]
        x_e = [jnp.where(even, rhs_u[j], aa[j][0:c, 0:LANE]) for j in n]
        x_o = [jnp.where(even, aa[j][0:c, LANE:2 * LANE], rhs_u[j]) for j in n]
        covered = 1
        while covered < c:
            e16 = [x.astype(BF16) for x in x_e]
            o16 = [x.astype(BF16) for x in x_o]
            x_e = [_dot(e16[j], jnp.concatenate([z16, e16[j]], axis=0)) + first(x_e[j]) for j in n]
            x_o = [_dot(o16[j], jnp.concatenate([o16[j], z16], axis=0)) + second(x_o[j]) for j in n]
            covered *= 2
        u16 = [jnp.where(even, x_e[j], x_o[j]).astype(BF16) for j in n]
        for j, (d, p) in enumerate(probs):
            uv_rows = jnp.concatenate([v_e[j], first(u16[j]), second(u16[j]), v_o[j]], axis=0)
            acc_ref[d, ops[d]["rows"], lanes[p]] = sp[j][c:2 * c] + _dot(aa16[j][c:2 * c], uv_rows)
            upd = _dot_tn(jnp.concatenate([u16[j], v[j]], axis=0), jnp.concatenate([be[j], ke[j]], axis=0))
            st_ref[d, p] = st[j] * ops[d]["dtot"][:, lanes[p]] + jnp.where(same_head, upd, 0.0)
        return carry

    st_ref[...] = jnp.zeros_like(st_ref)
    if has_s0:
        for d in range(2):
            for h in range(RW_HEADS):
                off = (h % 2) * RW_HEAD
                st_ref[d, h // 2, off:off + RW_HEAD, off:off + RW_HEAD] = s0_ref[d, h]
    lax.fori_loop(0, seq // blk, pre_step, 0)
    lax.fori_loop(0, nc, main_step, 0)
    for d in range(2):
        for h in range(RW_HEADS):
            off = (h % 2) * RW_HEAD
            sout_ref[d, h] = st_ref[d, h // 2, off:off + RW_HEAD, off:off + RW_HEAD]
    lax.fori_loop(0, seq // blk, post_step, 0)


def _rwkv_call(proj, s0, layer, batch, seq, wts):
    has_s0 = s0 is not None
    blk = lambda w, cb: pl.BlockSpec((seq, w), lambda b: (b, cb))
    in_specs = [blk(512, COL_RW_R // 512), blk(512, COL_RW_K // 512), blk(512, COL_RW_V // 512),
                blk(LANE, COL_RW_WA // LANE)]
    args = [proj, proj, proj, proj]
    if has_s0:
        in_specs.append(pl.BlockSpec((None, None, 2, RW_HEADS, RW_HEAD, RW_HEAD),
                                     lambda b: (b, layer, 0, 0, 0, 0)))
        args.append(s0)
    for w in wts:
        in_specs.append(pl.BlockSpec(w.shape, lambda b, n=w.ndim: (0,) * n))
        args.append(w)
    return pl.pallas_call(
        functools.partial(_rwkv_kernel, seq=seq, has_s0=has_s0),
        grid=(batch,),
        in_specs=in_specs,
        out_specs=[pl.BlockSpec((seq, RW_WIDTH), lambda b: (b, 0)),
                   pl.BlockSpec((None, 2, RW_HEADS, RW_HEAD, RW_HEAD), lambda b: (b, 0, 0, 0, 0))],
        out_shape=[jax.ShapeDtypeStruct((batch * seq, RW_WIDTH), BF16),
                   jax.ShapeDtypeStruct((batch, 2, RW_HEADS, RW_HEAD, RW_HEAD), F32)],
        scratch_shapes=[pltpu.VMEM((2, seq, RW_WIDTH), F32),
                        pltpu.VMEM((seq, RW_WIDTH), F32),
                        pltpu.VMEM((2, RW_PAIRS, LANE, LANE), F32)],
        compiler_params=pltpu.CompilerParams(vmem_limit_bytes=VMEM_LIMIT),
        name="rwkv",
    )(*args)


GLA_LEVELS = (32, 16, 8, 4, 2, 1)


def _gla_constants():
    c = CHUNK
    nl = len(GLA_LEVELS)
    mexp = np.zeros((2, (nl + 1) * c, c), np.float32)
    bmask = np.zeros((2, nl + 1, c, c), np.float32)
    for d in range(2):
        pos = np.arange(c) if d == 0 else c - 1 - np.arange(c)
        pt = pos[:, None]
        pj = pos[None, :]
        mexp[d, nl * c:] = (pj <= pt)
        for li, m in enumerate(GLA_LEVELS):
            mid = (pos // (2 * m)) * (2 * m) + m
            second = pos >= mid
            mq = (pj >= mid[:, None]) & (pj <= pt) & second[:, None]
            mk = (pj > pt) & (pj <= mid[:, None] - 1) & (~second)[:, None]
            mexp[d, li * c:(li + 1) * c] = mq | mk
            same = (pos[:, None] // (2 * m)) == (pos[None, :] // (2 * m))
            bmask[d, li] = same & second[:, None] & (~second)[None, :]
        bmask[d, nl] = np.eye(c)
    return mexp, np.concatenate([bmask, bmask], axis=-1)


def _gla_kernel(*refs, seq, has_s0):
    if has_s0:
        q_ref, k_ref, v_ref, ad_ref, s0_ref = refs[:5]
        rest = refs[5:]
    else:
        q_ref, k_ref, v_ref, ad_ref = refs[:4]
        s0_ref = None
        rest = refs[4:]
    (a2_ref, ab_ref, ng_ref, mexp_ref, bmask_ref,
     o_ref, sout_ref, acc_ref, st_ref) = rest
    c = CHUNK
    nc = seq // c
    nl = len(GLA_LEVELS)

    def chunk_operands(d, ci):
        rows = pl.ds(pl.multiple_of(ci * c, c), c)
        q = q_ref[rows, :] * (GLA_DK ** -0.5)
        k = k_ref[rows, :]
        v = v_ref[rows, :]
        x = _dot(ad_ref[rows, :].astype(BF16), a2_ref[d]) + ab_ref[d:d + 1, :]
        g = (jnp.minimum(x, 0.0) - jnp.log1p(jnp.exp(-jnp.abs(x)))) * (1.0 / GLA_LOGIT_NORM)
        g_hi, g_lo = _split(g)
        sums = _dot(mexp_ref[d], g_hi)
        ex = jnp.exp(sums[0:nl * c])
        b = sums[nl * c:(nl + 1) * c] + _dot(mexp_ref[d, nl * c:(nl + 1) * c, :], g_lo)
        last = 0 if d == 1 else c - 1
        blast = b[last:last + 1, :]
        qb = (q * jnp.exp(b)).astype(BF16)
        kdec = (k * jnp.exp(blast - b)).astype(BF16)
        dtot = jnp.exp(blast)
        qs = [(q * ex[li * c:(li + 1) * c]).astype(BF16) for li in range(nl)] + [q.astype(BF16)]
        ks = [(k * ex[li * c:(li + 1) * c]).astype(BF16) for li in range(nl)] + [k.astype(BF16)]
        return dict(rows=rows, qs=qs, ks=ks, qb=qb, kdec=kdec, dtot=dtot, v16=v.astype(BF16))

    even = lax.broadcasted_iota(jnp.int32, (1, LANE), 1) < GLA_DK
    r2 = lax.broadcasted_iota(jnp.int32, (2 * GLA_DV, LANE), 0)
    c2 = lax.broadcasted_iota(jnp.int32, (2 * GLA_DV, LANE), 1)
    same_head = (r2 < GLA_DV) == (c2 < GLA_DK)
    zv = jnp.zeros((c, GLA_DV), BF16)
    npair = GLA_HEADS // 2
    kls = [slice(p * LANE, (p + 1) * LANE) for p in range(npair)]
    vls = [slice(p * 2 * GLA_DV, (p + 1) * 2 * GLA_DV) for p in range(npair)]
    probs = [(d, p) for d in range(2) for p in range(npair)]
    first = lambda x: jnp.where(even, x, jnp.zeros_like(x))
    second = lambda x: jnp.where(even, jnp.zeros_like(x), x)

    def main_step(i, carry):
        ops = [chunk_operands(0, i), chunk_operands(1, nc - 1 - i)]
        st = [st_ref[d, p] for d, p in probs]
        lvl = [[_dot_nt(ops[d]["qs"][li][:, kls[p]],
                        jnp.concatenate([first(ops[d]["ks"][li][:, kls[p]]), second(ops[d]["ks"][li][:, kls[p]])], axis=0))
                for d, p in probs] for li in range(nl + 1)]
        att = [sum(bmask_ref[d, li] * lvl[li][j] for li in range(nl + 1)).astype(BF16)
               for j, (d, p) in enumerate(probs)]
        inter = [_dot_nt(ops[d]["qb"][:, kls[p]], st[j].astype(BF16)) for j, (d, p) in enumerate(probs)]
        for j, (d, p) in enumerate(probs):
            v_p = ops[d]["v16"][:, vls[p]]
            v_bd = jnp.concatenate([jnp.concatenate([v_p[:, 0:GLA_DV], zv], axis=1),
                                    jnp.concatenate([zv, v_p[:, GLA_DV:2 * GLA_DV]], axis=1)], axis=0)
            acc_ref[d, ops[d]["rows"], vls[p]] = inter[j] + _dot(att[j], v_bd)
            upd = _dot_tn(v_p, ops[d]["kdec"][:, kls[p]])
            st_ref[d, p] = st[j] * ops[d]["dtot"][:, kls[p]] + jnp.where(same_head, upd, 0.0)
        return carry

    blk = min(RW_BLOCK_ROWS, seq)

    def post_step(i, carry):
        rows = pl.ds(pl.multiple_of(i * blk, blk), blk)
        for h in range(GLA_HEADS):
            sv = slice(h * GLA_DV, (h + 1) * GLA_DV)
            o_ref[rows, sv] = (_rms(acc_ref[0, rows, sv] + acc_ref[1, rows, sv]) * ng_ref[...]).astype(o_ref.dtype)
        return carry

    st_ref[...] = jnp.zeros_like(st_ref)
    if has_s0:
        for d in range(2):
            for h in range(GLA_HEADS):
                ro, co = (h % 2) * GLA_DV, (h % 2) * GLA_DK
                st_ref[d, h // 2, ro:ro + GLA_DV, co:co + GLA_DK] = s0_ref[d, h].T
    lax.fori_loop(0, nc, main_step, 0)
    for d in range(2):
        for h in range(GLA_HEADS):
            ro, co = (h % 2) * GLA_DV, (h % 2) * GLA_DK
            sout_ref[d, h] = st_ref[d, h // 2, ro:ro + GLA_DV, co:co + GLA_DK].T
    lax.fori_loop(0, seq // blk, post_step, 0)


def _gla_call(proj, s0, layer, batch, seq, wts):
    has_s0 = s0 is not None
    blk = lambda w, cb: pl.BlockSpec((seq, w), lambda b: (b, cb))
    in_specs = [blk(GLA_KW, COL_GL_Q // GLA_KW), blk(GLA_KW, COL_GL_K // GLA_KW),
                blk(GLA_VW, COL_GL_V // GLA_VW), blk(LANE, COL_GL_AD // LANE)]
    args = [proj, proj, proj, proj]
    if has_s0:
        in_specs.append(pl.BlockSpec((None, None, 2, GLA_HEADS, GLA_DK, GLA_DV),
                                     lambda b: (b, layer, 0, 0, 0, 0)))
        args.append(s0)
    for w in wts:
        in_specs.append(pl.BlockSpec(w.shape, lambda b, n=w.ndim: (0,) * n))
        args.append(w)
    return pl.pallas_call(
        functools.partial(_gla_kernel, seq=seq, has_s0=has_s0),
        grid=(batch,),
        in_specs=in_specs,
        out_specs=[pl.BlockSpec((seq, GLA_VW), lambda b: (b, 0)),
                   pl.BlockSpec((None, 2, GLA_HEADS, GLA_DK, GLA_DV), lambda b: (b, 0, 0, 0, 0))],
        out_shape=[jax.ShapeDtypeStruct((batch * seq, GLA_VW), BF16),
                   jax.ShapeDtypeStruct((batch, 2, GLA_HEADS, GLA_DK, GLA_DV), F32)],
        scratch_shapes=[pltpu.VMEM((2, seq, GLA_VW), F32),
                        pltpu.VMEM((2, GLA_HEADS // 2, 2 * GLA_DV, LANE), F32)],
        compiler_params=pltpu.CompilerParams(vmem_limit_bytes=VMEM_LIMIT),
        name="gla",
    )(*args)


def _mla_kernel(*refs, seq, past, tq):
    has_ctx = past > 0
    if has_ctx:
        (qd_ref, kvd_ref, kpe_ref, kpesw_ref, cckv_ref, ckpe_ref,
         qng_ref, wq_ref, wqsw_ref, kvng_ref, wk_ref, wv_ref,
         cosq_ref, sinq_ref, cosk_ref, sink_ref, epos_ref,
         o_ref, ckv_ref, kcat_ref, vv_ref) = refs
    else:
        (qd_ref, kvd_ref, kpe_ref,
         qng_ref, wq_ref, kvng_ref, wk_ref, wv_ref,
         o_ref, ckv_ref, kcat_ref, vv_ref) = refs

    @pl.when(pl.program_id(1) == 0)
    def _():
        ckv = _rms(kvd_ref[...]) * kvng_ref[...]
        ckv_ref[...] = ckv
        if has_ctx:
            kpos = kpe_ref[...] * cosk_ref[...] + kpesw_ref[...] * sink_ref[...]
        else:
            kpos = kpe_ref[...]
        segs = [(0, seq, ckv, kpos)]
        if has_ctx:
            cpos = _dot(ckpe_ref[...].astype(BF16), epos_ref[...])
            segs.append((seq, past, cckv_ref[...], cpos))
        for start, n, lat, pos in segs:
            lat16 = lat.astype(BF16)
            kc = _dot(lat16, wk_ref[...])
            for h in range(MLA_HEADS):
                kcat_ref[h, start:start + n, :] = (kc[:, h * LANE:(h + 1) * LANE] + pos).astype(BF16)
            vv_ref[start:start + n, :] = _dot(lat16, wv_ref[...]).astype(BF16)

    qlat = (_rms(qd_ref[...]) * qng_ref[...]).astype(BF16)
    qc = _dot(qlat, wq_ref[...])
    if has_ctx:
        qsw = _dot(qlat, wqsw_ref[...])
    for h in range(MLA_HEADS):
        hl = slice(h * LANE, (h + 1) * LANE)
        q_raw = qc[:, hl]
        if has_ctx:
            q_self = (q_raw * cosq_ref[...] + qsw[:, hl] * sinq_ref[...]).astype(BF16)
        else:
            q_self = q_raw.astype(BF16)
        s1 = _dot_nt(q_self, kcat_ref[h, 0:seq, :])
        m = jnp.max(s1, axis=-1, keepdims=True)
        if has_ctx:
            s2 = _dot_nt(q_raw.astype(BF16), kcat_ref[h, seq:seq + past, :])
            m = jnp.maximum(m, jnp.max(s2, axis=-1, keepdims=True))
        p1 = jnp.exp(s1 - m)
        l = jnp.sum(p1, axis=-1, keepdims=True)
        vl = slice(h * MLA_V, (h + 1) * MLA_V)
        o_h = _dot(p1.astype(BF16), vv_ref[0:seq, vl])
        if has_ctx:
            p2 = jnp.exp(s2 - m)
            l = l + jnp.sum(p2, axis=-1, keepdims=True)
            o_h = o_h + _dot(p2.astype(BF16), vv_ref[seq:seq + past, vl])
        o_ref[:, vl] = (o_h / l).astype(o_ref.dtype)


def _mla_call(proj, cache_ckv, cache_kpe, layer, batch, seq, wts, tables):
    has_ctx = cache_ckv is not None
    past = cache_ckv.shape[2] if has_ctx else 0
    tq = 256
    nq = seq // tq
    full = lambda cb: pl.BlockSpec((seq, LANE), lambda b, i: (b, cb))
    in_specs = [pl.BlockSpec((tq, MLA_Q_RANK), lambda b, i: (b * nq + i, COL_ML_QD // MLA_Q_RANK)),
                full(COL_ML_KVD // LANE), full(COL_KPE // LANE)]
    args = [proj, proj, proj]
    if has_ctx:
        in_specs += [full(COL_KPE_SW // LANE),
                     pl.BlockSpec((None, None, past, MLA_KV_RANK), lambda b, i: (b, layer, 0, 0)),
                     pl.BlockSpec((None, None, past, MLA_ROPE), lambda b, i: (b, layer, 0, 0))]
        args += [proj, cache_ckv, cache_kpe]
    qn_g, wq_cat, wq_sw, kvn_g, wk_pad, wv = wts
    const = lambda w: pl.BlockSpec(w.shape, lambda b, i, n=w.ndim: (0,) * n)
    if has_ctx:
        cosq, sinq, cosk, sink, epos = tables
        wlist = [qn_g, wq_cat, wq_sw, kvn_g, wk_pad, wv]
        in_specs += [const(w) for w in wlist]
        in_specs += [pl.BlockSpec((tq, LANE), lambda b, i: (i, 0)), pl.BlockSpec((tq, LANE), lambda b, i: (i, 0)),
                     const(cosk), const(sink), const(epos)]
        args += wlist + [cosq, sinq, cosk, sink, epos]
    else:
        wlist = [qn_g, wq_cat, kvn_g, wk_pad, wv]
        in_specs += [const(w) for w in wlist]
        args += wlist
    return pl.pallas_call(
        functools.partial(_mla_kernel, seq=seq, past=past, tq=tq),
        grid=(batch, nq),
        in_specs=in_specs,
        out_specs=[pl.BlockSpec((tq, MLA_VW), lambda b, i: (b * nq + i, 0)),
                   pl.BlockSpec((seq, MLA_KV_RANK), lambda b, i: (b, 0))],
        out_shape=[jax.ShapeDtypeStruct((batch * seq, MLA_VW), BF16),
                   jax.ShapeDtypeStruct((batch * seq, MLA_KV_RANK), F32)],
        scratch_shapes=[pltpu.VMEM((MLA_HEADS, seq + past, LANE), BF16),
                        pltpu.VMEM((seq + past, MLA_VW), BF16)],
        compiler_params=pltpu.CompilerParams(vmem_limit_bytes=VMEM_LIMIT,
                                             dimension_semantics=("arbitrary", "arbitrary")),
        name="mla",
    )(*args)


def _outproj_kernel(x_ref, oa_ref, ob_ref, oc_ref, ga_ref, gb_ref, gc_ref, ma_ref, mb_ref, mc_ref,
                    mod_ref, wa_ref, wb_ref, wc_ref, wo_ref, fg_ref, o_ref, *, final):
    def branch(o_r, g_r, m_r, w_r):
        h = 0.5 * g_r[...]
        act = o_r[...] * (h * (1.0 + jnp.tanh(h)))
        y = _dot(act, w_r[...]).astype(BF16)
        return (0.5 * (1.0 + jnp.tanh(0.5 * m_r[...]))) * y

    y = (branch(oa_ref, ga_ref, ma_ref, wa_ref) + branch(ob_ref, gb_ref, mb_ref, wb_ref)
         + branch(oc_ref, gc_ref, mc_ref, wc_ref))
    y = _dot(y, wo_ref[...])
    gate = mod_ref[...][:, 2 * D_MODEL:3 * D_MODEL]
    hn = x_ref[...] + gate * y
    if final:
        hn = _rms(hn) * fg_ref[...]
    o_ref[...] = hn


def _outproj_call(x2d, o_a, o_b, o_c, proj, mod3, wts, final_g, rows_per_mod, mod_base, final):
    rows = x2d.shape[0]
    tm = OUTPROJ_TM
    tiles_per_mod = rows_per_mod // tm
    row = lambda w, cb=0: pl.BlockSpec((tm, w), lambda i: (i, cb))
    const = lambda w: pl.BlockSpec(w.shape, lambda i, n=w.ndim: (0,) * n)
    in_specs = [row(D_MODEL), row(512), row(512), row(512),
                row(512, COL_GATES // 512), row(512, COL_GATES // 512 + 1), row(512, COL_GATES // 512 + 2),
                row(D_MODEL, COL_MERGE // D_MODEL), row(D_MODEL, COL_MERGE // D_MODEL + 1),
                row(D_MODEL, COL_MERGE // D_MODEL + 2),
                pl.BlockSpec((None, 1, 3 * D_MODEL), lambda i: (mod_base + i // tiles_per_mod, 0, 0))]
    in_specs += [const(w) for w in wts] + [const(final_g)]
    return pl.pallas_call(
        functools.partial(_outproj_kernel, final=final),
        grid=(rows // tm,),
        in_specs=in_specs,
        out_specs=row(D_MODEL),
        out_shape=jax.ShapeDtypeStruct((rows, D_MODEL), F32),
        compiler_params=pltpu.CompilerParams(vmem_limit_bytes=VMEM_LIMIT),
        name="outproj",
    )(x2d, o_a, o_b, o_c, proj, proj, proj, proj, proj, proj, mod3, *wts, final_g)


def _pack_w_in(w):
    seg = lambda i: w[:, _IN_OFF[i]:_IN_OFF[i + 1]]
    z = lambda n: jnp.zeros((D_MODEL, n), w.dtype)
    kpe = seg(13)
    q = MLA_ROPE // 4
    kpe_sw = jnp.concatenate([kpe[:, q:2 * q], kpe[:, 0:q], kpe[:, 3 * q:4 * q], kpe[:, 2 * q:3 * q]], axis=1)
    tail = LANE - KPE_LANE - MLA_ROPE
    cols = [w[:, _IN_OFF[0]:_IN_OFF[5]],
            seg(9), z(LANE - GLA_GATE_RANK),
            seg(11),
            seg(6), seg(7), seg(8),
            seg(12),
            z(KPE_LANE), kpe, z(tail),
            z(KPE_LANE), kpe_sw, z(tail),
            z(PACK_A - COL_KPE_SW - LANE)]
    part_b = [seg(15),
              seg(5), seg(10), seg(14)]
    return jnp.concatenate(cols, axis=1).astype(BF16), jnp.concatenate(part_b, axis=1).astype(BF16)


def _rope_lane_tables(seq):
    n_freq = MLA_ROPE // 4
    t = np.arange(seq)
    inv = ROPE_THETA ** (-np.arange(n_freq, dtype=np.float64) / n_freq)
    ang = np.stack([(t // GRID_W)[:, None] * inv, (t % GRID_W)[:, None] * inv], axis=1)
    cos = np.repeat(np.cos(ang)[:, :, None, :], 2, axis=2).reshape(seq, MLA_ROPE)
    sin = np.stack([-np.sin(ang), np.sin(ang)], axis=2).reshape(seq, MLA_ROPE)
    return cos, sin


def _mla_tables(seq):
    cos, sin = _rope_lane_tables(seq)
    cosq = np.zeros((seq, LANE), np.float32)
    sinq = np.zeros((seq, LANE), np.float32)
    cosq[:, :KPE_LANE] = 1.0
    cosq[:, KPE_LANE:KPE_LANE + MLA_ROPE] = cos
    sinq[:, KPE_LANE:KPE_LANE + MLA_ROPE] = sin
    cosk = np.zeros((seq, LANE), np.float32)
    cosk[:, KPE_LANE:KPE_LANE + MLA_ROPE] = cos
    epos = np.zeros((MLA_ROPE, LANE), np.float32)
    epos[np.arange(MLA_ROPE), KPE_LANE + np.arange(MLA_ROPE)] = 1.0
    return (jnp.asarray(cosq), jnp.asarray(sinq), jnp.asarray(cosk), jnp.asarray(sinq),
            jnp.asarray(epos, dtype=BF16))


def _pack_mla_weights(qn_g, wq_up, kvn_g, wkv_up):
    scale = (MLA_NOPE + MLA_ROPE) ** -0.5
    wq = wq_up.reshape(MLA_Q_RANK, MLA_HEADS, MLA_NOPE + MLA_ROPE) * scale
    nope, rope = wq[..., :MLA_NOPE], wq[..., MLA_NOPE:]
    q = MLA_ROPE // 4
    rope_sw = jnp.concatenate([rope[..., q:2 * q], rope[..., 0:q], rope[..., 3 * q:4 * q], rope[..., 2 * q:3 * q]], -1)
    tail = jnp.zeros((MLA_Q_RANK, MLA_HEADS, LANE - KPE_LANE - MLA_ROPE), wq.dtype)
    wq_cat = jnp.concatenate([nope, rope, tail], -1).reshape(MLA_Q_RANK, MLA_HEADS * LANE).astype(BF16)
    wq_sw = jnp.concatenate([jnp.zeros_like(nope), rope_sw, tail], -1).reshape(MLA_Q_RANK, MLA_HEADS * LANE).astype(BF16)
    wkv = wkv_up.reshape(MLA_KV_RANK, MLA_HEADS, MLA_NOPE + MLA_V)
    wk = jnp.concatenate([wkv[..., :MLA_NOPE], jnp.zeros((MLA_KV_RANK, MLA_HEADS, LANE - MLA_NOPE), wkv.dtype)], -1)
    wk_pad = wk.reshape(MLA_KV_RANK, MLA_HEADS * LANE).astype(BF16)
    wv = wkv[..., MLA_NOPE:].reshape(MLA_KV_RANK, MLA_VW).astype(BF16)
    return (qn_g.reshape(1, -1), wq_cat, wq_sw, kvn_g.reshape(1, -1), wk_pad, wv)


def _head_sum_matrix():
    lane = np.arange(RW_WIDTH)
    return jnp.asarray((lane[:, None] // RW_HEAD == lane[None, :] // RW_HEAD).astype(np.float32), dtype=BF16)


def _trunk(x_prompt, x_sample, c, cache_ckv, cache_kpe, state_rwkv, state_gla, c_ctx,
           norm_g, w_mod, b_mod, w_in, rw_w0, rw_w2, rw_a0, rw_a2, rw_k_k, rw_k_a, rw_r_k,
           rw_ln_g, rw_ln_b, rw_out, gla_a2, gla_ab, gla_norm_g, gla_out,
           mla_qn_g, mla_wq_up, mla_kvn_g, mla_wkv_up, mla_out, w_out, final_g):
    bp, tp, _ = x_prompt.shape
    bs, ts, _ = x_sample.shape
    hp = x_prompt.reshape(bp * tp, D_MODEL)
    hs = x_sample.reshape(bs * ts, D_MODEL)
    cvec8 = jnp.concatenate([c, c_ctx[None, :], jnp.zeros((8 - bs - 1, D_MODEL), F32)], axis=0)
    ctx_row = bs
    hsum = _head_sum_matrix()
    mexp, bmask = _gla_constants()
    mexp = jnp.asarray(mexp, dtype=BF16)
    bmask = jnp.asarray(bmask)
    tables = _mla_tables(ts)
    fg = final_g.reshape(1, D_MODEL)
    ckv_l, kpe_l, rw_l, gla_l = [], [], [], []
    for l in range(DEPTH):
        mod3 = _mod_call(cvec8, w_mod, b_mod, l).reshape(8, 1, 3 * D_MODEL)
        w_a, w_b = _pack_w_in(w_in[l])
        ng = norm_g[l].reshape(1, D_MODEL)
        row = lambda a: a.reshape(1, -1)
        zr = jnp.zeros((2, RW_RANK, RW_WIDTH), F32)
        wcat = jnp.concatenate([jnp.concatenate([rw_w2[l], zr], axis=2),
                                jnp.concatenate([zr, rw_a2[l]], axis=2)], axis=1).astype(BF16)
        rw_wts = (wcat, jnp.concatenate([rw_w0[l], rw_a0[l]], axis=1), row(rw_k_k[l]), row(rw_k_a[l]),
                  row(rw_r_k[l]), row(rw_ln_g[l]), row(rw_ln_b[l]), hsum)
        a2p = jnp.concatenate([gla_a2[l], jnp.zeros((2, LANE - GLA_GATE_RANK, GLA_KW), F32)], axis=1).astype(BF16)
        gla_wts = (a2p, gla_ab[l], row(gla_norm_g[l]), mexp, bmask)
        mla_wts = _pack_mla_weights(mla_qn_g[l], mla_wq_up[l], mla_kvn_g[l], mla_wkv_up[l])
        out_wts = (rw_out[l].astype(BF16), gla_out[l].astype(BF16), mla_out[l].astype(BF16), w_out[l].astype(BF16))
        final = l == DEPTH - 1

        proj, proj_b = _inproj_call(hp, mod3, ng, w_a, w_b, bp * tp, ctx_row)
        o_a, s_rw = _rwkv_call(proj, None, l, bp, tp, rw_wts)
        o_b, s_gla = _gla_call(proj, None, l, bp, tp, gla_wts)
        o_c, ckv = _mla_call(proj, None, None, l, bp, tp, mla_wts, None)
        ckv_l.append(ckv.reshape(bp, tp, MLA_KV_RANK))
        kpe_l.append(proj[:, COL_KPE + KPE_LANE:COL_KPE + KPE_LANE + MLA_ROPE].reshape(bp, tp, MLA_ROPE))
        rw_l.append(s_rw)
        gla_l.append(s_gla)
        hp = _outproj_call(hp, o_a, o_b, o_c, proj_b, mod3, out_wts, fg, bp * tp, ctx_row, final)

        proj, proj_b = _inproj_call(hs, mod3, ng, w_a, w_b, ts, 0)
        o_a, _ = _rwkv_call(proj, state_rwkv, l, bs, ts, rw_wts)
        o_b, _ = _gla_call(proj, state_gla, l, bs, ts, gla_wts)
        o_c, _ = _mla_call(proj, cache_ckv, cache_kpe, l, bs, ts, mla_wts, tables)
        hs = _outproj_call(hs, o_a, o_b, o_c, proj_b, mod3, out_wts, fg, ts, 0, final)

    return (hp.reshape(bp, tp, D_MODEL), hs.reshape(bs, ts, D_MODEL),
            jnp.stack(ckv_l, axis=1), jnp.stack(kpe_l, axis=1),
            jnp.stack(rw_l, axis=1), jnp.stack(gla_l, axis=1))


_trunk_jit = jax.jit(_trunk)


def kernel(x_prompt, x_sample, c, cache_ckv, cache_kpe, state_rwkv, state_gla, c_ctx, norm_g, w_mod, b_mod, w_in, rw_w0, rw_w2, rw_a0, rw_a2, rw_k_k, rw_k_a, rw_r_k, rw_ln_g, rw_ln_b, rw_out, gla_a2, gla_ab, gla_norm_g, gla_out, mla_qn_g, mla_wq_up, mla_kvn_g, mla_wkv_up, mla_out, w_out, final_g):
    return _trunk_jit(x_prompt, x_sample, c, cache_ckv, cache_kpe, state_rwkv, state_gla, c_ctx, norm_g, w_mod, b_mod, w_in, rw_w0, rw_w2, rw_a0, rw_a2, rw_k_k, rw_k_a, rw_r_k, rw_ln_g, rw_ln_b, rw_out, gla_a2, gla_ab, gla_norm_g, gla_out, mla_qn_g, mla_wq_up, mla_kvn_g, mla_wkv_up, mla_out, w_out, final_g)
```

```python
import functools

import numpy as np
import jax
import jax.numpy as jnp
from jax import lax
from jax.experimental import pallas as pl
from jax.experimental.pallas import tpu as pltpu

F32 = jnp.float32
BF16 = jnp.bfloat16

D_MODEL = 1024
DEPTH = 2
GRID_W = 64
NORM_EPS = 1e-6
RW_HEADS = 8
RW_HEAD = 64
RW_WIDTH = RW_HEADS * RW_HEAD
RW_RANK = 64
RW_GN_EPS = 64e-5
GLA_HEADS = 4
GLA_DK = 64
GLA_DV = 128
GLA_KW = GLA_HEADS * GLA_DK
GLA_VW = GLA_HEADS * GLA_DV
GLA_GATE_RANK = 16
GLA_LOGIT_NORM = 16.0
MLA_HEADS = 8
MLA_NOPE = 64
MLA_ROPE = 32
MLA_V = 64
MLA_Q_RANK = 256
MLA_KV_RANK = 128
MLA_VW = MLA_HEADS * MLA_V
ROPE_THETA = 10000.0
N_BRANCH = 3

_IN_SIZES = (RW_WIDTH, RW_WIDTH, RW_WIDTH, RW_RANK, RW_RANK, RW_WIDTH,
             GLA_KW, GLA_KW, GLA_VW, GLA_GATE_RANK, GLA_VW,
             MLA_Q_RANK, MLA_KV_RANK, MLA_ROPE, MLA_VW, N_BRANCH * D_MODEL)
_IN_OFF = tuple(int(v) for v in np.concatenate([[0], np.cumsum(_IN_SIZES)]))

LANE = 128
COL_RW_R = 0
COL_RW_K = 512
COL_RW_V = 1024
COL_RW_WA = 1536
COL_GL_AD = 1664
COL_ML_QD = 1792
COL_GL_Q = 2048
COL_GL_K = 2304
COL_GL_V = 2560
COL_ML_KVD = 3072
COL_KPE = 3200
COL_KPE_SW = 3328
PACK_A = 3584
COL_MERGE = 0
COL_GATES = 3072
PACK_B = 4608
KPE_LANE = MLA_NOPE

CHUNK = 64
INPROJ_TM = 512
INPROJ_SUB = 256
INPROJ_TN = 512
OUTPROJ_TM = 512
VMEM_LIMIT = 48 * 1024 * 1024


def _dot(a, b, prec=None):
    return jnp.dot(a, b, preferred_element_type=F32, precision=prec)


def _dot_nt(a, b, prec=None):
    return lax.dot_general(a, b, (((1,), (1,)), ((), ())), preferred_element_type=F32, precision=prec)


def _dot_tn(a, b, prec=None):
    return lax.dot_general(a, b, (((0,), (0,)), ((), ())), preferred_element_type=F32, precision=prec)


def _split(x):
    hi = x.astype(BF16)
    return hi, (x - hi.astype(F32)).astype(BF16)


def _dot_split(a, b):
    ah, al = _split(a)
    bh, bl = _split(b)
    return _dot(ah, bh) + _dot(al, bh) + _dot(ah, bl)


def _dot_sel(sel16, x):
    xh, xl = _split(x)
    return _dot(sel16, xh) + _dot(sel16, xl)


def _softplus(z):
    return jnp.maximum(z, 0.0) + jnp.log1p(jnp.exp(-jnp.abs(z)))


def _sigmoid(x):
    return 0.5 * jnp.tanh(0.5 * x) + 0.5


def _rms(x, eps=NORM_EPS):
    return x * lax.rsqrt(jnp.mean(x * x, axis=-1, keepdims=True) + eps)


def _mod_kernel(c_ref, w_ref, b_ref, o_ref):
    c = c_ref[...]
    o_ref[...] = _dot_split(c * _sigmoid(c), w_ref[...]) + b_ref[...]


def _mod_call(cvec8, w_mod, b_mod, layer):
    tn = 1024
    return pl.pallas_call(
        _mod_kernel,
        grid=(3 * D_MODEL // tn,),
        in_specs=[pl.BlockSpec((8, D_MODEL), lambda j: (0, 0)),
                  pl.BlockSpec((None, D_MODEL, tn), lambda j: (layer, 0, j)),
                  pl.BlockSpec((None, 1, tn), lambda j: (layer, 0, j))],
        out_specs=pl.BlockSpec((8, tn), lambda j: (0, j)),
        out_shape=jax.ShapeDtypeStruct((8, 3 * D_MODEL), F32),
        compiler_params=pltpu.CompilerParams(vmem_limit_bytes=VMEM_LIMIT),
        name="mod",
    )(cvec8, w_mod, b_mod.reshape(DEPTH, 1, 3 * D_MODEL))


def _inproj_kernel(x_ref, mod_ref, g_ref, wa_ref, wb_ref, oa_ref, ob_ref):
    m = mod_ref[...]
    shift = m[:, 0:D_MODEL]
    scale1 = 1.0 + m[:, D_MODEL:2 * D_MODEL]
    tm = x_ref.shape[0]
    for r0 in range(0, tm, INPROJ_SUB):
        rows = slice(r0, r0 + INPROJ_SUB)
        h = (_rms(x_ref[rows, :]) * g_ref[...] * scale1 + shift).astype(BF16)
        for w_ref, o_ref in ((wa_ref, oa_ref), (wb_ref, ob_ref)):
            for c0 in range(0, w_ref.shape[1], INPROJ_TN):
                cols = slice(c0, c0 + INPROJ_TN)
                o_ref[rows, cols] = _dot(h, w_ref[:, cols]).astype(BF16)


def _inproj_call(x2d, mod3, norm_g, w_a, w_b, rows_per_mod, mod_base):
    rows = x2d.shape[0]
    tm = INPROJ_TM
    tiles_per_mod = rows_per_mod // tm
    resident = lambda w: pl.BlockSpec(w.shape, lambda i: (0, 0), pipeline_mode=pl.Buffered(1))
    return pl.pallas_call(
        _inproj_kernel,
        grid=(rows // tm,),
        in_specs=[pl.BlockSpec((tm, D_MODEL), lambda i: (i, 0)),
                  pl.BlockSpec((None, 1, 3 * D_MODEL), lambda i: (mod_base + i // tiles_per_mod, 0, 0)),
                  pl.BlockSpec((1, D_MODEL), lambda i: (0, 0)),
                  resident(w_a), resident(w_b)],
        out_specs=[pl.BlockSpec((tm, PACK_A), lambda i: (i, 0)),
                   pl.BlockSpec((tm, PACK_B), lambda i: (i, 0))],
        out_shape=[jax.ShapeDtypeStruct((rows, PACK_A), BF16),
                   jax.ShapeDtypeStruct((rows, PACK_B), BF16)],
        compiler_params=pltpu.CompilerParams(vmem_limit_bytes=VMEM_LIMIT),
        name="inproj",
    )(x2d, mod3, norm_g, w_a, w_b)


RW_CHUNK = 64
RW_PAIRS = RW_HEADS // 2
RW_BLOCK_ROWS = 256


def _rwkv_time_mask(c, reverse):
    row = lax.broadcasted_iota(jnp.int32, (2 * c, 4 * c), 0)
    col = lax.broadcasted_iota(jnp.int32, (2 * c, 4 * c), 1)
    t = jnp.where(row >= c, row - c, row)
    s = col & (c - 1)
    earlier = (s > t) if reverse else (s < t)
    return earlier | ((row >= c) & (s == t))


def _rwkv_kernel(*refs, seq, has_s0):
    if has_s0:
        r_ref, k_ref, v_ref, wa_ref, s0_ref = refs[:5]
        rest = refs[5:]
    else:
        r_ref, k_ref, v_ref, wa_ref = refs[:4]
        s0_ref = None
        rest = refs[4:]
    (wcat_ref, wa0_ref, kk_ref, ka_ref, rk_ref, lng_ref, lnb_ref, hsum_ref,
     o_ref, sout_ref, acc_ref, kn_ref, st_ref) = rest
    c = RW_CHUNK
    nc = seq // c
    blk = min(RW_BLOCK_ROWS, seq)
    hsum = hsum_ref[...]

    def pre_step(i, carry):
        rows = pl.ds(pl.multiple_of(i * blk, blk), blk)
        kk0 = k_ref[rows, :].astype(F32) * kk_ref[...]
        kn_ref[rows, :] = kk0 / jnp.maximum(jnp.sqrt(_dot((kk0 * kk0).astype(BF16), hsum)), 1e-12)
        return carry

    def post_step(i, carry):
        rows = pl.ds(pl.multiple_of(i * blk, blk), blk)
        o = acc_ref[0, rows, :] + acc_ref[1, rows, :]
        rk = r_ref[rows, :].astype(F32) * k_ref[rows, :].astype(F32) * rk_ref[...]
        red = _dot(jnp.concatenate([o, rk], axis=0).astype(BF16), hsum)
        dev = o - red[0:blk] * (1.0 / RW_HEAD)
        var = _dot((dev * dev).astype(BF16), hsum) * (1.0 / RW_HEAD)
        o = dev * lax.rsqrt(var + RW_GN_EPS) * lng_ref[...] + lnb_ref[...]
        o_ref[rows, :] = (o + red[blk:2 * blk] * v_ref[rows, :]).astype(o_ref.dtype)
        return carry

    even = lax.broadcasted_iota(jnp.int32, (1, LANE), 1) < RW_HEAD
    row = lax.broadcasted_iota(jnp.int32, (c, c), 0)
    col = lax.broadcasted_iota(jnp.int32, (c, c), 1)
    incl16 = [jnp.where(col <= row, 1.0, 0.0).astype(BF16), jnp.where(col >= row, 1.0, 0.0).astype(BF16)]
    tmask = [_rwkv_time_mask(c, False), _rwkv_time_mask(c, True)]
    r2 = lax.broadcasted_iota(jnp.int32, (LANE, LANE), 0)
    c2 = lax.broadcasted_iota(jnp.int32, (LANE, LANE), 1)
    same_head = (r2 < RW_HEAD) == (c2 < RW_HEAD)
    z16 = jnp.zeros((c, LANE), BF16)
    lanes = [slice(p * LANE, (p + 1) * LANE) for p in range(RW_PAIRS)]
    probs = [(d, p) for d in range(2) for p in range(RW_PAIRS)]

    def chunk_operands(d, ci):
        rows = pl.ds(pl.multiple_of(ci * c, c), c)
        r = r_ref[rows, :].astype(F32)
        k = k_ref[rows, :].astype(F32)
        wa = wa_ref[rows, :].astype(F32)
        kk = kn_ref[rows, :]
        lora = jnp.where(even, jnp.tanh(wa), wa)
        pre = _dot(lora.astype(BF16), wcat_ref[d]) + wa0_ref[d:d + 1, :]
        w_log = -_softplus(-pre[:, 0:RW_WIDTH]) - 0.5
        lw = -jnp.exp(w_log)
        a = _sigmoid(pre[:, RW_WIDTH:2 * RW_WIDTH])
        kd = k * (1.0 + (a - 1.0) * ka_ref[...])
        bv = kk * a
        gam = _dot_sel(incl16[d], lw)
        last = 0 if d == 1 else c - 1
        gtot = gam[last:last + 1, :]
        gref = gam[c // 2:c // 2 + 1, :]
        e_in = jnp.exp(gam - gref)
        e_out = jnp.exp(gref - gam)
        e_end = jnp.exp(gtot - gam)
        lhs = jnp.concatenate([-kk * (e_in * jnp.exp(-lw)), r * e_in], axis=0).astype(BF16)
        return dict(rows=rows, lhs=lhs, b16=(bv * e_out).astype(BF16), k16=(kd * e_out).astype(BF16),
                    be16=(bv * e_end).astype(BF16), ke16=(kd * e_end).astype(BF16),
                    v16=v_ref[rows, :].astype(BF16), dtot=jnp.exp(gtot), eref=jnp.exp(gref))

    def main_step(i, carry):
        ops = [chunk_operands(0, i), chunk_operands(1, nc - 1 - i)]
        n = range(len(probs))
        pick = lambda name: [ops[d][name][:, lanes[p]] for d, p in probs]
        lhs, b, k, be, ke, v = (pick(s) for s in ("lhs", "b16", "k16", "be16", "ke16", "v16"))
        st = [st_ref[d, p] for d, p in probs]
        first = lambda x: jnp.where(even, x, jnp.zeros_like(x))
        second = lambda x: jnp.where(even, jnp.zeros_like(x), x)
        v_e = [first(x) for x in v]
        v_o = [second(x) for x in v]
        aa = [jnp.where(tmask[d], _dot_nt(lhs[j], jnp.concatenate(
            [first(k[j]), first(b[j]), second(b[j]), second(k[j])], axis=0)), 0.0) for j, (d, p) in enumerate(probs)]
        sp = [_dot_nt(lhs[j], (st[j] * ops[d]["eref"][:, lanes[p]]).astype(BF16))
              for j, (d, p) in enumerate(probs)]
        aa16 = [x.astype(BF16) for x in aa]
        rhs_u = [_dot(aa16[j][0:c], jnp.concatenate([v_e[j], z16, z16, v_o[j]], axis=0)) + sp[j][0:c] for j in n]
        x_e = [jnp.where(even, rhs_u[j], aa[j][0:c, 0:LANE]) for j in n]
        x_o = [jnp.where(even, aa[j][0:c, LANE:2 * LANE], rhs_u[j]) for j in n]
        covered = 1
        while covered < c:
            e16 = [x.astype(BF16) for x in x_e]
            o16 = [x.astype(BF16) for x in x_o]
            x_e = [_dot(e16[j], jnp.concatenate([z16, e16[j]], axis=0)) + first(x_e[j]) for j in n]
            x_o = [_dot(o16[j], jnp.concatenate([o16[j], z16], axis=0)) + second(x_o[j]) for j in n]
            covered *= 2
        u16 = [jnp.where(even, x_e[j], x_o[j]).astype(BF16) for j in n]
        for j, (d, p) in enumerate(probs):
            uv_rows = jnp.concatenate([v_e[j], first(u16[j]), second(u16[j]), v_o[j]], axis=0)
            acc_ref[d, ops[d]["rows"], lanes[p]] = sp[j][c:2 * c] + _dot(aa16[j][c:2 * c], uv_rows)
            upd = _dot_tn(jnp.concatenate([u16[j], v[j]], axis=0), jnp.concatenate([be[j], ke[j]], axis=0))
            st_ref[d, p] = st[j] * ops[d]["dtot"][:, lanes[p]] + jnp.where(same_head, upd, 0.0)
        return carry

    st_ref[...] = jnp.zeros_like(st_ref)
    if has_s0:
        for d in range(2):
            for h in range(RW_HEADS):
                off = (h % 2) * RW_HEAD
                st_ref[d, h // 2, off:off + RW_HEAD, off:off + RW_HEAD] = s0_ref[d, h]
    lax.fori_loop(0, seq // blk, pre_step, 0)
    lax.fori_loop(0, nc, main_step, 0)
    for d in range(2):
        for h in range(RW_HEADS):
            off = (h % 2) * RW_HEAD
            sout_ref[d, h] = st_ref[d, h // 2, off:off + RW_HEAD, off:off + RW_HEAD]
    lax.fori_loop(0, seq // blk, post_step, 0)


def _rwkv_call(proj, s0, layer, batch, seq, wts):
    has_s0 = s0 is not None
    blk = lambda w, cb: pl.BlockSpec((seq, w), lambda b: (b, cb))
    in_specs = [blk(512, COL_RW_R // 512), blk(512, COL_RW_K // 512), blk(512, COL_RW_V // 512),
                blk(LANE, COL_RW_WA // LANE)]
    args = [proj, proj, proj, proj]
    if has_s0:
        in_specs.append(pl.BlockSpec((None, None, 2, RW_HEADS, RW_HEAD, RW_HEAD),
                                     lambda b: (b, layer, 0, 0, 0, 0)))
        args.append(s0)
    for w in wts:
        in_specs.append(pl.BlockSpec(w.shape, lambda b, n=w.ndim: (0,) * n))
        args.append(w)
    return pl.pallas_call(
        functools.partial(_rwkv_kernel, seq=seq, has_s0=has_s0),
        grid=(batch,),
        in_specs=in_specs,
        out_specs=[pl.BlockSpec((seq, RW_WIDTH), lambda b: (b, 0)),
                   pl.BlockSpec((None, 2, RW_HEADS, RW_HEAD, RW_HEAD), lambda b: (b, 0, 0, 0, 0))],
        out_shape=[jax.ShapeDtypeStruct((batch * seq, RW_WIDTH), BF16),
                   jax.ShapeDtypeStruct((batch, 2, RW_HEADS, RW_HEAD, RW_HEAD), F32)],
        scratch_shapes=[pltpu.VMEM((2, seq, RW_WIDTH), F32),
                        pltpu.VMEM((seq, RW_WIDTH), F32),
                        pltpu.VMEM((2, RW_PAIRS, LANE, LANE), F32)],
        compiler_params=pltpu.CompilerParams(vmem_limit_bytes=VMEM_LIMIT),
        name="rwkv",
    )(*args)


GLA_LEVELS = (32, 16, 8, 4, 2, 1)


def _gla_constants():
    c = CHUNK
    nl = len(GLA_LEVELS)
    mexp = np.zeros((2, (nl + 1) * c, c), np.float32)
    bmask = np.zeros((2, nl + 1, c, c), np.float32)
    for d in range(2):
        pos = np.arange(c) if d == 0 else c - 1 - np.arange(c)
        pt = pos[:, None]
        pj = pos[None, :]
        mexp[d, nl * c:] = (pj <= pt)
        for li, m in enumerate(GLA_LEVELS):
            mid = (pos // (2 * m)) * (2 * m) + m
            second = pos >= mid
            mq = (pj >= mid[:, None]) & (pj <= pt) & second[:, None]
            mk = (pj > pt) & (pj <= mid[:, None] - 1) & (~second)[:, None]
            mexp[d, li * c:(li + 1) * c] = mq | mk
            same = (pos[:, None] // (2 * m)) == (pos[None, :] // (2 * m))
            bmask[d, li] = same & second[:, None] & (~second)[None, :]
        bmask[d, nl] = np.eye(c)
    return mexp, np.concatenate([bmask, bmask], axis=-1)


def _gla_kernel(*refs, seq, has_s0):
    if has_s0:
        q_ref, k_ref, v_ref, ad_ref, s0_ref = refs[:5]
        rest = refs[5:]
    else:
        q_ref, k_ref, v_ref, ad_ref = refs[:4]
        s0_ref = None
        rest = refs[4:]
    (a2_ref, ab_ref, ng_ref, mexp_ref, bmask_ref,
     o_ref, sout_ref, acc_ref, st_ref) = rest
    c = CHUNK
    nc = seq // c
    nl = len(GLA_LEVELS)

    def chunk_operands(d, ci):
        rows = pl.ds(pl.multiple_of(ci * c, c), c)
        q = q_ref[rows, :].astype(F32) * (GLA_DK ** -0.5)
        k = k_ref[rows, :].astype(F32)
        v = v_ref[rows, :]
        x = _dot(ad_ref[rows, :].astype(BF16), a2_ref[d]) + ab_ref[d:d + 1, :]
        g = (jnp.minimum(x, 0.0) - jnp.log1p(jnp.exp(-jnp.abs(x)))) * (1.0 / GLA_LOGIT_NORM)
        g_hi, g_lo = _split(g)
        sums = _dot(mexp_ref[d], g_hi)
        ex = jnp.exp(sums[0:nl * c])
        b = sums[nl * c:(nl + 1) * c] + _dot(mexp_ref[d, nl * c:(nl + 1) * c, :], g_lo)
        last = 0 if d == 1 else c - 1
        blast = b[last:last + 1, :]
        qb = (q * jnp.exp(b)).astype(BF16)
        kdec = (k * jnp.exp(blast - b)).astype(BF16)
        dtot = jnp.exp(blast)
        qs = [(q * ex[li * c:(li + 1) * c]).astype(BF16) for li in range(nl)] + [q.astype(BF16)]
        ks = [(k * ex[li * c:(li + 1) * c]).astype(BF16) for li in range(nl)] + [k.astype(BF16)]
        return dict(rows=rows, qs=qs, ks=ks, qb=qb, kdec=kdec, dtot=dtot, v16=v.astype(BF16))

    even = lax.broadcasted_iota(jnp.int32, (1, LANE), 1) < GLA_DK
    r2 = lax.broadcasted_iota(jnp.int32, (2 * GLA_DV, LANE), 0)
    c2 = lax.broadcasted_iota(jnp.int32, (2 * GLA_DV, LANE), 1)
    same_head = (r2 < GLA_DV) == (c2 < GLA_DK)
    zv = jnp.zeros((c, GLA_DV), BF16)
    npair = GLA_HEADS // 2
    kls = [slice(p * LANE, (p + 1) * LANE) for p in range(npair)]
    vls = [slice(p * 2 * GLA_DV, (p + 1) * 2 * GLA_DV) for p in range(npair)]
    probs = [(d, p) for d in range(2) for p in range(npair)]
    first = lambda x: jnp.where(even, x, jnp.zeros_like(x))
    second = lambda x: jnp.where(even, jnp.zeros_like(x), x)

    def main_step(i, carry):
        ops = [chunk_operands(0, i), chunk_operands(1, nc - 1 - i)]
        st = [st_ref[d, p] for d, p in probs]
        lvl = [[_dot_nt(ops[d]["qs"][li][:, kls[p]],
                        jnp.concatenate([first(ops[d]["ks"][li][:, kls[p]]), second(ops[d]["ks"][li][:, kls[p]])], axis=0))
                for d, p in probs] for li in range(nl + 1)]
        att = [sum(bmask_ref[d, li] * lvl[li][j] for li in range(nl + 1)).astype(BF16)
               for j, (d, p) in enumerate(probs)]
        inter = [_dot_nt(ops[d]["qb"][:, kls[p]], st[j].astype(BF16)) for j, (d, p) in enumerate(probs)]
        for j, (d, p) in enumerate(probs):
            v_p = ops[d]["v16"][:, vls[p]]
            v_bd = jnp.concatenate([jnp.concatenate([v_p[:, 0:GLA_DV], zv], axis=1),
                                    jnp.concatenate([zv, v_p[:, GLA_DV:2 * GLA_DV]], axis=1)], axis=0)
            acc_ref[d, ops[d]["rows"], vls[p]] = inter[j] + _dot(att[j], v_bd)
            upd = _dot_tn(v_p, ops[d]["kdec"][:, kls[p]])
            st_ref[d, p] = st[j] * ops[d]["dtot"][:, kls[p]] + jnp.where(same_head, upd, 0.0)
        return carry

    blk = min(RW_BLOCK_ROWS, seq)

    def post_step(i, carry):
        rows = pl.ds(pl.multiple_of(i * blk, blk), blk)
        for h in range(GLA_HEADS):
            sv = slice(h * GLA_DV, (h + 1) * GLA_DV)
            o_ref[rows, sv] = (_rms(acc_ref[0, rows, sv] + acc_ref[1, rows, sv]) * ng_ref[...]).astype(o_ref.dtype)
        return carry

    st_ref[...] = jnp.zeros_like(st_ref)
    if has_s0:
        for d in range(2):
            for h in range(GLA_HEADS):
                ro, co = (h % 2) * GLA_DV, (h % 2) * GLA_DK
                st_ref[d, h // 2, ro:ro + GLA_DV, co:co + GLA_DK] = s0_ref[d, h].T
    lax.fori_loop(0, nc, main_step, 0)
    for d in range(2):
        for h in range(GLA_HEADS):
            ro, co = (h % 2) * GLA_DV, (h % 2) * GLA_DK
            sout_ref[d, h] = st_ref[d, h // 2, ro:ro + GLA_DV, co:co + GLA_DK].T
    lax.fori_loop(0, seq // blk, post_step, 0)


def _gla_call(proj, s0, layer, batch, seq, wts):
    has_s0 = s0 is not None
    blk = lambda w, cb: pl.BlockSpec((seq, w), lambda b: (b, cb))
    in_specs = [blk(GLA_KW, COL_GL_Q // GLA_KW), blk(GLA_KW, COL_GL_K // GLA_KW),
                blk(GLA_VW, COL_GL_V // GLA_VW), blk(LANE, COL_GL_AD // LANE)]
    args = [proj, proj, proj, proj]
    if has_s0:
        in_specs.append(pl.BlockSpec((None, None, 2, GLA_HEADS, GLA_DK, GLA_DV),
                                     lambda b: (b, layer, 0, 0, 0, 0)))
        args.append(s0)
    for w in wts:
        in_specs.append(pl.BlockSpec(w.shape, lambda b, n=w.ndim: (0,) * n))
        args.append(w)
    return pl.pallas_call(
        functools.partial(_gla_kernel, seq=seq, has_s0=has_s0),
        grid=(batch,),
        in_specs=in_specs,
        out_specs=[pl.BlockSpec((seq, GLA_VW), lambda b: (b, 0)),
                   pl.BlockSpec((None, 2, GLA_HEADS, GLA_DK, GLA_DV), lambda b: (b, 0, 0, 0, 0))],
        out_shape=[jax.ShapeDtypeStruct((batch * seq, GLA_VW), BF16),
                   jax.ShapeDtypeStruct((batch, 2, GLA_HEADS, GLA_DK, GLA_DV), F32)],
        scratch_shapes=[pltpu.VMEM((2, seq, GLA_VW), F32),
                        pltpu.VMEM((2, GLA_HEADS // 2, 2 * GLA_DV, LANE), F32)],
        compiler_params=pltpu.CompilerParams(vmem_limit_bytes=VMEM_LIMIT),
        name="gla",
    )(*args)


def _mla_kernel(*refs, seq, past, tq):
    has_ctx = past > 0
    if has_ctx:
        (qd_ref, kvd_ref, kpe_ref, kpesw_ref, cckv_ref, ckpe_ref,
         qng_ref, wq_ref, wqsw_ref, kvng_ref, wk_ref, wv_ref,
         cosq_ref, sinq_ref, cosk_ref, sink_ref, epos_ref,
         o_ref, ckv_ref, kcat_ref, vv_ref) = refs
    else:
        (qd_ref, kvd_ref, kpe_ref,
         qng_ref, wq_ref, kvng_ref, wk_ref, wv_ref,
         o_ref, ckv_ref, kcat_ref, vv_ref) = refs

    @pl.when(pl.program_id(1) == 0)
    def _():
        ckv = _rms(kvd_ref[...].astype(F32)) * kvng_ref[...]
        ckv_ref[...] = ckv
        if has_ctx:
            kpos = kpe_ref[...] * cosk_ref[...] + kpesw_ref[...] * sink_ref[...]
        else:
            kpos = kpe_ref[...]
        segs = [(0, seq, ckv, kpos)]
        if has_ctx:
            cpos = _dot(ckpe_ref[...].astype(BF16), epos_ref[...])
            segs.append((seq, past, cckv_ref[...], cpos))
        for start, n, lat, pos in segs:
            lat16 = lat.astype(BF16)
            kc = _dot(lat16, wk_ref[...])
            for h in range(MLA_HEADS):
                kcat_ref[h, start:start + n, :] = (kc[:, h * LANE:(h + 1) * LANE] + pos).astype(BF16)
            vv_ref[start:start + n, :] = _dot(lat16, wv_ref[...]).astype(BF16)

    qlat = (_rms(qd_ref[...].astype(F32)) * qng_ref[...]).astype(BF16)
    qc = _dot(qlat, wq_ref[...])
    if has_ctx:
        qsw = _dot(qlat, wqsw_ref[...])
    for h in range(MLA_HEADS):
        hl = slice(h * LANE, (h + 1) * LANE)
        q_raw = qc[:, hl]
        if has_ctx:
            q_self = (q_raw * cosq_ref[...] + qsw[:, hl] * sinq_ref[...]).astype(BF16)
        else:
            q_self = q_raw.astype(BF16)
        s1 = _dot_nt(q_self, kcat_ref[h, 0:seq, :])
        m = jnp.max(s1, axis=-1, keepdims=True)
        if has_ctx:
            s2 = _dot_nt(q_raw.astype(BF16), kcat_ref[h, seq:seq + past, :])
            m = jnp.maximum(m, jnp.max(s2, axis=-1, keepdims=True))
        p1 = jnp.exp(s1 - m)
        l = jnp.sum(p1, axis=-1, keepdims=True)
        vl = slice(h * MLA_V, (h + 1) * MLA_V)
        o_h = _dot(p1.astype(BF16), vv_ref[0:seq, vl])
        if has_ctx:
            p2 = jnp.exp(s2 - m)
            l = l + jnp.sum(p2, axis=-1, keepdims=True)
            o_h = o_h + _dot(p2.astype(BF16), vv_ref[seq:seq + past, vl])
        o_ref[:, vl] = (o_h / l).astype(o_ref.dtype)


def _mla_call(proj, cache_ckv, cache_kpe, layer, batch, seq, wts, tables):
    has_ctx = cache_ckv is not None
    past = cache_ckv.shape[2] if has_ctx else 0
    tq = 256
    nq = seq // tq
    full = lambda cb: pl.BlockSpec((seq, LANE), lambda b, i: (b, cb))
    in_specs = [pl.BlockSpec((tq, MLA_Q_RANK), lambda b, i: (b * nq + i, COL_ML_QD // MLA_Q_RANK)),
                full(COL_ML_KVD // LANE), full(COL_KPE // LANE)]
    args = [proj, proj, proj]
    if has_ctx:
        in_specs += [full(COL_KPE_SW // LANE),
                     pl.BlockSpec((None, None, past, MLA_KV_RANK), lambda b, i: (b, layer, 0, 0)),
                     pl.BlockSpec((None, None, past, MLA_ROPE), lambda b, i: (b, layer, 0, 0))]
        args += [proj, cache_ckv, cache_kpe]
    qn_g, wq_cat, wq_sw, kvn_g, wk_pad, wv = wts
    const = lambda w: pl.BlockSpec(w.shape, lambda b, i, n=w.ndim: (0,) * n)
    if has_ctx:
        cosq, sinq, cosk, sink, epos = tables
        wlist = [qn_g, wq_cat, wq_sw, kvn_g, wk_pad, wv]
        in_specs += [const(w) for w in wlist]
        in_specs += [pl.BlockSpec((tq, LANE), lambda b, i: (i, 0)), pl.BlockSpec((tq, LANE), lambda b, i: (i, 0)),
                     const(cosk), const(sink), const(epos)]
        args += wlist + [cosq, sinq, cosk, sink, epos]
    else:
        wlist = [qn_g, wq_cat, kvn_g, wk_pad, wv]
        in_specs += [const(w) for w in wlist]
        args += wlist
    return pl.pallas_call(
        functools.partial(_mla_kernel, seq=seq, past=past, tq=tq),
        grid=(batch, nq),
        in_specs=in_specs,
        out_specs=[pl.BlockSpec((tq, MLA_VW), lambda b, i: (b * nq + i, 0)),
                   pl.BlockSpec((seq, MLA_KV_RANK), lambda b, i: (b, 0))],
        out_shape=[jax.ShapeDtypeStruct((batch * seq, MLA_VW), BF16),
                   jax.ShapeDtypeStruct((batch * seq, MLA_KV_RANK), F32)],
        scratch_shapes=[pltpu.VMEM((MLA_HEADS, seq + past, LANE), BF16),
                        pltpu.VMEM((seq + past, MLA_VW), BF16)],
        compiler_params=pltpu.CompilerParams(vmem_limit_bytes=VMEM_LIMIT,
                                             dimension_semantics=("arbitrary", "arbitrary")),
        name="mla",
    )(*args)


def _outproj_kernel(x_ref, oa_ref, ob_ref, oc_ref, ga_ref, gb_ref, gc_ref, ma_ref, mb_ref, mc_ref,
                    mod_ref, wa_ref, wb_ref, wc_ref, wo_ref, fg_ref, o_ref, *, final):
    def branch(o_r, g_r, m_r, w_r):
        h = 0.5 * g_r[...]
        act = o_r[...] * (h * (1.0 + jnp.tanh(h)))
        y = _dot(act, w_r[...]).astype(BF16)
        return (0.5 * (1.0 + jnp.tanh(0.5 * m_r[...]))) * y

    y = (branch(oa_ref, ga_ref, ma_ref, wa_ref) + branch(ob_ref, gb_ref, mb_ref, wb_ref)
         + branch(oc_ref, gc_ref, mc_ref, wc_ref))
    y = _dot(y, wo_ref[...])
    gate = mod_ref[...][:, 2 * D_MODEL:3 * D_MODEL]
    hn = x_ref[...] + gate * y
    if final:
        hn = _rms(hn) * fg_ref[...]
    o_ref[...] = hn


def _outproj_call(x2d, o_a, o_b, o_c, proj, mod3, wts, final_g, rows_per_mod, mod_base, final):
    rows = x2d.shape[0]
    tm = OUTPROJ_TM
    tiles_per_mod = rows_per_mod // tm
    row = lambda w, cb=0: pl.BlockSpec((tm, w), lambda i: (i, cb))
    const = lambda w: pl.BlockSpec(w.shape, lambda i, n=w.ndim: (0,) * n)
    in_specs = [row(D_MODEL), row(512), row(512), row(512),
                row(512, COL_GATES // 512), row(512, COL_GATES // 512 + 1), row(512, COL_GATES // 512 + 2),
                row(D_MODEL, COL_MERGE // D_MODEL), row(D_MODEL, COL_MERGE // D_MODEL + 1),
                row(D_MODEL, COL_MERGE // D_MODEL + 2),
                pl.BlockSpec((None, 1, 3 * D_MODEL), lambda i: (mod_base + i // tiles_per_mod, 0, 0))]
    in_specs += [const(w) for w in wts] + [const(final_g)]
    return pl.pallas_call(
        functools.partial(_outproj_kernel, final=final),
        grid=(rows // tm,),
        in_specs=in_specs,
        out_specs=row(D_MODEL),
        out_shape=jax.ShapeDtypeStruct((rows, D_MODEL), F32),
        compiler_params=pltpu.CompilerParams(vmem_limit_bytes=VMEM_LIMIT),
        name="outproj",
    )(x2d, o_a, o_b, o_c, proj, proj, proj, proj, proj, proj, mod3, *wts, final_g)


def _pack_w_in(w):
    seg = lambda i: w[:, _IN_OFF[i]:_IN_OFF[i + 1]]
    z = lambda n: jnp.zeros((D_MODEL, n), w.dtype)
    kpe = seg(13)
    q = MLA_ROPE // 4
    kpe_sw = jnp.concatenate([kpe[:, q:2 * q], kpe[:, 0:q], kpe[:, 3 * q:4 * q], kpe[:, 2 * q:3 * q]], axis=1)
    tail = LANE - KPE_LANE - MLA_ROPE
    cols = [w[:, _IN_OFF[0]:_IN_OFF[5]],
            seg(9), z(LANE - GLA_GATE_RANK),
            seg(11),
            seg(6), seg(7), seg(8),
            seg(12),
            z(KPE_LANE), kpe, z(tail),
            z(KPE_LANE), kpe_sw, z(tail),
            z(PACK_A - COL_KPE_SW - LANE)]
    part_b = [seg(15),
              seg(5), seg(10), seg(14)]
    return jnp.concatenate(cols, axis=1).astype(BF16), jnp.concatenate(part_b, axis=1).astype(BF16)


def _rope_lane_tables(seq):
    n_freq = MLA_ROPE // 4
    t = np.arange(seq)
    inv = ROPE_THETA ** (-np.arange(n_freq, dtype=np.float64) / n_freq)
    ang = np.stack([(t // GRID_W)[:, None] * inv, (t % GRID_W)[:, None] * inv], axis=1)
    cos = np.repeat(np.cos(ang)[:, :, None, :], 2, axis=2).reshape(seq, MLA_ROPE)
    sin = np.stack([-np.sin(ang), np.sin(ang)], axis=2).reshape(seq, MLA_ROPE)
    return cos, sin


def _mla_tables(seq):
    cos, sin = _rope_lane_tables(seq)
    cosq = np.zeros((seq, LANE), np.float32)
    sinq = np.zeros((seq, LANE), np.float32)
    cosq[:, :KPE_LANE] = 1.0
    cosq[:, KPE_LANE:KPE_LANE + MLA_ROPE] = cos
    sinq[:, KPE_LANE:KPE_LANE + MLA_ROPE] = sin
    cosk = np.zeros((seq, LANE), np.float32)
    cosk[:, KPE_LANE:KPE_LANE + MLA_ROPE] = cos
    epos = np.zeros((MLA_ROPE, LANE), np.float32)
    epos[np.arange(MLA_ROPE), KPE_LANE + np.arange(MLA_ROPE)] = 1.0
    return (jnp.asarray(cosq), jnp.asarray(sinq), jnp.asarray(cosk), jnp.asarray(sinq),
            jnp.asarray(epos, dtype=BF16))


def _pack_mla_weights(qn_g, wq_up, kvn_g, wkv_up):
    scale = (MLA_NOPE + MLA_ROPE) ** -0.5
    wq = wq_up.reshape(MLA_Q_RANK, MLA_HEADS, MLA_NOPE + MLA_ROPE) * scale
    nope, rope = wq[..., :MLA_NOPE], wq[..., MLA_NOPE:]
    q = MLA_ROPE // 4
    rope_sw = jnp.concatenate([rope[..., q:2 * q], rope[..., 0:q], rope[..., 3 * q:4 * q], rope[..., 2 * q:3 * q]], -1)
    tail = jnp.zeros((MLA_Q_RANK, MLA_HEADS, LANE - KPE_LANE - MLA_ROPE), wq.dtype)
    wq_cat = jnp.concatenate([nope, rope, tail], -1).reshape(MLA_Q_RANK, MLA_HEADS * LANE).astype(BF16)
    wq_sw = jnp.concatenate([jnp.zeros_like(nope), rope_sw, tail], -1).reshape(MLA_Q_RANK, MLA_HEADS * LANE).astype(BF16)
    wkv = wkv_up.reshape(MLA_KV_RANK, MLA_HEADS, MLA_NOPE + MLA_V)
    wk = jnp.concatenate([wkv[..., :MLA_NOPE], jnp.zeros((MLA_KV_RANK, MLA_HEADS, LANE - MLA_NOPE), wkv.dtype)], -1)
    wk_pad = wk.reshape(MLA_KV_RANK, MLA_HEADS * LANE).astype(BF16)
    wv = wkv[..., MLA_NOPE:].reshape(MLA_KV_RANK, MLA_VW).astype(BF16)
    return (qn_g.reshape(1, -1), wq_cat, wq_sw, kvn_g.reshape(1, -1), wk_pad, wv)


def _head_sum_matrix():
    lane = np.arange(RW_WIDTH)
    return jnp.asarray((lane[:, None] // RW_HEAD == lane[None, :] // RW_HEAD).astype(np.float32), dtype=BF16)


def _trunk(x_prompt, x_sample, c, cache_ckv, cache_kpe, state_rwkv, state_gla, c_ctx,
           norm_g, w_mod, b_mod, w_in, rw_w0, rw_w2, rw_a0, rw_a2, rw_k_k, rw_k_a, rw_r_k,
           rw_ln_g, rw_ln_b, rw_out, gla_a2, gla_ab, gla_norm_g, gla_out,
           mla_qn_g, mla_wq_up, mla_kvn_g, mla_wkv_up, mla_out, w_out, final_g):
    bp, tp, _ = x_prompt.shape
    bs, ts, _ = x_sample.shape
    hp = x_prompt.reshape(bp * tp, D_MODEL)
    hs = x_sample.reshape(bs * ts, D_MODEL)
    cvec8 = jnp.concatenate([c, c_ctx[None, :], jnp.zeros((8 - bs - 1, D_MODEL), F32)], axis=0)
    ctx_row = bs
    hsum = _head_sum_matrix()
    mexp, bmask = _gla_constants()
    mexp = jnp.asarray(mexp, dtype=BF16)
    bmask = jnp.asarray(bmask)
    tables = _mla_tables(ts)
    fg = final_g.reshape(1, D_MODEL)
    ckv_l, kpe_l, rw_l, gla_l = [], [], [], []
    for l in range(DEPTH):
        mod3 = _mod_call(cvec8, w_mod, b_mod, l).reshape(8, 1, 3 * D_MODEL)
        w_a, w_b = _pack_w_in(w_in[l])
        ng = norm_g[l].reshape(1, D_MODEL)
        row = lambda a: a.reshape(1, -1)
        zr = jnp.zeros((2, RW_RANK, RW_WIDTH), F32)
        wcat = jnp.concatenate([jnp.concatenate([rw_w2[l], zr], axis=2),
                                jnp.concatenate([zr, rw_a2[l]], axis=2)], axis=1).astype(BF16)
        rw_wts = (wcat, jnp.concatenate([rw_w0[l], rw_a0[l]], axis=1), row(rw_k_k[l]), row(rw_k_a[l]),
                  row(rw_r_k[l]), row(rw_ln_g[l]), row(rw_ln_b[l]), hsum)
        a2p = jnp.concatenate([gla_a2[l], jnp.zeros((2, LANE - GLA_GATE_RANK, GLA_KW), F32)], axis=1).astype(BF16)
        gla_wts = (a2p, gla_ab[l], row(gla_norm_g[l]), mexp, bmask)
        mla_wts = _pack_mla_weights(mla_qn_g[l], mla_wq_up[l], mla_kvn_g[l], mla_wkv_up[l])
        out_wts = (rw_out[l].astype(BF16), gla_out[l].astype(BF16), mla_out[l].astype(BF16), w_out[l].astype(BF16))
        final = l == DEPTH - 1

        proj, proj_b = _inproj_call(hp, mod3, ng, w_a, w_b, bp * tp, ctx_row)
        o_a, s_rw = _rwkv_call(proj, None, l, bp, tp, rw_wts)
        o_b, s_gla = _gla_call(proj, None, l, bp, tp, gla_wts)
        o_c, ckv = _mla_call(proj, None, None, l, bp, tp, mla_wts, None)
        ckv_l.append(ckv.reshape(bp, tp, MLA_KV_RANK))
        kpe_l.append(proj[:, COL_KPE + KPE_LANE:COL_KPE + KPE_LANE + MLA_ROPE].astype(F32).reshape(bp, tp, MLA_ROPE))
        rw_l.append(s_rw)
        gla_l.append(s_gla)
        hp = _outproj_call(hp, o_a, o_b, o_c, proj_b, mod3, out_wts, fg, bp * tp, ctx_row, final)

        proj, proj_b = _inproj_call(hs, mod3, ng, w_a, w_b, ts, 0)
        o_a, _ = _rwkv_call(proj, state_rwkv, l, bs, ts, rw_wts)
        o_b, _ = _gla_call(proj, state_gla, l, bs, ts, gla_wts)
        o_c, _ = _mla_call(proj, cache_ckv, cache_kpe, l, bs, ts, mla_wts, tables)
        hs = _outproj_call(hs, o_a, o_b, o_c, proj_b, mod3, out_wts, fg, ts, 0, final)

    return (hp.reshape(bp, tp, D_MODEL), hs.reshape(bs, ts, D_MODEL),
            jnp.stack(ckv_l, axis=1), jnp.stack(kpe_l, axis=1),
            jnp.stack(rw_l, axis=1), jnp.stack(gla_l, axis=1))


_trunk_jit = jax.jit(_trunk)


def kernel(x_prompt, x_sample, c, cache_ckv, cache_kpe, state_rwkv, state_gla, c_ctx, norm_g, w_mod, b_mod, w_in, rw_w0, rw_w2, rw_a0, rw_a2, rw_k_k, rw_k_a, rw_r_k, rw_ln_g, rw_ln_b, rw_out, gla_a2, gla_ab, gla_norm_g, gla_out, mla_qn_g, mla_wq_up, mla_kvn_g, mla_wkv_up, mla_out, w_out, final_g):
    return _trunk_jit(x_prompt, x_sample, c, cache_ckv, cache_kpe, state_rwkv, state_gla, c_ctx, norm_g, w_mod, b_mod, w_in, rw_w0, rw_w2, rw_a0, rw_a2, rw_k_k, rw_k_a, rw_r_k, rw_ln_g, rw_ln_b, rw_out, gla_a2, gla_ab, gla_norm_g, gla_out, mla_qn_g, mla_wq_up, mla_kvn_g, mla_wkv_up, mla_out, w_out, final_g)
```

```python
import functools

import numpy as np
import jax
import jax.numpy as jnp
from jax import lax
from jax.experimental import pallas as pl
from jax.experimental.pallas import tpu as pltpu

F32 = jnp.float32
BF16 = jnp.bfloat16

D_MODEL = 1024
DEPTH = 2
GRID_W = 64
NORM_EPS = 1e-6
RW_HEADS = 8
RW_HEAD = 64
RW_WIDTH = RW_HEADS * RW_HEAD
RW_RANK = 64
RW_GN_EPS = 64e-5
GLA_HEADS = 4
GLA_DK = 64
GLA_DV = 128
GLA_KW = GLA_HEADS * GLA_DK
GLA_VW = GLA_HEADS * GLA_DV
GLA_GATE_RANK = 16
GLA_LOGIT_NORM = 16.0
MLA_HEADS = 8
MLA_NOPE = 64
MLA_ROPE = 32
MLA_V = 64
MLA_Q_RANK = 256
MLA_KV_RANK = 128
MLA_VW = MLA_HEADS * MLA_V
ROPE_THETA = 10000.0
N_BRANCH = 3

_IN_SIZES = (RW_WIDTH, RW_WIDTH, RW_WIDTH, RW_RANK, RW_RANK, RW_WIDTH,
             GLA_KW, GLA_KW, GLA_VW, GLA_GATE_RANK, GLA_VW,
             MLA_Q_RANK, MLA_KV_RANK, MLA_ROPE, MLA_VW, N_BRANCH * D_MODEL)
_IN_OFF = tuple(int(v) for v in np.concatenate([[0], np.cumsum(_IN_SIZES)]))

LANE = 128
COL_RW_R = 0
COL_RW_K = 512
COL_RW_V = 1024
COL_RW_WA = 1536
COL_GL_AD = 1664
COL_ML_QD = 1792
COL_GL_Q = 2048
COL_GL_K = 2304
COL_GL_V = 2560
COL_ML_KVD = 3072
COL_KPE = 3200
COL_KPE_SW = 3328
PACK_A = 3584
COL_MERGE = 0
COL_GATES = 3072
PACK_B = 4608
KPE_LANE = MLA_NOPE

CHUNK = 64
INPROJ_TM = 512
INPROJ_SUB = 256
INPROJ_TN = 512
OUTPROJ_TM = 512
MLA_TQ = 1024
VMEM_LIMIT = 48 * 1024 * 1024


def _dot(a, b, prec=None):
    return jnp.dot(a, b, preferred_element_type=F32, precision=prec)


def _dot_nt(a, b, prec=None):
    return lax.dot_general(a, b, (((1,), (1,)), ((), ())), preferred_element_type=F32, precision=prec)


def _dot_tn(a, b, prec=None):
    return lax.dot_general(a, b, (((0,), (0,)), ((), ())), preferred_element_type=F32, precision=prec)


def _split(x):
    hi = x.astype(BF16)
    return hi, (x - hi.astype(F32)).astype(BF16)


def _dot_split(a, b):
    ah, al = _split(a)
    bh, bl = _split(b)
    return _dot(ah, bh) + _dot(al, bh) + _dot(ah, bl)


def _dot_sel(sel16, x):
    xh, xl = _split(x)
    return _dot(sel16, xh) + _dot(sel16, xl)


def _softplus(z):
    return jnp.maximum(z, 0.0) + jnp.log1p(jnp.exp(-jnp.abs(z)))


def _sigmoid(x):
    return 0.5 * jnp.tanh(0.5 * x) + 0.5


def _rms(x, eps=NORM_EPS):
    return x * lax.rsqrt(jnp.mean(x * x, axis=-1, keepdims=True) + eps)


def _mod_kernel(c_ref, w_ref, b_ref, o_ref):
    c = c_ref[...]
    o_ref[...] = _dot_split(c * _sigmoid(c), w_ref[...]) + b_ref[...]


def _mod_call(cvec8, w_mod, b_mod, layer):
    tn = 1024
    return pl.pallas_call(
        _mod_kernel,
        grid=(3 * D_MODEL // tn,),
        in_specs=[pl.BlockSpec((8, D_MODEL), lambda j: (0, 0)),
                  pl.BlockSpec((None, D_MODEL, tn), lambda j: (layer, 0, j)),
                  pl.BlockSpec((None, 1, tn), lambda j: (layer, 0, j))],
        out_specs=pl.BlockSpec((8, tn), lambda j: (0, j)),
        out_shape=jax.ShapeDtypeStruct((8, 3 * D_MODEL), F32),
        compiler_params=pltpu.CompilerParams(vmem_limit_bytes=VMEM_LIMIT),
        name="mod",
    )(cvec8, w_mod, b_mod.reshape(DEPTH, 1, 3 * D_MODEL))


def _inproj_kernel(x_ref, mod_ref, g_ref, wa_ref, wb_ref, oa_ref, ob_ref):
    m = mod_ref[...]
    shift = m[:, 0:D_MODEL]
    scale1 = 1.0 + m[:, D_MODEL:2 * D_MODEL]
    tm = x_ref.shape[0]
    for r0 in range(0, tm, INPROJ_SUB):
        rows = slice(r0, r0 + INPROJ_SUB)
        h = (_rms(x_ref[rows, :]) * g_ref[...] * scale1 + shift).astype(BF16)
        for w_ref, o_ref in ((wa_ref, oa_ref), (wb_ref, ob_ref)):
            for c0 in range(0, w_ref.shape[1], INPROJ_TN):
                cols = slice(c0, c0 + INPROJ_TN)
                o_ref[rows, cols] = _dot(h, w_ref[:, cols]).astype(BF16)


def _inproj_call(x2d, mod3, norm_g, w_a, w_b, rows_per_mod, mod_base):
    rows = x2d.shape[0]
    tm = INPROJ_TM
    tiles_per_mod = rows_per_mod // tm
    resident = lambda w: pl.BlockSpec(w.shape, lambda i: (0, 0), pipeline_mode=pl.Buffered(1))
    return pl.pallas_call(
        _inproj_kernel,
        grid=(rows // tm,),
        in_specs=[pl.BlockSpec((tm, D_MODEL), lambda i: (i, 0)),
                  pl.BlockSpec((None, 1, 3 * D_MODEL), lambda i: (mod_base + i // tiles_per_mod, 0, 0)),
                  pl.BlockSpec((1, D_MODEL), lambda i: (0, 0)),
                  resident(w_a), resident(w_b)],
        out_specs=[pl.BlockSpec((tm, PACK_A), lambda i: (i, 0)),
                   pl.BlockSpec((tm, PACK_B), lambda i: (i, 0))],
        out_shape=[jax.ShapeDtypeStruct((rows, PACK_A), BF16),
                   jax.ShapeDtypeStruct((rows, PACK_B), BF16)],
        compiler_params=pltpu.CompilerParams(vmem_limit_bytes=VMEM_LIMIT),
        name="inproj",
    )(x2d, mod3, norm_g, w_a, w_b)


RW_CHUNK = 64
RW_PAIRS = RW_HEADS // 2
RW_BLOCK_ROWS = 256


def _rwkv_time_mask(c, reverse):
    row = lax.broadcasted_iota(jnp.int32, (2 * c, 4 * c), 0)
    col = lax.broadcasted_iota(jnp.int32, (2 * c, 4 * c), 1)
    t = jnp.where(row >= c, row - c, row)
    s = col & (c - 1)
    earlier = (s > t) if reverse else (s < t)
    return earlier | ((row >= c) & (s == t))


def _rwkv_kernel(*refs, seq, has_s0):
    if has_s0:
        r_ref, k_ref, v_ref, wa_ref, s0_ref = refs[:5]
        rest = refs[5:]
    else:
        r_ref, k_ref, v_ref, wa_ref = refs[:4]
        s0_ref = None
        rest = refs[4:]
    (wcat_ref, wa0_ref, kk_ref, ka_ref, rk_ref, lng_ref, lnb_ref, hsum_ref,
     o_ref, sout_ref, acc_ref, kn_ref, st_ref) = rest
    c = RW_CHUNK
    nc = seq // c
    blk = min(RW_BLOCK_ROWS, seq)
    hsum = hsum_ref[...]

    def pre_step(i, carry):
        rows = pl.ds(pl.multiple_of(i * blk, blk), blk)
        kk0 = k_ref[rows, :].astype(F32) * kk_ref[...]
        kn_ref[rows, :] = kk0 / jnp.maximum(jnp.sqrt(_dot((kk0 * kk0).astype(BF16), hsum)), 1e-12)
        return carry

    def post_step(i, carry):
        rows = pl.ds(pl.multiple_of(i * blk, blk), blk)
        o = acc_ref[0, rows, :] + acc_ref[1, rows, :]
        rk = r_ref[rows, :].astype(F32) * k_ref[rows, :].astype(F32) * rk_ref[...]
        red = _dot(jnp.concatenate([o, rk], axis=0).astype(BF16), hsum)
        dev = o - red[0:blk] * (1.0 / RW_HEAD)
        var = _dot((dev * dev).astype(BF16), hsum) * (1.0 / RW_HEAD)
        o = dev * lax.rsqrt(var + RW_GN_EPS) * lng_ref[...] + lnb_ref[...]
        o_ref[rows, :] = (o + red[blk:2 * blk] * v_ref[rows, :]).astype(o_ref.dtype)
        return carry

    even = lax.broadcasted_iota(jnp.int32, (1, LANE), 1) < RW_HEAD
    row = lax.broadcasted_iota(jnp.int32, (c, c), 0)
    col = lax.broadcasted_iota(jnp.int32, (c, c), 1)
    incl16 = [jnp.where(col <= row, 1.0, 0.0).astype(BF16), jnp.where(col >= row, 1.0, 0.0).astype(BF16)]
    tmask = [_rwkv_time_mask(c, False), _rwkv_time_mask(c, True)]
    r2 = lax.broadcasted_iota(jnp.int32, (LANE, LANE), 0)
    c2 = lax.broadcasted_iota(jnp.int32, (LANE, LANE), 1)
    same_head = (r2 < RW_HEAD) == (c2 < RW_HEAD)
    z16 = jnp.zeros((c, LANE), BF16)
    lanes = [slice(p * LANE, (p + 1) * LANE) for p in range(RW_PAIRS)]
    probs = [(d, p) for d in range(2) for p in range(RW_PAIRS)]

    def chunk_operands(d, ci):
        rows = pl.ds(pl.multiple_of(ci * c, c), c)
        r = r_ref[rows, :].astype(F32)
        k = k_ref[rows, :].astype(F32)
        wa = wa_ref[rows, :].astype(F32)
        kk = kn_ref[rows, :]
        lora = jnp.where(even, jnp.tanh(wa), wa)
        pre = _dot(lora.astype(BF16), wcat_ref[d]) + wa0_ref[d:d + 1, :]
        w_log = -_softplus(-pre[:, 0:RW_WIDTH]) - 0.5
        lw = -jnp.exp(w_log)
        a = _sigmoid(pre[:, RW_WIDTH:2 * RW_WIDTH])
        kd = k * (1.0 + (a - 1.0) * ka_ref[...])
        bv = kk * a
        gam = _dot_sel(incl16[d], lw)
        last = 0 if d == 1 else c - 1
        gtot = gam[last:last + 1, :]
        gref = gam[c // 2:c // 2 + 1, :]
        e_in = jnp.exp(gam - gref)
        e_out = jnp.exp(gref - gam)
        e_end = jnp.exp(gtot - gam)
        lhs = jnp.concatenate([-kk * (e_in * jnp.exp(-lw)), r * e_in], axis=0).astype(BF16)
        return dict(rows=rows, lhs=lhs, b16=(bv * e_out).astype(BF16), k16=(kd * e_out).astype(BF16),
                    be16=(bv * e_end).astype(BF16), ke16=(kd * e_end).astype(BF16),
                    v16=v_ref[rows, :].astype(BF16), dtot=jnp.exp(gtot), eref=jnp.exp(gref))

    def main_step(i, carry):
        ops = [chunk_operands(0, i), chunk_operands(1, nc - 1 - i)]
        n = range(len(probs))
        pick = lambda name: [ops[d][name][:, lanes[p]] for d, p in probs]
        lhs, b, k, be, ke, v = (pick(s) for s in ("lhs", "b16", "k16", "be16", "ke16", "v16"))
        st = [st_ref[d, p] for d, p in probs]
        first = lambda x: jnp.where(even, x, jnp.zeros_like(x))
        second = lambda x: jnp.where(even, jnp.zeros_like(x), x)
        v_e = [first(x) for x in v]
        v_o = [second(x) for x in v]
        aa = [jnp.where(tmask[d], _dot_nt(lhs[j], jnp.concatenate(
            [first(k[j]), first(b[j]), second(b[j]), second(k[j])], axis=0)), 0.0) for j, (d, p) in enumerate(probs)]
        sp = [_dot_nt(lhs[j], (st[j] * ops[d]["eref"][:, lanes[p]]).astype(BF16))
              for j, (d, p) in enumerate(probs)]
        aa16 = [x.astype(BF16) for x in aa]
        rhs_u = [_dot(aa16[j][0:c], jnp.concatenate([v_e[j], z16, z16, v_o[j]], axis=0)) + sp[j][0:c] for j in n]
        x_e = [jnp.where(even, rhs_u[j], aa[j][0:c, 0:LANE]) for j in n]
        x_o = [jnp.where(even, aa[j][0:c, LANE:2 * LANE], rhs_u[j]) for j in n]
        covered = 1
        while covered < c:
            e16 = [x.astype(BF16) for x in x_e]
            o16 = [x.astype(BF16) for x in x_o]
            x_e = [_dot(e16[j], jnp.concatenate([z16, e16[j]], axis=0)) + first(x_e[j]) for j in n]
            x_o = [_dot(o16[j], jnp.concatenate([o16[j], z16], axis=0)) + second(x_o[j]) for j in n]
            covered *= 2
        u16 = [jnp.where(even, x_e[j], x_o[j]).astype(BF16) for j in n]
        for j, (d, p) in enumerate(probs):
            uv_rows = jnp.concatenate([v_e[j], first(u16[j]), second(u16[j]), v_o[j]], axis=0)
            acc_ref[d, ops[d]["rows"], lanes[p]] = sp[j][c:2 * c] + _dot(aa16[j][c:2 * c], uv_rows)
            upd = _dot_tn(jnp.concatenate([u16[j], v[j]], axis=0), jnp.concatenate([be[j], ke[j]], axis=0))
            st_ref[d, p] = st[j] * ops[d]["dtot"][:, lanes[p]] + jnp.where(same_head, upd, 0.0)
        return carry

    st_ref[...] = jnp.zeros_like(st_ref)
    if has_s0:
        for d in range(2):
            for h in range(RW_HEADS):
                off = (h % 2) * RW_HEAD
                st_ref[d, h // 2, off:off + RW_HEAD, off:off + RW_HEAD] = s0_ref[d, h]
    lax.fori_loop(0, seq // blk, pre_step, 0)
    lax.fori_loop(0, nc, main_step, 0)
    for d in range(2):
        for h in range(RW_HEADS):
            off = (h % 2) * RW_HEAD
            sout_ref[d, h] = st_ref[d, h // 2, off:off + RW_HEAD, off:off + RW_HEAD]
    lax.fori_loop(0, seq // blk, post_step, 0)


def _rwkv_call(proj, s0, layer, batch, seq, wts):
    has_s0 = s0 is not None
    blk = lambda w, cb: pl.BlockSpec((seq, w), lambda b: (b, cb))
    in_specs = [blk(512, COL_RW_R // 512), blk(512, COL_RW_K // 512), blk(512, COL_RW_V // 512),
                blk(LANE, COL_RW_WA // LANE)]
    args = [proj, proj, proj, proj]
    if has_s0:
        in_specs.append(pl.BlockSpec((None, None, 2, RW_HEADS, RW_HEAD, RW_HEAD),
                                     lambda b: (b, layer, 0, 0, 0, 0)))
        args.append(s0)
    for w in wts:
        in_specs.append(pl.BlockSpec(w.shape, lambda b, n=w.ndim: (0,) * n))
        args.append(w)
    return pl.pallas_call(
        functools.partial(_rwkv_kernel, seq=seq, has_s0=has_s0),
        grid=(batch,),
        in_specs=in_specs,
        out_specs=[pl.BlockSpec((seq, RW_WIDTH), lambda b: (b, 0)),
                   pl.BlockSpec((None, 2, RW_HEADS, RW_HEAD, RW_HEAD), lambda b: (b, 0, 0, 0, 0))],
        out_shape=[jax.ShapeDtypeStruct((batch * seq, RW_WIDTH), BF16),
                   jax.ShapeDtypeStruct((batch, 2, RW_HEADS, RW_HEAD, RW_HEAD), F32)],
        scratch_shapes=[pltpu.VMEM((2, seq, RW_WIDTH), F32),
                        pltpu.VMEM((seq, RW_WIDTH), F32),
                        pltpu.VMEM((2, RW_PAIRS, LANE, LANE), F32)],
        compiler_params=pltpu.CompilerParams(vmem_limit_bytes=VMEM_LIMIT),
        name="rwkv",
    )(*args)


GLA_LEVELS = (32, 16, 8, 4, 2, 1)
GLA_GROUP = 4


def _gla_constants():
    c = CHUNK
    nl = len(GLA_LEVELS)
    mexp = np.zeros((2, (nl + 1) * c, c), np.float32)
    bmask = np.zeros((2, nl + 1, c, c), np.float32)
    for d in range(2):
        pos = np.arange(c) if d == 0 else c - 1 - np.arange(c)
        pt = pos[:, None]
        pj = pos[None, :]
        mexp[d, nl * c:] = (pj <= pt)
        for li, m in enumerate(GLA_LEVELS):
            mid = (pos // (2 * m)) * (2 * m) + m
            second = pos >= mid
            mq = (pj >= mid[:, None]) & (pj <= pt) & second[:, None]
            mk = (pj > pt) & (pj <= mid[:, None] - 1) & (~second)[:, None]
            mexp[d, li * c:(li + 1) * c] = mq | mk
            same = (pos[:, None] // (2 * m)) == (pos[None, :] // (2 * m))
            bmask[d, li] = same & second[:, None] & (~second)[None, :]
        bmask[d, nl] = np.eye(c)
    return mexp, np.concatenate([bmask, bmask], axis=-1)


def _gla_kernel(*refs, seq, has_s0):
    if has_s0:
        q_ref, k_ref, v_ref, ad_ref, s0_ref = refs[:5]
        rest = refs[5:]
    else:
        q_ref, k_ref, v_ref, ad_ref = refs[:4]
        s0_ref = None
        rest = refs[4:]
    (a2_ref, ab_ref, ng_ref, mexp_ref, bmask_ref,
     o_ref, sout_ref, acc_ref, st_ref) = rest
    c = CHUNK
    nc = seq // c
    nl = len(GLA_LEVELS)

    def chunk_operands(d, ci):
        rows = pl.ds(pl.multiple_of(ci * c, c), c)
        q = q_ref[rows, :].astype(F32) * (GLA_DK ** -0.5)
        k = k_ref[rows, :].astype(F32)
        v = v_ref[rows, :]
        x = _dot(ad_ref[rows, :].astype(BF16), a2_ref[d]) + ab_ref[d:d + 1, :]
        g = (jnp.minimum(x, 0.0) - jnp.log1p(jnp.exp(-jnp.abs(x)))) * (1.0 / GLA_LOGIT_NORM)
        g_hi, g_lo = _split(g)
        sums = _dot(mexp_ref[d], g_hi)
        ex = jnp.exp(sums[0:nl * c])
        b = sums[nl * c:(nl + 1) * c] + _dot(mexp_ref[d, nl * c:(nl + 1) * c, :], g_lo)
        last = 0 if d == 1 else c - 1
        blast = b[last:last + 1, :]
        qb = (q * jnp.exp(b)).astype(BF16)
        kdec = (k * jnp.exp(blast - b)).astype(BF16)
        dtot = jnp.exp(blast)
        qs = [(q * ex[li * c:(li + 1) * c]).astype(BF16) for li in range(nl)] + [q.astype(BF16)]
        ks = [(k * ex[li * c:(li + 1) * c]).astype(BF16) for li in range(nl)] + [k.astype(BF16)]
        return dict(rows=rows, qs=qs, ks=ks, qb=qb, kdec=kdec, dtot=dtot, v16=v.astype(BF16))

    even = lax.broadcasted_iota(jnp.int32, (1, LANE), 1) < GLA_DK
    r2 = lax.broadcasted_iota(jnp.int32, (2 * GLA_DV, LANE), 0)
    c2 = lax.broadcasted_iota(jnp.int32, (2 * GLA_DV, LANE), 1)
    same_head = (r2 < GLA_DV) == (c2 < GLA_DK)
    zv = jnp.zeros((c, GLA_DV), BF16)
    npair = GLA_HEADS // 2
    kls = [slice(p * LANE, (p + 1) * LANE) for p in range(npair)]
    vls = [slice(p * 2 * GLA_DV, (p + 1) * 2 * GLA_DV) for p in range(npair)]
    probs = [(d, p) for d in range(2) for p in range(npair)]
    first = lambda x: jnp.where(even, x, jnp.zeros_like(x))
    second = lambda x: jnp.where(even, jnp.zeros_like(x), x)

    group = min(GLA_GROUP, nc)

    def main_step(i, carry):
        ops = [[chunk_operands(0, i * group + g) for g in range(group)],
               [chunk_operands(1, nc - 1 - (i * group + g)) for g in range(group)]]
        allp = [(d, p, g) for d, p in probs for g in range(group)]
        lvl = [[_dot_nt(ops[d][g]["qs"][li][:, kls[p]],
                        jnp.concatenate([first(ops[d][g]["ks"][li][:, kls[p]]),
                                         second(ops[d][g]["ks"][li][:, kls[p]])], axis=0))
                for d, p, g in allp] for li in range(nl + 1)]
        att = [sum(bmask_ref[d, li] * lvl[li][j] for li in range(nl + 1)).astype(BF16)
               for j, (d, p, g) in enumerate(allp)]
        upd = [jnp.where(same_head, _dot_tn(ops[d][g]["v16"][:, vls[p]], ops[d][g]["kdec"][:, kls[p]]), 0.0)
               for d, p, g in allp]
        states = []
        for d, p in probs:
            st = st_ref[d, p]
            for g in range(group):
                states.append(st)
                st = st * ops[d][g]["dtot"][:, kls[p]] + upd[len(states) - 1]
            st_ref[d, p] = st
        for j, (d, p, g) in enumerate(allp):
            v_p = ops[d][g]["v16"][:, vls[p]]
            v_bd = jnp.concatenate([jnp.concatenate([v_p[:, 0:GLA_DV], zv], axis=1),
                                    jnp.concatenate([zv, v_p[:, GLA_DV:2 * GLA_DV]], axis=1)], axis=0)
            inter = _dot_nt(ops[d][g]["qb"][:, kls[p]], states[j].astype(BF16))
            acc_ref[d, ops[d][g]["rows"], vls[p]] = inter + _dot(att[j], v_bd)
        return carry

    blk = min(RW_BLOCK_ROWS, seq)

    def post_step(i, carry):
        rows = pl.ds(pl.multiple_of(i * blk, blk), blk)
        for h in range(GLA_HEADS):
            sv = slice(h * GLA_DV, (h + 1) * GLA_DV)
            o_ref[rows, sv] = (_rms(acc_ref[0, rows, sv] + acc_ref[1, rows, sv]) * ng_ref[...]).astype(o_ref.dtype)
        return carry

    st_ref[...] = jnp.zeros_like(st_ref)
    if has_s0:
        for d in range(2):
            for h in range(GLA_HEADS):
                ro, co = (h % 2) * GLA_DV, (h % 2) * GLA_DK
                st_ref[d, h // 2, ro:ro + GLA_DV, co:co + GLA_DK] = s0_ref[d, h].T
    lax.fori_loop(0, nc // group, main_step, 0)
    for d in range(2):
        for h in range(GLA_HEADS):
            ro, co = (h % 2) * GLA_DV, (h % 2) * GLA_DK
            sout_ref[d, h] = st_ref[d, h // 2, ro:ro + GLA_DV, co:co + GLA_DK].T
    lax.fori_loop(0, seq // blk, post_step, 0)


def _gla_call(proj, s0, layer, batch, seq, wts):
    has_s0 = s0 is not None
    blk = lambda w, cb: pl.BlockSpec((seq, w), lambda b: (b, cb))
    in_specs = [blk(GLA_KW, COL_GL_Q // GLA_KW), blk(GLA_KW, COL_GL_K // GLA_KW),
                blk(GLA_VW, COL_GL_V // GLA_VW), blk(LANE, COL_GL_AD // LANE)]
    args = [proj, proj, proj, proj]
    if has_s0:
        in_specs.append(pl.BlockSpec((None, None, 2, GLA_HEADS, GLA_DK, GLA_DV),
                                     lambda b: (b, layer, 0, 0, 0, 0)))
        args.append(s0)
    for w in wts:
        in_specs.append(pl.BlockSpec(w.shape, lambda b, n=w.ndim: (0,) * n))
        args.append(w)
    return pl.pallas_call(
        functools.partial(_gla_kernel, seq=seq, has_s0=has_s0),
        grid=(batch,),
        in_specs=in_specs,
        out_specs=[pl.BlockSpec((seq, GLA_VW), lambda b: (b, 0)),
                   pl.BlockSpec((None, 2, GLA_HEADS, GLA_DK, GLA_DV), lambda b: (b, 0, 0, 0, 0))],
        out_shape=[jax.ShapeDtypeStruct((batch * seq, GLA_VW), BF16),
                   jax.ShapeDtypeStruct((batch, 2, GLA_HEADS, GLA_DK, GLA_DV), F32)],
        scratch_shapes=[pltpu.VMEM((2, seq, GLA_VW), F32),
                        pltpu.VMEM((2, GLA_HEADS // 2, 2 * GLA_DV, LANE), F32)],
        compiler_params=pltpu.CompilerParams(vmem_limit_bytes=VMEM_LIMIT),
        name="gla",
    )(*args)


def _mla_kernel(*refs, seq, past, tq):
    has_ctx = past > 0
    if has_ctx:
        (qd_ref, kvd_ref, kpe_ref, kpesw_ref, cckv_ref, ckpe_ref,
         qng_ref, wq_ref, wqsw_ref, kvng_ref, wk_ref, wv_ref,
         cosq_ref, sinq_ref, cosk_ref, sink_ref, epos_ref,
         o_ref, ckv_ref, kcat_ref, vv_ref) = refs
    else:
        (qd_ref, kvd_ref, kpe_ref,
         qng_ref, wq_ref, kvng_ref, wk_ref, wv_ref,
         o_ref, ckv_ref, kcat_ref, vv_ref) = refs

    ones_hi = jnp.where(lax.broadcasted_iota(jnp.int32, (1, LANE), 1) >= MLA_V, 1.0, 0.0)

    @pl.when(pl.program_id(1) == 0)
    def _():
        ckv = _rms(kvd_ref[...].astype(F32)) * kvng_ref[...]
        ckv_ref[...] = ckv
        if has_ctx:
            kpos = kpe_ref[...] * cosk_ref[...] + kpesw_ref[...] * sink_ref[...]
        else:
            kpos = kpe_ref[...]
        segs = [(0, seq, ckv, kpos)]
        if has_ctx:
            cpos = _dot(ckpe_ref[...].astype(BF16), epos_ref[...])
            segs.append((seq, past, cckv_ref[...], cpos))
        for start, n, lat, pos in segs:
            lat16 = lat.astype(BF16)
            kc = _dot(lat16, wk_ref[...])
            vc = _dot(lat16, wv_ref[...])
            for h in range(MLA_HEADS):
                hl = slice(h * LANE, (h + 1) * LANE)
                kcat_ref[h, start:start + n, :] = (kc[:, hl] + pos).astype(BF16)
                vv_ref[h, start:start + n, :] = (vc[:, hl] + ones_hi).astype(BF16)

    qlat = (_rms(qd_ref[...].astype(F32)) * qng_ref[...]).astype(BF16)
    qc = _dot(qlat, wq_ref[...])
    if has_ctx:
        qsw = _dot(qlat, wqsw_ref[...])
    for h in range(MLA_HEADS):
        hl = slice(h * LANE, (h + 1) * LANE)
        q_raw = qc[:, hl]
        if has_ctx:
            q_self = (q_raw * cosq_ref[...] + qsw[:, hl] * sinq_ref[...]).astype(BF16)
        else:
            q_self = q_raw.astype(BF16)
        s1 = _dot_nt(q_self, kcat_ref[h, 0:seq, :])
        m = jnp.max(s1, axis=-1, keepdims=True)
        if has_ctx:
            s2 = _dot_nt(q_raw.astype(BF16), kcat_ref[h, seq:seq + past, :])
            m = jnp.maximum(m, jnp.max(s2, axis=-1, keepdims=True))
        o_h = _dot(jnp.exp((s1 - m).astype(BF16)), vv_ref[h, 0:seq, :])
        if has_ctx:
            o_h = o_h + _dot(jnp.exp((s2 - m).astype(BF16)), vv_ref[h, seq:seq + past, :])
        o_h = o_h / o_h[:, MLA_V:MLA_V + 1]
        o_ref[:, h * MLA_V:(h + 1) * MLA_V] = o_h[:, 0:MLA_V].astype(o_ref.dtype)


def _mla_call(proj, cache_ckv, cache_kpe, layer, batch, seq, wts, tables):
    has_ctx = cache_ckv is not None
    past = cache_ckv.shape[2] if has_ctx else 0
    tq = min(MLA_TQ, seq)
    nq = seq // tq
    full = lambda cb: pl.BlockSpec((seq, LANE), lambda b, i: (b, cb))
    in_specs = [pl.BlockSpec((tq, MLA_Q_RANK), lambda b, i: (b * nq + i, COL_ML_QD // MLA_Q_RANK)),
                full(COL_ML_KVD // LANE), full(COL_KPE // LANE)]
    args = [proj, proj, proj]
    if has_ctx:
        in_specs += [full(COL_KPE_SW // LANE),
                     pl.BlockSpec((None, None, past, MLA_KV_RANK), lambda b, i: (b, layer, 0, 0)),
                     pl.BlockSpec((None, None, past, MLA_ROPE), lambda b, i: (b, layer, 0, 0))]
        args += [proj, cache_ckv, cache_kpe]
    qn_g, wq_cat, wq_sw, kvn_g, wk_pad, wv = wts
    const = lambda w: pl.BlockSpec(w.shape, lambda b, i, n=w.ndim: (0,) * n)
    if has_ctx:
        cosq, sinq, cosk, sink, epos = tables
        wlist = [qn_g, wq_cat, wq_sw, kvn_g, wk_pad, wv]
        in_specs += [const(w) for w in wlist]
        in_specs += [pl.BlockSpec((tq, LANE), lambda b, i: (i, 0)), pl.BlockSpec((tq, LANE), lambda b, i: (i, 0)),
                     const(cosk), const(sink), const(epos)]
        args += wlist + [cosq, sinq, cosk, sink, epos]
    else:
        wlist = [qn_g, wq_cat, kvn_g, wk_pad, wv]
        in_specs += [const(w) for w in wlist]
        args += wlist
    return pl.pallas_call(
        functools.partial(_mla_kernel, seq=seq, past=past, tq=tq),
        grid=(batch, nq),
        in_specs=in_specs,
        out_specs=[pl.BlockSpec((tq, MLA_VW), lambda b, i: (b * nq + i, 0)),
                   pl.BlockSpec((seq, MLA_KV_RANK), lambda b, i: (b, 0))],
        out_shape=[jax.ShapeDtypeStruct((batch * seq, MLA_VW), BF16),
                   jax.ShapeDtypeStruct((batch * seq, MLA_KV_RANK), F32)],
        scratch_shapes=[pltpu.VMEM((MLA_HEADS, seq + past, LANE), BF16),
                        pltpu.VMEM((MLA_HEADS, seq + past, LANE), BF16)],
        compiler_params=pltpu.CompilerParams(vmem_limit_bytes=VMEM_LIMIT,
                                             dimension_semantics=("arbitrary", "arbitrary")),
        name="mla",
    )(*args)


def _outproj_kernel(x_ref, oa_ref, ob_ref, oc_ref, ga_ref, gb_ref, gc_ref, ma_ref, mb_ref, mc_ref,
                    mod_ref, wa_ref, wb_ref, wc_ref, wo_ref, fg_ref, o_ref, *, final):
    def branch(o_r, g_r, m_r, w_r):
        h = 0.5 * g_r[...]
        act = o_r[...] * (h * (1.0 + jnp.tanh(h)))
        y = _dot(act, w_r[...]).astype(BF16)
        return (0.5 * (1.0 + jnp.tanh(0.5 * m_r[...]))) * y

    y = (branch(oa_ref, ga_ref, ma_ref, wa_ref) + branch(ob_ref, gb_ref, mb_ref, wb_ref)
         + branch(oc_ref, gc_ref, mc_ref, wc_ref))
    y = _dot(y, wo_ref[...])
    gate = mod_ref[...][:, 2 * D_MODEL:3 * D_MODEL]
    hn = x_ref[...] + gate * y
    if final:
        hn = _rms(hn) * fg_ref[...]
    o_ref[...] = hn


def _outproj_call(x2d, o_a, o_b, o_c, proj, mod3, wts, final_g, rows_per_mod, mod_base, final):
    rows = x2d.shape[0]
    tm = OUTPROJ_TM
    tiles_per_mod = rows_per_mod // tm
    row = lambda w, cb=0: pl.BlockSpec((tm, w), lambda i: (i, cb))
    const = lambda w: pl.BlockSpec(w.shape, lambda i, n=w.ndim: (0,) * n)
    in_specs = [row(D_MODEL), row(512), row(512), row(512),
                row(512, COL_GATES // 512), row(512, COL_GATES // 512 + 1), row(512, COL_GATES // 512 + 2),
                row(D_MODEL, COL_MERGE // D_MODEL), row(D_MODEL, COL_MERGE // D_MODEL + 1),
                row(D_MODEL, COL_MERGE // D_MODEL + 2),
                pl.BlockSpec((None, 1, 3 * D_MODEL), lambda i: (mod_base + i // tiles_per_mod, 0, 0))]
    in_specs += [const(w) for w in wts] + [const(final_g)]
    return pl.pallas_call(
        functools.partial(_outproj_kernel, final=final),
        grid=(rows // tm,),
        in_specs=in_specs,
        out_specs=row(D_MODEL),
        out_shape=jax.ShapeDtypeStruct((rows, D_MODEL), F32),
        compiler_params=pltpu.CompilerParams(vmem_limit_bytes=VMEM_LIMIT),
        name="outproj",
    )(x2d, o_a, o_b, o_c, proj, proj, proj, proj, proj, proj, mod3, *wts, final_g)


def _pack_w_in(w):
    seg = lambda i: w[:, _IN_OFF[i]:_IN_OFF[i + 1]]
    z = lambda n: jnp.zeros((D_MODEL, n), w.dtype)
    kpe = seg(13)
    q = MLA_ROPE // 4
    kpe_sw = jnp.concatenate([kpe[:, q:2 * q], kpe[:, 0:q], kpe[:, 3 * q:4 * q], kpe[:, 2 * q:3 * q]], axis=1)
    tail = LANE - KPE_LANE - MLA_ROPE
    cols = [w[:, _IN_OFF[0]:_IN_OFF[5]],
            seg(9), z(LANE - GLA_GATE_RANK),
            seg(11),
            seg(6), seg(7), seg(8),
            seg(12),
            z(KPE_LANE), kpe, z(tail),
            z(KPE_LANE), kpe_sw, z(tail),
            z(PACK_A - COL_KPE_SW - LANE)]
    part_b = [seg(15),
              seg(5), seg(10), seg(14)]
    return jnp.concatenate(cols, axis=1).astype(BF16), jnp.concatenate(part_b, axis=1).astype(BF16)


def _rope_lane_tables(seq):
    n_freq = MLA_ROPE // 4
    t = np.arange(seq)
    inv = ROPE_THETA ** (-np.arange(n_freq, dtype=np.float64) / n_freq)
    ang = np.stack([(t // GRID_W)[:, None] * inv, (t % GRID_W)[:, None] * inv], axis=1)
    cos = np.repeat(np.cos(ang)[:, :, None, :], 2, axis=2).reshape(seq, MLA_ROPE)
    sin = np.stack([-np.sin(ang), np.sin(ang)], axis=2).reshape(seq, MLA_ROPE)
    return cos, sin


def _mla_tables(seq):
    cos, sin = _rope_lane_tables(seq)
    cosq = np.zeros((seq, LANE), np.float32)
    sinq = np.zeros((seq, LANE), np.float32)
    cosq[:, :KPE_LANE] = 1.0
    cosq[:, KPE_LANE:KPE_LANE + MLA_ROPE] = cos
    sinq[:, KPE_LANE:KPE_LANE + MLA_ROPE] = sin
    cosk = np.zeros((seq, LANE), np.float32)
    cosk[:, KPE_LANE:KPE_LANE + MLA_ROPE] = cos
    epos = np.zeros((MLA_ROPE, LANE), np.float32)
    epos[np.arange(MLA_ROPE), KPE_LANE + np.arange(MLA_ROPE)] = 1.0
    return (jnp.asarray(cosq), jnp.asarray(sinq), jnp.asarray(cosk), jnp.asarray(sinq),
            jnp.asarray(epos, dtype=BF16))


def _pack_mla_weights(qn_g, wq_up, kvn_g, wkv_up):
    scale = (MLA_NOPE + MLA_ROPE) ** -0.5
    wq = wq_up.reshape(MLA_Q_RANK, MLA_HEADS, MLA_NOPE + MLA_ROPE) * scale
    nope, rope = wq[..., :MLA_NOPE], wq[..., MLA_NOPE:]
    q = MLA_ROPE // 4
    rope_sw = jnp.concatenate([rope[..., q:2 * q], rope[..., 0:q], rope[..., 3 * q:4 * q], rope[..., 2 * q:3 * q]], -1)
    tail = jnp.zeros((MLA_Q_RANK, MLA_HEADS, LANE - KPE_LANE - MLA_ROPE), wq.dtype)
    wq_cat = jnp.concatenate([nope, rope, tail], -1).reshape(MLA_Q_RANK, MLA_HEADS * LANE).astype(BF16)
    wq_sw = jnp.concatenate([jnp.zeros_like(nope), rope_sw, tail], -1).reshape(MLA_Q_RANK, MLA_HEADS * LANE).astype(BF16)
    wkv = wkv_up.reshape(MLA_KV_RANK, MLA_HEADS, MLA_NOPE + MLA_V)
    wk = jnp.concatenate([wkv[..., :MLA_NOPE], jnp.zeros((MLA_KV_RANK, MLA_HEADS, LANE - MLA_NOPE), wkv.dtype)], -1)
    wk_pad = wk.reshape(MLA_KV_RANK, MLA_HEADS * LANE).astype(BF16)
    wv = jnp.concatenate([wkv[..., MLA_NOPE:], jnp.zeros((MLA_KV_RANK, MLA_HEADS, LANE - MLA_V), wkv.dtype)], -1)
    wv = wv.reshape(MLA_KV_RANK, MLA_HEADS * LANE).astype(BF16)
    return (qn_g.reshape(1, -1), wq_cat, wq_sw, kvn_g.reshape(1, -1), wk_pad, wv)


def _head_sum_matrix():
    lane = np.arange(RW_WIDTH)
    return jnp.asarray((lane[:, None] // RW_HEAD == lane[None, :] // RW_HEAD).astype(np.float32), dtype=BF16)


def _trunk(x_prompt, x_sample, c, cache_ckv, cache_kpe, state_rwkv, state_gla, c_ctx,
           norm_g, w_mod, b_mod, w_in, rw_w0, rw_w2, rw_a0, rw_a2, rw_k_k, rw_k_a, rw_r_k,
           rw_ln_g, rw_ln_b, rw_out, gla_a2, gla_ab, gla_norm_g, gla_out,
           mla_qn_g, mla_wq_up, mla_kvn_g, mla_wkv_up, mla_out, w_out, final_g):
    bp, tp, _ = x_prompt.shape
    bs, ts, _ = x_sample.shape
    hp = x_prompt.reshape(bp * tp, D_MODEL)
    hs = x_sample.reshape(bs * ts, D_MODEL)
    cvec8 = jnp.concatenate([c, c_ctx[None, :], jnp.zeros((8 - bs - 1, D_MODEL), F32)], axis=0)
    ctx_row = bs
    hsum = _head_sum_matrix()
    mexp, bmask = _gla_constants()
    mexp = jnp.asarray(mexp, dtype=BF16)
    bmask = jnp.asarray(bmask)
    tables = _mla_tables(ts)
    fg = final_g.reshape(1, D_MODEL)
    ckv_l, kpe_l, rw_l, gla_l = [], [], [], []
    for l in range(DEPTH):
        mod3 = _mod_call(cvec8, w_mod, b_mod, l).reshape(8, 1, 3 * D_MODEL)
        w_a, w_b = _pack_w_in(w_in[l])
        ng = norm_g[l].reshape(1, D_MODEL)
        row = lambda a: a.reshape(1, -1)
        zr = jnp.zeros((2, RW_RANK, RW_WIDTH), F32)
        wcat = jnp.concatenate([jnp.concatenate([rw_w2[l], zr], axis=2),
                                jnp.concatenate([zr, rw_a2[l]], axis=2)], axis=1).astype(BF16)
        rw_wts = (wcat, jnp.concatenate([rw_w0[l], rw_a0[l]], axis=1), row(rw_k_k[l]), row(rw_k_a[l]),
                  row(rw_r_k[l]), row(rw_ln_g[l]), row(rw_ln_b[l]), hsum)
        a2p = jnp.concatenate([gla_a2[l], jnp.zeros((2, LANE - GLA_GATE_RANK, GLA_KW), F32)], axis=1).astype(BF16)
        gla_wts = (a2p, gla_ab[l], row(gla_norm_g[l]), mexp, bmask)
        mla_wts = _pack_mla_weights(mla_qn_g[l], mla_wq_up[l], mla_kvn_g[l], mla_wkv_up[l])
        out_wts = (rw_out[l].astype(BF16), gla_out[l].astype(BF16), mla_out[l].astype(BF16), w_out[l].astype(BF16))
        final = l == DEPTH - 1

        proj, proj_b = _inproj_call(hp, mod3, ng, w_a, w_b, bp * tp, ctx_row)
        o_a, s_rw = _rwkv_call(proj, None, l, bp, tp, rw_wts)
        o_b, s_gla = _gla_call(proj, None, l, bp, tp, gla_wts)
        o_c, ckv = _mla_call(proj, None, None, l, bp, tp, mla_wts, None)
        ckv_l.append(ckv.reshape(bp, tp, MLA_KV_RANK))
        kpe_l.append(proj[:, COL_KPE + KPE_LANE:COL_KPE + KPE_LANE + MLA_ROPE].astype(F32).reshape(bp, tp, MLA_ROPE))
        rw_l.append(s_rw)
        gla_l.append(s_gla)
        hp = _outproj_call(hp, o_a, o_b, o_c, proj_b, mod3, out_wts, fg, bp * tp, ctx_row, final)

        proj, proj_b = _inproj_call(hs, mod3, ng, w_a, w_b, ts, 0)
        o_a, _ = _rwkv_call(proj, state_rwkv, l, bs, ts, rw_wts)
        o_b, _ = _gla_call(proj, state_gla, l, bs, ts, gla_wts)
        o_c, _ = _mla_call(proj, cache_ckv, cache_kpe, l, bs, ts, mla_wts, tables)
        hs = _outproj_call(hs, o_a, o_b, o_c, proj_b, mod3, out_wts, fg, ts, 0, final)

    return (hp.reshape(bp, tp, D_MODEL), hs.reshape(bs, ts, D_MODEL),
            jnp.stack(ckv_l, axis=1), jnp.stack(kpe_l, axis=1),
            jnp.stack(rw_l, axis=1), jnp.stack(gla_l, axis=1))


_trunk_jit = jax.jit(_trunk)


def kernel(x_prompt, x_sample, c, cache_ckv, cache_kpe, state_rwkv, state_gla, c_ctx, norm_g, w_mod, b_mod, w_in, rw_w0, rw_w2, rw_a0, rw_a2, rw_k_k, rw_k_a, rw_r_k, rw_ln_g, rw_ln_b, rw_out, gla_a2, gla_ab, gla_norm_g, gla_out, mla_qn_g, mla_wq_up, mla_kvn_g, mla_wkv_up, mla_out, w_out, final_g):
    return _trunk_jit(x_prompt, x_sample, c, cache_ckv, cache_kpe, state_rwkv, state_gla, c_ctx, norm_g, w_mod, b_mod, w_in, rw_w0, rw_w2, rw_a0, rw_a2, rw_k_k, rw_k_a, rw_r_k, rw_ln_g, rw_ln_b, rw_out, gla_a2, gla_ab, gla_norm_g, gla_out, mla_qn_g, mla_wq_up, mla_kvn_g, mla_wkv_up, mla_out, w_out, final_g)
```

```python
import functools

import numpy as np
import jax
import jax.numpy as jnp
from jax import lax
from jax.experimental import pallas as pl
from jax.experimental.pallas import tpu as pltpu

F32 = jnp.float32
BF16 = jnp.bfloat16

D_MODEL = 1024
DEPTH = 2
GRID_W = 64
NORM_EPS = 1e-6
RW_HEADS = 8
RW_HEAD = 64
RW_WIDTH = RW_HEADS * RW_HEAD
RW_RANK = 64
RW_GN_EPS = 64e-5
GLA_HEADS = 4
GLA_DK = 64
GLA_DV = 128
GLA_KW = GLA_HEADS * GLA_DK
GLA_VW = GLA_HEADS * GLA_DV
GLA_GATE_RANK = 16
GLA_LOGIT_NORM = 16.0
MLA_HEADS = 8
MLA_NOPE = 64
MLA_ROPE = 32
MLA_V = 64
MLA_Q_RANK = 256
MLA_KV_RANK = 128
MLA_VW = MLA_HEADS * MLA_V
ROPE_THETA = 10000.0
N_BRANCH = 3

_IN_SIZES = (RW_WIDTH, RW_WIDTH, RW_WIDTH, RW_RANK, RW_RANK, RW_WIDTH,
             GLA_KW, GLA_KW, GLA_VW, GLA_GATE_RANK, GLA_VW,
             MLA_Q_RANK, MLA_KV_RANK, MLA_ROPE, MLA_VW, N_BRANCH * D_MODEL)
_IN_OFF = tuple(int(v) for v in np.concatenate([[0], np.cumsum(_IN_SIZES)]))

LANE = 128
COL_RW_R = 0
COL_RW_K = 512
COL_RW_V = 1024
COL_RW_WA = 1536
COL_GL_AD = 1664
COL_ML_QD = 1792
COL_GL_Q = 2048
COL_GL_K = 2304
COL_GL_V = 2560
COL_ML_KVD = 3072
COL_KPE = 3200
COL_KPE_SW = 3328
PACK_A = 3584
COL_MERGE = 0
COL_GATES = 3072
PACK_B = 4608
KPE_LANE = MLA_NOPE

CHUNK = 64
INPROJ_TM = 512
INPROJ_SUB = 256
INPROJ_TN = 512
OUTPROJ_TM = 512
MLA_TQ = 1024
VMEM_LIMIT = 48 * 1024 * 1024


def _dot(a, b, prec=None):
    return jnp.dot(a, b, preferred_element_type=F32, precision=prec)


def _dot_nt(a, b, prec=None):
    return lax.dot_general(a, b, (((1,), (1,)), ((), ())), preferred_element_type=F32, precision=prec)


def _dot_tn(a, b, prec=None):
    return lax.dot_general(a, b, (((0,), (0,)), ((), ())), preferred_element_type=F32, precision=prec)


def _split(x):
    hi = x.astype(BF16)
    return hi, (x - hi.astype(F32)).astype(BF16)


def _dot_split(a, b):
    ah, al = _split(a)
    bh, bl = _split(b)
    return _dot(ah, bh) + _dot(al, bh) + _dot(ah, bl)


def _dot_sel(sel16, x):
    xh, xl = _split(x)
    return _dot(sel16, xh) + _dot(sel16, xl)


def _sigmoid(x):
    return 0.5 * jnp.tanh(0.5 * x) + 0.5


def _rms(x, eps=NORM_EPS):
    return x * lax.rsqrt(jnp.mean(x * x, axis=-1, keepdims=True) + eps)


def _mod_kernel(c_ref, w_ref, b_ref, o_ref):
    c = c_ref[...]
    o_ref[...] = _dot_split(c * _sigmoid(c), w_ref[...]) + b_ref[...]


def _mod_call(cvec8, w_mod, b_mod, layer):
    tn = 1024
    return pl.pallas_call(
        _mod_kernel,
        grid=(3 * D_MODEL // tn,),
        in_specs=[pl.BlockSpec((8, D_MODEL), lambda j: (0, 0)),
                  pl.BlockSpec((None, D_MODEL, tn), lambda j: (layer, 0, j)),
                  pl.BlockSpec((None, 1, tn), lambda j: (layer, 0, j))],
        out_specs=pl.BlockSpec((8, tn), lambda j: (0, j)),
        out_shape=jax.ShapeDtypeStruct((8, 3 * D_MODEL), F32),
        compiler_params=pltpu.CompilerParams(vmem_limit_bytes=VMEM_LIMIT),
        name="mod",
    )(cvec8, w_mod, b_mod.reshape(DEPTH, 1, 3 * D_MODEL))


def _inproj_kernel(x_ref, mod_ref, g_ref, wa_ref, wb_ref, oa_ref, ob_ref):
    m = mod_ref[...]
    shift = m[:, 0:D_MODEL]
    scale1 = 1.0 + m[:, D_MODEL:2 * D_MODEL]
    tm = x_ref.shape[0]
    for r0 in range(0, tm, INPROJ_SUB):
        rows = slice(r0, r0 + INPROJ_SUB)
        h = (_rms(x_ref[rows, :]) * g_ref[...] * scale1 + shift).astype(BF16)
        for w_ref, o_ref in ((wa_ref, oa_ref), (wb_ref, ob_ref)):
            for c0 in range(0, w_ref.shape[1], INPROJ_TN):
                cols = slice(c0, c0 + INPROJ_TN)
                o_ref[rows, cols] = _dot(h, w_ref[:, cols]).astype(BF16)


def _inproj_call(x2d, mod3, norm_g, w_a, w_b, layer, rows_per_mod, mod_base):
    rows = x2d.shape[0]
    tm = INPROJ_TM
    tiles_per_mod = rows_per_mod // tm
    resident = lambda w: pl.BlockSpec((None,) + w.shape[1:], lambda i: (layer, 0, 0), pipeline_mode=pl.Buffered(1))
    return pl.pallas_call(
        _inproj_kernel,
        grid=(rows // tm,),
        in_specs=[pl.BlockSpec((tm, D_MODEL), lambda i: (i, 0)),
                  pl.BlockSpec((None, 1, 3 * D_MODEL), lambda i: (mod_base + i // tiles_per_mod, 0, 0)),
                  pl.BlockSpec((1, D_MODEL), lambda i: (0, 0)),
                  resident(w_a), resident(w_b)],
        out_specs=[pl.BlockSpec((tm, PACK_A), lambda i: (i, 0)),
                   pl.BlockSpec((tm, PACK_B), lambda i: (i, 0))],
        out_shape=[jax.ShapeDtypeStruct((rows, PACK_A), BF16),
                   jax.ShapeDtypeStruct((rows, PACK_B), BF16)],
        compiler_params=pltpu.CompilerParams(vmem_limit_bytes=VMEM_LIMIT),
        name="inproj",
    )(x2d, mod3, norm_g, w_a, w_b)


RW_CHUNK = 64
RW_PAIRS = RW_HEADS // 2
RW_BLOCK_ROWS = 256
RW_NEG_DECAY_SCALE = -float(np.exp(-0.5))


def _rwkv_time_mask(c, reverse):
    row = lax.broadcasted_iota(jnp.int32, (2 * c, 4 * c), 0)
    col = lax.broadcasted_iota(jnp.int32, (2 * c, 4 * c), 1)
    t = jnp.where(row >= c, row - c, row)
    s = col & (c - 1)
    earlier = (s > t) if reverse else (s < t)
    return earlier | ((row >= c) & (s == t))


def _rwkv_kernel(*refs, seq, has_s0):
    if has_s0:
        r_ref, k_ref, v_ref, wa_ref, s0_ref = refs[:5]
        rest = refs[5:]
    else:
        r_ref, k_ref, v_ref, wa_ref = refs[:4]
        s0_ref = None
        rest = refs[4:]
    (wcat_ref, wa0_ref, kk_ref, ka_ref, rk_ref, lng_ref, lnb_ref, hsum_ref,
     o_ref, sout_ref, acc_ref, kn_ref, st_ref) = rest
    c = RW_CHUNK
    nc = seq // c
    blk = min(RW_BLOCK_ROWS, seq)
    hsum = hsum_ref[...]

    def pre_step(i, carry):
        rows = pl.ds(pl.multiple_of(i * blk, blk), blk)
        kk0 = k_ref[rows, :].astype(F32) * kk_ref[...]
        kn_ref[rows, :] = kk0 / jnp.maximum(jnp.sqrt(_dot((kk0 * kk0).astype(BF16), hsum)), 1e-12)
        return carry

    def post_step(i, carry):
        rows = pl.ds(pl.multiple_of(i * blk, blk), blk)
        o = acc_ref[0, rows, :] + acc_ref[1, rows, :]
        rk = r_ref[rows, :].astype(F32) * k_ref[rows, :].astype(F32) * rk_ref[...]
        red = _dot(jnp.concatenate([o, rk], axis=0).astype(BF16), hsum)
        dev = o - red[0:blk] * (1.0 / RW_HEAD)
        var = _dot((dev * dev).astype(BF16), hsum) * (1.0 / RW_HEAD)
        o = dev * lax.rsqrt(var + RW_GN_EPS) * lng_ref[...] + lnb_ref[...]
        o_ref[rows, :] = (o + red[blk:2 * blk] * v_ref[rows, :]).astype(o_ref.dtype)
        return carry

    even = lax.broadcasted_iota(jnp.int32, (1, LANE), 1) < RW_HEAD
    row = lax.broadcasted_iota(jnp.int32, (c, c), 0)
    col = lax.broadcasted_iota(jnp.int32, (c, c), 1)
    incl16 = [jnp.where(col <= row, 1.0, 0.0).astype(BF16), jnp.where(col >= row, 1.0, 0.0).astype(BF16)]
    tmask = [_rwkv_time_mask(c, False), _rwkv_time_mask(c, True)]
    r2 = lax.broadcasted_iota(jnp.int32, (LANE, LANE), 0)
    c2 = lax.broadcasted_iota(jnp.int32, (LANE, LANE), 1)
    same_head = (r2 < RW_HEAD) == (c2 < RW_HEAD)
    z16 = jnp.zeros((c, LANE), BF16)
    lanes = [slice(p * LANE, (p + 1) * LANE) for p in range(RW_PAIRS)]
    probs = [(d, p) for d in range(2) for p in range(RW_PAIRS)]

    def chunk_operands(d, ci):
        rows = pl.ds(pl.multiple_of(ci * c, c), c)
        r = r_ref[rows, :].astype(F32)
        k = k_ref[rows, :].astype(F32)
        wa = wa_ref[rows, :].astype(F32)
        kk = kn_ref[rows, :]
        lora = jnp.where(even, jnp.tanh(wa), wa)
        pre = _dot(lora.astype(BF16), wcat_ref[d]) + wa0_ref[d:d + 1, :]
        lw = RW_NEG_DECAY_SCALE * _sigmoid(pre[:, 0:RW_WIDTH])
        a = _sigmoid(pre[:, RW_WIDTH:2 * RW_WIDTH])
        kd = k * (1.0 + (a - 1.0) * ka_ref[...])
        bv = kk * a
        gam = _dot_sel(incl16[d], lw)
        last = 0 if d == 1 else c - 1
        gtot = gam[last:last + 1, :]
        gref = gam[c // 2:c // 2 + 1, :]
        e_in = jnp.exp(gam - gref)
        e_out = jnp.exp(gref - gam)
        e_end = e_out * jnp.exp(gtot - gref)
        lhs = jnp.concatenate([-kk * jnp.exp(gam - lw - gref), r * e_in], axis=0).astype(BF16)
        return dict(rows=rows, lhs=lhs, b16=(bv * e_out).astype(BF16), k16=(kd * e_out).astype(BF16),
                    be16=(bv * e_end).astype(BF16), ke16=(kd * e_end).astype(BF16),
                    v16=v_ref[rows, :].astype(BF16), dtot=jnp.exp(gtot), eref=jnp.exp(gref))

    def main_step(i, carry):
        ops = [chunk_operands(0, i), chunk_operands(1, nc - 1 - i)]
        n = range(len(probs))
        pick = lambda name: [ops[d][name][:, lanes[p]] for d, p in probs]
        lhs, b, k, be, ke, v = (pick(s) for s in ("lhs", "b16", "k16", "be16", "ke16", "v16"))
        st = [st_ref[d, p] for d, p in probs]
        first = lambda x: jnp.where(even, x, jnp.zeros_like(x))
        second = lambda x: jnp.where(even, jnp.zeros_like(x), x)
        v_e = [first(x) for x in v]
        v_o = [second(x) for x in v]
        aa = [jnp.where(tmask[d], _dot_nt(lhs[j], jnp.concatenate(
            [first(k[j]), first(b[j]), second(b[j]), second(k[j])], axis=0)), 0.0) for j, (d, p) in enumerate(probs)]
        sp = [_dot_nt(lhs[j], (st[j] * ops[d]["eref"][:, lanes[p]]).astype(BF16))
              for j, (d, p) in enumerate(probs)]
        aa16 = [x.astype(BF16) for x in aa]
        rhs_u = [_dot(aa16[j][0:c], jnp.concatenate([v_e[j], z16, z16, v_o[j]], axis=0)) + sp[j][0:c] for j in n]
        x_e = [jnp.where(even, rhs_u[j], aa[j][0:c, 0:LANE]) for j in n]
        x_o = [jnp.where(even, aa[j][0:c, LANE:2 * LANE], rhs_u[j]) for j in n]
        covered = 1
        while covered < c:
            e16 = [x.astype(BF16) for x in x_e]
            o16 = [x.astype(BF16) for x in x_o]
            x_e = [_dot(e16[j], jnp.concatenate([z16, e16[j]], axis=0)) + first(x_e[j]) for j in n]
            x_o = [_dot(o16[j], jnp.concatenate([o16[j], z16], axis=0)) + second(x_o[j]) for j in n]
            covered *= 2
        u16 = [jnp.where(even, x_e[j], x_o[j]).astype(BF16) for j in n]
        for j, (d, p) in enumerate(probs):
            uv_rows = jnp.concatenate([v_e[j], first(u16[j]), second(u16[j]), v_o[j]], axis=0)
            acc_ref[d, ops[d]["rows"], lanes[p]] = sp[j][c:2 * c] + _dot(aa16[j][c:2 * c], uv_rows)
            upd = _dot_tn(jnp.concatenate([u16[j], v[j]], axis=0), jnp.concatenate([be[j], ke[j]], axis=0))
            st_ref[d, p] = st[j] * ops[d]["dtot"][:, lanes[p]] + jnp.where(same_head, upd, 0.0)
        return carry

    st_ref[...] = jnp.zeros_like(st_ref)
    if has_s0:
        for d in range(2):
            for h in range(RW_HEADS):
                off = (h % 2) * RW_HEAD
                st_ref[d, h // 2, off:off + RW_HEAD, off:off + RW_HEAD] = s0_ref[d, h]
    lax.fori_loop(0, seq // blk, pre_step, 0)
    lax.fori_loop(0, nc, main_step, 0)
    for d in range(2):
        for h in range(RW_HEADS):
            off = (h % 2) * RW_HEAD
            sout_ref[d, h] = st_ref[d, h // 2, off:off + RW_HEAD, off:off + RW_HEAD]
    lax.fori_loop(0, seq // blk, post_step, 0)


def _rwkv_call(proj, s0, layer, batch, seq, wts):
    has_s0 = s0 is not None
    blk = lambda w, cb: pl.BlockSpec((seq, w), lambda b: (b, cb))
    in_specs = [blk(512, COL_RW_R // 512), blk(512, COL_RW_K // 512), blk(512, COL_RW_V // 512),
                blk(LANE, COL_RW_WA // LANE)]
    args = [proj, proj, proj, proj]
    if has_s0:
        in_specs.append(pl.BlockSpec((None, None, 2, RW_HEADS, RW_HEAD, RW_HEAD),
                                     lambda b: (b, layer, 0, 0, 0, 0)))
        args.append(s0)
    for w in wts:
        in_specs.append(pl.BlockSpec(w.shape, lambda b, n=w.ndim: (0,) * n))
        args.append(w)
    return pl.pallas_call(
        functools.partial(_rwkv_kernel, seq=seq, has_s0=has_s0),
        grid=(batch,),
        in_specs=in_specs,
        out_specs=[pl.BlockSpec((seq, RW_WIDTH), lambda b: (b, 0)),
                   pl.BlockSpec((None, 2, RW_HEADS, RW_HEAD, RW_HEAD), lambda b: (b, 0, 0, 0, 0))],
        out_shape=[jax.ShapeDtypeStruct((batch * seq, RW_WIDTH), BF16),
                   jax.ShapeDtypeStruct((batch, 2, RW_HEADS, RW_HEAD, RW_HEAD), F32)],
        scratch_shapes=[pltpu.VMEM((2, seq, RW_WIDTH), F32),
                        pltpu.VMEM((seq, RW_WIDTH), F32),
                        pltpu.VMEM((2, RW_PAIRS, LANE, LANE), F32)],
        compiler_params=pltpu.CompilerParams(vmem_limit_bytes=VMEM_LIMIT),
        name="rwkv",
    )(*args)


GLA_LEVELS = (32, 16, 8, 4, 2, 1)
GLA_GROUP = 4


def _gla_constants():
    c = CHUNK
    nl = len(GLA_LEVELS)
    mexp = np.zeros((2, (nl + 1) * c, c), np.float32)
    bmask = np.zeros((2, nl + 1, c, c), np.float32)
    for d in range(2):
        pos = np.arange(c) if d == 0 else c - 1 - np.arange(c)
        pt = pos[:, None]
        pj = pos[None, :]
        mexp[d, nl * c:] = (pj <= pt)
        for li, m in enumerate(GLA_LEVELS):
            mid = (pos // (2 * m)) * (2 * m) + m
            second = pos >= mid
            mq = (pj >= mid[:, None]) & (pj <= pt) & second[:, None]
            mk = (pj > pt) & (pj <= mid[:, None] - 1) & (~second)[:, None]
            mexp[d, li * c:(li + 1) * c] = mq | mk
            same = (pos[:, None] // (2 * m)) == (pos[None, :] // (2 * m))
            bmask[d, li] = same & second[:, None] & (~second)[None, :]
        bmask[d, nl] = np.eye(c)
    return mexp, np.concatenate([bmask, bmask], axis=-1)


def _gla_kernel(*refs, seq, has_s0):
    if has_s0:
        q_ref, k_ref, v_ref, ad_ref, s0_ref = refs[:5]
        rest = refs[5:]
    else:
        q_ref, k_ref, v_ref, ad_ref = refs[:4]
        s0_ref = None
        rest = refs[4:]
    (a2_ref, ab_ref, ng_ref, mexp_ref, bmask_ref,
     o_ref, sout_ref, acc_ref, st_ref) = rest
    c = CHUNK
    nc = seq // c
    nl = len(GLA_LEVELS)

    def chunk_operands(d, ci):
        rows = pl.ds(pl.multiple_of(ci * c, c), c)
        q = q_ref[rows, :].astype(F32) * (GLA_DK ** -0.5)
        k = k_ref[rows, :].astype(F32)
        v = v_ref[rows, :]
        x = _dot(ad_ref[rows, :].astype(BF16), a2_ref[d]) + ab_ref[d:d + 1, :]
        g = (jnp.minimum(x, 0.0) - jnp.log1p(jnp.exp(-jnp.abs(x)))) * (1.0 / GLA_LOGIT_NORM)
        g_hi, g_lo = _split(g)
        sums = _dot(mexp_ref[d], g_hi)
        ex = jnp.exp(sums[0:nl * c])
        b = sums[nl * c:(nl + 1) * c] + _dot(mexp_ref[d, nl * c:(nl + 1) * c, :], g_lo)
        last = 0 if d == 1 else c - 1
        blast = b[last:last + 1, :]
        qb = (q * jnp.exp(b)).astype(BF16)
        kdec = (k * jnp.exp(blast - b)).astype(BF16)
        dtot = jnp.exp(blast)
        qs = [(q * ex[li * c:(li + 1) * c]).astype(BF16) for li in range(nl)] + [q.astype(BF16)]
        ks = [(k * ex[li * c:(li + 1) * c]).astype(BF16) for li in range(nl)] + [k.astype(BF16)]
        return dict(rows=rows, qs=qs, ks=ks, qb=qb, kdec=kdec, dtot=dtot, v16=v.astype(BF16))

    even = lax.broadcasted_iota(jnp.int32, (1, LANE), 1) < GLA_DK
    r2 = lax.broadcasted_iota(jnp.int32, (2 * GLA_DV, LANE), 0)
    c2 = lax.broadcasted_iota(jnp.int32, (2 * GLA_DV, LANE), 1)
    same_head = (r2 < GLA_DV) == (c2 < GLA_DK)
    zv = jnp.zeros((c, GLA_DV), BF16)
    npair = GLA_HEADS // 2
    kls = [slice(p * LANE, (p + 1) * LANE) for p in range(npair)]
    vls = [slice(p * 2 * GLA_DV, (p + 1) * 2 * GLA_DV) for p in range(npair)]
    probs = [(d, p) for d in range(2) for p in range(npair)]
    first = lambda x: jnp.where(even, x, jnp.zeros_like(x))
    second = lambda x: jnp.where(even, jnp.zeros_like(x), x)

    group = min(GLA_GROUP, nc)

    def main_step(i, carry):
        ops = [[chunk_operands(0, i * group + g) for g in range(group)],
               [chunk_operands(1, nc - 1 - (i * group + g)) for g in range(group)]]
        allp = [(d, p, g) for d, p in probs for g in range(group)]
        lvl = [[_dot_nt(ops[d][g]["qs"][li][:, kls[p]],
                        jnp.concatenate([first(ops[d][g]["ks"][li][:, kls[p]]),
                                         second(ops[d][g]["ks"][li][:, kls[p]])], axis=0))
                for d, p, g in allp] for li in range(nl + 1)]
        att = [sum(bmask_ref[d, li] * lvl[li][j] for li in range(nl + 1)).astype(BF16)
               for j, (d, p, g) in enumerate(allp)]
        upd = [jnp.where(same_head, _dot_tn(ops[d][g]["v16"][:, vls[p]], ops[d][g]["kdec"][:, kls[p]]), 0.0)
               for d, p, g in allp]
        states = []
        for d, p in probs:
            st = st_ref[d, p]
            for g in range(group):
                states.append(st)
                st = st * ops[d][g]["dtot"][:, kls[p]] + upd[len(states) - 1]
            st_ref[d, p] = st
        for j, (d, p, g) in enumerate(allp):
            v_p = ops[d][g]["v16"][:, vls[p]]
            v_bd = jnp.concatenate([jnp.concatenate([v_p[:, 0:GLA_DV], zv], axis=1),
                                    jnp.concatenate([zv, v_p[:, GLA_DV:2 * GLA_DV]], axis=1)], axis=0)
            inter = _dot_nt(ops[d][g]["qb"][:, kls[p]], states[j].astype(BF16))
            acc_ref[d, ops[d][g]["rows"], vls[p]] = inter + _dot(att[j], v_bd)
        return carry

    blk = min(RW_BLOCK_ROWS, seq)

    def post_step(i, carry):
        rows = pl.ds(pl.multiple_of(i * blk, blk), blk)
        for h in range(GLA_HEADS):
            sv = slice(h * GLA_DV, (h + 1) * GLA_DV)
            o_ref[rows, sv] = (_rms(acc_ref[0, rows, sv] + acc_ref[1, rows, sv]) * ng_ref[...]).astype(o_ref.dtype)
        return carry

    st_ref[...] = jnp.zeros_like(st_ref)
    if has_s0:
        for d in range(2):
            for h in range(GLA_HEADS):
                ro, co = (h % 2) * GLA_DV, (h % 2) * GLA_DK
                st_ref[d, h // 2, ro:ro + GLA_DV, co:co + GLA_DK] = s0_ref[d, h].T
    lax.fori_loop(0, nc // group, main_step, 0)
    for d in range(2):
        for h in range(GLA_HEADS):
            ro, co = (h % 2) * GLA_DV, (h % 2) * GLA_DK
            sout_ref[d, h] = st_ref[d, h // 2, ro:ro + GLA_DV, co:co + GLA_DK].T
    lax.fori_loop(0, seq // blk, post_step, 0)


def _gla_call(proj, s0, layer, batch, seq, wts):
    has_s0 = s0 is not None
    blk = lambda w, cb: pl.BlockSpec((seq, w), lambda b: (b, cb))
    in_specs = [blk(GLA_KW, COL_GL_Q // GLA_KW), blk(GLA_KW, COL_GL_K // GLA_KW),
                blk(GLA_VW, COL_GL_V // GLA_VW), blk(LANE, COL_GL_AD // LANE)]
    args = [proj, proj, proj, proj]
    if has_s0:
        in_specs.append(pl.BlockSpec((None, None, 2, GLA_HEADS, GLA_DK, GLA_DV),
                                     lambda b: (b, layer, 0, 0, 0, 0)))
        args.append(s0)
    for w in wts:
        in_specs.append(pl.BlockSpec(w.shape, lambda b, n=w.ndim: (0,) * n))
        args.append(w)
    return pl.pallas_call(
        functools.partial(_gla_kernel, seq=seq, has_s0=has_s0),
        grid=(batch,),
        in_specs=in_specs,
        out_specs=[pl.BlockSpec((seq, GLA_VW), lambda b: (b, 0)),
                   pl.BlockSpec((None, 2, GLA_HEADS, GLA_DK, GLA_DV), lambda b: (b, 0, 0, 0, 0))],
        out_shape=[jax.ShapeDtypeStruct((batch * seq, GLA_VW), BF16),
                   jax.ShapeDtypeStruct((batch, 2, GLA_HEADS, GLA_DK, GLA_DV), F32)],
        scratch_shapes=[pltpu.VMEM((2, seq, GLA_VW), F32),
                        pltpu.VMEM((2, GLA_HEADS // 2, 2 * GLA_DV, LANE), F32)],
        compiler_params=pltpu.CompilerParams(vmem_limit_bytes=VMEM_LIMIT),
        name="gla",
    )(*args)


def _mla_kernel(*refs, seq, past, tq):
    has_ctx = past > 0
    if has_ctx:
        (qd_ref, kvd_ref, kpe_ref, kpesw_ref, cckv_ref, ckpe_ref,
         qng_ref, wq_ref, wqsw_ref, kvng_ref, wk_ref, wv_ref,
         cosq_ref, sinq_ref, cosk_ref, sink_ref, epos_ref,
         o_ref, ckv_ref, kcat_ref, vv_ref) = refs
    else:
        (qd_ref, kvd_ref, kpe_ref,
         qng_ref, wq_ref, kvng_ref, wk_ref, wv_ref,
         o_ref, ckv_ref, kcat_ref, vv_ref) = refs

    ones_hi = jnp.where(lax.broadcasted_iota(jnp.int32, (1, LANE), 1) >= MLA_V, 1.0, 0.0)

    @pl.when(pl.program_id(1) == 0)
    def _():
        ckv = _rms(kvd_ref[...].astype(F32)) * kvng_ref[...]
        ckv_ref[...] = ckv
        if has_ctx:
            kpos = kpe_ref[...] * cosk_ref[...] + kpesw_ref[...] * sink_ref[...]
        else:
            kpos = kpe_ref[...]
        segs = [(0, seq, ckv, kpos)]
        if has_ctx:
            cpos = _dot(ckpe_ref[...].astype(BF16), epos_ref[...])
            segs.append((seq, past, cckv_ref[...], cpos))
        for start, n, lat, pos in segs:
            lat16 = lat.astype(BF16)
            kc = _dot(lat16, wk_ref[...])
            vc = _dot(lat16, wv_ref[...])
            for h in range(MLA_HEADS):
                hl = slice(h * LANE, (h + 1) * LANE)
                kcat_ref[h, start:start + n, :] = (kc[:, hl] + pos).astype(BF16)
                vv_ref[h, start:start + n, :] = (vc[:, hl] + ones_hi).astype(BF16)

    qlat = (_rms(qd_ref[...].astype(F32)) * qng_ref[...]).astype(BF16)
    qc = _dot(qlat, wq_ref[...])
    if has_ctx:
        qsw = _dot(qlat, wqsw_ref[...])
    for h in range(MLA_HEADS):
        hl = slice(h * LANE, (h + 1) * LANE)
        q_raw = qc[:, hl]
        if has_ctx:
            q_self = (q_raw * cosq_ref[...] + qsw[:, hl] * sinq_ref[...]).astype(BF16)
        else:
            q_self = q_raw.astype(BF16)
        s1 = _dot_nt(q_self, kcat_ref[h, 0:seq, :])
        m = jnp.max(s1, axis=-1, keepdims=True)
        if has_ctx:
            s2 = _dot_nt(q_raw.astype(BF16), kcat_ref[h, seq:seq + past, :])
            m = jnp.maximum(m, jnp.max(s2, axis=-1, keepdims=True))
        o_h = _dot(jnp.exp((s1 - m).astype(BF16)), vv_ref[h, 0:seq, :])
        if has_ctx:
            o_h = o_h + _dot(jnp.exp((s2 - m).astype(BF16)), vv_ref[h, seq:seq + past, :])
        o_h = o_h / o_h[:, MLA_V:MLA_V + 1]
        o_ref[:, h * MLA_V:(h + 1) * MLA_V] = o_h[:, 0:MLA_V].astype(o_ref.dtype)


def _mla_call(proj, cache_ckv, cache_kpe, layer, batch, seq, wts, tables):
    has_ctx = cache_ckv is not None
    past = cache_ckv.shape[2] if has_ctx else 0
    tq = min(MLA_TQ, seq)
    nq = seq // tq
    full = lambda cb: pl.BlockSpec((seq, LANE), lambda b, i: (b, cb))
    in_specs = [pl.BlockSpec((tq, MLA_Q_RANK), lambda b, i: (b * nq + i, COL_ML_QD // MLA_Q_RANK)),
                full(COL_ML_KVD // LANE), full(COL_KPE // LANE)]
    args = [proj, proj, proj]
    if has_ctx:
        in_specs += [full(COL_KPE_SW // LANE),
                     pl.BlockSpec((None, None, past, MLA_KV_RANK), lambda b, i: (b, layer, 0, 0)),
                     pl.BlockSpec((None, None, past, MLA_ROPE), lambda b, i: (b, layer, 0, 0))]
        args += [proj, cache_ckv, cache_kpe]
    qn_g, wq_cat, wq_sw, kvn_g, wk_pad, wv = wts
    const = lambda w: pl.BlockSpec(w.shape, lambda b, i, n=w.ndim: (0,) * n)
    if has_ctx:
        cosq, sinq, cosk, sink, epos = tables
        wlist = [qn_g, wq_cat, wq_sw, kvn_g, wk_pad, wv]
        in_specs += [const(w) for w in wlist]
        in_specs += [pl.BlockSpec((tq, LANE), lambda b, i: (i, 0)), pl.BlockSpec((tq, LANE), lambda b, i: (i, 0)),
                     const(cosk), const(sink), const(epos)]
        args += wlist + [cosq, sinq, cosk, sink, epos]
    else:
        wlist = [qn_g, wq_cat, kvn_g, wk_pad, wv]
        in_specs += [const(w) for w in wlist]
        args += wlist
    return pl.pallas_call(
        functools.partial(_mla_kernel, seq=seq, past=past, tq=tq),
        grid=(batch, nq),
        in_specs=in_specs,
        out_specs=[pl.BlockSpec((tq, MLA_VW), lambda b, i: (b * nq + i, 0)),
                   pl.BlockSpec((seq, MLA_KV_RANK), lambda b, i: (b, 0))],
        out_shape=[jax.ShapeDtypeStruct((batch * seq, MLA_VW), BF16),
                   jax.ShapeDtypeStruct((batch * seq, MLA_KV_RANK), F32)],
        scratch_shapes=[pltpu.VMEM((MLA_HEADS, seq + past, LANE), BF16),
                        pltpu.VMEM((MLA_HEADS, seq + past, LANE), BF16)],
        compiler_params=pltpu.CompilerParams(vmem_limit_bytes=VMEM_LIMIT,
                                             dimension_semantics=("arbitrary", "arbitrary")),
        name="mla",
    )(*args)


def _outproj_kernel(x_ref, oa_ref, ob_ref, oc_ref, ga_ref, gb_ref, gc_ref, ma_ref, mb_ref, mc_ref,
                    mod_ref, wa_ref, wb_ref, wc_ref, wo_ref, fg_ref, o_ref, *, final):
    def branch(o_r, g_r, m_r, w_r):
        h = 0.5 * g_r[...]
        act = o_r[...] * (h * (1.0 + jnp.tanh(h)))
        y = _dot(act, w_r[...]).astype(BF16)
        return (0.5 * (1.0 + jnp.tanh(0.5 * m_r[...]))) * y

    y = (branch(oa_ref, ga_ref, ma_ref, wa_ref) + branch(ob_ref, gb_ref, mb_ref, wb_ref)
         + branch(oc_ref, gc_ref, mc_ref, wc_ref))
    y = _dot(y, wo_ref[...])
    gate = mod_ref[...][:, 2 * D_MODEL:3 * D_MODEL]
    hn = x_ref[...] + gate * y
    if final:
        hn = _rms(hn) * fg_ref[...]
    o_ref[...] = hn


def _outproj_call(x2d, o_a, o_b, o_c, proj, mod3, wts, final_g, rows_per_mod, mod_base, final):
    rows = x2d.shape[0]
    tm = OUTPROJ_TM
    tiles_per_mod = rows_per_mod // tm
    row = lambda w, cb=0: pl.BlockSpec((tm, w), lambda i: (i, cb))
    const = lambda w: pl.BlockSpec(w.shape, lambda i, n=w.ndim: (0,) * n)
    in_specs = [row(D_MODEL), row(512), row(512), row(512),
                row(512, COL_GATES // 512), row(512, COL_GATES // 512 + 1), row(512, COL_GATES // 512 + 2),
                row(D_MODEL, COL_MERGE // D_MODEL), row(D_MODEL, COL_MERGE // D_MODEL + 1),
                row(D_MODEL, COL_MERGE // D_MODEL + 2),
                pl.BlockSpec((None, 1, 3 * D_MODEL), lambda i: (mod_base + i // tiles_per_mod, 0, 0))]
    in_specs += [const(w) for w in wts] + [const(final_g)]
    return pl.pallas_call(
        functools.partial(_outproj_kernel, final=final),
        grid=(rows // tm,),
        in_specs=in_specs,
        out_specs=row(D_MODEL),
        out_shape=jax.ShapeDtypeStruct((rows, D_MODEL), F32),
        compiler_params=pltpu.CompilerParams(vmem_limit_bytes=VMEM_LIMIT),
        name="outproj",
    )(x2d, o_a, o_b, o_c, proj, proj, proj, proj, proj, proj, mod3, *wts, final_g)


def _pack_kernel(w_ref, wa_ref, wb_ref):
    seg = lambda i: w_ref[:, _IN_OFF[i]:_IN_OFF[i + 1]]
    rows = w_ref.shape[0]
    z = lambda n: jnp.zeros((rows, n), F32)

    def put(ref, col, x):
        ref[:, col:col + x.shape[1]] = x.astype(BF16)

    put(wa_ref, COL_RW_R, w_ref[:, _IN_OFF[0]:_IN_OFF[5]])
    put(wa_ref, COL_GL_AD, jnp.concatenate([seg(9), z(LANE - GLA_GATE_RANK)], axis=1))
    put(wa_ref, COL_ML_QD, seg(11))
    put(wa_ref, COL_GL_Q, w_ref[:, _IN_OFF[6]:_IN_OFF[9]])
    put(wa_ref, COL_ML_KVD, seg(12))
    kpe = seg(13)
    q = MLA_ROPE // 4
    kpe_sw = jnp.concatenate([kpe[:, q:2 * q], kpe[:, 0:q], kpe[:, 3 * q:4 * q], kpe[:, 2 * q:3 * q]], axis=1)
    tail = LANE - KPE_LANE - MLA_ROPE
    put(wa_ref, COL_KPE, jnp.concatenate([z(KPE_LANE), kpe, z(tail)], axis=1))
    put(wa_ref, COL_KPE_SW, jnp.concatenate([z(KPE_LANE), kpe_sw, z(tail)], axis=1))
    put(wa_ref, COL_KPE_SW + LANE, z(PACK_A - COL_KPE_SW - LANE))
    put(wb_ref, COL_MERGE, seg(15))
    put(wb_ref, COL_GATES, seg(5))
    put(wb_ref, COL_GATES + RW_WIDTH, seg(10))
    put(wb_ref, COL_GATES + RW_WIDTH + GLA_VW, seg(14))


def _pack_call(w_in):
    tr = 256
    width = w_in.shape[2]
    return pl.pallas_call(
        _pack_kernel,
        grid=(DEPTH, D_MODEL // tr),
        in_specs=[pl.BlockSpec((None, tr, width), lambda l, i: (l, i, 0))],
        out_specs=[pl.BlockSpec((None, tr, PACK_A), lambda l, i: (l, i, 0)),
                   pl.BlockSpec((None, tr, PACK_B), lambda l, i: (l, i, 0))],
        out_shape=[jax.ShapeDtypeStruct((DEPTH, D_MODEL, PACK_A), BF16),
                   jax.ShapeDtypeStruct((DEPTH, D_MODEL, PACK_B), BF16)],
        compiler_params=pltpu.CompilerParams(vmem_limit_bytes=VMEM_LIMIT),
        name="pack",
    )(w_in)


def _rope_lane_tables(seq):
    n_freq = MLA_ROPE // 4
    t = np.arange(seq)
    inv = ROPE_THETA ** (-np.arange(n_freq, dtype=np.float64) / n_freq)
    ang = np.stack([(t // GRID_W)[:, None] * inv, (t % GRID_W)[:, None] * inv], axis=1)
    cos = np.repeat(np.cos(ang)[:, :, None, :], 2, axis=2).reshape(seq, MLA_ROPE)
    sin = np.stack([-np.sin(ang), np.sin(ang)], axis=2).reshape(seq, MLA_ROPE)
    return cos, sin


def _mla_tables(seq):
    cos, sin = _rope_lane_tables(seq)
    cosq = np.zeros((seq, LANE), np.float32)
    sinq = np.zeros((seq, LANE), np.float32)
    cosq[:, :KPE_LANE] = 1.0
    cosq[:, KPE_LANE:KPE_LANE + MLA_ROPE] = cos
    sinq[:, KPE_LANE:KPE_LANE + MLA_ROPE] = sin
    cosk = np.zeros((seq, LANE), np.float32)
    cosk[:, KPE_LANE:KPE_LANE + MLA_ROPE] = cos
    epos = np.zeros((MLA_ROPE, LANE), np.float32)
    epos[np.arange(MLA_ROPE), KPE_LANE + np.arange(MLA_ROPE)] = 1.0
    return (jnp.asarray(cosq), jnp.asarray(sinq), jnp.asarray(cosk), jnp.asarray(sinq),
            jnp.asarray(epos, dtype=BF16))


def _pack_mla_weights(qn_g, wq_up, kvn_g, wkv_up):
    scale = (MLA_NOPE + MLA_ROPE) ** -0.5
    wq = wq_up.reshape(MLA_Q_RANK, MLA_HEADS, MLA_NOPE + MLA_ROPE) * scale
    nope, rope = wq[..., :MLA_NOPE], wq[..., MLA_NOPE:]
    q = MLA_ROPE // 4
    rope_sw = jnp.concatenate([rope[..., q:2 * q], rope[..., 0:q], rope[..., 3 * q:4 * q], rope[..., 2 * q:3 * q]], -1)
    tail = jnp.zeros((MLA_Q_RANK, MLA_HEADS, LANE - KPE_LANE - MLA_ROPE), wq.dtype)
    wq_cat = jnp.concatenate([nope, rope, tail], -1).reshape(MLA_Q_RANK, MLA_HEADS * LANE).astype(BF16)
    wq_sw = jnp.concatenate([jnp.zeros_like(nope), rope_sw, tail], -1).reshape(MLA_Q_RANK, MLA_HEADS * LANE).astype(BF16)
    wkv = wkv_up.reshape(MLA_KV_RANK, MLA_HEADS, MLA_NOPE + MLA_V)
    wk = jnp.concatenate([wkv[..., :MLA_NOPE], jnp.zeros((MLA_KV_RANK, MLA_HEADS, LANE - MLA_NOPE), wkv.dtype)], -1)
    wk_pad = wk.reshape(MLA_KV_RANK, MLA_HEADS * LANE).astype(BF16)
    wv = jnp.concatenate([wkv[..., MLA_NOPE:], jnp.zeros((MLA_KV_RANK, MLA_HEADS, LANE - MLA_V), wkv.dtype)], -1)
    wv = wv.reshape(MLA_KV_RANK, MLA_HEADS * LANE).astype(BF16)
    return (qn_g.reshape(1, -1), wq_cat, wq_sw, kvn_g.reshape(1, -1), wk_pad, wv)


def _head_sum_matrix():
    lane = np.arange(RW_WIDTH)
    return jnp.asarray((lane[:, None] // RW_HEAD == lane[None, :] // RW_HEAD).astype(np.float32), dtype=BF16)


def _trunk(x_prompt, x_sample, c, cache_ckv, cache_kpe, state_rwkv, state_gla, c_ctx,
           norm_g, w_mod, b_mod, w_in, rw_w0, rw_w2, rw_a0, rw_a2, rw_k_k, rw_k_a, rw_r_k,
           rw_ln_g, rw_ln_b, rw_out, gla_a2, gla_ab, gla_norm_g, gla_out,
           mla_qn_g, mla_wq_up, mla_kvn_g, mla_wkv_up, mla_out, w_out, final_g):
    bp, tp, _ = x_prompt.shape
    bs, ts, _ = x_sample.shape
    hp = x_prompt.reshape(bp * tp, D_MODEL)
    hs = x_sample.reshape(bs * ts, D_MODEL)
    cvec8 = jnp.concatenate([c, c_ctx[None, :], jnp.zeros((8 - bs - 1, D_MODEL), F32)], axis=0)
    ctx_row = bs
    hsum = _head_sum_matrix()
    mexp, bmask = _gla_constants()
    mexp = jnp.asarray(mexp, dtype=BF16)
    bmask = jnp.asarray(bmask)
    tables = _mla_tables(ts)
    w_a, w_b = _pack_call(w_in)
    fg = final_g.reshape(1, D_MODEL)
    ckv_l, kpe_l, rw_l, gla_l = [], [], [], []
    for l in range(DEPTH):
        mod3 = _mod_call(cvec8, w_mod, b_mod, l).reshape(8, 1, 3 * D_MODEL)
        ng = norm_g[l].reshape(1, D_MODEL)
        row = lambda a: a.reshape(1, -1)
        zr = jnp.zeros((2, RW_RANK, RW_WIDTH), F32)
        wcat = jnp.concatenate([jnp.concatenate([rw_w2[l], zr], axis=2),
                                jnp.concatenate([zr, rw_a2[l]], axis=2)], axis=1).astype(BF16)
        rw_wts = (wcat, jnp.concatenate([rw_w0[l], rw_a0[l]], axis=1), row(rw_k_k[l]), row(rw_k_a[l]),
                  row(rw_r_k[l]), row(rw_ln_g[l]), row(rw_ln_b[l]), hsum)
        a2p = jnp.concatenate([gla_a2[l], jnp.zeros((2, LANE - GLA_GATE_RANK, GLA_KW), F32)], axis=1).astype(BF16)
        gla_wts = (a2p, gla_ab[l], row(gla_norm_g[l]), mexp, bmask)
        mla_wts = _pack_mla_weights(mla_qn_g[l], mla_wq_up[l], mla_kvn_g[l], mla_wkv_up[l])
        out_wts = (rw_out[l].astype(BF16), gla_out[l].astype(BF16), mla_out[l].astype(BF16), w_out[l].astype(BF16))
        final = l == DEPTH - 1

        proj, proj_b = _inproj_call(hp, mod3, ng, w_a, w_b, l, bp * tp, ctx_row)
        o_a, s_rw = _rwkv_call(proj, None, l, bp, tp, rw_wts)
        o_b, s_gla = _gla_call(proj, None, l, bp, tp, gla_wts)
        o_c, ckv = _mla_call(proj, None, None, l, bp, tp, mla_wts, None)
        ckv_l.append(ckv.reshape(bp, tp, MLA_KV_RANK))
        kpe_l.append(proj[:, COL_KPE + KPE_LANE:COL_KPE + KPE_LANE + MLA_ROPE].astype(F32).reshape(bp, tp, MLA_ROPE))
        rw_l.append(s_rw)
        gla_l.append(s_gla)
        hp = _outproj_call(hp, o_a, o_b, o_c, proj_b, mod3, out_wts, fg, bp * tp, ctx_row, final)

        proj, proj_b = _inproj_call(hs, mod3, ng, w_a, w_b, l, ts, 0)
        o_a, _ = _rwkv_call(proj, state_rwkv, l, bs, ts, rw_wts)
        o_b, _ = _gla_call(proj, state_gla, l, bs, ts, gla_wts)
        o_c, _ = _mla_call(proj, cache_ckv, cache_kpe, l, bs, ts, mla_wts, tables)
        hs = _outproj_call(hs, o_a, o_b, o_c, proj_b, mod3, out_wts, fg, ts, 0, final)

    return (hp.reshape(bp, tp, D_MODEL), hs.reshape(bs, ts, D_MODEL),
            jnp.stack(ckv_l, axis=1), jnp.stack(kpe_l, axis=1),
            jnp.stack(rw_l, axis=1), jnp.stack(gla_l, axis=1))


_trunk_jit = jax.jit(_trunk)


def kernel(x_prompt, x_sample, c, cache_ckv, cache_kpe, state_rwkv, state_gla, c_ctx, norm_g, w_mod, b_mod, w_in, rw_w0, rw_w2, rw_a0, rw_a2, rw_k_k, rw_k_a, rw_r_k, rw_ln_g, rw_ln_b, rw_out, gla_a2, gla_ab, gla_norm_g, gla_out, mla_qn_g, mla_wq_up, mla_kvn_g, mla_wkv_up, mla_out, w_out, final_g):
    return _trunk_jit(x_prompt, x_sample, c, cache_ckv, cache_kpe, state_rwkv, state_gla, c_ctx, norm_g, w_mod, b_mod, w_in, rw_w0, rw_w2, rw_a0, rw_a2, rw_k_k, rw_k_a, rw_r_k, rw_ln_g, rw_ln_b, rw_out, gla_a2, gla_ab, gla_norm_g, gla_out, mla_qn_g, mla_wq_up, mla_kvn_g, mla_wkv_up, mla_out, w_out, final_g)
```

```python
import functools

import numpy as np
import jax
import jax.numpy as jnp
from jax import lax
from jax.experimental import pallas as pl
from jax.experimental.pallas import tpu as pltpu

F32 = jnp.float32
BF16 = jnp.bfloat16

D_MODEL = 1024
DEPTH = 2
GRID_W = 64
NORM_EPS = 1e-6
RW_HEADS = 8
RW_HEAD = 64
RW_WIDTH = RW_HEADS * RW_HEAD
RW_RANK = 64
RW_GN_EPS = 64e-5
GLA_HEADS = 4
GLA_DK = 64
GLA_DV = 128
GLA_KW = GLA_HEADS * GLA_DK
GLA_VW = GLA_HEADS * GLA_DV
GLA_GATE_RANK = 16
GLA_LOGIT_NORM = 16.0
MLA_HEADS = 8
MLA_NOPE = 64
MLA_ROPE = 32
MLA_V = 64
MLA_Q_RANK = 256
MLA_KV_RANK = 128
MLA_VW = MLA_HEADS * MLA_V
ROPE_THETA = 10000.0
N_BRANCH = 3

_IN_SIZES = (RW_WIDTH, RW_WIDTH, RW_WIDTH, RW_RANK, RW_RANK, RW_WIDTH,
             GLA_KW, GLA_KW, GLA_VW, GLA_GATE_RANK, GLA_VW,
             MLA_Q_RANK, MLA_KV_RANK, MLA_ROPE, MLA_VW, N_BRANCH * D_MODEL)
_IN_OFF = tuple(int(v) for v in np.concatenate([[0], np.cumsum(_IN_SIZES)]))

LANE = 128
COL_RW_R = 0
COL_RW_K = 512
COL_RW_V = 1024
COL_RW_WA = 1536
COL_GL_AD = 1664
COL_ML_QD = 1792
COL_GL_Q = 2048
COL_GL_K = 2304
COL_GL_V = 2560
COL_ML_KVD = 3072
COL_KPE = 3200
COL_KPE_SW = 3328
PACK_A = 3584
COL_MERGE = 0
COL_GATES = 3072
PACK_B = 4608
KPE_LANE = MLA_NOPE

CHUNK = 64
INPROJ_TM = 512
INPROJ_SUB = 256
INPROJ_TN = 512
OUTPROJ_TM = 512
MLA_TQ = 1024
VMEM_LIMIT = 48 * 1024 * 1024


def _dot(a, b, prec=None):
    return jnp.dot(a, b, preferred_element_type=F32, precision=prec)


def _dot_nt(a, b, prec=None):
    return lax.dot_general(a, b, (((1,), (1,)), ((), ())), preferred_element_type=F32, precision=prec)


def _dot_tn(a, b, prec=None):
    return lax.dot_general(a, b, (((0,), (0,)), ((), ())), preferred_element_type=F32, precision=prec)


def _split(x):
    hi = x.astype(BF16)
    return hi, (x - hi.astype(F32)).astype(BF16)


def _dot_split(a, b):
    ah, al = _split(a)
    bh, bl = _split(b)
    return _dot(ah, bh) + _dot(al, bh) + _dot(ah, bl)


def _dot_sel(sel16, x):
    xh, xl = _split(x)
    return _dot(sel16, xh) + _dot(sel16, xl)


def _sigmoid(x):
    return 0.5 * jnp.tanh(0.5 * x) + 0.5


def _rms(x, eps=NORM_EPS):
    return x * lax.rsqrt(jnp.mean(x * x, axis=-1, keepdims=True) + eps)


def _mod_kernel(c_ref, w_ref, b_ref, o_ref):
    c = c_ref[...]
    o_ref[...] = _dot_split(c * _sigmoid(c), w_ref[...]) + b_ref[...]


def _mod_call(cvec8, w_mod, b_mod, layer):
    tn = 1024
    return pl.pallas_call(
        _mod_kernel,
        grid=(3 * D_MODEL // tn,),
        in_specs=[pl.BlockSpec((8, D_MODEL), lambda j: (0, 0)),
                  pl.BlockSpec((None, D_MODEL, tn), lambda j: (layer, 0, j)),
                  pl.BlockSpec((None, 1, tn), lambda j: (layer, 0, j))],
        out_specs=pl.BlockSpec((8, tn), lambda j: (0, j)),
        out_shape=jax.ShapeDtypeStruct((8, 3 * D_MODEL), F32),
        compiler_params=pltpu.CompilerParams(vmem_limit_bytes=VMEM_LIMIT),
        name="mod",
    )(cvec8, w_mod, b_mod.reshape(DEPTH, 1, 3 * D_MODEL))


def _inproj_kernel(x_ref, mod_ref, g_ref, wa_ref, wb_ref, oa_ref, ob_ref):
    m = mod_ref[...]
    shift = m[:, 0:D_MODEL]
    scale1 = 1.0 + m[:, D_MODEL:2 * D_MODEL]
    tm = x_ref.shape[0]
    for r0 in range(0, tm, INPROJ_SUB):
        rows = slice(r0, r0 + INPROJ_SUB)
        h = (_rms(x_ref[rows, :]) * g_ref[...] * scale1 + shift).astype(BF16)
        for w_ref, o_ref in ((wa_ref, oa_ref), (wb_ref, ob_ref)):
            for c0 in range(0, w_ref.shape[1], INPROJ_TN):
                cols = slice(c0, c0 + INPROJ_TN)
                o_ref[rows, cols] = _dot(h, w_ref[:, cols]).astype(BF16)


def _inproj_call(x2d, mod3, norm_g, w_a, w_b, layer, rows_per_mod, mod_base):
    rows = x2d.shape[0]
    tm = INPROJ_TM
    tiles_per_mod = rows_per_mod // tm
    resident = lambda w: pl.BlockSpec((None,) + w.shape[1:], lambda i: (layer, 0, 0), pipeline_mode=pl.Buffered(1))
    return pl.pallas_call(
        _inproj_kernel,
        grid=(rows // tm,),
        in_specs=[pl.BlockSpec((tm, D_MODEL), lambda i: (i, 0)),
                  pl.BlockSpec((None, 1, 3 * D_MODEL), lambda i: (mod_base + i // tiles_per_mod, 0, 0)),
                  pl.BlockSpec((1, D_MODEL), lambda i: (0, 0)),
                  resident(w_a), resident(w_b)],
        out_specs=[pl.BlockSpec((tm, PACK_A), lambda i: (i, 0)),
                   pl.BlockSpec((tm, PACK_B), lambda i: (i, 0))],
        out_shape=[jax.ShapeDtypeStruct((rows, PACK_A), BF16),
                   jax.ShapeDtypeStruct((rows, PACK_B), BF16)],
        compiler_params=pltpu.CompilerParams(vmem_limit_bytes=VMEM_LIMIT),
        name="inproj",
    )(x2d, mod3, norm_g, w_a, w_b)


RW_CHUNK = 64
RW_PAIRS = RW_HEADS // 2
RW_BLOCK_ROWS = 256
RW_NEG_DECAY_SCALE = -float(np.exp(-0.5))


def _rwkv_time_mask(c, reverse):
    row = lax.broadcasted_iota(jnp.int32, (2 * c, 4 * c), 0)
    col = lax.broadcasted_iota(jnp.int32, (2 * c, 4 * c), 1)
    t = jnp.where(row >= c, row - c, row)
    s = col & (c - 1)
    earlier = (s > t) if reverse else (s < t)
    return earlier | ((row >= c) & (s == t))


def _rwkv_kernel(*refs, seq, has_s0):
    if has_s0:
        r_ref, k_ref, v_ref, wa_ref, s0_ref = refs[:5]
        rest = refs[5:]
    else:
        r_ref, k_ref, v_ref, wa_ref = refs[:4]
        s0_ref = None
        rest = refs[4:]
    (wcat_ref, wa0_ref, kk_ref, ka_ref, rk_ref, lng_ref, lnb_ref, hsum_ref,
     o_ref, sout_ref, acc_ref, kn_ref, st_ref) = rest
    c = RW_CHUNK
    nc = seq // c
    blk = min(RW_BLOCK_ROWS, seq)
    hsum = hsum_ref[...]

    def pre_step(i, carry):
        rows = pl.ds(pl.multiple_of(i * blk, blk), blk)
        kk0 = k_ref[rows, :].astype(F32) * kk_ref[...]
        kn_ref[rows, :] = kk0 / jnp.maximum(jnp.sqrt(_dot((kk0 * kk0).astype(BF16), hsum)), 1e-12)
        return carry

    def post_step(i, carry):
        rows = pl.ds(pl.multiple_of(i * blk, blk), blk)
        o = acc_ref[0, rows, :] + acc_ref[1, rows, :]
        rk = r_ref[rows, :].astype(F32) * k_ref[rows, :].astype(F32) * rk_ref[...]
        red = _dot(jnp.concatenate([o, rk], axis=0).astype(BF16), hsum)
        dev = o - red[0:blk] * (1.0 / RW_HEAD)
        var = _dot((dev * dev).astype(BF16), hsum) * (1.0 / RW_HEAD)
        o = dev * lax.rsqrt(var + RW_GN_EPS) * lng_ref[...] + lnb_ref[...]
        o_ref[rows, :] = (o + red[blk:2 * blk] * v_ref[rows, :]).astype(o_ref.dtype)
        return carry

    even = lax.broadcasted_iota(jnp.int32, (1, LANE), 1) < RW_HEAD
    row = lax.broadcasted_iota(jnp.int32, (c, c), 0)
    col = lax.broadcasted_iota(jnp.int32, (c, c), 1)
    incl16 = [jnp.where(col <= row, 1.0, 0.0).astype(BF16), jnp.where(col >= row, 1.0, 0.0).astype(BF16)]
    tmask = [_rwkv_time_mask(c, False), _rwkv_time_mask(c, True)]
    r2 = lax.broadcasted_iota(jnp.int32, (LANE, LANE), 0)
    c2 = lax.broadcasted_iota(jnp.int32, (LANE, LANE), 1)
    same_head = (r2 < RW_HEAD) == (c2 < RW_HEAD)
    z16 = jnp.zeros((c, LANE), BF16)
    lanes = [slice(p * LANE, (p + 1) * LANE) for p in range(RW_PAIRS)]
    probs = [(d, p) for d in range(2) for p in range(RW_PAIRS)]

    def chunk_operands(d, ci):
        rows = pl.ds(pl.multiple_of(ci * c, c), c)
        r = r_ref[rows, :].astype(F32)
        k = k_ref[rows, :].astype(F32)
        wa = wa_ref[rows, :].astype(F32)
        kk = kn_ref[rows, :]
        lora = jnp.where(even, jnp.tanh(wa), wa)
        pre = _dot(lora.astype(BF16), wcat_ref[d]) + wa0_ref[d:d + 1, :]
        lw = RW_NEG_DECAY_SCALE * _sigmoid(pre[:, 0:RW_WIDTH])
        a = _sigmoid(pre[:, RW_WIDTH:2 * RW_WIDTH])
        kd = k * (1.0 + (a - 1.0) * ka_ref[...])
        bv = kk * a
        gam = _dot_sel(incl16[d], lw)
        last = 0 if d == 1 else c - 1
        gtot = gam[last:last + 1, :]
        gref = gam[c // 2:c // 2 + 1, :]
        e_in = jnp.exp(gam - gref)
        e_out = jnp.exp(gref - gam)
        e_end = e_out * jnp.exp(gtot - gref)
        lhs = jnp.concatenate([-kk * jnp.exp(gam - lw - gref), r * e_in], axis=0).astype(BF16)
        return dict(rows=rows, lhs=lhs, b16=(bv * e_out).astype(BF16), k16=(kd * e_out).astype(BF16),
                    be16=(bv * e_end).astype(BF16), ke16=(kd * e_end).astype(BF16),
                    v16=v_ref[rows, :].astype(BF16), dtot=jnp.exp(gtot), eref=jnp.exp(gref))

    def main_step(i, carry):
        ops = [chunk_operands(0, i), chunk_operands(1, nc - 1 - i)]
        n = range(len(probs))
        pick = lambda name: [ops[d][name][:, lanes[p]] for d, p in probs]
        lhs, b, k, be, ke, v = (pick(s) for s in ("lhs", "b16", "k16", "be16", "ke16", "v16"))
        st = [st_ref[d, p] for d, p in probs]
        first = lambda x: jnp.where(even, x, jnp.zeros_like(x))
        second = lambda x: jnp.where(even, jnp.zeros_like(x), x)
        v_e = [first(x) for x in v]
        v_o = [second(x) for x in v]
        aa = [jnp.where(tmask[d], _dot_nt(lhs[j], jnp.concatenate(
            [first(k[j]), first(b[j]), second(b[j]), second(k[j])], axis=0)), 0.0) for j, (d, p) in enumerate(probs)]
        sp = [_dot_nt(lhs[j], (st[j] * ops[d]["eref"][:, lanes[p]]).astype(BF16))
              for j, (d, p) in enumerate(probs)]
        aa16 = [x.astype(BF16) for x in aa]
        rhs_u = [_dot(aa16[j][0:c], jnp.concatenate([v_e[j], z16, z16, v_o[j]], axis=0)) + sp[j][0:c] for j in n]
        x_e = [jnp.where(even, rhs_u[j], aa[j][0:c, 0:LANE]) for j in n]
        x_o = [jnp.where(even, aa[j][0:c, LANE:2 * LANE], rhs_u[j]) for j in n]
        covered = 1
        while covered < c:
            e16 = [x.astype(BF16) for x in x_e]
            o16 = [x.astype(BF16) for x in x_o]
            x_e = [_dot(e16[j], jnp.concatenate([z16, e16[j]], axis=0)) + first(x_e[j]) for j in n]
            x_o = [_dot(o16[j], jnp.concatenate([o16[j], z16], axis=0)) + second(x_o[j]) for j in n]
            covered *= 2
        u16 = [jnp.where(even, x_e[j], x_o[j]).astype(BF16) for j in n]
        for j, (d, p) in enumerate(probs):
            uv_rows = jnp.concatenate([v_e[j], first(u16[j]), second(u16[j]), v_o[j]], axis=0)
            acc_ref[d, ops[d]["rows"], lanes[p]] = sp[j][c:2 * c] + _dot(aa16[j][c:2 * c], uv_rows)
            upd = _dot_tn(jnp.concatenate([u16[j], v[j]], axis=0), jnp.concatenate([be[j], ke[j]], axis=0))
            st_ref[d, p] = st[j] * ops[d]["dtot"][:, lanes[p]] + jnp.where(same_head, upd, 0.0)
        return carry

    st_ref[...] = jnp.zeros_like(st_ref)
    if has_s0:
        for d in range(2):
            for h in range(RW_HEADS):
                off = (h % 2) * RW_HEAD
                st_ref[d, h // 2, off:off + RW_HEAD, off:off + RW_HEAD] = s0_ref[d, h]
    lax.fori_loop(0, seq // blk, pre_step, 0)
    lax.fori_loop(0, nc, main_step, 0)
    for d in range(2):
        for h in range(RW_HEADS):
            off = (h % 2) * RW_HEAD
            sout_ref[d, h] = st_ref[d, h // 2, off:off + RW_HEAD, off:off + RW_HEAD]
    lax.fori_loop(0, seq // blk, post_step, 0)


def _rwkv_call(proj, s0, layer, batch, seq, wts):
    has_s0 = s0 is not None
    blk = lambda w, cb: pl.BlockSpec((seq, w), lambda b: (b, cb))
    in_specs = [blk(512, COL_RW_R // 512), blk(512, COL_RW_K // 512), blk(512, COL_RW_V // 512),
                blk(LANE, COL_RW_WA // LANE)]
    args = [proj, proj, proj, proj]
    if has_s0:
        in_specs.append(pl.BlockSpec((None, None, 2, RW_HEADS, RW_HEAD, RW_HEAD),
                                     lambda b: (b, layer, 0, 0, 0, 0)))
        args.append(s0)
    for w in wts:
        in_specs.append(pl.BlockSpec(w.shape, lambda b, n=w.ndim: (0,) * n))
        args.append(w)
    return pl.pallas_call(
        functools.partial(_rwkv_kernel, seq=seq, has_s0=has_s0),
        grid=(batch,),
        in_specs=in_specs,
        out_specs=[pl.BlockSpec((seq, RW_WIDTH), lambda b: (b, 0)),
                   pl.BlockSpec((None, 2, RW_HEADS, RW_HEAD, RW_HEAD), lambda b: (b, 0, 0, 0, 0))],
        out_shape=[jax.ShapeDtypeStruct((batch * seq, RW_WIDTH), BF16),
                   jax.ShapeDtypeStruct((batch, 2, RW_HEADS, RW_HEAD, RW_HEAD), F32)],
        scratch_shapes=[pltpu.VMEM((2, seq, RW_WIDTH), F32),
                        pltpu.VMEM((seq, RW_WIDTH), F32),
                        pltpu.VMEM((2, RW_PAIRS, LANE, LANE), F32)],
        compiler_params=pltpu.CompilerParams(vmem_limit_bytes=VMEM_LIMIT),
        name="rwkv",
    )(*args)


GLA_LEVELS = (32, 16, 8, 4, 2, 1)
GLA_GROUP = 4


def _gla_constants():
    c = CHUNK
    nl = len(GLA_LEVELS)
    mexp = np.zeros((2, (nl + 1) * c, c), np.float32)
    bmask = np.zeros((2, nl + 1, c, c), np.float32)
    for d in range(2):
        pos = np.arange(c) if d == 0 else c - 1 - np.arange(c)
        pt = pos[:, None]
        pj = pos[None, :]
        mexp[d, nl * c:] = (pj <= pt)
        for li, m in enumerate(GLA_LEVELS):
            mid = (pos // (2 * m)) * (2 * m) + m
            second = pos >= mid
            mq = (pj >= mid[:, None]) & (pj <= pt) & second[:, None]
            mk = (pj > pt) & (pj <= mid[:, None] - 1) & (~second)[:, None]
            mexp[d, li * c:(li + 1) * c] = mq | mk
            same = (pos[:, None] // (2 * m)) == (pos[None, :] // (2 * m))
            bmask[d, li] = same & second[:, None] & (~second)[None, :]
        bmask[d, nl] = np.eye(c)
    return mexp, np.concatenate([bmask, bmask], axis=-1)


def _gla_kernel(*refs, seq, has_s0):
    if has_s0:
        q_ref, k_ref, v_ref, ad_ref, s0_ref = refs[:5]
        rest = refs[5:]
    else:
        q_ref, k_ref, v_ref, ad_ref = refs[:4]
        s0_ref = None
        rest = refs[4:]
    (a2_ref, ab_ref, ng_ref, mexp_ref, bmask_ref,
     o_ref, sout_ref, acc_ref, st_ref) = rest
    c = CHUNK
    nc = seq // c
    nl = len(GLA_LEVELS)

    def chunk_operands(d, ci):
        rows = pl.ds(pl.multiple_of(ci * c, c), c)
        q = q_ref[rows, :].astype(F32) * (GLA_DK ** -0.5)
        k = k_ref[rows, :].astype(F32)
        v = v_ref[rows, :]
        x = _dot(ad_ref[rows, :].astype(BF16), a2_ref[d]) + ab_ref[d:d + 1, :]
        g = (jnp.minimum(x, 0.0) - jnp.log1p(jnp.exp(-jnp.abs(x)))) * (1.0 / GLA_LOGIT_NORM)
        g_hi, g_lo = _split(g)
        sums = _dot(mexp_ref[d], g_hi)
        ex = jnp.exp(sums[0:nl * c])
        b = sums[nl * c:(nl + 1) * c] + _dot(mexp_ref[d, nl * c:(nl + 1) * c, :], g_lo)
        last = 0 if d == 1 else c - 1
        blast = b[last:last + 1, :]
        qb = (q * jnp.exp(b)).astype(BF16)
        kdec = (k * jnp.exp(blast - b)).astype(BF16)
        dtot = jnp.exp(blast)
        qs = [(q * ex[li * c:(li + 1) * c]).astype(BF16) for li in range(nl)] + [q.astype(BF16)]
        ks = [(k * ex[li * c:(li + 1) * c]).astype(BF16) for li in range(nl)] + [k.astype(BF16)]
        return dict(rows=rows, qs=qs, ks=ks, qb=qb, kdec=kdec, dtot=dtot, v16=v.astype(BF16))

    even = lax.broadcasted_iota(jnp.int32, (1, LANE), 1) < GLA_DK
    r2 = lax.broadcasted_iota(jnp.int32, (2 * GLA_DV, LANE), 0)
    c2 = lax.broadcasted_iota(jnp.int32, (2 * GLA_DV, LANE), 1)
    same_head = (r2 < GLA_DV) == (c2 < GLA_DK)
    zv = jnp.zeros((c, GLA_DV), BF16)
    npair = GLA_HEADS // 2
    kls = [slice(p * LANE, (p + 1) * LANE) for p in range(npair)]
    vls = [slice(p * 2 * GLA_DV, (p + 1) * 2 * GLA_DV) for p in range(npair)]
    probs = [(d, p) for d in range(2) for p in range(npair)]
    first = lambda x: jnp.where(even, x, jnp.zeros_like(x))
    second = lambda x: jnp.where(even, jnp.zeros_like(x), x)

    group = min(GLA_GROUP, nc)

    def main_step(i, carry):
        ops = [[chunk_operands(0, i * group + g) for g in range(group)],
               [chunk_operands(1, nc - 1 - (i * group + g)) for g in range(group)]]
        allp = [(d, p, g) for d, p in probs for g in range(group)]
        lvl = [[_dot_nt(ops[d][g]["qs"][li][:, kls[p]],
                        jnp.concatenate([first(ops[d][g]["ks"][li][:, kls[p]]),
                                         second(ops[d][g]["ks"][li][:, kls[p]])], axis=0))
                for d, p, g in allp] for li in range(nl + 1)]
        att = [sum(bmask_ref[d, li] * lvl[li][j] for li in range(nl + 1)).astype(BF16)
               for j, (d, p, g) in enumerate(allp)]
        upd = [jnp.where(same_head, _dot_tn(ops[d][g]["v16"][:, vls[p]], ops[d][g]["kdec"][:, kls[p]]), 0.0)
               for d, p, g in allp]
        states = []
        for d, p in probs:
            st = st_ref[d, p]
            for g in range(group):
                states.append(st)
                st = st * ops[d][g]["dtot"][:, kls[p]] + upd[len(states) - 1]
            st_ref[d, p] = st
        for j, (d, p, g) in enumerate(allp):
            v_p = ops[d][g]["v16"][:, vls[p]]
            v_bd = jnp.concatenate([jnp.concatenate([v_p[:, 0:GLA_DV], zv], axis=1),
                                    jnp.concatenate([zv, v_p[:, GLA_DV:2 * GLA_DV]], axis=1)], axis=0)
            inter = _dot_nt(ops[d][g]["qb"][:, kls[p]], states[j].astype(BF16))
            acc_ref[d, ops[d][g]["rows"], vls[p]] = inter + _dot(att[j], v_bd)
        return carry

    blk = min(RW_BLOCK_ROWS, seq)

    def post_step(i, carry):
        rows = pl.ds(pl.multiple_of(i * blk, blk), blk)
        for h in range(GLA_HEADS):
            sv = slice(h * GLA_DV, (h + 1) * GLA_DV)
            o_ref[rows, sv] = (_rms(acc_ref[0, rows, sv] + acc_ref[1, rows, sv]) * ng_ref[...]).astype(o_ref.dtype)
        return carry

    st_ref[...] = jnp.zeros_like(st_ref)
    if has_s0:
        for d in range(2):
            for h in range(GLA_HEADS):
                ro, co = (h % 2) * GLA_DV, (h % 2) * GLA_DK
                st_ref[d, h // 2, ro:ro + GLA_DV, co:co + GLA_DK] = s0_ref[d, h].T
    lax.fori_loop(0, nc // group, main_step, 0)
    for d in range(2):
        for h in range(GLA_HEADS):
            ro, co = (h % 2) * GLA_DV, (h % 2) * GLA_DK
            sout_ref[d, h] = st_ref[d, h // 2, ro:ro + GLA_DV, co:co + GLA_DK].T
    lax.fori_loop(0, seq // blk, post_step, 0)


def _gla_call(proj, s0, layer, batch, seq, wts):
    has_s0 = s0 is not None
    blk = lambda w, cb: pl.BlockSpec((seq, w), lambda b: (b, cb))
    in_specs = [blk(GLA_KW, COL_GL_Q // GLA_KW), blk(GLA_KW, COL_GL_K // GLA_KW),
                blk(GLA_VW, COL_GL_V // GLA_VW), blk(LANE, COL_GL_AD // LANE)]
    args = [proj, proj, proj, proj]
    if has_s0:
        in_specs.append(pl.BlockSpec((None, None, 2, GLA_HEADS, GLA_DK, GLA_DV),
                                     lambda b: (b, layer, 0, 0, 0, 0)))
        args.append(s0)
    for w in wts:
        in_specs.append(pl.BlockSpec(w.shape, lambda b, n=w.ndim: (0,) * n))
        args.append(w)
    return pl.pallas_call(
        functools.partial(_gla_kernel, seq=seq, has_s0=has_s0),
        grid=(batch,),
        in_specs=in_specs,
        out_specs=[pl.BlockSpec((seq, GLA_VW), lambda b: (b, 0)),
                   pl.BlockSpec((None, 2, GLA_HEADS, GLA_DK, GLA_DV), lambda b: (b, 0, 0, 0, 0))],
        out_shape=[jax.ShapeDtypeStruct((batch * seq, GLA_VW), BF16),
                   jax.ShapeDtypeStruct((batch, 2, GLA_HEADS, GLA_DK, GLA_DV), F32)],
        scratch_shapes=[pltpu.VMEM((2, seq, GLA_VW), F32),
                        pltpu.VMEM((2, GLA_HEADS // 2, 2 * GLA_DV, LANE), F32)],
        compiler_params=pltpu.CompilerParams(vmem_limit_bytes=VMEM_LIMIT),
        name="gla",
    )(*args)


def _mla_kernel(*refs, seq, past, tq):
    has_ctx = past > 0
    if has_ctx:
        (qd_ref, kvd_ref, kpe_ref, kpesw_ref, cckv_ref, ckpe_ref,
         qng_ref, wq_ref, wqsw_ref, kvng_ref, wk_ref, wv_ref,
         cosq_ref, sinq_ref, cosk_ref, sink_ref, epos_ref,
         o_ref, ckv_ref, kcat_ref, vv_ref) = refs
    else:
        (qd_ref, kvd_ref, kpe_ref,
         qng_ref, wq_ref, kvng_ref, wk_ref, wv_ref,
         o_ref, ckv_ref, kcat_ref, vv_ref) = refs

    ones_hi = jnp.where(lax.broadcasted_iota(jnp.int32, (1, LANE), 1) >= MLA_V, 1.0, 0.0)

    @pl.when(pl.program_id(1) == 0)
    def _():
        ckv = _rms(kvd_ref[...].astype(F32)) * kvng_ref[...]
        ckv_ref[...] = ckv
        if has_ctx:
            kpos = kpe_ref[...] * cosk_ref[...] + kpesw_ref[...] * sink_ref[...]
        else:
            kpos = kpe_ref[...]
        segs = [(0, seq, ckv, kpos)]
        if has_ctx:
            cpos = _dot(ckpe_ref[...].astype(BF16), epos_ref[...])
            segs.append((seq, past, cckv_ref[...], cpos))
        for start, n, lat, pos in segs:
            lat16 = lat.astype(BF16)
            kc = _dot(lat16, wk_ref[...])
            vc = _dot(lat16, wv_ref[...])
            for h in range(MLA_HEADS):
                hl = slice(h * LANE, (h + 1) * LANE)
                kcat_ref[h, start:start + n, :] = (kc[:, hl] + pos).astype(BF16)
                vv_ref[h, start:start + n, :] = (vc[:, hl] + ones_hi).astype(BF16)

    qlat = (_rms(qd_ref[...].astype(F32)) * qng_ref[...]).astype(BF16)
    qc = _dot(qlat, wq_ref[...])
    if has_ctx:
        qsw = _dot(qlat, wqsw_ref[...])
    for h in range(MLA_HEADS):
        hl = slice(h * LANE, (h + 1) * LANE)
        q_raw = qc[:, hl]
        if has_ctx:
            q_self = (q_raw * cosq_ref[...] + qsw[:, hl] * sinq_ref[...]).astype(BF16)
        else:
            q_self = q_raw.astype(BF16)
        s1 = _dot_nt(q_self, kcat_ref[h, 0:seq, :])
        m = jnp.max(s1, axis=-1, keepdims=True)
        if has_ctx:
            s2 = _dot_nt(q_raw.astype(BF16), kcat_ref[h, seq:seq + past, :])
            m = jnp.maximum(m, jnp.max(s2, axis=-1, keepdims=True))
        o_h = _dot(jnp.exp((s1 - m).astype(BF16)), vv_ref[h, 0:seq, :])
        if has_ctx:
            o_h = o_h + _dot(jnp.exp((s2 - m).astype(BF16)), vv_ref[h, seq:seq + past, :])
        o_h = o_h / o_h[:, MLA_V:MLA_V + 1]
        o_ref[:, h * MLA_V:(h + 1) * MLA_V] = o_h[:, 0:MLA_V].astype(o_ref.dtype)


def _mla_call(proj, cache_ckv, cache_kpe, layer, batch, seq, wts, tables):
    has_ctx = cache_ckv is not None
    past = cache_ckv.shape[2] if has_ctx else 0
    tq = min(MLA_TQ, seq)
    nq = seq // tq
    full = lambda cb: pl.BlockSpec((seq, LANE), lambda b, i: (b, cb))
    in_specs = [pl.BlockSpec((tq, MLA_Q_RANK), lambda b, i: (b * nq + i, COL_ML_QD // MLA_Q_RANK)),
                full(COL_ML_KVD // LANE), full(COL_KPE // LANE)]
    args = [proj, proj, proj]
    if has_ctx:
        in_specs += [full(COL_KPE_SW // LANE),
                     pl.BlockSpec((None, None, past, MLA_KV_RANK), lambda b, i: (b, layer, 0, 0)),
                     pl.BlockSpec((None, None, past, MLA_ROPE), lambda b, i: (b, layer, 0, 0))]
        args += [proj, cache_ckv, cache_kpe]
    qn_g, wq_cat, wq_sw, kvn_g, wk_pad, wv = wts
    const = lambda w: pl.BlockSpec(w.shape, lambda b, i, n=w.ndim: (0,) * n)
    if has_ctx:
        cosq, sinq, cosk, sink, epos = tables
        wlist = [qn_g, wq_cat, wq_sw, kvn_g, wk_pad, wv]
        in_specs += [const(w) for w in wlist]
        in_specs += [pl.BlockSpec((tq, LANE), lambda b, i: (i, 0)), pl.BlockSpec((tq, LANE), lambda b, i: (i, 0)),
                     const(cosk), const(sink), const(epos)]
        args += wlist + [cosq, sinq, cosk, sink, epos]
    else:
        wlist = [qn_g, wq_cat, kvn_g, wk_pad, wv]
        in_specs += [const(w) for w in wlist]
        args += wlist
    return pl.pallas_call(
        functools.partial(_mla_kernel, seq=seq, past=past, tq=tq),
        grid=(batch, nq),
        in_specs=in_specs,
        out_specs=[pl.BlockSpec((tq, MLA_VW), lambda b, i: (b * nq + i, 0)),
                   pl.BlockSpec((seq, MLA_KV_RANK), lambda b, i: (b, 0))],
        out_shape=[jax.ShapeDtypeStruct((batch * seq, MLA_VW), BF16),
                   jax.ShapeDtypeStruct((batch * seq, MLA_KV_RANK), F32)],
        scratch_shapes=[pltpu.VMEM((MLA_HEADS, seq + past, LANE), BF16),
                        pltpu.VMEM((MLA_HEADS, seq + past, LANE), BF16)],
        compiler_params=pltpu.CompilerParams(vmem_limit_bytes=VMEM_LIMIT,
                                             dimension_semantics=("arbitrary", "arbitrary")),
        name="mla",
    )(*args)


def _outproj_kernel(x_ref, oa_ref, ob_ref, oc_ref, ga_ref, gb_ref, gc_ref, ma_ref, mb_ref, mc_ref,
                    mod_ref, wa_ref, wb_ref, wc_ref, wo_ref, fg_ref, o_ref, *, final):
    def branch(o_r, g_r, m_r, w_r):
        h = 0.5 * g_r[...]
        act = o_r[...] * (h * (1.0 + jnp.tanh(h)))
        y = _dot(act, w_r[...]).astype(BF16)
        return (0.5 * (1.0 + jnp.tanh(0.5 * m_r[...]))) * y

    y = (branch(oa_ref, ga_ref, ma_ref, wa_ref) + branch(ob_ref, gb_ref, mb_ref, wb_ref)
         + branch(oc_ref, gc_ref, mc_ref, wc_ref))
    y = _dot(y, wo_ref[...])
    gate = mod_ref[...][:, 2 * D_MODEL:3 * D_MODEL]
    hn = x_ref[...] + gate * y
    if final:
        hn = _rms(hn) * fg_ref[...]
    o_ref[...] = hn


def _outproj_call(x2d, o_a, o_b, o_c, proj, mod3, wts, final_g, rows_per_mod, mod_base, final):
    rows = x2d.shape[0]
    tm = OUTPROJ_TM
    tiles_per_mod = rows_per_mod // tm
    row = lambda w, cb=0: pl.BlockSpec((tm, w), lambda i: (i, cb))
    const = lambda w: pl.BlockSpec(w.shape, lambda i, n=w.ndim: (0,) * n)
    in_specs = [row(D_MODEL), row(512), row(512), row(512),
                row(512, COL_GATES // 512), row(512, COL_GATES // 512 + 1), row(512, COL_GATES // 512 + 2),
                row(D_MODEL, COL_MERGE // D_MODEL), row(D_MODEL, COL_MERGE // D_MODEL + 1),
                row(D_MODEL, COL_MERGE // D_MODEL + 2),
                pl.BlockSpec((None, 1, 3 * D_MODEL), lambda i: (mod_base + i // tiles_per_mod, 0, 0))]
    in_specs += [const(w) for w in wts] + [const(final_g)]
    return pl.pallas_call(
        functools.partial(_outproj_kernel, final=final),
        grid=(rows // tm,),
        in_specs=in_specs,
        out_specs=row(D_MODEL),
        out_shape=jax.ShapeDtypeStruct((rows, D_MODEL), F32),
        compiler_params=pltpu.CompilerParams(vmem_limit_bytes=VMEM_LIMIT),
        name="outproj",
    )(x2d, o_a, o_b, o_c, proj, proj, proj, proj, proj, proj, mod3, *wts, final_g)


def _pack_kernel(wt_ref, wa_ref, wb_ref):
    seg = lambda i: wt_ref[_IN_OFF[i]:_IN_OFF[i + 1], :]
    tr = wt_ref.shape[1]
    z = lambda n: jnp.zeros((n, tr), F32)

    def put(ref, col, xt):
        ref[:, col:col + xt.shape[0]] = xt.T.astype(BF16)

    put(wa_ref, COL_RW_R, wt_ref[_IN_OFF[0]:_IN_OFF[5], :])
    put(wa_ref, COL_GL_AD, jnp.concatenate([seg(9), z(LANE - GLA_GATE_RANK)], axis=0))
    put(wa_ref, COL_ML_QD, seg(11))
    put(wa_ref, COL_GL_Q, wt_ref[_IN_OFF[6]:_IN_OFF[9], :])
    put(wa_ref, COL_ML_KVD, seg(12))
    kpe = seg(13)
    q = MLA_ROPE // 4
    kpe_sw = jnp.concatenate([kpe[q:2 * q], kpe[0:q], kpe[3 * q:4 * q], kpe[2 * q:3 * q]], axis=0)
    tail = LANE - KPE_LANE - MLA_ROPE
    put(wa_ref, COL_KPE, jnp.concatenate([z(KPE_LANE), kpe, z(tail)], axis=0))
    put(wa_ref, COL_KPE_SW, jnp.concatenate([z(KPE_LANE), kpe_sw, z(tail)], axis=0))
    wa_ref[:, COL_KPE_SW + LANE:PACK_A] = jnp.zeros((tr, PACK_A - COL_KPE_SW - LANE), BF16)
    put(wb_ref, COL_MERGE, seg(15))
    put(wb_ref, COL_GATES, seg(5))
    put(wb_ref, COL_GATES + RW_WIDTH, seg(10))
    put(wb_ref, COL_GATES + RW_WIDTH + GLA_VW, seg(14))


def _pack_call(w_in):
    tr = 256
    wt = jnp.swapaxes(w_in, 1, 2)
    width = wt.shape[1]
    return pl.pallas_call(
        _pack_kernel,
        grid=(DEPTH, D_MODEL // tr),
        in_specs=[pl.BlockSpec((None, width, tr), lambda l, i: (l, 0, i))],
        out_specs=[pl.BlockSpec((None, tr, PACK_A), lambda l, i: (l, i, 0)),
                   pl.BlockSpec((None, tr, PACK_B), lambda l, i: (l, i, 0))],
        out_shape=[jax.ShapeDtypeStruct((DEPTH, D_MODEL, PACK_A), BF16),
                   jax.ShapeDtypeStruct((DEPTH, D_MODEL, PACK_B), BF16)],
        compiler_params=pltpu.CompilerParams(vmem_limit_bytes=VMEM_LIMIT),
        name="pack",
    )(wt)


def _rope_lane_tables(seq):
    n_freq = MLA_ROPE // 4
    t = np.arange(seq)
    inv = ROPE_THETA ** (-np.arange(n_freq, dtype=np.float64) / n_freq)
    ang = np.stack([(t // GRID_W)[:, None] * inv, (t % GRID_W)[:, None] * inv], axis=1)
    cos = np.repeat(np.cos(ang)[:, :, None, :], 2, axis=2).reshape(seq, MLA_ROPE)
    sin = np.stack([-np.sin(ang), np.sin(ang)], axis=2).reshape(seq, MLA_ROPE)
    return cos, sin


def _mla_tables(seq):
    cos, sin = _rope_lane_tables(seq)
    cosq = np.zeros((seq, LANE), np.float32)
    sinq = np.zeros((seq, LANE), np.float32)
    cosq[:, :KPE_LANE] = 1.0
    cosq[:, KPE_LANE:KPE_LANE + MLA_ROPE] = cos
    sinq[:, KPE_LANE:KPE_LANE + MLA_ROPE] = sin
    cosk = np.zeros((seq, LANE), np.float32)
    cosk[:, KPE_LANE:KPE_LANE + MLA_ROPE] = cos
    epos = np.zeros((MLA_ROPE, LANE), np.float32)
    epos[np.arange(MLA_ROPE), KPE_LANE + np.arange(MLA_ROPE)] = 1.0
    return (jnp.asarray(cosq), jnp.asarray(sinq), jnp.asarray(cosk), jnp.asarray(sinq),
            jnp.asarray(epos, dtype=BF16))


def _pack_mla_weights(qn_g, wq_up, kvn_g, wkv_up):
    scale = (MLA_NOPE + MLA_ROPE) ** -0.5
    wq = wq_up.reshape(MLA_Q_RANK, MLA_HEADS, MLA_NOPE + MLA_ROPE) * scale
    nope, rope = wq[..., :MLA_NOPE], wq[..., MLA_NOPE:]
    q = MLA_ROPE // 4
    rope_sw = jnp.concatenate([rope[..., q:2 * q], rope[..., 0:q], rope[..., 3 * q:4 * q], rope[..., 2 * q:3 * q]], -1)
    tail = jnp.zeros((MLA_Q_RANK, MLA_HEADS, LANE - KPE_LANE - MLA_ROPE), wq.dtype)
    wq_cat = jnp.concatenate([nope, rope, tail], -1).reshape(MLA_Q_RANK, MLA_HEADS * LANE).astype(BF16)
    wq_sw = jnp.concatenate([jnp.zeros_like(nope), rope_sw, tail], -1).reshape(MLA_Q_RANK, MLA_HEADS * LANE).astype(BF16)
    wkv = wkv_up.reshape(MLA_KV_RANK, MLA_HEADS, MLA_NOPE + MLA_V)
    wk = jnp.concatenate([wkv[..., :MLA_NOPE], jnp.zeros((MLA_KV_RANK, MLA_HEADS, LANE - MLA_NOPE), wkv.dtype)], -1)
    wk_pad = wk.reshape(MLA_KV_RANK, MLA_HEADS * LANE).astype(BF16)
    wv = jnp.concatenate([wkv[..., MLA_NOPE:], jnp.zeros((MLA_KV_RANK, MLA_HEADS, LANE - MLA_V), wkv.dtype)], -1)
    wv = wv.reshape(MLA_KV_RANK, MLA_HEADS * LANE).astype(BF16)
    return (qn_g.reshape(1, -1), wq_cat, wq_sw, kvn_g.reshape(1, -1), wk_pad, wv)


def _head_sum_matrix():
    lane = np.arange(RW_WIDTH)
    return jnp.asarray((lane[:, None] // RW_HEAD == lane[None, :] // RW_HEAD).astype(np.float32), dtype=BF16)


def _trunk(x_prompt, x_sample, c, cache_ckv, cache_kpe, state_rwkv, state_gla, c_ctx,
           norm_g, w_mod, b_mod, w_in, rw_w0, rw_w2, rw_a0, rw_a2, rw_k_k, rw_k_a, rw_r_k,
           rw_ln_g, rw_ln_b, rw_out, gla_a2, gla_ab, gla_norm_g, gla_out,
           mla_qn_g, mla_wq_up, mla_kvn_g, mla_wkv_up, mla_out, w_out, final_g):
    bp, tp, _ = x_prompt.shape
    bs, ts, _ = x_sample.shape
    hp = x_prompt.reshape(bp * tp, D_MODEL)
    hs = x_sample.reshape(bs * ts, D_MODEL)
    cvec8 = jnp.concatenate([c, c_ctx[None, :], jnp.zeros((8 - bs - 1, D_MODEL), F32)], axis=0)
    ctx_row = bs
    hsum = _head_sum_matrix()
    mexp, bmask = _gla_constants()
    mexp = jnp.asarray(mexp, dtype=BF16)
    bmask = jnp.asarray(bmask)
    tables = _mla_tables(ts)
    w_a, w_b = _pack_call(w_in)
    fg = final_g.reshape(1, D_MODEL)
    ckv_l, kpe_l, rw_l, gla_l = [], [], [], []
    for l in range(DEPTH):
        mod3 = _mod_call(cvec8, w_mod, b_mod, l).reshape(8, 1, 3 * D_MODEL)
        ng = norm_g[l].reshape(1, D_MODEL)
        row = lambda a: a.reshape(1, -1)
        zr = jnp.zeros((2, RW_RANK, RW_WIDTH), F32)
        wcat = jnp.concatenate([jnp.concatenate([rw_w2[l], zr], axis=2),
                                jnp.concatenate([zr, rw_a2[l]], axis=2)], axis=1).astype(BF16)
        rw_wts = (wcat, jnp.concatenate([rw_w0[l], rw_a0[l]], axis=1), row(rw_k_k[l]), row(rw_k_a[l]),
                  row(rw_r_k[l]), row(rw_ln_g[l]), row(rw_ln_b[l]), hsum)
        a2p = jnp.concatenate([gla_a2[l], jnp.zeros((2, LANE - GLA_GATE_RANK, GLA_KW), F32)], axis=1).astype(BF16)
        gla_wts = (a2p, gla_ab[l], row(gla_norm_g[l]), mexp, bmask)
        mla_wts = _pack_mla_weights(mla_qn_g[l], mla_wq_up[l], mla_kvn_g[l], mla_wkv_up[l])
        out_wts = (rw_out[l].astype(BF16), gla_out[l].astype(BF16), mla_out[l].astype(BF16), w_out[l].astype(BF16))
        final = l == DEPTH - 1

        proj, proj_b = _inproj_call(hp, mod3, ng, w_a, w_b, l, bp * tp, ctx_row)
        o_a, s_rw = _rwkv_call(proj, None, l, bp, tp, rw_wts)
        o_b, s_gla = _gla_call(proj, None, l, bp, tp, gla_wts)
        o_c, ckv = _mla_call(proj, None, None, l, bp, tp, mla_wts, None)
        ckv_l.append(ckv.reshape(bp, tp, MLA_KV_RANK))
        kpe_l.append(proj[:, COL_KPE + KPE_LANE:COL_KPE + KPE_LANE + MLA_ROPE].astype(F32).reshape(bp, tp, MLA_ROPE))
        rw_l.append(s_rw)
        gla_l.append(s_gla)
        hp = _outproj_call(hp, o_a, o_b, o_c, proj_b, mod3, out_wts, fg, bp * tp, ctx_row, final)

        proj, proj_b = _inproj_call(hs, mod3, ng, w_a, w_b, l, ts, 0)
        o_a, _ = _rwkv_call(proj, state_rwkv, l, bs, ts, rw_wts)
        o_b, _ = _gla_call(proj, state_gla, l, bs, ts, gla_wts)
        o_c, _ = _mla_call(proj, cache_ckv, cache_kpe, l, bs, ts, mla_wts, tables)
        hs = _outproj_call(hs, o_a, o_b, o_c, proj_b, mod3, out_wts, fg, ts, 0, final)

    return (hp.reshape(bp, tp, D_MODEL), hs.reshape(bs, ts, D_MODEL),
            jnp.stack(ckv_l, axis=1), jnp.stack(kpe_l, axis=1),
            jnp.stack(rw_l, axis=1), jnp.stack(gla_l, axis=1))


_trunk_jit = jax.jit(_trunk)


def kernel(x_prompt, x_sample, c, cache_ckv, cache_kpe, state_rwkv, state_gla, c_ctx, norm_g, w_mod, b_mod, w_in, rw_w0, rw_w2, rw_a0, rw_a2, rw_k_k, rw_k_a, rw_r_k, rw_ln_g, rw_ln_b, rw_out, gla_a2, gla_ab, gla_norm_g, gla_out, mla_qn_g, mla_wq_up, mla_kvn_g, mla_wkv_up, mla_out, w_out, final_g):
    return _trunk_jit(x_prompt, x_sample, c, cache_ckv, cache_kpe, state_rwkv, state_gla, c_ctx, norm_g, w_mod, b_mod, w_in, rw_w0, rw_w2, rw_a0, rw_a2, rw_k_k, rw_k_a, rw_r_k, rw_ln_g, rw_ln_b, rw_out, gla_a2, gla_ab, gla_norm_g, gla_out, mla_qn_g, mla_wq_up, mla_kvn_g, mla_wkv_up, mla_out, w_out, final_g)
```

```python
import functools

import numpy as np
import jax
import jax.numpy as jnp
from jax import lax
from jax.experimental import pallas as pl
from jax.experimental.pallas import tpu as pltpu

F32 = jnp.float32
BF16 = jnp.bfloat16

D_MODEL = 1024
DEPTH = 2
GRID_W = 64
NORM_EPS = 1e-6
RW_HEADS = 8
RW_HEAD = 64
RW_WIDTH = RW_HEADS * RW_HEAD
RW_RANK = 64
RW_GN_EPS = 64e-5
GLA_HEADS = 4
GLA_DK = 64
GLA_DV = 128
GLA_KW = GLA_HEADS * GLA_DK
GLA_VW = GLA_HEADS * GLA_DV
GLA_GATE_RANK = 16
GLA_LOGIT_NORM = 16.0
MLA_HEADS = 8
MLA_NOPE = 64
MLA_ROPE = 32
MLA_V = 64
MLA_Q_RANK = 256
MLA_KV_RANK = 128
MLA_VW = MLA_HEADS * MLA_V
ROPE_THETA = 10000.0
N_BRANCH = 3

_IN_SIZES = (RW_WIDTH, RW_WIDTH, RW_WIDTH, RW_RANK, RW_RANK, RW_WIDTH,
             GLA_KW, GLA_KW, GLA_VW, GLA_GATE_RANK, GLA_VW,
             MLA_Q_RANK, MLA_KV_RANK, MLA_ROPE, MLA_VW, N_BRANCH * D_MODEL)
_IN_OFF = tuple(int(v) for v in np.concatenate([[0], np.cumsum(_IN_SIZES)]))

LANE = 128
COL_RW_R = 0
COL_RW_K = 512
COL_RW_V = 1024
COL_RW_WA = 1536
COL_GL_AD = 1664
COL_ML_QD = 1792
COL_GL_Q = 2048
COL_GL_K = 2304
COL_GL_V = 2560
COL_ML_KVD = 3072
COL_KPE = 3200
COL_KPE_SW = 3328
PACK_A = 3584
COL_MERGE = 0
COL_GATES = 3072
PACK_B = 4608
KPE_LANE = MLA_NOPE

CHUNK = 64
INPROJ_TM = 512
INPROJ_SUB = 256
INPROJ_TN = 512
OUTPROJ_TM = 512
MLA_TQ = 1024
VMEM_LIMIT = 48 * 1024 * 1024


def _dot(a, b, prec=None):
    return jnp.dot(a, b, preferred_element_type=F32, precision=prec)


def _dot_nt(a, b, prec=None):
    return lax.dot_general(a, b, (((1,), (1,)), ((), ())), preferred_element_type=F32, precision=prec)


def _dot_tn(a, b, prec=None):
    return lax.dot_general(a, b, (((0,), (0,)), ((), ())), preferred_element_type=F32, precision=prec)


def _split(x):
    hi = x.astype(BF16)
    return hi, (x - hi.astype(F32)).astype(BF16)


def _dot_split(a, b):
    ah, al = _split(a)
    bh, bl = _split(b)
    return _dot(ah, bh) + _dot(al, bh) + _dot(ah, bl)


def _dot_sel(sel16, x):
    xh, xl = _split(x)
    return _dot(sel16, xh) + _dot(sel16, xl)


def _sigmoid(x):
    return 0.5 * jnp.tanh(0.5 * x) + 0.5


def _rms(x, eps=NORM_EPS):
    return x * lax.rsqrt(jnp.mean(x * x, axis=-1, keepdims=True) + eps)


def _mod_kernel(c_ref, w_ref, b_ref, o_ref):
    c = c_ref[...]
    o_ref[...] = _dot_split(c * _sigmoid(c), w_ref[...]) + b_ref[...]


def _mod_call(cvec8, w_mod, b_mod, layer):
    tn = 1024
    return pl.pallas_call(
        _mod_kernel,
        grid=(3 * D_MODEL // tn,),
        in_specs=[pl.BlockSpec((8, D_MODEL), lambda j: (0, 0)),
                  pl.BlockSpec((None, D_MODEL, tn), lambda j: (layer, 0, j)),
                  pl.BlockSpec((None, 1, tn), lambda j: (layer, 0, j))],
        out_specs=pl.BlockSpec((8, tn), lambda j: (0, j)),
        out_shape=jax.ShapeDtypeStruct((8, 3 * D_MODEL), F32),
        compiler_params=pltpu.CompilerParams(vmem_limit_bytes=VMEM_LIMIT),
        name="mod",
    )(cvec8, w_mod, b_mod.reshape(DEPTH, 1, 3 * D_MODEL))


def _inproj_kernel(x_ref, mod_ref, g_ref, wa_ref, wb_ref, oa_ref, ob_ref):
    m = mod_ref[...]
    shift = m[:, 0:D_MODEL]
    scale1 = 1.0 + m[:, D_MODEL:2 * D_MODEL]
    tm = x_ref.shape[0]
    for r0 in range(0, tm, INPROJ_SUB):
        rows = slice(r0, r0 + INPROJ_SUB)
        h = (_rms(x_ref[rows, :]) * g_ref[...] * scale1 + shift).astype(BF16)
        for w_ref, o_ref in ((wa_ref, oa_ref), (wb_ref, ob_ref)):
            for c0 in range(0, w_ref.shape[1], INPROJ_TN):
                cols = slice(c0, c0 + INPROJ_TN)
                o_ref[rows, cols] = _dot(h, w_ref[:, cols]).astype(BF16)


def _inproj_call(x2d, mod3, norm_g, w_a, w_b, layer, rows_per_mod, mod_base):
    rows = x2d.shape[0]
    tm = INPROJ_TM
    tiles_per_mod = rows_per_mod // tm
    resident = lambda w: pl.BlockSpec((None,) + w.shape[1:], lambda i: (layer, 0, 0), pipeline_mode=pl.Buffered(1))
    return pl.pallas_call(
        _inproj_kernel,
        grid=(rows // tm,),
        in_specs=[pl.BlockSpec((tm, D_MODEL), lambda i: (i, 0)),
                  pl.BlockSpec((None, 1, 3 * D_MODEL), lambda i: (mod_base + i // tiles_per_mod, 0, 0)),
                  pl.BlockSpec((1, D_MODEL), lambda i: (0, 0)),
                  resident(w_a), resident(w_b)],
        out_specs=[pl.BlockSpec((tm, PACK_A), lambda i: (i, 0)),
                   pl.BlockSpec((tm, PACK_B), lambda i: (i, 0))],
        out_shape=[jax.ShapeDtypeStruct((rows, PACK_A), BF16),
                   jax.ShapeDtypeStruct((rows, PACK_B), BF16)],
        compiler_params=pltpu.CompilerParams(vmem_limit_bytes=VMEM_LIMIT),
        name="inproj",
    )(x2d, mod3, norm_g, w_a, w_b)


RW_CHUNK = 64
RW_PAIRS = RW_HEADS // 2
RW_BLOCK_ROWS = 256
RW_NEG_DECAY_SCALE = -float(np.exp(-0.5))
RW_UNROLL = 2


def _rwkv_time_mask(c, reverse):
    row = lax.broadcasted_iota(jnp.int32, (2 * c, 4 * c), 0)
    col = lax.broadcasted_iota(jnp.int32, (2 * c, 4 * c), 1)
    t = jnp.where(row >= c, row - c, row)
    s = col & (c - 1)
    earlier = (s > t) if reverse else (s < t)
    return earlier | ((row >= c) & (s == t))


def _rwkv_kernel(*refs, seq, has_s0):
    if has_s0:
        r_ref, k_ref, v_ref, wa_ref, s0_ref = refs[:5]
        rest = refs[5:]
    else:
        r_ref, k_ref, v_ref, wa_ref = refs[:4]
        s0_ref = None
        rest = refs[4:]
    (wcat_ref, wa0_ref, kk_ref, ka_ref, rk_ref, lng_ref, lnb_ref, hsum_ref,
     o_ref, sout_ref, acc_ref, kn_ref, st_ref) = rest
    c = RW_CHUNK
    nc = seq // c
    blk = min(RW_BLOCK_ROWS, seq)
    hsum = hsum_ref[...]

    def pre_step(i, carry):
        rows = pl.ds(pl.multiple_of(i * blk, blk), blk)
        kk0 = k_ref[rows, :].astype(F32) * kk_ref[...]
        kn_ref[rows, :] = kk0 / jnp.maximum(jnp.sqrt(_dot((kk0 * kk0).astype(BF16), hsum)), 1e-12)
        return carry

    def post_step(i, carry):
        rows = pl.ds(pl.multiple_of(i * blk, blk), blk)
        o = acc_ref[0, rows, :] + acc_ref[1, rows, :]
        rk = r_ref[rows, :].astype(F32) * k_ref[rows, :].astype(F32) * rk_ref[...]
        red = _dot(jnp.concatenate([o, rk], axis=0).astype(BF16), hsum)
        dev = o - red[0:blk] * (1.0 / RW_HEAD)
        var = _dot((dev * dev).astype(BF16), hsum) * (1.0 / RW_HEAD)
        o = dev * lax.rsqrt(var + RW_GN_EPS) * lng_ref[...] + lnb_ref[...]
        o_ref[rows, :] = (o + red[blk:2 * blk] * v_ref[rows, :]).astype(o_ref.dtype)
        return carry

    even = lax.broadcasted_iota(jnp.int32, (1, LANE), 1) < RW_HEAD
    row = lax.broadcasted_iota(jnp.int32, (c, c), 0)
    col = lax.broadcasted_iota(jnp.int32, (c, c), 1)
    incl16 = [jnp.where(col <= row, 1.0, 0.0).astype(BF16), jnp.where(col >= row, 1.0, 0.0).astype(BF16)]
    tmask = [_rwkv_time_mask(c, False), _rwkv_time_mask(c, True)]
    r2 = lax.broadcasted_iota(jnp.int32, (LANE, LANE), 0)
    c2 = lax.broadcasted_iota(jnp.int32, (LANE, LANE), 1)
    same_head = (r2 < RW_HEAD) == (c2 < RW_HEAD)
    z16 = jnp.zeros((c, LANE), BF16)
    lanes = [slice(p * LANE, (p + 1) * LANE) for p in range(RW_PAIRS)]
    probs = [(d, p) for d in range(2) for p in range(RW_PAIRS)]

    def chunk_operands(d, ci):
        rows = pl.ds(pl.multiple_of(ci * c, c), c)
        r = r_ref[rows, :].astype(F32)
        k = k_ref[rows, :].astype(F32)
        wa = wa_ref[rows, :].astype(F32)
        kk = kn_ref[rows, :]
        lora = jnp.where(even, jnp.tanh(wa), wa)
        pre = _dot(lora.astype(BF16), wcat_ref[d]) + wa0_ref[d:d + 1, :]
        lw = RW_NEG_DECAY_SCALE * _sigmoid(pre[:, 0:RW_WIDTH])
        a = _sigmoid(pre[:, RW_WIDTH:2 * RW_WIDTH])
        kd = k * (1.0 + (a - 1.0) * ka_ref[...])
        bv = kk * a
        gam = _dot_sel(incl16[d], lw)
        last = 0 if d == 1 else c - 1
        gtot = gam[last:last + 1, :]
        gref = gam[c // 2:c // 2 + 1, :]
        e_in = jnp.exp(gam - gref)
        e_out = jnp.exp(gref - gam)
        e_end = e_out * jnp.exp(gtot - gref)
        lhs = jnp.concatenate([-kk * jnp.exp(gam - lw - gref), r * e_in], axis=0).astype(BF16)
        return dict(rows=rows, lhs=lhs, b16=(bv * e_out).astype(BF16), k16=(kd * e_out).astype(BF16),
                    be16=(bv * e_end).astype(BF16), ke16=(kd * e_end).astype(BF16),
                    v16=v_ref[rows, :].astype(BF16), dtot=jnp.exp(gtot), eref=jnp.exp(gref))

    def main_step(i, carry):
        steps = [i * RW_UNROLL + s for s in range(RW_UNROLL)]
        prepared = [[chunk_operands(0, t), chunk_operands(1, nc - 1 - t)] for t in steps]
        for ops in prepared:
            chain_step(ops)
        return carry

    def chain_step(ops):
        n = range(len(probs))
        pick = lambda name: [ops[d][name][:, lanes[p]] for d, p in probs]
        lhs, b, k, be, ke, v = (pick(s) for s in ("lhs", "b16", "k16", "be16", "ke16", "v16"))
        st = [st_ref[d, p] for d, p in probs]
        first = lambda x: jnp.where(even, x, jnp.zeros_like(x))
        second = lambda x: jnp.where(even, jnp.zeros_like(x), x)
        v_e = [first(x) for x in v]
        v_o = [second(x) for x in v]
        aa = [jnp.where(tmask[d], _dot_nt(lhs[j], jnp.concatenate(
            [first(k[j]), first(b[j]), second(b[j]), second(k[j])], axis=0)), 0.0) for j, (d, p) in enumerate(probs)]
        sp = [_dot_nt(lhs[j], (st[j] * ops[d]["eref"][:, lanes[p]]).astype(BF16))
              for j, (d, p) in enumerate(probs)]
        aa16 = [x.astype(BF16) for x in aa]
        rhs_u = [_dot(aa16[j][0:c], jnp.concatenate([v_e[j], z16, z16, v_o[j]], axis=0)) + sp[j][0:c] for j in n]
        x_e = [jnp.where(even, rhs_u[j], aa[j][0:c, 0:LANE]) for j in n]
        x_o = [jnp.where(even, aa[j][0:c, LANE:2 * LANE], rhs_u[j]) for j in n]
        covered = 1
        while covered < c:
            e16 = [x.astype(BF16) for x in x_e]
            o16 = [x.astype(BF16) for x in x_o]
            x_e = [_dot(e16[j], jnp.concatenate([z16, e16[j]], axis=0)) + first(x_e[j]) for j in n]
            x_o = [_dot(o16[j], jnp.concatenate([o16[j], z16], axis=0)) + second(x_o[j]) for j in n]
            covered *= 2
        u16 = [jnp.where(even, x_e[j], x_o[j]).astype(BF16) for j in n]
        for j, (d, p) in enumerate(probs):
            uv_rows = jnp.concatenate([v_e[j], first(u16[j]), second(u16[j]), v_o[j]], axis=0)
            acc_ref[d, ops[d]["rows"], lanes[p]] = sp[j][c:2 * c] + _dot(aa16[j][c:2 * c], uv_rows)
            upd = _dot_tn(jnp.concatenate([u16[j], v[j]], axis=0), jnp.concatenate([be[j], ke[j]], axis=0))
            st_ref[d, p] = st[j] * ops[d]["dtot"][:, lanes[p]] + jnp.where(same_head, upd, 0.0)

    st_ref[...] = jnp.zeros_like(st_ref)
    if has_s0:
        for d in range(2):
            for h in range(RW_HEADS):
                off = (h % 2) * RW_HEAD
                st_ref[d, h // 2, off:off + RW_HEAD, off:off + RW_HEAD] = s0_ref[d, h]
    lax.fori_loop(0, seq // blk, pre_step, 0)
    lax.fori_loop(0, nc // RW_UNROLL, main_step, 0)
    for d in range(2):
        for h in range(RW_HEADS):
            off = (h % 2) * RW_HEAD
            sout_ref[d, h] = st_ref[d, h // 2, off:off + RW_HEAD, off:off + RW_HEAD]
    lax.fori_loop(0, seq // blk, post_step, 0)


def _rwkv_call(proj, s0, layer, batch, seq, wts):
    has_s0 = s0 is not None
    blk = lambda w, cb: pl.BlockSpec((seq, w), lambda b: (b, cb))
    in_specs = [blk(512, COL_RW_R // 512), blk(512, COL_RW_K // 512), blk(512, COL_RW_V // 512),
                blk(LANE, COL_RW_WA // LANE)]
    args = [proj, proj, proj, proj]
    if has_s0:
        in_specs.append(pl.BlockSpec((None, None, 2, RW_HEADS, RW_HEAD, RW_HEAD),
                                     lambda b: (b, layer, 0, 0, 0, 0)))
        args.append(s0)
    for w in wts:
        in_specs.append(pl.BlockSpec(w.shape, lambda b, n=w.ndim: (0,) * n))
        args.append(w)
    return pl.pallas_call(
        functools.partial(_rwkv_kernel, seq=seq, has_s0=has_s0),
        grid=(batch,),
        in_specs=in_specs,
        out_specs=[pl.BlockSpec((seq, RW_WIDTH), lambda b: (b, 0)),
                   pl.BlockSpec((None, 2, RW_HEADS, RW_HEAD, RW_HEAD), lambda b: (b, 0, 0, 0, 0))],
        out_shape=[jax.ShapeDtypeStruct((batch * seq, RW_WIDTH), BF16),
                   jax.ShapeDtypeStruct((batch, 2, RW_HEADS, RW_HEAD, RW_HEAD), F32)],
        scratch_shapes=[pltpu.VMEM((2, seq, RW_WIDTH), F32),
                        pltpu.VMEM((seq, RW_WIDTH), F32),
                        pltpu.VMEM((2, RW_PAIRS, LANE, LANE), F32)],
        compiler_params=pltpu.CompilerParams(vmem_limit_bytes=VMEM_LIMIT),
        name="rwkv",
    )(*args)


GLA_LEVELS = (32, 16, 8, 4, 2, 1)
GLA_GROUP = 4


def _gla_constants():
    c = CHUNK
    nl = len(GLA_LEVELS)
    mexp = np.zeros((2, (nl + 1) * c, c), np.float32)
    bmask = np.zeros((2, nl + 1, c, c), np.float32)
    for d in range(2):
        pos = np.arange(c) if d == 0 else c - 1 - np.arange(c)
        pt = pos[:, None]
        pj = pos[None, :]
        mexp[d, nl * c:] = (pj <= pt)
        for li, m in enumerate(GLA_LEVELS):
            mid = (pos // (2 * m)) * (2 * m) + m
            second = pos >= mid
            mq = (pj >= mid[:, None]) & (pj <= pt) & second[:, None]
            mk = (pj > pt) & (pj <= mid[:, None] - 1) & (~second)[:, None]
            mexp[d, li * c:(li + 1) * c] = mq | mk
            same = (pos[:, None] // (2 * m)) == (pos[None, :] // (2 * m))
            bmask[d, li] = same & second[:, None] & (~second)[None, :]
        bmask[d, nl] = np.eye(c)
    return mexp, np.concatenate([bmask, bmask], axis=-1)


def _gla_kernel(*refs, seq, has_s0):
    if has_s0:
        q_ref, k_ref, v_ref, ad_ref, s0_ref = refs[:5]
        rest = refs[5:]
    else:
        q_ref, k_ref, v_ref, ad_ref = refs[:4]
        s0_ref = None
        rest = refs[4:]
    (a2_ref, ab_ref, ng_ref, mexp_ref, bmask_ref,
     o_ref, sout_ref, acc_ref, st_ref) = rest
    c = CHUNK
    nc = seq // c
    nl = len(GLA_LEVELS)

    def chunk_operands(d, ci):
        rows = pl.ds(pl.multiple_of(ci * c, c), c)
        q = q_ref[rows, :].astype(F32) * (GLA_DK ** -0.5)
        k = k_ref[rows, :].astype(F32)
        v = v_ref[rows, :]
        x = _dot(ad_ref[rows, :].astype(BF16), a2_ref[d]) + ab_ref[d:d + 1, :]
        g = (jnp.minimum(x, 0.0) - jnp.log1p(jnp.exp(-jnp.abs(x)))) * (1.0 / GLA_LOGIT_NORM)
        g_hi, g_lo = _split(g)
        sums = _dot(mexp_ref[d], g_hi)
        ex = jnp.exp(sums[0:nl * c])
        b = sums[nl * c:(nl + 1) * c] + _dot(mexp_ref[d, nl * c:(nl + 1) * c, :], g_lo)
        last = 0 if d == 1 else c - 1
        blast = b[last:last + 1, :]
        qb = (q * jnp.exp(b)).astype(BF16)
        kdec = (k * jnp.exp(blast - b)).astype(BF16)
        dtot = jnp.exp(blast)
        qs = [(q * ex[li * c:(li + 1) * c]).astype(BF16) for li in range(nl)] + [q.astype(BF16)]
        ks = [(k * ex[li * c:(li + 1) * c]).astype(BF16) for li in range(nl)] + [k.astype(BF16)]
        return dict(rows=rows, qs=qs, ks=ks, qb=qb, kdec=kdec, dtot=dtot, v16=v.astype(BF16))

    even = lax.broadcasted_iota(jnp.int32, (1, LANE), 1) < GLA_DK
    r2 = lax.broadcasted_iota(jnp.int32, (2 * GLA_DV, LANE), 0)
    c2 = lax.broadcasted_iota(jnp.int32, (2 * GLA_DV, LANE), 1)
    same_head = (r2 < GLA_DV) == (c2 < GLA_DK)
    zv = jnp.zeros((c, GLA_DV), BF16)
    npair = GLA_HEADS // 2
    kls = [slice(p * LANE, (p + 1) * LANE) for p in range(npair)]
    vls = [slice(p * 2 * GLA_DV, (p + 1) * 2 * GLA_DV) for p in range(npair)]
    probs = [(d, p) for d in range(2) for p in range(npair)]
    first = lambda x: jnp.where(even, x, jnp.zeros_like(x))
    second = lambda x: jnp.where(even, jnp.zeros_like(x), x)

    group = min(GLA_GROUP, nc)

    def main_step(i, carry):
        ops = [[chunk_operands(0, i * group + g) for g in range(group)],
               [chunk_operands(1, nc - 1 - (i * group + g)) for g in range(group)]]
        allp = [(d, p, g) for d, p in probs for g in range(group)]
        lvl = [[_dot_nt(ops[d][g]["qs"][li][:, kls[p]],
                        jnp.concatenate([first(ops[d][g]["ks"][li][:, kls[p]]),
                                         second(ops[d][g]["ks"][li][:, kls[p]])], axis=0))
                for d, p, g in allp] for li in range(nl + 1)]
        att = [sum(bmask_ref[d, li] * lvl[li][j] for li in range(nl + 1)).astype(BF16)
               for j, (d, p, g) in enumerate(allp)]
        upd = [jnp.where(same_head, _dot_tn(ops[d][g]["v16"][:, vls[p]], ops[d][g]["kdec"][:, kls[p]]), 0.0)
               for d, p, g in allp]
        states = []
        for d, p in probs:
            st = st_ref[d, p]
            for g in range(group):
                states.append(st)
                st = st * ops[d][g]["dtot"][:, kls[p]] + upd[len(states) - 1]
            st_ref[d, p] = st
        for j, (d, p, g) in enumerate(allp):
            v_p = ops[d][g]["v16"][:, vls[p]]
            v_bd = jnp.concatenate([jnp.concatenate([v_p[:, 0:GLA_DV], zv], axis=1),
                                    jnp.concatenate([zv, v_p[:, GLA_DV:2 * GLA_DV]], axis=1)], axis=0)
            inter = _dot_nt(ops[d][g]["qb"][:, kls[p]], states[j].astype(BF16))
            acc_ref[d, ops[d][g]["rows"], vls[p]] = inter + _dot(att[j], v_bd)
        return carry

    blk = min(RW_BLOCK_ROWS, seq)

    def post_step(i, carry):
        rows = pl.ds(pl.multiple_of(i * blk, blk), blk)
        for h in range(GLA_HEADS):
            sv = slice(h * GLA_DV, (h + 1) * GLA_DV)
            o_ref[rows, sv] = (_rms(acc_ref[0, rows, sv] + acc_ref[1, rows, sv]) * ng_ref[...]).astype(o_ref.dtype)
        return carry

    st_ref[...] = jnp.zeros_like(st_ref)
    if has_s0:
        for d in range(2):
            for h in range(GLA_HEADS):
                ro, co = (h % 2) * GLA_DV, (h % 2) * GLA_DK
                st_ref[d, h // 2, ro:ro + GLA_DV, co:co + GLA_DK] = s0_ref[d, h].T
    lax.fori_loop(0, nc // group, main_step, 0)
    for d in range(2):
        for h in range(GLA_HEADS):
            ro, co = (h % 2) * GLA_DV, (h % 2) * GLA_DK
            sout_ref[d, h] = st_ref[d, h // 2, ro:ro + GLA_DV, co:co + GLA_DK].T
    lax.fori_loop(0, seq // blk, post_step, 0)


def _gla_call(proj, s0, layer, batch, seq, wts):
    has_s0 = s0 is not None
    blk = lambda w, cb: pl.BlockSpec((seq, w), lambda b: (b, cb))
    in_specs = [blk(GLA_KW, COL_GL_Q // GLA_KW), blk(GLA_KW, COL_GL_K // GLA_KW),
                blk(GLA_VW, COL_GL_V // GLA_VW), blk(LANE, COL_GL_AD // LANE)]
    args = [proj, proj, proj, proj]
    if has_s0:
        in_specs.append(pl.BlockSpec((None, None, 2, GLA_HEADS, GLA_DK, GLA_DV),
                                     lambda b: (b, layer, 0, 0, 0, 0)))
        args.append(s0)
    for w in wts:
        in_specs.append(pl.BlockSpec(w.shape, lambda b, n=w.ndim: (0,) * n))
        args.append(w)
    return pl.pallas_call(
        functools.partial(_gla_kernel, seq=seq, has_s0=has_s0),
        grid=(batch,),
        in_specs=in_specs,
        out_specs=[pl.BlockSpec((seq, GLA_VW), lambda b: (b, 0)),
                   pl.BlockSpec((None, 2, GLA_HEADS, GLA_DK, GLA_DV), lambda b: (b, 0, 0, 0, 0))],
        out_shape=[jax.ShapeDtypeStruct((batch * seq, GLA_VW), BF16),
                   jax.ShapeDtypeStruct((batch, 2, GLA_HEADS, GLA_DK, GLA_DV), F32)],
        scratch_shapes=[pltpu.VMEM((2, seq, GLA_VW), F32),
                        pltpu.VMEM((2, GLA_HEADS // 2, 2 * GLA_DV, LANE), F32)],
        compiler_params=pltpu.CompilerParams(vmem_limit_bytes=VMEM_LIMIT),
        name="gla",
    )(*args)


def _mla_kernel(*refs, seq, past, tq):
    has_ctx = past > 0
    if has_ctx:
        (qd_ref, kvd_ref, kpe_ref, kpesw_ref, cckv_ref, ckpe_ref,
         qng_ref, wq_ref, wqsw_ref, kvng_ref, wk_ref, wv_ref,
         cosq_ref, sinq_ref, cosk_ref, sink_ref, epos_ref,
         o_ref, ckv_ref, kcat_ref, vv_ref) = refs
    else:
        (qd_ref, kvd_ref, kpe_ref,
         qng_ref, wq_ref, kvng_ref, wk_ref, wv_ref,
         o_ref, ckv_ref, kcat_ref, vv_ref) = refs

    ones_hi = jnp.where(lax.broadcasted_iota(jnp.int32, (1, LANE), 1) >= MLA_V, 1.0, 0.0)

    @pl.when(pl.program_id(1) == 0)
    def _():
        ckv = _rms(kvd_ref[...].astype(F32)) * kvng_ref[...]
        ckv_ref[...] = ckv
        if has_ctx:
            kpos = kpe_ref[...] * cosk_ref[...] + kpesw_ref[...] * sink_ref[...]
        else:
            kpos = kpe_ref[...]
        segs = [(0, seq, ckv, kpos)]
        if has_ctx:
            cpos = _dot(ckpe_ref[...].astype(BF16), epos_ref[...])
            segs.append((seq, past, cckv_ref[...], cpos))
        for start, n, lat, pos in segs:
            lat16 = lat.astype(BF16)
            kc = _dot(lat16, wk_ref[...])
            vc = _dot(lat16, wv_ref[...])
            for h in range(MLA_HEADS):
                hl = slice(h * LANE, (h + 1) * LANE)
                kcat_ref[h, start:start + n, :] = (kc[:, hl] + pos).astype(BF16)
                vv_ref[h, start:start + n, :] = (vc[:, hl] + ones_hi).astype(BF16)

    qlat = (_rms(qd_ref[...].astype(F32)) * qng_ref[...]).astype(BF16)
    qc = _dot(qlat, wq_ref[...])
    if has_ctx:
        qsw = _dot(qlat, wqsw_ref[...])
    for h in range(MLA_HEADS):
        hl = slice(h * LANE, (h + 1) * LANE)
        q_raw = qc[:, hl]
        if has_ctx:
            q_self = (q_raw * cosq_ref[...] + qsw[:, hl] * sinq_ref[...]).astype(BF16)
        else:
            q_self = q_raw.astype(BF16)
        s1 = _dot_nt(q_self, kcat_ref[h, 0:seq, :])
        m = jnp.max(s1, axis=-1, keepdims=True)
        if has_ctx:
            s2 = _dot_nt(q_raw.astype(BF16), kcat_ref[h, seq:seq + past, :])
            m = jnp.maximum(m, jnp.max(s2, axis=-1, keepdims=True))
        o_h = _dot(jnp.exp((s1 - m).astype(BF16)), vv_ref[h, 0:seq, :])
        if has_ctx:
            o_h = o_h + _dot(jnp.exp((s2 - m).astype(BF16)), vv_ref[h, seq:seq + past, :])
        o_h = o_h / o_h[:, MLA_V:MLA_V + 1]
        o_ref[:, h * MLA_V:(h + 1) * MLA_V] = o_h[:, 0:MLA_V].astype(o_ref.dtype)


def _mla_call(proj, cache_ckv, cache_kpe, layer, batch, seq, wts, tables):
    has_ctx = cache_ckv is not None
    past = cache_ckv.shape[2] if has_ctx else 0
    tq = min(MLA_TQ, seq)
    nq = seq // tq
    full = lambda cb: pl.BlockSpec((seq, LANE), lambda b, i: (b, cb))
    in_specs = [pl.BlockSpec((tq, MLA_Q_RANK), lambda b, i: (b * nq + i, COL_ML_QD // MLA_Q_RANK)),
                full(COL_ML_KVD // LANE), full(COL_KPE // LANE)]
    args = [proj, proj, proj]
    if has_ctx:
        in_specs += [full(COL_KPE_SW // LANE),
                     pl.BlockSpec((None, None, past, MLA_KV_RANK), lambda b, i: (b, layer, 0, 0)),
                     pl.BlockSpec((None, None, past, MLA_ROPE), lambda b, i: (b, layer, 0, 0))]
        args += [proj, cache_ckv, cache_kpe]
    qn_g, wq_cat, wq_sw, kvn_g, wk_pad, wv = wts
    const = lambda w: pl.BlockSpec(w.shape, lambda b, i, n=w.ndim: (0,) * n)
    if has_ctx:
        cosq, sinq, cosk, sink, epos = tables
        wlist = [qn_g, wq_cat, wq_sw, kvn_g, wk_pad, wv]
        in_specs += [const(w) for w in wlist]
        in_specs += [pl.BlockSpec((tq, LANE), lambda b, i: (i, 0)), pl.BlockSpec((tq, LANE), lambda b, i: (i, 0)),
                     const(cosk), const(sink), const(epos)]
        args += wlist + [cosq, sinq, cosk, sink, epos]
    else:
        wlist = [qn_g, wq_cat, kvn_g, wk_pad, wv]
        in_specs += [const(w) for w in wlist]
        args += wlist
    return pl.pallas_call(
        functools.partial(_mla_kernel, seq=seq, past=past, tq=tq),
        grid=(batch, nq),
        in_specs=in_specs,
        out_specs=[pl.BlockSpec((tq, MLA_VW), lambda b, i: (b * nq + i, 0)),
                   pl.BlockSpec((seq, MLA_KV_RANK), lambda b, i: (b, 0))],
        out_shape=[jax.ShapeDtypeStruct((batch * seq, MLA_VW), BF16),
                   jax.ShapeDtypeStruct((batch * seq, MLA_KV_RANK), F32)],
        scratch_shapes=[pltpu.VMEM((MLA_HEADS, seq + past, LANE), BF16),
                        pltpu.VMEM((MLA_HEADS, seq + past, LANE), BF16)],
        compiler_params=pltpu.CompilerParams(vmem_limit_bytes=VMEM_LIMIT,
                                             dimension_semantics=("arbitrary", "arbitrary")),
        name="mla",
    )(*args)


def _outproj_kernel(x_ref, oa_ref, ob_ref, oc_ref, ga_ref, gb_ref, gc_ref, ma_ref, mb_ref, mc_ref,
                    mod_ref, wa_ref, wb_ref, wc_ref, wo_ref, fg_ref, o_ref, *, final):
    def branch(o_r, g_r, m_r, w_r):
        h = 0.5 * g_r[...]
        act = o_r[...] * (h * (1.0 + jnp.tanh(h)))
        y = _dot(act, w_r[...]).astype(BF16)
        return (0.5 * (1.0 + jnp.tanh(0.5 * m_r[...]))) * y

    y = (branch(oa_ref, ga_ref, ma_ref, wa_ref) + branch(ob_ref, gb_ref, mb_ref, wb_ref)
         + branch(oc_ref, gc_ref, mc_ref, wc_ref))
    y = _dot(y, wo_ref[...])
    gate = mod_ref[...][:, 2 * D_MODEL:3 * D_MODEL]
    hn = x_ref[...] + gate * y
    if final:
        hn = _rms(hn) * fg_ref[...]
    o_ref[...] = hn


def _outproj_call(x2d, o_a, o_b, o_c, proj, mod3, wts, final_g, rows_per_mod, mod_base, final):
    rows = x2d.shape[0]
    tm = OUTPROJ_TM
    tiles_per_mod = rows_per_mod // tm
    row = lambda w, cb=0: pl.BlockSpec((tm, w), lambda i: (i, cb))
    const = lambda w: pl.BlockSpec(w.shape, lambda i, n=w.ndim: (0,) * n)
    in_specs = [row(D_MODEL), row(512), row(512), row(512),
                row(512, COL_GATES // 512), row(512, COL_GATES // 512 + 1), row(512, COL_GATES // 512 + 2),
                row(D_MODEL, COL_MERGE // D_MODEL), row(D_MODEL, COL_MERGE // D_MODEL + 1),
                row(D_MODEL, COL_MERGE // D_MODEL + 2),
                pl.BlockSpec((None, 1, 3 * D_MODEL), lambda i: (mod_base + i // tiles_per_mod, 0, 0))]
    in_specs += [const(w) for w in wts] + [const(final_g)]
    return pl.pallas_call(
        functools.partial(_outproj_kernel, final=final),
        grid=(rows // tm,),
        in_specs=in_specs,
        out_specs=row(D_MODEL),
        out_shape=jax.ShapeDtypeStruct((rows, D_MODEL), F32),
        compiler_params=pltpu.CompilerParams(vmem_limit_bytes=VMEM_LIMIT),
        name="outproj",
    )(x2d, o_a, o_b, o_c, proj, proj, proj, proj, proj, proj, mod3, *wts, final_g)


def _pack_kernel(wt_ref, wa_ref, wb_ref):
    seg = lambda i: wt_ref[_IN_OFF[i]:_IN_OFF[i + 1], :]
    tr = wt_ref.shape[1]
    z = lambda n: jnp.zeros((n, tr), F32)

    def put(ref, col, xt):
        ref[:, col:col + xt.shape[0]] = xt.T.astype(BF16)

    put(wa_ref, COL_RW_R, wt_ref[_IN_OFF[0]:_IN_OFF[5], :])
    put(wa_ref, COL_GL_AD, jnp.concatenate([seg(9), z(LANE - GLA_GATE_RANK)], axis=0))
    put(wa_ref, COL_ML_QD, seg(11))
    put(wa_ref, COL_GL_Q, wt_ref[_IN_OFF[6]:_IN_OFF[9], :])
    put(wa_ref, COL_ML_KVD, seg(12))
    kpe = seg(13)
    q = MLA_ROPE // 4
    kpe_sw = jnp.concatenate([kpe[q:2 * q], kpe[0:q], kpe[3 * q:4 * q], kpe[2 * q:3 * q]], axis=0)
    tail = LANE - KPE_LANE - MLA_ROPE
    put(wa_ref, COL_KPE, jnp.concatenate([z(KPE_LANE), kpe, z(tail)], axis=0))
    put(wa_ref, COL_KPE_SW, jnp.concatenate([z(KPE_LANE), kpe_sw, z(tail)], axis=0))
    wa_ref[:, COL_KPE_SW + LANE:PACK_A] = jnp.zeros((tr, PACK_A - COL_KPE_SW - LANE), BF16)
    put(wb_ref, COL_MERGE, seg(15))
    put(wb_ref, COL_GATES, seg(5))
    put(wb_ref, COL_GATES + RW_WIDTH, seg(10))
    put(wb_ref, COL_GATES + RW_WIDTH + GLA_VW, seg(14))


def _pack_call(w_in):
    tr = 256
    wt = jnp.swapaxes(w_in, 1, 2)
    width = wt.shape[1]
    return pl.pallas_call(
        _pack_kernel,
        grid=(DEPTH, D_MODEL // tr),
        in_specs=[pl.BlockSpec((None, width, tr), lambda l, i: (l, 0, i))],
        out_specs=[pl.BlockSpec((None, tr, PACK_A), lambda l, i: (l, i, 0)),
                   pl.BlockSpec((None, tr, PACK_B), lambda l, i: (l, i, 0))],
        out_shape=[jax.ShapeDtypeStruct((DEPTH, D_MODEL, PACK_A), BF16),
                   jax.ShapeDtypeStruct((DEPTH, D_MODEL, PACK_B), BF16)],
        compiler_params=pltpu.CompilerParams(vmem_limit_bytes=VMEM_LIMIT),
        name="pack",
    )(wt)


def _rope_lane_tables(seq):
    n_freq = MLA_ROPE // 4
    t = np.arange(seq)
    inv = ROPE_THETA ** (-np.arange(n_freq, dtype=np.float64) / n_freq)
    ang = np.stack([(t // GRID_W)[:, None] * inv, (t % GRID_W)[:, None] * inv], axis=1)
    cos = np.repeat(np.cos(ang)[:, :, None, :], 2, axis=2).reshape(seq, MLA_ROPE)
    sin = np.stack([-np.sin(ang), np.sin(ang)], axis=2).reshape(seq, MLA_ROPE)
    return cos, sin


def _mla_tables(seq):
    cos, sin = _rope_lane_tables(seq)
    cosq = np.zeros((seq, LANE), np.float32)
    sinq = np.zeros((seq, LANE), np.float32)
    cosq[:, :KPE_LANE] = 1.0
    cosq[:, KPE_LANE:KPE_LANE + MLA_ROPE] = cos
    sinq[:, KPE_LANE:KPE_LANE + MLA_ROPE] = sin
    cosk = np.zeros((seq, LANE), np.float32)
    cosk[:, KPE_LANE:KPE_LANE + MLA_ROPE] = cos
    epos = np.zeros((MLA_ROPE, LANE), np.float32)
    epos[np.arange(MLA_ROPE), KPE_LANE + np.arange(MLA_ROPE)] = 1.0
    return (jnp.asarray(cosq), jnp.asarray(sinq), jnp.asarray(cosk), jnp.asarray(sinq),
            jnp.asarray(epos, dtype=BF16))


def _pack_mla_weights(qn_g, wq_up, kvn_g, wkv_up):
    scale = (MLA_NOPE + MLA_ROPE) ** -0.5
    wq = wq_up.reshape(MLA_Q_RANK, MLA_HEADS, MLA_NOPE + MLA_ROPE) * scale
    nope, rope = wq[..., :MLA_NOPE], wq[..., MLA_NOPE:]
    q = MLA_ROPE // 4
    rope_sw = jnp.concatenate([rope[..., q:2 * q], rope[..., 0:q], rope[..., 3 * q:4 * q], rope[..., 2 * q:3 * q]], -1)
    tail = jnp.zeros((MLA_Q_RANK, MLA_HEADS, LANE - KPE_LANE - MLA_ROPE), wq.dtype)
    wq_cat = jnp.concatenate([nope, rope, tail], -1).reshape(MLA_Q_RANK, MLA_HEADS * LANE).astype(BF16)
    wq_sw = jnp.concatenate([jnp.zeros_like(nope), rope_sw, tail], -1).reshape(MLA_Q_RANK, MLA_HEADS * LANE).astype(BF16)
    wkv = wkv_up.reshape(MLA_KV_RANK, MLA_HEADS, MLA_NOPE + MLA_V)
    wk = jnp.concatenate([wkv[..., :MLA_NOPE], jnp.zeros((MLA_KV_RANK, MLA_HEADS, LANE - MLA_NOPE), wkv.dtype)], -1)
    wk_pad = wk.reshape(MLA_KV_RANK, MLA_HEADS * LANE).astype(BF16)
    wv = jnp.concatenate([wkv[..., MLA_NOPE:], jnp.zeros((MLA_KV_RANK, MLA_HEADS, LANE - MLA_V), wkv.dtype)], -1)
    wv = wv.reshape(MLA_KV_RANK, MLA_HEADS * LANE).astype(BF16)
    return (qn_g.reshape(1, -1), wq_cat, wq_sw, kvn_g.reshape(1, -1), wk_pad, wv)


def _head_sum_matrix():
    lane = np.arange(RW_WIDTH)
    return jnp.asarray((lane[:, None] // RW_HEAD == lane[None, :] // RW_HEAD).astype(np.float32), dtype=BF16)


def _trunk(x_prompt, x_sample, c, cache_ckv, cache_kpe, state_rwkv, state_gla, c_ctx,
           norm_g, w_mod, b_mod, w_in, rw_w0, rw_w2, rw_a0, rw_a2, rw_k_k, rw_k_a, rw_r_k,
           rw_ln_g, rw_ln_b, rw_out, gla_a2, gla_ab, gla_norm_g, gla_out,
           mla_qn_g, mla_wq_up, mla_kvn_g, mla_wkv_up, mla_out, w_out, final_g):
    bp, tp, _ = x_prompt.shape
    bs, ts, _ = x_sample.shape
    hp = x_prompt.reshape(bp * tp, D_MODEL)
    hs = x_sample.reshape(bs * ts, D_MODEL)
    cvec8 = jnp.concatenate([c, c_ctx[None, :], jnp.zeros((8 - bs - 1, D_MODEL), F32)], axis=0)
    ctx_row = bs
    hsum = _head_sum_matrix()
    mexp, bmask = _gla_constants()
    mexp = jnp.asarray(mexp, dtype=BF16)
    bmask = jnp.asarray(bmask)
    tables = _mla_tables(ts)
    w_a, w_b = _pack_call(w_in)
    fg = final_g.reshape(1, D_MODEL)
    ckv_l, kpe_l, rw_l, gla_l = [], [], [], []
    for l in range(DEPTH):
        mod3 = _mod_call(cvec8, w_mod, b_mod, l).reshape(8, 1, 3 * D_MODEL)
        ng = norm_g[l].reshape(1, D_MODEL)
        row = lambda a: a.reshape(1, -1)
        zr = jnp.zeros((2, RW_RANK, RW_WIDTH), F32)
        wcat = jnp.concatenate([jnp.concatenate([rw_w2[l], zr], axis=2),
                                jnp.concatenate([zr, rw_a2[l]], axis=2)], axis=1).astype(BF16)
        rw_wts = (wcat, jnp.concatenate([rw_w0[l], rw_a0[l]], axis=1), row(rw_k_k[l]), row(rw_k_a[l]),
                  row(rw_r_k[l]), row(rw_ln_g[l]), row(rw_ln_b[l]), hsum)
        a2p = jnp.concatenate([gla_a2[l], jnp.zeros((2, LANE - GLA_GATE_RANK, GLA_KW), F32)], axis=1).astype(BF16)
        gla_wts = (a2p, gla_ab[l], row(gla_norm_g[l]), mexp, bmask)
        mla_wts = _pack_mla_weights(mla_qn_g[l], mla_wq_up[l], mla_kvn_g[l], mla_wkv_up[l])
        out_wts = (rw_out[l].astype(BF16), gla_out[l].astype(BF16), mla_out[l].astype(BF16), w_out[l].astype(BF16))
        final = l == DEPTH - 1

        proj, proj_b = _inproj_call(hp, mod3, ng, w_a, w_b, l, bp * tp, ctx_row)
        o_a, s_rw = _rwkv_call(proj, None, l, bp, tp, rw_wts)
        o_b, s_gla = _gla_call(proj, None, l, bp, tp, gla_wts)
        o_c, ckv = _mla_call(proj, None, None, l, bp, tp, mla_wts, None)
        ckv_l.append(ckv.reshape(bp, tp, MLA_KV_RANK))
        kpe_l.append(proj[:, COL_KPE + KPE_LANE:COL_KPE + KPE_LANE + MLA_ROPE].astype(F32).reshape(bp, tp, MLA_ROPE))
        rw_l.append(s_rw)
        gla_l.append(s_gla)
        hp = _outproj_call(hp, o_a, o_b, o_c, proj_b, mod3, out_wts, fg, bp * tp, ctx_row, final)

        proj, proj_b = _inproj_call(hs, mod3, ng, w_a, w_b, l, ts, 0)
        o_a, _ = _rwkv_call(proj, state_rwkv, l, bs, ts, rw_wts)
        o_b, _ = _gla_call(proj, state_gla, l, bs, ts, gla_wts)
        o_c, _ = _mla_call(proj, cache_ckv, cache_kpe, l, bs, ts, mla_wts, tables)
        hs = _outproj_call(hs, o_a, o_b, o_c, proj_b, mod3, out_wts, fg, ts, 0, final)

    return (hp.reshape(bp, tp, D_MODEL), hs.reshape(bs, ts, D_MODEL),
            jnp.stack(ckv_l, axis=1), jnp.stack(kpe_l, axis=1),
            jnp.stack(rw_l, axis=1), jnp.stack(gla_l, axis=1))


_trunk_jit = jax.jit(_trunk)


def kernel(x_prompt, x_sample, c, cache_ckv, cache_kpe, state_rwkv, state_gla, c_ctx, norm_g, w_mod, b_mod, w_in, rw_w0, rw_w2, rw_a0, rw_a2, rw_k_k, rw_k_a, rw_r_k, rw_ln_g, rw_ln_b, rw_out, gla_a2, gla_ab, gla_norm_g, gla_out, mla_qn_g, mla_wq_up, mla_kvn_g, mla_wkv_up, mla_out, w_out, final_g):
    return _trunk_jit(x_prompt, x_sample, c, cache_ckv, cache_kpe, state_rwkv, state_gla, c_ctx, norm_g, w_mod, b_mod, w_in, rw_w0, rw_w2, rw_a0, rw_a2, rw_k_k, rw_k_a, rw_r_k, rw_ln_g, rw_ln_b, rw_out, gla_a2, gla_ab, gla_norm_g, gla_out, mla_qn_g, mla_wq_up, mla_kvn_g, mla_wkv_up, mla_out, w_out, final_g)
```

```python
import functools

import numpy as np
import jax
import jax.numpy as jnp
from jax import lax
from jax.experimental import pallas as pl
from jax.experimental.pallas import tpu as pltpu

F32 = jnp.float32
BF16 = jnp.bfloat16

D_MODEL = 1024
DEPTH = 2
GRID_W = 64
NORM_EPS = 1e-6
RW_HEADS = 8
RW_HEAD = 64
RW_WIDTH = RW_HEADS * RW_HEAD
RW_RANK = 64
RW_GN_EPS = 64e-5
GLA_HEADS = 4
GLA_DK = 64
GLA_DV = 128
GLA_KW = GLA_HEADS * GLA_DK
GLA_VW = GLA_HEADS * GLA_DV
GLA_GATE_RANK = 16
GLA_LOGIT_NORM = 16.0
MLA_HEADS = 8
MLA_NOPE = 64
MLA_ROPE = 32
MLA_V = 64
MLA_Q_RANK = 256
MLA_KV_RANK = 128
MLA_VW = MLA_HEADS * MLA_V
ROPE_THETA = 10000.0
N_BRANCH = 3

_IN_SIZES = (RW_WIDTH, RW_WIDTH, RW_WIDTH, RW_RANK, RW_RANK, RW_WIDTH,
             GLA_KW, GLA_KW, GLA_VW, GLA_GATE_RANK, GLA_VW,
             MLA_Q_RANK, MLA_KV_RANK, MLA_ROPE, MLA_VW, N_BRANCH * D_MODEL)
_IN_OFF = tuple(int(v) for v in np.concatenate([[0], np.cumsum(_IN_SIZES)]))

LANE = 128
COL_RW_R = 0
COL_RW_K = 512
COL_RW_V = 1024
COL_RW_WA = 1536
COL_GL_AD = 1664
COL_ML_QD = 1792
COL_GL_Q = 2048
COL_GL_K = 2304
COL_GL_V = 2560
COL_ML_KVD = 3072
COL_KPE = 3200
COL_KPE_SW = 3328
PACK_A = 3584
COL_MERGE = 0
COL_GATES = 3072
PACK_B = 4608
KPE_LANE = MLA_NOPE

CHUNK = 64
INPROJ_TM = 512
INPROJ_SUB = 256
INPROJ_TN = 512
OUTPROJ_TM = 512
MLA_TQ = 1024
VMEM_LIMIT = 48 * 1024 * 1024


def _dot(a, b, prec=None):
    return jnp.dot(a, b, preferred_element_type=F32, precision=prec)


def _dot_nt(a, b, prec=None):
    return lax.dot_general(a, b, (((1,), (1,)), ((), ())), preferred_element_type=F32, precision=prec)


def _dot_tn(a, b, prec=None):
    return lax.dot_general(a, b, (((0,), (0,)), ((), ())), preferred_element_type=F32, precision=prec)


def _split(x):
    hi = x.astype(BF16)
    return hi, (x - hi.astype(F32)).astype(BF16)


def _dot_split(a, b):
    ah, al = _split(a)
    bh, bl = _split(b)
    return _dot(ah, bh) + _dot(al, bh) + _dot(ah, bl)


def _dot_sel(sel16, x):
    xh, xl = _split(x)
    return _dot(sel16, xh) + _dot(sel16, xl)


def _sigmoid(x):
    return 0.5 * jnp.tanh(0.5 * x) + 0.5


def _rms(x, eps=NORM_EPS):
    return x * lax.rsqrt(jnp.mean(x * x, axis=-1, keepdims=True) + eps)


def _mod_kernel(c_ref, w_ref, b_ref, o_ref):
    c = c_ref[...]
    o_ref[...] = _dot_split(c * _sigmoid(c), w_ref[...]) + b_ref[...]


def _mod_call(cvec8, w_mod, b_mod, layer):
    tn = 1024
    return pl.pallas_call(
        _mod_kernel,
        grid=(3 * D_MODEL // tn,),
        in_specs=[pl.BlockSpec((8, D_MODEL), lambda j: (0, 0)),
                  pl.BlockSpec((None, D_MODEL, tn), lambda j: (layer, 0, j)),
                  pl.BlockSpec((None, 1, tn), lambda j: (layer, 0, j))],
        out_specs=pl.BlockSpec((8, tn), lambda j: (0, j)),
        out_shape=jax.ShapeDtypeStruct((8, 3 * D_MODEL), F32),
        compiler_params=pltpu.CompilerParams(vmem_limit_bytes=VMEM_LIMIT),
        name="mod",
    )(cvec8, w_mod, b_mod.reshape(DEPTH, 1, 3 * D_MODEL))


def _inproj_kernel(x_ref, mod_ref, g_ref, wa_ref, wb_ref, oa_ref, ob_ref):
    m = mod_ref[...]
    shift = m[:, 0:D_MODEL]
    scale1 = 1.0 + m[:, D_MODEL:2 * D_MODEL]
    tm = x_ref.shape[0]
    for r0 in range(0, tm, INPROJ_SUB):
        rows = slice(r0, r0 + INPROJ_SUB)
        h = (_rms(x_ref[rows, :]) * g_ref[...] * scale1 + shift).astype(BF16)
        for w_ref, o_ref in ((wa_ref, oa_ref), (wb_ref, ob_ref)):
            for c0 in range(0, w_ref.shape[1], INPROJ_TN):
                cols = slice(c0, c0 + INPROJ_TN)
                o_ref[rows, cols] = _dot(h, w_ref[:, cols]).astype(BF16)


def _inproj_call(x2d, mod3, norm_g, w_a, w_b, layer, rows_per_mod, mod_base):
    rows = x2d.shape[0]
    tm = INPROJ_TM
    tiles_per_mod = rows_per_mod // tm
    resident = lambda w: pl.BlockSpec((None,) + w.shape[1:], lambda i: (layer, 0, 0), pipeline_mode=pl.Buffered(1))
    return pl.pallas_call(
        _inproj_kernel,
        grid=(rows // tm,),
        in_specs=[pl.BlockSpec((tm, D_MODEL), lambda i: (i, 0)),
                  pl.BlockSpec((None, 1, 3 * D_MODEL), lambda i: (mod_base + i // tiles_per_mod, 0, 0)),
                  pl.BlockSpec((1, D_MODEL), lambda i: (0, 0)),
                  resident(w_a), resident(w_b)],
        out_specs=[pl.BlockSpec((tm, PACK_A), lambda i: (i, 0)),
                   pl.BlockSpec((tm, PACK_B), lambda i: (i, 0))],
        out_shape=[jax.ShapeDtypeStruct((rows, PACK_A), BF16),
                   jax.ShapeDtypeStruct((rows, PACK_B), BF16)],
        compiler_params=pltpu.CompilerParams(vmem_limit_bytes=VMEM_LIMIT),
        name="inproj",
    )(x2d, mod3, norm_g, w_a, w_b)


RW_CHUNK = 64
RW_PAIRS = RW_HEADS // 2
RW_BLOCK_ROWS = 256
RW_NEG_DECAY_SCALE = -float(np.exp(-0.5))
RW_UNROLL = 2


def _rwkv_time_mask(c, reverse):
    row = lax.broadcasted_iota(jnp.int32, (2 * c, 4 * c), 0)
    col = lax.broadcasted_iota(jnp.int32, (2 * c, 4 * c), 1)
    t = jnp.where(row >= c, row - c, row)
    s = col & (c - 1)
    earlier = (s > t) if reverse else (s < t)
    return earlier | ((row >= c) & (s == t))


def _rwkv_kernel(*refs, seq, has_s0):
    if has_s0:
        r_ref, k_ref, v_ref, wa_ref, s0_ref = refs[:5]
        rest = refs[5:]
    else:
        r_ref, k_ref, v_ref, wa_ref = refs[:4]
        s0_ref = None
        rest = refs[4:]
    (wcat_ref, wa0_ref, kk_ref, ka_ref, rk_ref, lng_ref, lnb_ref, hsum_ref,
     o_ref, sout_ref, acc_ref, kn_ref, st_ref) = rest
    c = RW_CHUNK
    nc = seq // c
    blk = min(RW_BLOCK_ROWS, seq)
    hsum = hsum_ref[...]

    def pre_step(i, carry):
        rows = pl.ds(pl.multiple_of(i * blk, blk), blk)
        kk0 = k_ref[rows, :].astype(F32) * kk_ref[...]
        kn_ref[rows, :] = kk0 / jnp.maximum(jnp.sqrt(_dot((kk0 * kk0).astype(BF16), hsum)), 1e-12)
        return carry

    def post_step(i, carry):
        rows = pl.ds(pl.multiple_of(i * blk, blk), blk)
        o = acc_ref[0, rows, :] + acc_ref[1, rows, :]
        rk = r_ref[rows, :].astype(F32) * k_ref[rows, :].astype(F32) * rk_ref[...]
        red = _dot(jnp.concatenate([o, rk], axis=0).astype(BF16), hsum)
        dev = o - red[0:blk] * (1.0 / RW_HEAD)
        var = _dot((dev * dev).astype(BF16), hsum) * (1.0 / RW_HEAD)
        o = dev * lax.rsqrt(var + RW_GN_EPS) * lng_ref[...] + lnb_ref[...]
        o_ref[rows, :] = (o + red[blk:2 * blk] * v_ref[rows, :]).astype(o_ref.dtype)
        return carry

    even = lax.broadcasted_iota(jnp.int32, (1, LANE), 1) < RW_HEAD
    row = lax.broadcasted_iota(jnp.int32, (c, c), 0)
    col = lax.broadcasted_iota(jnp.int32, (c, c), 1)
    incl16 = [jnp.where(col <= row, 1.0, 0.0).astype(BF16), jnp.where(col >= row, 1.0, 0.0).astype(BF16)]
    tmask = [_rwkv_time_mask(c, False), _rwkv_time_mask(c, True)]
    r2 = lax.broadcasted_iota(jnp.int32, (LANE, LANE), 0)
    c2 = lax.broadcasted_iota(jnp.int32, (LANE, LANE), 1)
    same_head = (r2 < RW_HEAD) == (c2 < RW_HEAD)
    p_lanes = ((lax.broadcasted_iota(jnp.int32, (2 * c, LANE), 0) < c)
               == (lax.broadcasted_iota(jnp.int32, (2 * c, LANE), 1) < RW_HEAD))
    lanes = [slice(p * LANE, (p + 1) * LANE) for p in range(RW_PAIRS)]
    probs = [(d, p) for d in range(2) for p in range(RW_PAIRS)]

    def chunk_operands(d, ci):
        rows = pl.ds(pl.multiple_of(ci * c, c), c)
        r = r_ref[rows, :].astype(F32)
        k = k_ref[rows, :].astype(F32)
        wa = wa_ref[rows, :].astype(F32)
        kk = kn_ref[rows, :]
        lora = jnp.where(even, jnp.tanh(wa), wa)
        pre = _dot(lora.astype(BF16), wcat_ref[d]) + wa0_ref[d:d + 1, :]
        lw = RW_NEG_DECAY_SCALE * _sigmoid(pre[:, 0:RW_WIDTH])
        a = _sigmoid(pre[:, RW_WIDTH:2 * RW_WIDTH])
        kd = k * (1.0 + (a - 1.0) * ka_ref[...])
        bv = kk * a
        gam = _dot_sel(incl16[d], lw)
        last = 0 if d == 1 else c - 1
        gtot = gam[last:last + 1, :]
        gref = gam[c // 2:c // 2 + 1, :]
        e_in = jnp.exp(gam - gref)
        e_out = jnp.exp(gref - gam)
        e_end = e_out * jnp.exp(gtot - gref)
        lhs = jnp.concatenate([-kk * jnp.exp(gam - lw - gref), r * e_in], axis=0).astype(BF16)
        return dict(rows=rows, lhs=lhs, b16=(bv * e_out).astype(BF16), k16=(kd * e_out).astype(BF16),
                    be16=(bv * e_end).astype(BF16), ke16=(kd * e_end).astype(BF16),
                    v16=v_ref[rows, :].astype(BF16), dtot=jnp.exp(gtot), eref=jnp.exp(gref))

    def main_step(i, carry):
        steps = [i * RW_UNROLL + s for s in range(RW_UNROLL)]
        prepared = [[chunk_operands(0, t), chunk_operands(1, nc - 1 - t)] for t in steps]
        for ops in prepared:
            chain_step(ops)
        return carry

    def chain_step(ops):
        n = range(len(probs))
        pick = lambda name: [ops[d][name][:, lanes[p]] for d, p in probs]
        lhs, b, k, be, ke, v = (pick(s) for s in ("lhs", "b16", "k16", "be16", "ke16", "v16"))
        st = [st_ref[d, p] for d, p in probs]
        first = lambda x: jnp.where(even, x, jnp.zeros_like(x))
        second = lambda x: jnp.where(even, jnp.zeros_like(x), x)
        v_e = [first(x) for x in v]
        v_o = [second(x) for x in v]
        aa = [jnp.where(tmask[d], _dot_nt(lhs[j], jnp.concatenate(
            [first(k[j]), second(k[j]), second(b[j]), first(b[j])], axis=0)), 0.0) for j, (d, p) in enumerate(probs)]
        sp = [_dot_nt(lhs[j], (st[j] * ops[d]["eref"][:, lanes[p]]).astype(BF16))
              for j, (d, p) in enumerate(probs)]
        aa16 = [x.astype(BF16) for x in aa]
        rhs_u = [_dot(aa16[j][0:c, 0:LANE], jnp.concatenate([v_e[j], v_o[j]], axis=0)) + sp[j][0:c] for j in n]
        x = [jnp.concatenate([jnp.where(even, aa[j][0:c, LANE:2 * LANE], rhs_u[j]),
                              jnp.where(even, rhs_u[j], aa[j][0:c, LANE:2 * LANE])], axis=0) for j in n]
        covered = 1
        while covered < c:
            x16 = [xx.astype(BF16) for xx in x]
            x = [_dot(jnp.where(p_lanes, x16[j], jnp.zeros_like(x16[j])), x16[j])
                 + jnp.where(p_lanes, 0.0, x[j]) for j in n]
            covered *= 2
        u16 = [jnp.where(even, x[j][c:2 * c], x[j][0:c]).astype(BF16) for j in n]
        for j, (d, p) in enumerate(probs):
            uv_rows = jnp.concatenate([v_e[j], v_o[j], second(u16[j]), first(u16[j])], axis=0)
            acc_ref[d, ops[d]["rows"], lanes[p]] = sp[j][c:2 * c] + _dot(aa16[j][c:2 * c], uv_rows)
            upd = _dot_tn(jnp.concatenate([u16[j], v[j]], axis=0), jnp.concatenate([be[j], ke[j]], axis=0))
            st_ref[d, p] = st[j] * ops[d]["dtot"][:, lanes[p]] + jnp.where(same_head, upd, 0.0)

    st_ref[...] = jnp.zeros_like(st_ref)
    if has_s0:
        for d in range(2):
            for h in range(RW_HEADS):
                off = (h % 2) * RW_HEAD
                st_ref[d, h // 2, off:off + RW_HEAD, off:off + RW_HEAD] = s0_ref[d, h]
    lax.fori_loop(0, seq // blk, pre_step, 0)
    lax.fori_loop(0, nc // RW_UNROLL, main_step, 0)
    for d in range(2):
        for h in range(RW_HEADS):
            off = (h % 2) * RW_HEAD
            sout_ref[d, h] = st_ref[d, h // 2, off:off + RW_HEAD, off:off + RW_HEAD]
    lax.fori_loop(0, seq // blk, post_step, 0)


def _rwkv_call(proj, s0, layer, batch, seq, wts):
    has_s0 = s0 is not None
    blk = lambda w, cb: pl.BlockSpec((seq, w), lambda b: (b, cb))
    in_specs = [blk(512, COL_RW_R // 512), blk(512, COL_RW_K // 512), blk(512, COL_RW_V // 512),
                blk(LANE, COL_RW_WA // LANE)]
    args = [proj, proj, proj, proj]
    if has_s0:
        in_specs.append(pl.BlockSpec((None, None, 2, RW_HEADS, RW_HEAD, RW_HEAD),
                                     lambda b: (b, layer, 0, 0, 0, 0)))
        args.append(s0)
    for w in wts:
        in_specs.append(pl.BlockSpec(w.shape, lambda b, n=w.ndim: (0,) * n))
        args.append(w)
    return pl.pallas_call(
        functools.partial(_rwkv_kernel, seq=seq, has_s0=has_s0),
        grid=(batch,),
        in_specs=in_specs,
        out_specs=[pl.BlockSpec((seq, RW_WIDTH), lambda b: (b, 0)),
                   pl.BlockSpec((None, 2, RW_HEADS, RW_HEAD, RW_HEAD), lambda b: (b, 0, 0, 0, 0))],
        out_shape=[jax.ShapeDtypeStruct((batch * seq, RW_WIDTH), BF16),
                   jax.ShapeDtypeStruct((batch, 2, RW_HEADS, RW_HEAD, RW_HEAD), F32)],
        scratch_shapes=[pltpu.VMEM((2, seq, RW_WIDTH), F32),
                        pltpu.VMEM((seq, RW_WIDTH), F32),
                        pltpu.VMEM((2, RW_PAIRS, LANE, LANE), F32)],
        compiler_params=pltpu.CompilerParams(vmem_limit_bytes=VMEM_LIMIT),
        name="rwkv",
    )(*args)


GLA_LEVELS = (32, 16, 8, 4, 2, 1)
GLA_GROUP = 4


def _gla_constants():
    c = CHUNK
    nl = len(GLA_LEVELS)
    mexp = np.zeros((2, (nl + 1) * c, c), np.float32)
    bmask = np.zeros((2, nl + 1, c, c), np.float32)
    for d in range(2):
        pos = np.arange(c) if d == 0 else c - 1 - np.arange(c)
        pt = pos[:, None]
        pj = pos[None, :]
        mexp[d, nl * c:] = (pj <= pt)
        for li, m in enumerate(GLA_LEVELS):
            mid = (pos // (2 * m)) * (2 * m) + m
            second = pos >= mid
            mq = (pj >= mid[:, None]) & (pj <= pt) & second[:, None]
            mk = (pj > pt) & (pj <= mid[:, None] - 1) & (~second)[:, None]
            mexp[d, li * c:(li + 1) * c] = mq | mk
            same = (pos[:, None] // (2 * m)) == (pos[None, :] // (2 * m))
            bmask[d, li] = same & second[:, None] & (~second)[None, :]
        bmask[d, nl] = np.eye(c)
    return mexp, np.concatenate([bmask, bmask], axis=-1)


def _gla_kernel(*refs, seq, has_s0):
    if has_s0:
        q_ref, k_ref, v_ref, ad_ref, s0_ref = refs[:5]
        rest = refs[5:]
    else:
        q_ref, k_ref, v_ref, ad_ref = refs[:4]
        s0_ref = None
        rest = refs[4:]
    (a2_ref, ab_ref, ng_ref, mexp_ref, bmask_ref,
     o_ref, sout_ref, acc_ref, st_ref) = rest
    c = CHUNK
    nc = seq // c
    nl = len(GLA_LEVELS)

    def chunk_operands(d, ci):
        rows = pl.ds(pl.multiple_of(ci * c, c), c)
        q = q_ref[rows, :].astype(F32) * (GLA_DK ** -0.5)
        k = k_ref[rows, :].astype(F32)
        v = v_ref[rows, :]
        x = _dot(ad_ref[rows, :].astype(BF16), a2_ref[d]) + ab_ref[d:d + 1, :]
        g = (jnp.minimum(x, 0.0) - jnp.log1p(jnp.exp(-jnp.abs(x)))) * (1.0 / GLA_LOGIT_NORM)
        g_hi, g_lo = _split(g)
        sums = _dot(mexp_ref[d], g_hi)
        ex = jnp.exp(sums[0:nl * c])
        b = sums[nl * c:(nl + 1) * c] + _dot(mexp_ref[d, nl * c:(nl + 1) * c, :], g_lo)
        last = 0 if d == 1 else c - 1
        blast = b[last:last + 1, :]
        qb = (q * jnp.exp(b)).astype(BF16)
        kdec = (k * jnp.exp(blast - b)).astype(BF16)
        dtot = jnp.exp(blast)
        qs = [(q * ex[li * c:(li + 1) * c]).astype(BF16) for li in range(nl)] + [q.astype(BF16)]
        ks = [(k * ex[li * c:(li + 1) * c]).astype(BF16) for li in range(nl)] + [k.astype(BF16)]
        return dict(rows=rows, qs=qs, ks=ks, qb=qb, kdec=kdec, dtot=dtot, v16=v.astype(BF16))

    even = lax.broadcasted_iota(jnp.int32, (1, LANE), 1) < GLA_DK
    r2 = lax.broadcasted_iota(jnp.int32, (2 * GLA_DV, LANE), 0)
    c2 = lax.broadcasted_iota(jnp.int32, (2 * GLA_DV, LANE), 1)
    same_head = (r2 < GLA_DV) == (c2 < GLA_DK)
    zv = jnp.zeros((c, GLA_DV), BF16)
    npair = GLA_HEADS // 2
    kls = [slice(p * LANE, (p + 1) * LANE) for p in range(npair)]
    vls = [slice(p * 2 * GLA_DV, (p + 1) * 2 * GLA_DV) for p in range(npair)]
    probs = [(d, p) for d in range(2) for p in range(npair)]
    first = lambda x: jnp.where(even, x, jnp.zeros_like(x))
    second = lambda x: jnp.where(even, jnp.zeros_like(x), x)

    group = min(GLA_GROUP, nc)

    def main_step(i, carry):
        ops = [[chunk_operands(0, i * group + g) for g in range(group)],
               [chunk_operands(1, nc - 1 - (i * group + g)) for g in range(group)]]
        allp = [(d, p, g) for d, p in probs for g in range(group)]
        lvl = [[_dot_nt(ops[d][g]["qs"][li][:, kls[p]],
                        jnp.concatenate([first(ops[d][g]["ks"][li][:, kls[p]]),
                                         second(ops[d][g]["ks"][li][:, kls[p]])], axis=0))
                for d, p, g in allp] for li in range(nl + 1)]
        att = [sum(bmask_ref[d, li] * lvl[li][j] for li in range(nl + 1)).astype(BF16)
               for j, (d, p, g) in enumerate(allp)]
        upd = [jnp.where(same_head, _dot_tn(ops[d][g]["v16"][:, vls[p]], ops[d][g]["kdec"][:, kls[p]]), 0.0)
               for d, p, g in allp]
        states = []
        for d, p in probs:
            st = st_ref[d, p]
            for g in range(group):
                states.append(st)
                st = st * ops[d][g]["dtot"][:, kls[p]] + upd[len(states) - 1]
            st_ref[d, p] = st
        for j, (d, p, g) in enumerate(allp):
            v_p = ops[d][g]["v16"][:, vls[p]]
            v_bd = jnp.concatenate([jnp.concatenate([v_p[:, 0:GLA_DV], zv], axis=1),
                                    jnp.concatenate([zv, v_p[:, GLA_DV:2 * GLA_DV]], axis=1)], axis=0)
            inter = _dot_nt(ops[d][g]["qb"][:, kls[p]], states[j].astype(BF16))
            acc_ref[d, ops[d][g]["rows"], vls[p]] = inter + _dot(att[j], v_bd)
        return carry

    blk = min(RW_BLOCK_ROWS, seq)

    def post_step(i, carry):
        rows = pl.ds(pl.multiple_of(i * blk, blk), blk)
        for h in range(GLA_HEADS):
            sv = slice(h * GLA_DV, (h + 1) * GLA_DV)
            o_ref[rows, sv] = (_rms(acc_ref[0, rows, sv] + acc_ref[1, rows, sv]) * ng_ref[...]).astype(o_ref.dtype)
        return carry

    st_ref[...] = jnp.zeros_like(st_ref)
    if has_s0:
        for d in range(2):
            for h in range(GLA_HEADS):
                ro, co = (h % 2) * GLA_DV, (h % 2) * GLA_DK
                st_ref[d, h // 2, ro:ro + GLA_DV, co:co + GLA_DK] = s0_ref[d, h].T
    lax.fori_loop(0, nc // group, main_step, 0)
    for d in range(2):
        for h in range(GLA_HEADS):
            ro, co = (h % 2) * GLA_DV, (h % 2) * GLA_DK
            sout_ref[d, h] = st_ref[d, h // 2, ro:ro + GLA_DV, co:co + GLA_DK].T
    lax.fori_loop(0, seq // blk, post_step, 0)


def _gla_call(proj, s0, layer, batch, seq, wts):
    has_s0 = s0 is not None
    blk = lambda w, cb: pl.BlockSpec((seq, w), lambda b: (b, cb))
    in_specs = [blk(GLA_KW, COL_GL_Q // GLA_KW), blk(GLA_KW, COL_GL_K // GLA_KW),
                blk(GLA_VW, COL_GL_V // GLA_VW), blk(LANE, COL_GL_AD // LANE)]
    args = [proj, proj, proj, proj]
    if has_s0:
        in_specs.append(pl.BlockSpec((None, None, 2, GLA_HEADS, GLA_DK, GLA_DV),
                                     lambda b: (b, layer, 0, 0, 0, 0)))
        args.append(s0)
    for w in wts:
        in_specs.append(pl.BlockSpec(w.shape, lambda b, n=w.ndim: (0,) * n))
        args.append(w)
    return pl.pallas_call(
        functools.partial(_gla_kernel, seq=seq, has_s0=has_s0),
        grid=(batch,),
        in_specs=in_specs,
        out_specs=[pl.BlockSpec((seq, GLA_VW), lambda b: (b, 0)),
                   pl.BlockSpec((None, 2, GLA_HEADS, GLA_DK, GLA_DV), lambda b: (b, 0, 0, 0, 0))],
        out_shape=[jax.ShapeDtypeStruct((batch * seq, GLA_VW), BF16),
                   jax.ShapeDtypeStruct((batch, 2, GLA_HEADS, GLA_DK, GLA_DV), F32)],
        scratch_shapes=[pltpu.VMEM((2, seq, GLA_VW), F32),
                        pltpu.VMEM((2, GLA_HEADS // 2, 2 * GLA_DV, LANE), F32)],
        compiler_params=pltpu.CompilerParams(vmem_limit_bytes=VMEM_LIMIT),
        name="gla",
    )(*args)


def _mla_kernel(*refs, seq, past, tq):
    has_ctx = past > 0
    if has_ctx:
        (qd_ref, kvd_ref, kpe_ref, kpesw_ref, cckv_ref, ckpe_ref,
         qng_ref, wq_ref, wqsw_ref, kvng_ref, wk_ref, wv_ref,
         cosq_ref, sinq_ref, cosk_ref, sink_ref, epos_ref,
         o_ref, ckv_ref, kcat_ref, vv_ref) = refs
    else:
        (qd_ref, kvd_ref, kpe_ref,
         qng_ref, wq_ref, kvng_ref, wk_ref, wv_ref,
         o_ref, ckv_ref, kcat_ref, vv_ref) = refs

    ones_hi = jnp.where(lax.broadcasted_iota(jnp.int32, (1, LANE), 1) >= MLA_V, 1.0, 0.0)

    @pl.when(pl.program_id(1) == 0)
    def _():
        ckv = _rms(kvd_ref[...].astype(F32)) * kvng_ref[...]
        ckv_ref[...] = ckv
        if has_ctx:
            kpos = kpe_ref[...] * cosk_ref[...] + kpesw_ref[...] * sink_ref[...]
        else:
            kpos = kpe_ref[...]
        segs = [(0, seq, ckv, kpos)]
        if has_ctx:
            cpos = _dot(ckpe_ref[...].astype(BF16), epos_ref[...])
            segs.append((seq, past, cckv_ref[...], cpos))
        for start, n, lat, pos in segs:
            lat16 = lat.astype(BF16)
            kc = _dot(lat16, wk_ref[...])
            vc = _dot(lat16, wv_ref[...])
            for h in range(MLA_HEADS):
                hl = slice(h * LANE, (h + 1) * LANE)
                kcat_ref[h, start:start + n, :] = (kc[:, hl] + pos).astype(BF16)
                vv_ref[h, start:start + n, :] = (vc[:, hl] + ones_hi).astype(BF16)

    qlat = (_rms(qd_ref[...].astype(F32)) * qng_ref[...]).astype(BF16)
    qc = _dot(qlat, wq_ref[...])
    if has_ctx:
        qsw = _dot(qlat, wqsw_ref[...])
    for h in range(MLA_HEADS):
        hl = slice(h * LANE, (h + 1) * LANE)
        q_raw = qc[:, hl]
        if has_ctx:
            q_self = (q_raw * cosq_ref[...] + qsw[:, hl] * sinq_ref[...]).astype(BF16)
        else:
            q_self = q_raw.astype(BF16)
        s1 = _dot_nt(q_self, kcat_ref[h, 0:seq, :])
        m = jnp.max(s1, axis=-1, keepdims=True)
        if has_ctx:
            s2 = _dot_nt(q_raw.astype(BF16), kcat_ref[h, seq:seq + past, :])
            m = jnp.maximum(m, jnp.max(s2, axis=-1, keepdims=True))
        o_h = _dot(jnp.exp((s1 - m).astype(BF16)), vv_ref[h, 0:seq, :])
        if has_ctx:
            o_h = o_h + _dot(jnp.exp((s2 - m).astype(BF16)), vv_ref[h, seq:seq + past, :])
        o_h = o_h / o_h[:, MLA_V:MLA_V + 1]
        o_ref[:, h * MLA_V:(h + 1) * MLA_V] = o_h[:, 0:MLA_V].astype(o_ref.dtype)


def _mla_call(proj, cache_ckv, cache_kpe, layer, batch, seq, wts, tables):
    has_ctx = cache_ckv is not None
    past = cache_ckv.shape[2] if has_ctx else 0
    tq = min(MLA_TQ, seq)
    nq = seq // tq
    full = lambda cb: pl.BlockSpec((seq, LANE), lambda b, i: (b, cb))
    in_specs = [pl.BlockSpec((tq, MLA_Q_RANK), lambda b, i: (b * nq + i, COL_ML_QD // MLA_Q_RANK)),
                full(COL_ML_KVD // LANE), full(COL_KPE // LANE)]
    args = [proj, proj, proj]
    if has_ctx:
        in_specs += [full(COL_KPE_SW // LANE),
                     pl.BlockSpec((None, None, past, MLA_KV_RANK), lambda b, i: (b, layer, 0, 0)),
                     pl.BlockSpec((None, None, past, MLA_ROPE), lambda b, i: (b, layer, 0, 0))]
        args += [proj, cache_ckv, cache_kpe]
    qn_g, wq_cat, wq_sw, kvn_g, wk_pad, wv = wts
    const = lambda w: pl.BlockSpec(w.shape, lambda b, i, n=w.ndim: (0,) * n)
    if has_ctx:
        cosq, sinq, cosk, sink, epos = tables
        wlist = [qn_g, wq_cat, wq_sw, kvn_g, wk_pad, wv]
        in_specs += [const(w) for w in wlist]
        in_specs += [pl.BlockSpec((tq, LANE), lambda b, i: (i, 0)), pl.BlockSpec((tq, LANE), lambda b, i: (i, 0)),
                     const(cosk), const(sink), const(epos)]
        args += wlist + [cosq, sinq, cosk, sink, epos]
    else:
        wlist = [qn_g, wq_cat, kvn_g, wk_pad, wv]
        in_specs += [const(w) for w in wlist]
        args += wlist
    return pl.pallas_call(
        functools.partial(_mla_kernel, seq=seq, past=past, tq=tq),
        grid=(batch, nq),
        in_specs=in_specs,
        out_specs=[pl.BlockSpec((tq, MLA_VW), lambda b, i: (b * nq + i, 0)),
                   pl.BlockSpec((seq, MLA_KV_RANK), lambda b, i: (b, 0))],
        out_shape=[jax.ShapeDtypeStruct((batch * seq, MLA_VW), BF16),
                   jax.ShapeDtypeStruct((batch * seq, MLA_KV_RANK), F32)],
        scratch_shapes=[pltpu.VMEM((MLA_HEADS, seq + past, LANE), BF16),
                        pltpu.VMEM((MLA_HEADS, seq + past, LANE), BF16)],
        compiler_params=pltpu.CompilerParams(vmem_limit_bytes=VMEM_LIMIT,
                                             dimension_semantics=("arbitrary", "arbitrary")),
        name="mla",
    )(*args)


def _outproj_kernel(x_ref, oa_ref, ob_ref, oc_ref, ga_ref, gb_ref, gc_ref, ma_ref, mb_ref, mc_ref,
                    mod_ref, wa_ref, wb_ref, wc_ref, wo_ref, fg_ref, o_ref, *, final):
    def branch(o_r, g_r, m_r, w_r):
        h = 0.5 * g_r[...]
        act = o_r[...] * (h * (1.0 + jnp.tanh(h)))
        y = _dot(act, w_r[...]).astype(BF16)
        return (0.5 * (1.0 + jnp.tanh(0.5 * m_r[...]))) * y

    y = (branch(oa_ref, ga_ref, ma_ref, wa_ref) + branch(ob_ref, gb_ref, mb_ref, wb_ref)
         + branch(oc_ref, gc_ref, mc_ref, wc_ref))
    y = _dot(y, wo_ref[...])
    gate = mod_ref[...][:, 2 * D_MODEL:3 * D_MODEL]
    hn = x_ref[...] + gate * y
    if final:
        hn = _rms(hn) * fg_ref[...]
    o_ref[...] = hn


def _outproj_call(x2d, o_a, o_b, o_c, proj, mod3, wts, final_g, rows_per_mod, mod_base, final):
    rows = x2d.shape[0]
    tm = OUTPROJ_TM
    tiles_per_mod = rows_per_mod // tm
    row = lambda w, cb=0: pl.BlockSpec((tm, w), lambda i: (i, cb))
    const = lambda w: pl.BlockSpec(w.shape, lambda i, n=w.ndim: (0,) * n)
    in_specs = [row(D_MODEL), row(512), row(512), row(512),
                row(512, COL_GATES // 512), row(512, COL_GATES // 512 + 1), row(512, COL_GATES // 512 + 2),
                row(D_MODEL, COL_MERGE // D_MODEL), row(D_MODEL, COL_MERGE // D_MODEL + 1),
                row(D_MODEL, COL_MERGE // D_MODEL + 2),
                pl.BlockSpec((None, 1, 3 * D_MODEL), lambda i: (mod_base + i // tiles_per_mod, 0, 0))]
    in_specs += [const(w) for w in wts] + [const(final_g)]
    return pl.pallas_call(
        functools.partial(_outproj_kernel, final=final),
        grid=(rows // tm,),
        in_specs=in_specs,
        out_specs=row(D_MODEL),
        out_shape=jax.ShapeDtypeStruct((rows, D_MODEL), F32),
        compiler_params=pltpu.CompilerParams(vmem_limit_bytes=VMEM_LIMIT),
        name="outproj",
    )(x2d, o_a, o_b, o_c, proj, proj, proj, proj, proj, proj, mod3, *wts, final_g)


def _pack_kernel(wt_ref, wa_ref, wb_ref):
    seg = lambda i: wt_ref[_IN_OFF[i]:_IN_OFF[i + 1], :]
    tr = wt_ref.shape[1]
    z = lambda n: jnp.zeros((n, tr), F32)

    def put(ref, col, xt):
        ref[:, col:col + xt.shape[0]] = xt.T.astype(BF16)

    put(wa_ref, COL_RW_R, wt_ref[_IN_OFF[0]:_IN_OFF[5], :])
    put(wa_ref, COL_GL_AD, jnp.concatenate([seg(9), z(LANE - GLA_GATE_RANK)], axis=0))
    put(wa_ref, COL_ML_QD, seg(11))
    put(wa_ref, COL_GL_Q, wt_ref[_IN_OFF[6]:_IN_OFF[9], :])
    put(wa_ref, COL_ML_KVD, seg(12))
    kpe = seg(13)
    q = MLA_ROPE // 4
    kpe_sw = jnp.concatenate([kpe[q:2 * q], kpe[0:q], kpe[3 * q:4 * q], kpe[2 * q:3 * q]], axis=0)
    tail = LANE - KPE_LANE - MLA_ROPE
    put(wa_ref, COL_KPE, jnp.concatenate([z(KPE_LANE), kpe, z(tail)], axis=0))
    put(wa_ref, COL_KPE_SW, jnp.concatenate([z(KPE_LANE), kpe_sw, z(tail)], axis=0))
    wa_ref[:, COL_KPE_SW + LANE:PACK_A] = jnp.zeros((tr, PACK_A - COL_KPE_SW - LANE), BF16)
    put(wb_ref, COL_MERGE, seg(15))
    put(wb_ref, COL_GATES, seg(5))
    put(wb_ref, COL_GATES + RW_WIDTH, seg(10))
    put(wb_ref, COL_GATES + RW_WIDTH + GLA_VW, seg(14))


def _pack_call(w_in):
    tr = 256
    wt = jnp.swapaxes(w_in, 1, 2)
    width = wt.shape[1]
    return pl.pallas_call(
        _pack_kernel,
        grid=(DEPTH, D_MODEL // tr),
        in_specs=[pl.BlockSpec((None, width, tr), lambda l, i: (l, 0, i))],
        out_specs=[pl.BlockSpec((None, tr, PACK_A), lambda l, i: (l, i, 0)),
                   pl.BlockSpec((None, tr, PACK_B), lambda l, i: (l, i, 0))],
        out_shape=[jax.ShapeDtypeStruct((DEPTH, D_MODEL, PACK_A), BF16),
                   jax.ShapeDtypeStruct((DEPTH, D_MODEL, PACK_B), BF16)],
        compiler_params=pltpu.CompilerParams(vmem_limit_bytes=VMEM_LIMIT),
        name="pack",
    )(wt)


def _rope_lane_tables(seq):
    n_freq = MLA_ROPE // 4
    t = np.arange(seq)
    inv = ROPE_THETA ** (-np.arange(n_freq, dtype=np.float64) / n_freq)
    ang = np.stack([(t // GRID_W)[:, None] * inv, (t % GRID_W)[:, None] * inv], axis=1)
    cos = np.repeat(np.cos(ang)[:, :, None, :], 2, axis=2).reshape(seq, MLA_ROPE)
    sin = np.stack([-np.sin(ang), np.sin(ang)], axis=2).reshape(seq, MLA_ROPE)
    return cos, sin


def _mla_tables(seq):
    cos, sin = _rope_lane_tables(seq)
    cosq = np.zeros((seq, LANE), np.float32)
    sinq = np.zeros((seq, LANE), np.float32)
    cosq[:, :KPE_LANE] = 1.0
    cosq[:, KPE_LANE:KPE_LANE + MLA_ROPE] = cos
    sinq[:, KPE_LANE:KPE_LANE + MLA_ROPE] = sin
    cosk = np.zeros((seq, LANE), np.float32)
    cosk[:, KPE_LANE:KPE_LANE + MLA_ROPE] = cos
    epos = np.zeros((MLA_ROPE, LANE), np.float32)
    epos[np.arange(MLA_ROPE), KPE_LANE + np.arange(MLA_ROPE)] = 1.0
    return (jnp.asarray(cosq), jnp.asarray(sinq), jnp.asarray(cosk), jnp.asarray(sinq),
            jnp.asarray(epos, dtype=BF16))


def _pack_mla_weights(qn_g, wq_up, kvn_g, wkv_up):
    scale = (MLA_NOPE + MLA_ROPE) ** -0.5
    wq = wq_up.reshape(MLA_Q_RANK, MLA_HEADS, MLA_NOPE + MLA_ROPE) * scale
    nope, rope = wq[..., :MLA_NOPE], wq[..., MLA_NOPE:]
    q = MLA_ROPE // 4
    rope_sw = jnp.concatenate([rope[..., q:2 * q], rope[..., 0:q], rope[..., 3 * q:4 * q], rope[..., 2 * q:3 * q]], -1)
    tail = jnp.zeros((MLA_Q_RANK, MLA_HEADS, LANE - KPE_LANE - MLA_ROPE), wq.dtype)
    wq_cat = jnp.concatenate([nope, rope, tail], -1).reshape(MLA_Q_RANK, MLA_HEADS * LANE).astype(BF16)
    wq_sw = jnp.concatenate([jnp.zeros_like(nope), rope_sw, tail], -1).reshape(MLA_Q_RANK, MLA_HEADS * LANE).astype(BF16)
    wkv = wkv_up.reshape(MLA_KV_RANK, MLA_HEADS, MLA_NOPE + MLA_V)
    wk = jnp.concatenate([wkv[..., :MLA_NOPE], jnp.zeros((MLA_KV_RANK, MLA_HEADS, LANE - MLA_NOPE), wkv.dtype)], -1)
    wk_pad = wk.reshape(MLA_KV_RANK, MLA_HEADS * LANE).astype(BF16)
    wv = jnp.concatenate([wkv[..., MLA_NOPE:], jnp.zeros((MLA_KV_RANK, MLA_HEADS, LANE - MLA_V), wkv.dtype)], -1)
    wv = wv.reshape(MLA_KV_RANK, MLA_HEADS * LANE).astype(BF16)
    return (qn_g.reshape(1, -1), wq_cat, wq_sw, kvn_g.reshape(1, -1), wk_pad, wv)


def _head_sum_matrix():
    lane = np.arange(RW_WIDTH)
    return jnp.asarray((lane[:, None] // RW_HEAD == lane[None, :] // RW_HEAD).astype(np.float32), dtype=BF16)


def _trunk(x_prompt, x_sample, c, cache_ckv, cache_kpe, state_rwkv, state_gla, c_ctx,
           norm_g, w_mod, b_mod, w_in, rw_w0, rw_w2, rw_a0, rw_a2, rw_k_k, rw_k_a, rw_r_k,
           rw_ln_g, rw_ln_b, rw_out, gla_a2, gla_ab, gla_norm_g, gla_out,
           mla_qn_g, mla_wq_up, mla_kvn_g, mla_wkv_up, mla_out, w_out, final_g):
    bp, tp, _ = x_prompt.shape
    bs, ts, _ = x_sample.shape
    hp = x_prompt.reshape(bp * tp, D_MODEL)
    hs = x_sample.reshape(bs * ts, D_MODEL)
    cvec8 = jnp.concatenate([c, c_ctx[None, :], jnp.zeros((8 - bs - 1, D_MODEL), F32)], axis=0)
    ctx_row = bs
    hsum = _head_sum_matrix()
    mexp, bmask = _gla_constants()
    mexp = jnp.asarray(mexp, dtype=BF16)
    bmask = jnp.asarray(bmask)
    tables = _mla_tables(ts)
    w_a, w_b = _pack_call(w_in)
    fg = final_g.reshape(1, D_MODEL)
    ckv_l, kpe_l, rw_l, gla_l = [], [], [], []
    for l in range(DEPTH):
        mod3 = _mod_call(cvec8, w_mod, b_mod, l).reshape(8, 1, 3 * D_MODEL)
        ng = norm_g[l].reshape(1, D_MODEL)
        row = lambda a: a.reshape(1, -1)
        zr = jnp.zeros((2, RW_RANK, RW_WIDTH), F32)
        wcat = jnp.concatenate([jnp.concatenate([rw_w2[l], zr], axis=2),
                                jnp.concatenate([zr, rw_a2[l]], axis=2)], axis=1).astype(BF16)
        rw_wts = (wcat, jnp.concatenate([rw_w0[l], rw_a0[l]], axis=1), row(rw_k_k[l]), row(rw_k_a[l]),
                  row(rw_r_k[l]), row(rw_ln_g[l]), row(rw_ln_b[l]), hsum)
        a2p = jnp.concatenate([gla_a2[l], jnp.zeros((2, LANE - GLA_GATE_RANK, GLA_KW), F32)], axis=1).astype(BF16)
        gla_wts = (a2p, gla_ab[l], row(gla_norm_g[l]), mexp, bmask)
        mla_wts = _pack_mla_weights(mla_qn_g[l], mla_wq_up[l], mla_kvn_g[l], mla_wkv_up[l])
        out_wts = (rw_out[l].astype(BF16), gla_out[l].astype(BF16), mla_out[l].astype(BF16), w_out[l].astype(BF16))
        final = l == DEPTH - 1

        proj, proj_b = _inproj_call(hp, mod3, ng, w_a, w_b, l, bp * tp, ctx_row)
        o_a, s_rw = _rwkv_call(proj, None, l, bp, tp, rw_wts)
        o_b, s_gla = _gla_call(proj, None, l, bp, tp, gla_wts)
        o_c, ckv = _mla_call(proj, None, None, l, bp, tp, mla_wts, None)
        ckv_l.append(ckv.reshape(bp, tp, MLA_KV_RANK))
        kpe_l.append(proj[:, COL_KPE + KPE_LANE:COL_KPE + KPE_LANE + MLA_ROPE].astype(F32).reshape(bp, tp, MLA_ROPE))
        rw_l.append(s_rw)
        gla_l.append(s_gla)
        hp = _outproj_call(hp, o_a, o_b, o_c, proj_b, mod3, out_wts, fg, bp * tp, ctx_row, final)

        proj, proj_b = _inproj_call(hs, mod3, ng, w_a, w_b, l, ts, 0)
        o_a, _ = _rwkv_call(proj, state_rwkv, l, bs, ts, rw_wts)
        o_b, _ = _gla_call(proj, state_gla, l, bs, ts, gla_wts)
        o_c, _ = _mla_call(proj, cache_ckv, cache_kpe, l, bs, ts, mla_wts, tables)
        hs = _outproj_call(hs, o_a, o_b, o_c, proj_b, mod3, out_wts, fg, ts, 0, final)

    return (hp.reshape(bp, tp, D_MODEL), hs.reshape(bs, ts, D_MODEL),
            jnp.stack(ckv_l, axis=1), jnp.stack(kpe_l, axis=1),
            jnp.stack(rw_l, axis=1), jnp.stack(gla_l, axis=1))


_trunk_jit = jax.jit(_trunk)


def kernel(x_prompt, x_sample, c, cache_ckv, cache_kpe, state_rwkv, state_gla, c_ctx, norm_g, w_mod, b_mod, w_in, rw_w0, rw_w2, rw_a0, rw_a2, rw_k_k, rw_k_a, rw_r_k, rw_ln_g, rw_ln_b, rw_out, gla_a2, gla_ab, gla_norm_g, gla_out, mla_qn_g, mla_wq_up, mla_kvn_g, mla_wkv_up, mla_out, w_out, final_g):
    return _trunk_jit(x_prompt, x_sample, c, cache_ckv, cache_kpe, state_rwkv, state_gla, c_ctx, norm_g, w_mod, b_mod, w_in, rw_w0, rw_w2, rw_a0, rw_a2, rw_k_k, rw_k_a, rw_r_k, rw_ln_g, rw_ln_b, rw_out, gla_a2, gla_ab, gla_norm_g, gla_out, mla_qn_g, mla_wq_up, mla_kvn_g, mla_wkv_up, mla_out, w_out, final_g)
```

```python
import functools

import numpy as np
import jax
import jax.numpy as jnp
from jax import lax
from jax.experimental import pallas as pl
from jax.experimental.pallas import tpu as pltpu

F32 = jnp.float32
BF16 = jnp.bfloat16

D_MODEL = 1024
DEPTH = 2
GRID_W = 64
NORM_EPS = 1e-6
RW_HEADS = 8
RW_HEAD = 64
RW_WIDTH = RW_HEADS * RW_HEAD
RW_RANK = 64
RW_GN_EPS = 64e-5
GLA_HEADS = 4
GLA_DK = 64
GLA_DV = 128
GLA_KW = GLA_HEADS * GLA_DK
GLA_VW = GLA_HEADS * GLA_DV
GLA_GATE_RANK = 16
GLA_LOGIT_NORM = 16.0
MLA_HEADS = 8
MLA_NOPE = 64
MLA_ROPE = 32
MLA_V = 64
MLA_Q_RANK = 256
MLA_KV_RANK = 128
MLA_VW = MLA_HEADS * MLA_V
ROPE_THETA = 10000.0
N_BRANCH = 3

_IN_SIZES = (RW_WIDTH, RW_WIDTH, RW_WIDTH, RW_RANK, RW_RANK, RW_WIDTH,
             GLA_KW, GLA_KW, GLA_VW, GLA_GATE_RANK, GLA_VW,
             MLA_Q_RANK, MLA_KV_RANK, MLA_ROPE, MLA_VW, N_BRANCH * D_MODEL)
_IN_OFF = tuple(int(v) for v in np.concatenate([[0], np.cumsum(_IN_SIZES)]))

LANE = 128
COL_RW_R = 0
COL_RW_K = 512
COL_RW_V = 1024
COL_RW_WA = 1536
COL_GL_AD = 1664
COL_ML_QD = 1792
COL_GL_Q = 2048
COL_GL_K = 2304
COL_GL_V = 2560
COL_ML_KVD = 3072
COL_KPE = 3200
COL_KPE_SW = 3328
PACK_A = 3584
COL_MERGE = 0
COL_GATES = 3072
PACK_B = 4608
KPE_LANE = MLA_NOPE

CHUNK = 64
INPROJ_TM = 512
INPROJ_SUB = 256
INPROJ_TN = 512
OUTPROJ_TM = 512
MLA_TQ = 1024
MLA_GROUP_SCORES = 256 * 256
VMEM_LIMIT = 48 * 1024 * 1024


def _dot(a, b, prec=None):
    return jnp.dot(a, b, preferred_element_type=F32, precision=prec)


def _dot_nt(a, b, prec=None):
    return lax.dot_general(a, b, (((1,), (1,)), ((), ())), preferred_element_type=F32, precision=prec)


def _dot_tn(a, b, prec=None):
    return lax.dot_general(a, b, (((0,), (0,)), ((), ())), preferred_element_type=F32, precision=prec)


def _split(x):
    hi = x.astype(BF16)
    return hi, (x - hi.astype(F32)).astype(BF16)


def _dot_split(a, b):
    ah, al = _split(a)
    bh, bl = _split(b)
    return _dot(ah, bh) + _dot(al, bh) + _dot(ah, bl)


def _dot_sel(sel16, x):
    xh, xl = _split(x)
    return _dot(sel16, xh) + _dot(sel16, xl)


def _sigmoid(x):
    return 0.5 * jnp.tanh(0.5 * x) + 0.5


def _rms(x, eps=NORM_EPS):
    return x * lax.rsqrt(jnp.mean(x * x, axis=-1, keepdims=True) + eps)


def _mod_kernel(c_ref, w_ref, b_ref, o_ref):
    c = c_ref[...]
    o_ref[...] = _dot_split(c * _sigmoid(c), w_ref[...]) + b_ref[...]


def _mod_call(cvec8, w_mod, b_mod, layer):
    tn = 1024
    return pl.pallas_call(
        _mod_kernel,
        grid=(3 * D_MODEL // tn,),
        in_specs=[pl.BlockSpec((8, D_MODEL), lambda j: (0, 0)),
                  pl.BlockSpec((None, D_MODEL, tn), lambda j: (layer, 0, j)),
                  pl.BlockSpec((None, 1, tn), lambda j: (layer, 0, j))],
        out_specs=pl.BlockSpec((8, tn), lambda j: (0, j)),
        out_shape=jax.ShapeDtypeStruct((8, 3 * D_MODEL), F32),
        compiler_params=pltpu.CompilerParams(vmem_limit_bytes=VMEM_LIMIT),
        name="mod",
    )(cvec8, w_mod, b_mod.reshape(DEPTH, 1, 3 * D_MODEL))


def _inproj_kernel(x_ref, mod_ref, g_ref, wa_ref, wb_ref, oa_ref, ob_ref):
    m = mod_ref[...]
    shift = m[:, 0:D_MODEL]
    scale1 = 1.0 + m[:, D_MODEL:2 * D_MODEL]
    tm = x_ref.shape[0]
    for r0 in range(0, tm, INPROJ_SUB):
        rows = slice(r0, r0 + INPROJ_SUB)
        h = (_rms(x_ref[rows, :]) * g_ref[...] * scale1 + shift).astype(BF16)
        for w_ref, o_ref in ((wa_ref, oa_ref), (wb_ref, ob_ref)):
            for c0 in range(0, w_ref.shape[1], INPROJ_TN):
                cols = slice(c0, c0 + INPROJ_TN)
                o_ref[rows, cols] = _dot(h, w_ref[:, cols]).astype(BF16)


def _inproj_call(x2d, mod3, norm_g, w_a, w_b, layer, rows_per_mod, mod_base):
    rows = x2d.shape[0]
    tm = INPROJ_TM
    tiles_per_mod = rows_per_mod // tm
    resident = lambda w: pl.BlockSpec((None,) + w.shape[1:], lambda i: (layer, 0, 0), pipeline_mode=pl.Buffered(1))
    return pl.pallas_call(
        _inproj_kernel,
        grid=(rows // tm,),
        in_specs=[pl.BlockSpec((tm, D_MODEL), lambda i: (i, 0)),
                  pl.BlockSpec((None, 1, 3 * D_MODEL), lambda i: (mod_base + i // tiles_per_mod, 0, 0)),
                  pl.BlockSpec((1, D_MODEL), lambda i: (0, 0)),
                  resident(w_a), resident(w_b)],
        out_specs=[pl.BlockSpec((tm, PACK_A), lambda i: (i, 0)),
                   pl.BlockSpec((tm, PACK_B), lambda i: (i, 0))],
        out_shape=[jax.ShapeDtypeStruct((rows, PACK_A), BF16),
                   jax.ShapeDtypeStruct((rows, PACK_B), BF16)],
        compiler_params=pltpu.CompilerParams(vmem_limit_bytes=VMEM_LIMIT),
        name="inproj",
    )(x2d, mod3, norm_g, w_a, w_b)


RW_CHUNK = 64
RW_PAIRS = RW_HEADS // 2
RW_BLOCK_ROWS = 256
RW_NEG_DECAY_SCALE = -float(np.exp(-0.5))
RW_UNROLL = 2


def _rwkv_time_mask(c, reverse):
    row = lax.broadcasted_iota(jnp.int32, (2 * c, 4 * c), 0)
    col = lax.broadcasted_iota(jnp.int32, (2 * c, 4 * c), 1)
    t = jnp.where(row >= c, row - c, row)
    s = col & (c - 1)
    earlier = (s > t) if reverse else (s < t)
    return earlier | ((row >= c) & (s == t))


def _rwkv_kernel(*refs, seq, has_s0):
    if has_s0:
        r_ref, k_ref, v_ref, wa_ref, s0_ref = refs[:5]
        rest = refs[5:]
    else:
        r_ref, k_ref, v_ref, wa_ref = refs[:4]
        s0_ref = None
        rest = refs[4:]
    (wcat_ref, wa0_ref, kk_ref, ka_ref, rk_ref, lng_ref, lnb_ref, hsum_ref,
     o_ref, sout_ref, acc_ref, kn_ref, st_ref) = rest
    c = RW_CHUNK
    nc = seq // c
    blk = min(RW_BLOCK_ROWS, seq)
    hsum = hsum_ref[...]

    def pre_step(i, carry):
        rows = pl.ds(pl.multiple_of(i * blk, blk), blk)
        kk0 = k_ref[rows, :].astype(F32) * kk_ref[...]
        kn_ref[rows, :] = kk0 / jnp.maximum(jnp.sqrt(_dot((kk0 * kk0).astype(BF16), hsum)), 1e-12)
        return carry

    def post_step(i, carry):
        rows = pl.ds(pl.multiple_of(i * blk, blk), blk)
        o = acc_ref[0, rows, :] + acc_ref[1, rows, :]
        rk = r_ref[rows, :].astype(F32) * k_ref[rows, :].astype(F32) * rk_ref[...]
        red = _dot(jnp.concatenate([o, rk], axis=0).astype(BF16), hsum)
        dev = o - red[0:blk] * (1.0 / RW_HEAD)
        var = _dot((dev * dev).astype(BF16), hsum) * (1.0 / RW_HEAD)
        o = dev * lax.rsqrt(var + RW_GN_EPS) * lng_ref[...] + lnb_ref[...]
        o_ref[rows, :] = (o + red[blk:2 * blk] * v_ref[rows, :]).astype(o_ref.dtype)
        return carry

    even = lax.broadcasted_iota(jnp.int32, (1, LANE), 1) < RW_HEAD
    row = lax.broadcasted_iota(jnp.int32, (c, c), 0)
    col = lax.broadcasted_iota(jnp.int32, (c, c), 1)
    incl16 = [jnp.where(col <= row, 1.0, 0.0).astype(BF16), jnp.where(col >= row, 1.0, 0.0).astype(BF16)]
    tmask = [_rwkv_time_mask(c, False), _rwkv_time_mask(c, True)]
    r2 = lax.broadcasted_iota(jnp.int32, (LANE, LANE), 0)
    c2 = lax.broadcasted_iota(jnp.int32, (LANE, LANE), 1)
    same_head = (r2 < RW_HEAD) == (c2 < RW_HEAD)
    p_lanes = ((lax.broadcasted_iota(jnp.int32, (2 * c, LANE), 0) < c)
               == (lax.broadcasted_iota(jnp.int32, (2 * c, LANE), 1) < RW_HEAD))
    lanes = [slice(p * LANE, (p + 1) * LANE) for p in range(RW_PAIRS)]
    probs = [(d, p) for d in range(2) for p in range(RW_PAIRS)]

    def chunk_operands(d, ci):
        rows = pl.ds(pl.multiple_of(ci * c, c), c)
        r = r_ref[rows, :].astype(F32)
        k = k_ref[rows, :].astype(F32)
        wa = wa_ref[rows, :].astype(F32)
        kk = kn_ref[rows, :]
        lora = jnp.where(even, jnp.tanh(wa), wa)
        pre = _dot(lora.astype(BF16), wcat_ref[d]) + wa0_ref[d:d + 1, :]
        lw = RW_NEG_DECAY_SCALE * _sigmoid(pre[:, 0:RW_WIDTH])
        a = _sigmoid(pre[:, RW_WIDTH:2 * RW_WIDTH])
        kd = k * (1.0 + (a - 1.0) * ka_ref[...])
        bv = kk * a
        gam = _dot_sel(incl16[d], lw)
        last = 0 if d == 1 else c - 1
        gtot = gam[last:last + 1, :]
        gref = gam[c // 2:c // 2 + 1, :]
        e_in = jnp.exp(gam - gref)
        e_out = jnp.exp(gref - gam)
        e_end = e_out * jnp.exp(gtot - gref)
        lhs = jnp.concatenate([-kk * jnp.exp(gam - lw - gref), r * e_in], axis=0).astype(BF16)
        return dict(rows=rows, lhs=lhs, b16=(bv * e_out).astype(BF16), k16=(kd * e_out).astype(BF16),
                    be16=(bv * e_end).astype(BF16), ke16=(kd * e_end).astype(BF16),
                    v16=v_ref[rows, :].astype(BF16), dtot=jnp.exp(gtot), eref=jnp.exp(gref))

    def main_step(i, carry):
        steps = [i * RW_UNROLL + s for s in range(RW_UNROLL)]
        prepared = [[chunk_operands(0, t), chunk_operands(1, nc - 1 - t)] for t in steps]
        for ops in prepared:
            chain_step(ops)
        return carry

    def chain_step(ops):
        n = range(len(probs))
        pick = lambda name: [ops[d][name][:, lanes[p]] for d, p in probs]
        lhs, b, k, be, ke, v = (pick(s) for s in ("lhs", "b16", "k16", "be16", "ke16", "v16"))
        st = [st_ref[d, p] for d, p in probs]
        first = lambda x: jnp.where(even, x, jnp.zeros_like(x))
        second = lambda x: jnp.where(even, jnp.zeros_like(x), x)
        v_e = [first(x) for x in v]
        v_o = [second(x) for x in v]
        aa = [jnp.where(tmask[d], _dot_nt(lhs[j], jnp.concatenate(
            [first(k[j]), second(k[j]), second(b[j]), first(b[j])], axis=0)), 0.0) for j, (d, p) in enumerate(probs)]
        sp = [_dot_nt(lhs[j], (st[j] * ops[d]["eref"][:, lanes[p]]).astype(BF16))
              for j, (d, p) in enumerate(probs)]
        aa16 = [x.astype(BF16) for x in aa]
        rhs_u = [_dot(aa16[j][0:c, 0:LANE], jnp.concatenate([v_e[j], v_o[j]], axis=0)) + sp[j][0:c] for j in n]
        x = [jnp.concatenate([jnp.where(even, aa[j][0:c, LANE:2 * LANE], rhs_u[j]),
                              jnp.where(even, rhs_u[j], aa[j][0:c, LANE:2 * LANE])], axis=0) for j in n]
        covered = 1
        while covered < c:
            x16 = [xx.astype(BF16) for xx in x]
            x = [_dot(jnp.where(p_lanes, x16[j], jnp.zeros_like(x16[j])), x16[j])
                 + jnp.where(p_lanes, 0.0, x[j]) for j in n]
            covered *= 2
        u16 = [jnp.where(even, x[j][c:2 * c], x[j][0:c]).astype(BF16) for j in n]
        for j, (d, p) in enumerate(probs):
            uv_rows = jnp.concatenate([v_e[j], v_o[j], second(u16[j]), first(u16[j])], axis=0)
            acc_ref[d, ops[d]["rows"], lanes[p]] = sp[j][c:2 * c] + _dot(aa16[j][c:2 * c], uv_rows)
            upd = _dot_tn(jnp.concatenate([u16[j], v[j]], axis=0), jnp.concatenate([be[j], ke[j]], axis=0))
            st_ref[d, p] = st[j] * ops[d]["dtot"][:, lanes[p]] + jnp.where(same_head, upd, 0.0)

    st_ref[...] = jnp.zeros_like(st_ref)
    if has_s0:
        for d in range(2):
            for h in range(RW_HEADS):
                off = (h % 2) * RW_HEAD
                st_ref[d, h // 2, off:off + RW_HEAD, off:off + RW_HEAD] = s0_ref[d, h]
    lax.fori_loop(0, seq // blk, pre_step, 0)
    lax.fori_loop(0, nc // RW_UNROLL, main_step, 0)
    for d in range(2):
        for h in range(RW_HEADS):
            off = (h % 2) * RW_HEAD
            sout_ref[d, h] = st_ref[d, h // 2, off:off + RW_HEAD, off:off + RW_HEAD]
    lax.fori_loop(0, seq // blk, post_step, 0)


def _rwkv_call(proj, s0, layer, batch, seq, wts):
    has_s0 = s0 is not None
    blk = lambda w, cb: pl.BlockSpec((seq, w), lambda b: (b, cb))
    in_specs = [blk(512, COL_RW_R // 512), blk(512, COL_RW_K // 512), blk(512, COL_RW_V // 512),
                blk(LANE, COL_RW_WA // LANE)]
    args = [proj, proj, proj, proj]
    if has_s0:
        in_specs.append(pl.BlockSpec((None, None, 2, RW_HEADS, RW_HEAD, RW_HEAD),
                                     lambda b: (b, layer, 0, 0, 0, 0)))
        args.append(s0)
    for w in wts:
        in_specs.append(pl.BlockSpec(w.shape, lambda b, n=w.ndim: (0,) * n))
        args.append(w)
    return pl.pallas_call(
        functools.partial(_rwkv_kernel, seq=seq, has_s0=has_s0),
        grid=(batch,),
        in_specs=in_specs,
        out_specs=[pl.BlockSpec((seq, RW_WIDTH), lambda b: (b, 0)),
                   pl.BlockSpec((None, 2, RW_HEADS, RW_HEAD, RW_HEAD), lambda b: (b, 0, 0, 0, 0))],
        out_shape=[jax.ShapeDtypeStruct((batch * seq, RW_WIDTH), BF16),
                   jax.ShapeDtypeStruct((batch, 2, RW_HEADS, RW_HEAD, RW_HEAD), F32)],
        scratch_shapes=[pltpu.VMEM((2, seq, RW_WIDTH), F32),
                        pltpu.VMEM((seq, RW_WIDTH), F32),
                        pltpu.VMEM((2, RW_PAIRS, LANE, LANE), F32)],
        compiler_params=pltpu.CompilerParams(vmem_limit_bytes=VMEM_LIMIT),
        name="rwkv",
    )(*args)


GLA_LEVELS = (32, 16, 8, 4, 2, 1)
GLA_GROUP = 4


def _gla_constants():
    c = CHUNK
    nl = len(GLA_LEVELS)
    mexp = np.zeros((2, (nl + 1) * c, c), np.float32)
    bmask = np.zeros((2, nl + 1, c, c), np.float32)
    for d in range(2):
        pos = np.arange(c) if d == 0 else c - 1 - np.arange(c)
        pt = pos[:, None]
        pj = pos[None, :]
        mexp[d, nl * c:] = (pj <= pt)
        for li, m in enumerate(GLA_LEVELS):
            mid = (pos // (2 * m)) * (2 * m) + m
            second = pos >= mid
            mq = (pj >= mid[:, None]) & (pj <= pt) & second[:, None]
            mk = (pj > pt) & (pj <= mid[:, None] - 1) & (~second)[:, None]
            mexp[d, li * c:(li + 1) * c] = mq | mk
            same = (pos[:, None] // (2 * m)) == (pos[None, :] // (2 * m))
            bmask[d, li] = same & second[:, None] & (~second)[None, :]
        bmask[d, nl] = np.eye(c)
    return mexp, np.concatenate([bmask, bmask], axis=-1)


def _gla_kernel(*refs, seq, has_s0):
    if has_s0:
        q_ref, k_ref, v_ref, ad_ref, s0_ref = refs[:5]
        rest = refs[5:]
    else:
        q_ref, k_ref, v_ref, ad_ref = refs[:4]
        s0_ref = None
        rest = refs[4:]
    (a2_ref, ab_ref, ng_ref, mexp_ref, bmask_ref,
     o_ref, sout_ref, acc_ref, st_ref) = rest
    c = CHUNK
    nc = seq // c
    nl = len(GLA_LEVELS)

    def chunk_operands(d, ci):
        rows = pl.ds(pl.multiple_of(ci * c, c), c)
        q = q_ref[rows, :].astype(F32) * (GLA_DK ** -0.5)
        k = k_ref[rows, :].astype(F32)
        v = v_ref[rows, :]
        x = _dot(ad_ref[rows, :].astype(BF16), a2_ref[d]) + ab_ref[d:d + 1, :]
        g = (jnp.minimum(x, 0.0) - jnp.log1p(jnp.exp(-jnp.abs(x)))) * (1.0 / GLA_LOGIT_NORM)
        g_hi, g_lo = _split(g)
        sums = _dot(mexp_ref[d], g_hi)
        ex = jnp.exp(sums[0:nl * c])
        b = sums[nl * c:(nl + 1) * c] + _dot(mexp_ref[d, nl * c:(nl + 1) * c, :], g_lo)
        last = 0 if d == 1 else c - 1
        blast = b[last:last + 1, :]
        qb = (q * jnp.exp(b)).astype(BF16)
        kdec = (k * jnp.exp(blast - b)).astype(BF16)
        dtot = jnp.exp(blast)
        qs = [(q * ex[li * c:(li + 1) * c]).astype(BF16) for li in range(nl)] + [q.astype(BF16)]
        ks = [(k * ex[li * c:(li + 1) * c]).astype(BF16) for li in range(nl)] + [k.astype(BF16)]
        return dict(rows=rows, qs=qs, ks=ks, qb=qb, kdec=kdec, dtot=dtot, v16=v.astype(BF16))

    even = lax.broadcasted_iota(jnp.int32, (1, LANE), 1) < GLA_DK
    r2 = lax.broadcasted_iota(jnp.int32, (2 * GLA_DV, LANE), 0)
    c2 = lax.broadcasted_iota(jnp.int32, (2 * GLA_DV, LANE), 1)
    same_head = (r2 < GLA_DV) == (c2 < GLA_DK)
    zv = jnp.zeros((c, GLA_DV), BF16)
    npair = GLA_HEADS // 2
    kls = [slice(p * LANE, (p + 1) * LANE) for p in range(npair)]
    vls = [slice(p * 2 * GLA_DV, (p + 1) * 2 * GLA_DV) for p in range(npair)]
    probs = [(d, p) for d in range(2) for p in range(npair)]
    first = lambda x: jnp.where(even, x, jnp.zeros_like(x))
    second = lambda x: jnp.where(even, jnp.zeros_like(x), x)

    group = min(GLA_GROUP, nc)

    def main_step(i, carry):
        ops = [[chunk_operands(0, i * group + g) for g in range(group)],
               [chunk_operands(1, nc - 1 - (i * group + g)) for g in range(group)]]
        allp = [(d, p, g) for d, p in probs for g in range(group)]
        lvl = [[_dot_nt(ops[d][g]["qs"][li][:, kls[p]],
                        jnp.concatenate([first(ops[d][g]["ks"][li][:, kls[p]]),
                                         second(ops[d][g]["ks"][li][:, kls[p]])], axis=0))
                for d, p, g in allp] for li in range(nl + 1)]
        att = [sum(bmask_ref[d, li] * lvl[li][j] for li in range(nl + 1)).astype(BF16)
               for j, (d, p, g) in enumerate(allp)]
        upd = [jnp.where(same_head, _dot_tn(ops[d][g]["v16"][:, vls[p]], ops[d][g]["kdec"][:, kls[p]]), 0.0)
               for d, p, g in allp]
        states = []
        for d, p in probs:
            st = st_ref[d, p]
            for g in range(group):
                states.append(st)
                st = st * ops[d][g]["dtot"][:, kls[p]] + upd[len(states) - 1]
            st_ref[d, p] = st
        for j, (d, p, g) in enumerate(allp):
            v_p = ops[d][g]["v16"][:, vls[p]]
            v_bd = jnp.concatenate([jnp.concatenate([v_p[:, 0:GLA_DV], zv], axis=1),
                                    jnp.concatenate([zv, v_p[:, GLA_DV:2 * GLA_DV]], axis=1)], axis=0)
            inter = _dot_nt(ops[d][g]["qb"][:, kls[p]], states[j].astype(BF16))
            acc_ref[d, ops[d][g]["rows"], vls[p]] = inter + _dot(att[j], v_bd)
        return carry

    blk = min(RW_BLOCK_ROWS, seq)

    def post_step(i, carry):
        rows = pl.ds(pl.multiple_of(i * blk, blk), blk)
        for h in range(GLA_HEADS):
            sv = slice(h * GLA_DV, (h + 1) * GLA_DV)
            o_ref[rows, sv] = (_rms(acc_ref[0, rows, sv] + acc_ref[1, rows, sv]) * ng_ref[...]).astype(o_ref.dtype)
        return carry

    st_ref[...] = jnp.zeros_like(st_ref)
    if has_s0:
        for d in range(2):
            for h in range(GLA_HEADS):
                ro, co = (h % 2) * GLA_DV, (h % 2) * GLA_DK
                st_ref[d, h // 2, ro:ro + GLA_DV, co:co + GLA_DK] = s0_ref[d, h].T
    lax.fori_loop(0, nc // group, main_step, 0)
    for d in range(2):
        for h in range(GLA_HEADS):
            ro, co = (h % 2) * GLA_DV, (h % 2) * GLA_DK
            sout_ref[d, h] = st_ref[d, h // 2, ro:ro + GLA_DV, co:co + GLA_DK].T
    lax.fori_loop(0, seq // blk, post_step, 0)


def _gla_call(proj, s0, layer, batch, seq, wts):
    has_s0 = s0 is not None
    blk = lambda w, cb: pl.BlockSpec((seq, w), lambda b: (b, cb))
    in_specs = [blk(GLA_KW, COL_GL_Q // GLA_KW), blk(GLA_KW, COL_GL_K // GLA_KW),
                blk(GLA_VW, COL_GL_V // GLA_VW), blk(LANE, COL_GL_AD // LANE)]
    args = [proj, proj, proj, proj]
    if has_s0:
        in_specs.append(pl.BlockSpec((None, None, 2, GLA_HEADS, GLA_DK, GLA_DV),
                                     lambda b: (b, layer, 0, 0, 0, 0)))
        args.append(s0)
    for w in wts:
        in_specs.append(pl.BlockSpec(w.shape, lambda b, n=w.ndim: (0,) * n))
        args.append(w)
    return pl.pallas_call(
        functools.partial(_gla_kernel, seq=seq, has_s0=has_s0),
        grid=(batch,),
        in_specs=in_specs,
        out_specs=[pl.BlockSpec((seq, GLA_VW), lambda b: (b, 0)),
                   pl.BlockSpec((None, 2, GLA_HEADS, GLA_DK, GLA_DV), lambda b: (b, 0, 0, 0, 0))],
        out_shape=[jax.ShapeDtypeStruct((batch * seq, GLA_VW), BF16),
                   jax.ShapeDtypeStruct((batch, 2, GLA_HEADS, GLA_DK, GLA_DV), F32)],
        scratch_shapes=[pltpu.VMEM((2, seq, GLA_VW), F32),
                        pltpu.VMEM((2, GLA_HEADS // 2, 2 * GLA_DV, LANE), F32)],
        compiler_params=pltpu.CompilerParams(vmem_limit_bytes=VMEM_LIMIT),
        name="gla",
    )(*args)


def _mla_kernel(*refs, seq, past, tq):
    has_ctx = past > 0
    if has_ctx:
        (qd_ref, kvd_ref, kpe_ref, kpesw_ref, cckv_ref, ckpe_ref,
         qng_ref, wq_ref, wqsw_ref, kvng_ref, wk_ref, wv_ref,
         cosq_ref, sinq_ref, cosk_ref, sink_ref, epos_ref,
         o_ref, ckv_ref, kcat_ref, vv_ref) = refs
    else:
        (qd_ref, kvd_ref, kpe_ref,
         qng_ref, wq_ref, kvng_ref, wk_ref, wv_ref,
         o_ref, ckv_ref, kcat_ref, vv_ref) = refs

    ones_hi = jnp.where(lax.broadcasted_iota(jnp.int32, (1, LANE), 1) >= MLA_V, 1.0, 0.0)

    @pl.when(pl.program_id(1) == 0)
    def _():
        ckv = _rms(kvd_ref[...].astype(F32)) * kvng_ref[...]
        ckv_ref[...] = ckv
        if has_ctx:
            kpos = kpe_ref[...] * cosk_ref[...] + kpesw_ref[...] * sink_ref[...]
        else:
            kpos = kpe_ref[...]
        segs = [(0, seq, ckv, kpos)]
        if has_ctx:
            cpos = _dot(ckpe_ref[...].astype(BF16), epos_ref[...])
            segs.append((seq, past, cckv_ref[...], cpos))
        for start, n, lat, pos in segs:
            lat16 = lat.astype(BF16)
            kc = _dot(lat16, wk_ref[...])
            vc = _dot(lat16, wv_ref[...])
            for h in range(MLA_HEADS):
                hl = slice(h * LANE, (h + 1) * LANE)
                kcat_ref[h, start:start + n, :] = (kc[:, hl] + pos).astype(BF16)
                vv_ref[h, start:start + n, :] = (vc[:, hl] + ones_hi).astype(BF16)

    qlat = (_rms(qd_ref[...].astype(F32)) * qng_ref[...]).astype(BF16)
    qc = _dot(qlat, wq_ref[...])
    if has_ctx:
        qsw = _dot(qlat, wqsw_ref[...])
    hg = MLA_HEADS if tq * (seq + past) <= MLA_GROUP_SCORES else 1
    for h0 in range(0, MLA_HEADS, hg):
        heads = range(h0, h0 + hg)
        q_raw = [qc[:, h * LANE:(h + 1) * LANE] for h in heads]
        if has_ctx:
            q_self = [(q_raw[i] * cosq_ref[...] + qsw[:, h * LANE:(h + 1) * LANE] * sinq_ref[...]).astype(BF16)
                      for i, h in enumerate(heads)]
        else:
            q_self = [x.astype(BF16) for x in q_raw]
        s1 = [_dot_nt(q_self[i], kcat_ref[h, 0:seq, :]) for i, h in enumerate(heads)]
        m = [jnp.max(s, axis=-1, keepdims=True) for s in s1]
        if has_ctx:
            s2 = [_dot_nt(q_raw[i].astype(BF16), kcat_ref[h, seq:seq + past, :]) for i, h in enumerate(heads)]
            m = [jnp.maximum(m[i], jnp.max(s2[i], axis=-1, keepdims=True)) for i in range(hg)]
        o = [_dot(jnp.exp((s1[i] - m[i]).astype(BF16)), vv_ref[h, 0:seq, :]) for i, h in enumerate(heads)]
        if has_ctx:
            o = [o[i] + _dot(jnp.exp((s2[i] - m[i]).astype(BF16)), vv_ref[h, seq:seq + past, :])
                 for i, h in enumerate(heads)]
        for i, h in enumerate(heads):
            o_h = o[i] / o[i][:, MLA_V:MLA_V + 1]
            o_ref[:, h * MLA_V:(h + 1) * MLA_V] = o_h[:, 0:MLA_V].astype(o_ref.dtype)


def _mla_call(proj, cache_ckv, cache_kpe, layer, batch, seq, wts, tables):
    has_ctx = cache_ckv is not None
    past = cache_ckv.shape[2] if has_ctx else 0
    tq = min(MLA_TQ, seq)
    nq = seq // tq
    full = lambda cb: pl.BlockSpec((seq, LANE), lambda b, i: (b, cb))
    in_specs = [pl.BlockSpec((tq, MLA_Q_RANK), lambda b, i: (b * nq + i, COL_ML_QD // MLA_Q_RANK)),
                full(COL_ML_KVD // LANE), full(COL_KPE // LANE)]
    args = [proj, proj, proj]
    if has_ctx:
        in_specs += [full(COL_KPE_SW // LANE),
                     pl.BlockSpec((None, None, past, MLA_KV_RANK), lambda b, i: (b, layer, 0, 0)),
                     pl.BlockSpec((None, None, past, MLA_ROPE), lambda b, i: (b, layer, 0, 0))]
        args += [proj, cache_ckv, cache_kpe]
    qn_g, wq_cat, wq_sw, kvn_g, wk_pad, wv = wts
    const = lambda w: pl.BlockSpec(w.shape, lambda b, i, n=w.ndim: (0,) * n)
    if has_ctx:
        cosq, sinq, cosk, sink, epos = tables
        wlist = [qn_g, wq_cat, wq_sw, kvn_g, wk_pad, wv]
        in_specs += [const(w) for w in wlist]
        in_specs += [pl.BlockSpec((tq, LANE), lambda b, i: (i, 0)), pl.BlockSpec((tq, LANE), lambda b, i: (i, 0)),
                     const(cosk), const(sink), const(epos)]
        args += wlist + [cosq, sinq, cosk, sink, epos]
    else:
        wlist = [qn_g, wq_cat, kvn_g, wk_pad, wv]
        in_specs += [const(w) for w in wlist]
        args += wlist
    return pl.pallas_call(
        functools.partial(_mla_kernel, seq=seq, past=past, tq=tq),
        grid=(batch, nq),
        in_specs=in_specs,
        out_specs=[pl.BlockSpec((tq, MLA_VW), lambda b, i: (b * nq + i, 0)),
                   pl.BlockSpec((seq, MLA_KV_RANK), lambda b, i: (b, 0))],
        out_shape=[jax.ShapeDtypeStruct((batch * seq, MLA_VW), BF16),
                   jax.ShapeDtypeStruct((batch * seq, MLA_KV_RANK), F32)],
        scratch_shapes=[pltpu.VMEM((MLA_HEADS, seq + past, LANE), BF16),
                        pltpu.VMEM((MLA_HEADS, seq + past, LANE), BF16)],
        compiler_params=pltpu.CompilerParams(vmem_limit_bytes=VMEM_LIMIT,
                                             dimension_semantics=("arbitrary", "arbitrary")),
        name="mla",
    )(*args)


def _outproj_kernel(x_ref, oa_ref, ob_ref, oc_ref, ga_ref, gb_ref, gc_ref, ma_ref, mb_ref, mc_ref,
                    mod_ref, wa_ref, wb_ref, wc_ref, wo_ref, fg_ref, o_ref, *, final):
    def branch(o_r, g_r, m_r, w_r):
        h = 0.5 * g_r[...]
        act = o_r[...] * (h * (1.0 + jnp.tanh(h)))
        y = _dot(act, w_r[...]).astype(BF16)
        return (0.5 * (1.0 + jnp.tanh(0.5 * m_r[...]))) * y

    y = (branch(oa_ref, ga_ref, ma_ref, wa_ref) + branch(ob_ref, gb_ref, mb_ref, wb_ref)
         + branch(oc_ref, gc_ref, mc_ref, wc_ref))
    y = _dot(y, wo_ref[...])
    gate = mod_ref[...][:, 2 * D_MODEL:3 * D_MODEL]
    hn = x_ref[...] + gate * y
    if final:
        hn = _rms(hn) * fg_ref[...]
    o_ref[...] = hn


def _outproj_call(x2d, o_a, o_b, o_c, proj, mod3, wts, final_g, rows_per_mod, mod_base, final):
    rows = x2d.shape[0]
    tm = OUTPROJ_TM
    tiles_per_mod = rows_per_mod // tm
    row = lambda w, cb=0: pl.BlockSpec((tm, w), lambda i: (i, cb))
    const = lambda w: pl.BlockSpec(w.shape, lambda i, n=w.ndim: (0,) * n)
    in_specs = [row(D_MODEL), row(512), row(512), row(512),
                row(512, COL_GATES // 512), row(512, COL_GATES // 512 + 1), row(512, COL_GATES // 512 + 2),
                row(D_MODEL, COL_MERGE // D_MODEL), row(D_MODEL, COL_MERGE // D_MODEL + 1),
                row(D_MODEL, COL_MERGE // D_MODEL + 2),
                pl.BlockSpec((None, 1, 3 * D_MODEL), lambda i: (mod_base + i // tiles_per_mod, 0, 0))]
    in_specs += [const(w) for w in wts] + [const(final_g)]
    return pl.pallas_call(
        functools.partial(_outproj_kernel, final=final),
        grid=(rows // tm,),
        in_specs=in_specs,
        out_specs=row(D_MODEL),
        out_shape=jax.ShapeDtypeStruct((rows, D_MODEL), F32),
        compiler_params=pltpu.CompilerParams(vmem_limit_bytes=VMEM_LIMIT),
        name="outproj",
    )(x2d, o_a, o_b, o_c, proj, proj, proj, proj, proj, proj, mod3, *wts, final_g)


def _pack_kernel(wt_ref, wa_ref, wb_ref):
    seg = lambda i: wt_ref[_IN_OFF[i]:_IN_OFF[i + 1], :]
    tr = wt_ref.shape[1]
    z = lambda n: jnp.zeros((n, tr), F32)

    def put(ref, col, xt):
        ref[:, col:col + xt.shape[0]] = xt.T.astype(BF16)

    put(wa_ref, COL_RW_R, wt_ref[_IN_OFF[0]:_IN_OFF[5], :])
    put(wa_ref, COL_GL_AD, jnp.concatenate([seg(9), z(LANE - GLA_GATE_RANK)], axis=0))
    put(wa_ref, COL_ML_QD, seg(11))
    put(wa_ref, COL_GL_Q, wt_ref[_IN_OFF[6]:_IN_OFF[9], :])
    put(wa_ref, COL_ML_KVD, seg(12))
    kpe = seg(13)
    q = MLA_ROPE // 4
    kpe_sw = jnp.concatenate([kpe[q:2 * q], kpe[0:q], kpe[3 * q:4 * q], kpe[2 * q:3 * q]], axis=0)
    tail = LANE - KPE_LANE - MLA_ROPE
    put(wa_ref, COL_KPE, jnp.concatenate([z(KPE_LANE), kpe, z(tail)], axis=0))
    put(wa_ref, COL_KPE_SW, jnp.concatenate([z(KPE_LANE), kpe_sw, z(tail)], axis=0))
    wa_ref[:, COL_KPE_SW + LANE:PACK_A] = jnp.zeros((tr, PACK_A - COL_KPE_SW - LANE), BF16)
    put(wb_ref, COL_MERGE, seg(15))
    put(wb_ref, COL_GATES, seg(5))
    put(wb_ref, COL_GATES + RW_WIDTH, seg(10))
    put(wb_ref, COL_GATES + RW_WIDTH + GLA_VW, seg(14))


def _pack_call(w_in):
    tr = 256
    wt = jnp.swapaxes(w_in, 1, 2)
    width = wt.shape[1]
    return pl.pallas_call(
        _pack_kernel,
        grid=(DEPTH, D_MODEL // tr),
        in_specs=[pl.BlockSpec((None, width, tr), lambda l, i: (l, 0, i))],
        out_specs=[pl.BlockSpec((None, tr, PACK_A), lambda l, i: (l, i, 0)),
                   pl.BlockSpec((None, tr, PACK_B), lambda l, i: (l, i, 0))],
        out_shape=[jax.ShapeDtypeStruct((DEPTH, D_MODEL, PACK_A), BF16),
                   jax.ShapeDtypeStruct((DEPTH, D_MODEL, PACK_B), BF16)],
        compiler_params=pltpu.CompilerParams(vmem_limit_bytes=VMEM_LIMIT),
        name="pack",
    )(wt)


def _rope_lane_tables(seq):
    n_freq = MLA_ROPE // 4
    t = np.arange(seq)
    inv = ROPE_THETA ** (-np.arange(n_freq, dtype=np.float64) / n_freq)
    ang = np.stack([(t // GRID_W)[:, None] * inv, (t % GRID_W)[:, None] * inv], axis=1)
    cos = np.repeat(np.cos(ang)[:, :, None, :], 2, axis=2).reshape(seq, MLA_ROPE)
    sin = np.stack([-np.sin(ang), np.sin(ang)], axis=2).reshape(seq, MLA_ROPE)
    return cos, sin


def _mla_tables(seq):
    cos, sin = _rope_lane_tables(seq)
    cosq = np.zeros((seq, LANE), np.float32)
    sinq = np.zeros((seq, LANE), np.float32)
    cosq[:, :KPE_LANE] = 1.0
    cosq[:, KPE_LANE:KPE_LANE + MLA_ROPE] = cos
    sinq[:, KPE_LANE:KPE_LANE + MLA_ROPE] = sin
    cosk = np.zeros((seq, LANE), np.float32)
    cosk[:, KPE_LANE:KPE_LANE + MLA_ROPE] = cos
    epos = np.zeros((MLA_ROPE, LANE), np.float32)
    epos[np.arange(MLA_ROPE), KPE_LANE + np.arange(MLA_ROPE)] = 1.0
    return (jnp.asarray(cosq), jnp.asarray(sinq), jnp.asarray(cosk), jnp.asarray(sinq),
            jnp.asarray(epos, dtype=BF16))


def _pack_mla_weights(qn_g, wq_up, kvn_g, wkv_up):
    scale = (MLA_NOPE + MLA_ROPE) ** -0.5
    wq = wq_up.reshape(MLA_Q_RANK, MLA_HEADS, MLA_NOPE + MLA_ROPE) * scale
    nope, rope = wq[..., :MLA_NOPE], wq[..., MLA_NOPE:]
    q = MLA_ROPE // 4
    rope_sw = jnp.concatenate([rope[..., q:2 * q], rope[..., 0:q], rope[..., 3 * q:4 * q], rope[..., 2 * q:3 * q]], -1)
    tail = jnp.zeros((MLA_Q_RANK, MLA_HEADS, LANE - KPE_LANE - MLA_ROPE), wq.dtype)
    wq_cat = jnp.concatenate([nope, rope, tail], -1).reshape(MLA_Q_RANK, MLA_HEADS * LANE).astype(BF16)
    wq_sw = jnp.concatenate([jnp.zeros_like(nope), rope_sw, tail], -1).reshape(MLA_Q_RANK, MLA_HEADS * LANE).astype(BF16)
    wkv = wkv_up.reshape(MLA_KV_RANK, MLA_HEADS, MLA_NOPE + MLA_V)
    wk = jnp.concatenate([wkv[..., :MLA_NOPE], jnp.zeros((MLA_KV_RANK, MLA_HEADS, LANE - MLA_NOPE), wkv.dtype)], -1)
    wk_pad = wk.reshape(MLA_KV_RANK, MLA_HEADS * LANE).astype(BF16)
    wv = jnp.concatenate([wkv[..., MLA_NOPE:], jnp.zeros((MLA_KV_RANK, MLA_HEADS, LANE - MLA_V), wkv.dtype)], -1)
    wv = wv.reshape(MLA_KV_RANK, MLA_HEADS * LANE).astype(BF16)
    return (qn_g.reshape(1, -1), wq_cat, wq_sw, kvn_g.reshape(1, -1), wk_pad, wv)


def _head_sum_matrix():
    lane = np.arange(RW_WIDTH)
    return jnp.asarray((lane[:, None] // RW_HEAD == lane[None, :] // RW_HEAD).astype(np.float32), dtype=BF16)


def _trunk(x_prompt, x_sample, c, cache_ckv, cache_kpe, state_rwkv, state_gla, c_ctx,
           norm_g, w_mod, b_mod, w_in, rw_w0, rw_w2, rw_a0, rw_a2, rw_k_k, rw_k_a, rw_r_k,
           rw_ln_g, rw_ln_b, rw_out, gla_a2, gla_ab, gla_norm_g, gla_out,
           mla_qn_g, mla_wq_up, mla_kvn_g, mla_wkv_up, mla_out, w_out, final_g):
    bp, tp, _ = x_prompt.shape
    bs, ts, _ = x_sample.shape
    hp = x_prompt.reshape(bp * tp, D_MODEL)
    hs = x_sample.reshape(bs * ts, D_MODEL)
    cvec8 = jnp.concatenate([c, c_ctx[None, :], jnp.zeros((8 - bs - 1, D_MODEL), F32)], axis=0)
    ctx_row = bs
    hsum = _head_sum_matrix()
    mexp, bmask = _gla_constants()
    mexp = jnp.asarray(mexp, dtype=BF16)
    bmask = jnp.asarray(bmask)
    tables = _mla_tables(ts)
    w_a, w_b = _pack_call(w_in)
    fg = final_g.reshape(1, D_MODEL)
    ckv_l, kpe_l, rw_l, gla_l = [], [], [], []
    for l in range(DEPTH):
        mod3 = _mod_call(cvec8, w_mod, b_mod, l).reshape(8, 1, 3 * D_MODEL)
        ng = norm_g[l].reshape(1, D_MODEL)
        row = lambda a: a.reshape(1, -1)
        zr = jnp.zeros((2, RW_RANK, RW_WIDTH), F32)
        wcat = jnp.concatenate([jnp.concatenate([rw_w2[l], zr], axis=2),
                                jnp.concatenate([zr, rw_a2[l]], axis=2)], axis=1).astype(BF16)
        rw_wts = (wcat, jnp.concatenate([rw_w0[l], rw_a0[l]], axis=1), row(rw_k_k[l]), row(rw_k_a[l]),
                  row(rw_r_k[l]), row(rw_ln_g[l]), row(rw_ln_b[l]), hsum)
        a2p = jnp.concatenate([gla_a2[l], jnp.zeros((2, LANE - GLA_GATE_RANK, GLA_KW), F32)], axis=1).astype(BF16)
        gla_wts = (a2p, gla_ab[l], row(gla_norm_g[l]), mexp, bmask)
        mla_wts = _pack_mla_weights(mla_qn_g[l], mla_wq_up[l], mla_kvn_g[l], mla_wkv_up[l])
        out_wts = (rw_out[l].astype(BF16), gla_out[l].astype(BF16), mla_out[l].astype(BF16), w_out[l].astype(BF16))
        final = l == DEPTH - 1

        proj, proj_b = _inproj_call(hp, mod3, ng, w_a, w_b, l, bp * tp, ctx_row)
        o_a, s_rw = _rwkv_call(proj, None, l, bp, tp, rw_wts)
        o_b, s_gla = _gla_call(proj, None, l, bp, tp, gla_wts)
        o_c, ckv = _mla_call(proj, None, None, l, bp, tp, mla_wts, None)
        ckv_l.append(ckv.reshape(bp, tp, MLA_KV_RANK))
        kpe_l.append(proj[:, COL_KPE + KPE_LANE:COL_KPE + KPE_LANE + MLA_ROPE].astype(F32).reshape(bp, tp, MLA_ROPE))
        rw_l.append(s_rw)
        gla_l.append(s_gla)
        hp = _outproj_call(hp, o_a, o_b, o_c, proj_b, mod3, out_wts, fg, bp * tp, ctx_row, final)

        proj, proj_b = _inproj_call(hs, mod3, ng, w_a, w_b, l, ts, 0)
        o_a, _ = _rwkv_call(proj, state_rwkv, l, bs, ts, rw_wts)
        o_b, _ = _gla_call(proj, state_gla, l, bs, ts, gla_wts)
        o_c, _ = _mla_call(proj, cache_ckv, cache_kpe, l, bs, ts, mla_wts, tables)
        hs = _outproj_call(hs, o_a, o_b, o_c, proj_b, mod3, out_wts, fg, ts, 0, final)

    return (hp.reshape(bp, tp, D_MODEL), hs.reshape(bs, ts, D_MODEL),
            jnp.stack(ckv_l, axis=1), jnp.stack(kpe_l, axis=1),
            jnp.stack(rw_l, axis=1), jnp.stack(gla_l, axis=1))


_trunk_jit = jax.jit(_trunk)


def kernel(x_prompt, x_sample, c, cache_ckv, cache_kpe, state_rwkv, state_gla, c_ctx, norm_g, w_mod, b_mod, w_in, rw_w0, rw_w2, rw_a0, rw_a2, rw_k_k, rw_k_a, rw_r_k, rw_ln_g, rw_ln_b, rw_out, gla_a2, gla_ab, gla_norm_g, gla_out, mla_qn_g, mla_wq_up, mla_kvn_g, mla_wkv_up, mla_out, w_out, final_g):
    return _trunk_jit(x_prompt, x_sample, c, cache_ckv, cache_kpe, state_rwkv, state_gla, c_ctx, norm_g, w_mod, b_mod, w_in, rw_w0, rw_w2, rw_a0, rw_a2, rw_k_k, rw_k_a, rw_r_k, rw_ln_g, rw_ln_b, rw_out, gla_a2, gla_ab, gla_norm_g, gla_out, mla_qn_g, mla_wq_up, mla_kvn_g, mla_wkv_up, mla_out, w_out, final_g)
```

```python
import functools

import numpy as np
import jax
import jax.numpy as jnp
from jax import lax
from jax.experimental import pallas as pl
from jax.experimental.pallas import tpu as pltpu

F32 = jnp.float32
BF16 = jnp.bfloat16

D_MODEL = 1024
DEPTH = 2
GRID_W = 64
NORM_EPS = 1e-6
RW_HEADS = 8
RW_HEAD = 64
RW_WIDTH = RW_HEADS * RW_HEAD
RW_RANK = 64
RW_GN_EPS = 64e-5
GLA_HEADS = 4
GLA_DK = 64
GLA_DV = 128
GLA_KW = GLA_HEADS * GLA_DK
GLA_VW = GLA_HEADS * GLA_DV
GLA_GATE_RANK = 16
GLA_LOGIT_NORM = 16.0
MLA_HEADS = 8
MLA_NOPE = 64
MLA_ROPE = 32
MLA_V = 64
MLA_Q_RANK = 256
MLA_KV_RANK = 128
MLA_VW = MLA_HEADS * MLA_V
ROPE_THETA = 10000.0
N_BRANCH = 3

_IN_SIZES = (RW_WIDTH, RW_WIDTH, RW_WIDTH, RW_RANK, RW_RANK, RW_WIDTH,
             GLA_KW, GLA_KW, GLA_VW, GLA_GATE_RANK, GLA_VW,
             MLA_Q_RANK, MLA_KV_RANK, MLA_ROPE, MLA_VW, N_BRANCH * D_MODEL)
_IN_OFF = tuple(int(v) for v in np.concatenate([[0], np.cumsum(_IN_SIZES)]))

LANE = 128
COL_RW_R = 0
COL_RW_K = 512
COL_RW_V = 1024
COL_RW_WA = 1536
COL_GL_AD = 1664
COL_ML_QD = 1792
COL_GL_Q = 2048
COL_GL_K = 2304
COL_GL_V = 2560
COL_ML_KVD = 3072
COL_KPE = 3200
COL_KPE_SW = 3328
PACK_A = 3584
COL_MERGE = 0
COL_GATES = 3072
PACK_B = 4608
KPE_LANE = MLA_NOPE

CHUNK = 64
INPROJ_TM = 512
INPROJ_SUB = 256
INPROJ_TN = 512
OUTPROJ_TM = 512
MLA_TQ = 1024
MLA_GROUP_SCORES = 256 * 256
VMEM_LIMIT = 48 * 1024 * 1024


def _dot(a, b, prec=None):
    return jnp.dot(a, b, preferred_element_type=F32, precision=prec)


def _dot_nt(a, b, prec=None):
    return lax.dot_general(a, b, (((1,), (1,)), ((), ())), preferred_element_type=F32, precision=prec)


def _dot_tn(a, b, prec=None):
    return lax.dot_general(a, b, (((0,), (0,)), ((), ())), preferred_element_type=F32, precision=prec)


def _split(x):
    hi = x.astype(BF16)
    return hi, (x - hi.astype(F32)).astype(BF16)


def _dot_split(a, b):
    ah, al = _split(a)
    bh, bl = _split(b)
    return _dot(ah, bh) + _dot(al, bh) + _dot(ah, bl)


def _dot_sel(sel16, x):
    xh, xl = _split(x)
    return _dot(sel16, xh) + _dot(sel16, xl)


def _sigmoid(x):
    return 0.5 * jnp.tanh(0.5 * x) + 0.5


def _rms(x, eps=NORM_EPS):
    return x * lax.rsqrt(jnp.mean(x * x, axis=-1, keepdims=True) + eps)


def _mod_kernel(c_ref, w_ref, b_ref, o_ref):
    c = c_ref[...]
    o_ref[...] = _dot_split(c * _sigmoid(c), w_ref[...]) + b_ref[...]


def _mod_call(cvec8, w_mod, b_mod, layer):
    tn = 1024
    return pl.pallas_call(
        _mod_kernel,
        grid=(3 * D_MODEL // tn,),
        in_specs=[pl.BlockSpec((8, D_MODEL), lambda j: (0, 0)),
                  pl.BlockSpec((None, D_MODEL, tn), lambda j: (layer, 0, j)),
                  pl.BlockSpec((None, 1, tn), lambda j: (layer, 0, j))],
        out_specs=pl.BlockSpec((8, tn), lambda j: (0, j)),
        out_shape=jax.ShapeDtypeStruct((8, 3 * D_MODEL), F32),
        compiler_params=pltpu.CompilerParams(vmem_limit_bytes=VMEM_LIMIT),
        name="mod",
    )(cvec8, w_mod, b_mod.reshape(DEPTH, 1, 3 * D_MODEL))


def _inproj_kernel(x_ref, mod_ref, g_ref, wa_ref, wb_ref, oa_ref, ob_ref):
    m = mod_ref[...]
    shift = m[:, 0:D_MODEL]
    scale1 = 1.0 + m[:, D_MODEL:2 * D_MODEL]
    tm = x_ref.shape[0]
    for r0 in range(0, tm, INPROJ_SUB):
        rows = slice(r0, r0 + INPROJ_SUB)
        h = (_rms(x_ref[rows, :]) * g_ref[...] * scale1 + shift).astype(BF16)
        for w_ref, o_ref in ((wa_ref, oa_ref), (wb_ref, ob_ref)):
            for c0 in range(0, w_ref.shape[1], INPROJ_TN):
                cols = slice(c0, c0 + INPROJ_TN)
                o_ref[rows, cols] = _dot(h, w_ref[:, cols]).astype(BF16)


def _inproj_call(x2d, mod3, norm_g, w_a, w_b, layer, rows_per_mod, mod_base):
    rows = x2d.shape[0]
    tm = INPROJ_TM
    tiles_per_mod = rows_per_mod // tm
    resident = lambda w: pl.BlockSpec((None,) + w.shape[1:], lambda i: (layer, 0, 0), pipeline_mode=pl.Buffered(1))
    return pl.pallas_call(
        _inproj_kernel,
        grid=(rows // tm,),
        in_specs=[pl.BlockSpec((tm, D_MODEL), lambda i: (i, 0)),
                  pl.BlockSpec((None, 1, 3 * D_MODEL), lambda i: (mod_base + i // tiles_per_mod, 0, 0)),
                  pl.BlockSpec((1, D_MODEL), lambda i: (0, 0)),
                  resident(w_a), resident(w_b)],
        out_specs=[pl.BlockSpec((tm, PACK_A), lambda i: (i, 0)),
                   pl.BlockSpec((tm, PACK_B), lambda i: (i, 0))],
        out_shape=[jax.ShapeDtypeStruct((rows, PACK_A), BF16),
                   jax.ShapeDtypeStruct((rows, PACK_B), BF16)],
        compiler_params=pltpu.CompilerParams(vmem_limit_bytes=VMEM_LIMIT),
        name="inproj",
    )(x2d, mod3, norm_g, w_a, w_b)


RW_CHUNK = 64
RW_PAIRS = RW_HEADS // 2
RW_BLOCK_ROWS = 256
RW_NEG_DECAY_SCALE = -float(np.exp(-0.5))
RW_UNROLL = 2


def _rwkv_time_mask(c, reverse):
    row = lax.broadcasted_iota(jnp.int32, (2 * c, 4 * c), 0)
    col = lax.broadcasted_iota(jnp.int32, (2 * c, 4 * c), 1)
    t = jnp.where(row >= c, row - c, row)
    s = col & (c - 1)
    earlier = (s > t) if reverse else (s < t)
    return earlier | ((row >= c) & (s == t))


def _rwkv_kernel(*refs, seq, has_s0):
    if has_s0:
        r_ref, k_ref, v_ref, wa_ref, s0_ref = refs[:5]
        rest = refs[5:]
    else:
        r_ref, k_ref, v_ref, wa_ref = refs[:4]
        s0_ref = None
        rest = refs[4:]
    (wcat_ref, wa0_ref, kk_ref, ka_ref, rk_ref, lng_ref, lnb_ref, hsum_ref,
     o_ref, sout_ref, acc_ref, kn_ref, st_ref) = rest
    c = RW_CHUNK
    nc = seq // c
    blk = min(RW_BLOCK_ROWS, seq)
    hsum = hsum_ref[...]
    pair_lanes = [slice(p * LANE, (p + 1) * LANE) for p in range(RW_PAIRS)]

    def pre_step(i, carry):
        rows = pl.ds(pl.multiple_of(i * blk, blk), blk)
        kk0 = [k_ref[rows, pl_].astype(F32) * kk_ref[:, pl_] for pl_ in pair_lanes]
        ss = [_dot((x * x).astype(BF16), hsum) for x in kk0]
        for pl_, x, s in zip(pair_lanes, kk0, ss):
            kn_ref[rows, pl_] = x / jnp.maximum(jnp.sqrt(s), 1e-12)
        return carry

    def post_step(i, carry):
        rows = pl.ds(pl.multiple_of(i * blk, blk), blk)
        o = [acc_ref[0, rows, pl_] + acc_ref[1, rows, pl_] for pl_ in pair_lanes]
        rk = [r_ref[rows, pl_].astype(F32) * k_ref[rows, pl_].astype(F32) * rk_ref[:, pl_] for pl_ in pair_lanes]
        red = [_dot(jnp.concatenate([a, b], axis=0).astype(BF16), hsum) for a, b in zip(o, rk)]
        dev = [a - s[0:blk] * (1.0 / RW_HEAD) for a, s in zip(o, red)]
        var = [_dot((x * x).astype(BF16), hsum) * (1.0 / RW_HEAD) for x in dev]
        for pl_, x, vr, s in zip(pair_lanes, dev, var, red):
            normed = x * lax.rsqrt(vr + RW_GN_EPS) * lng_ref[:, pl_] + lnb_ref[:, pl_]
            o_ref[rows, pl_] = (normed + s[blk:2 * blk] * v_ref[rows, pl_]).astype(o_ref.dtype)
        return carry

    even = lax.broadcasted_iota(jnp.int32, (1, LANE), 1) < RW_HEAD
    row = lax.broadcasted_iota(jnp.int32, (c, c), 0)
    col = lax.broadcasted_iota(jnp.int32, (c, c), 1)
    incl16 = [jnp.where(col <= row, 1.0, 0.0).astype(BF16), jnp.where(col >= row, 1.0, 0.0).astype(BF16)]
    tmask = [_rwkv_time_mask(c, False), _rwkv_time_mask(c, True)]
    r2 = lax.broadcasted_iota(jnp.int32, (LANE, LANE), 0)
    c2 = lax.broadcasted_iota(jnp.int32, (LANE, LANE), 1)
    same_head = (r2 < RW_HEAD) == (c2 < RW_HEAD)
    p_lanes = ((lax.broadcasted_iota(jnp.int32, (2 * c, LANE), 0) < c)
               == (lax.broadcasted_iota(jnp.int32, (2 * c, LANE), 1) < RW_HEAD))
    lanes = [slice(p * LANE, (p + 1) * LANE) for p in range(RW_PAIRS)]
    probs = [(d, p) for d in range(2) for p in range(RW_PAIRS)]

    def chunk_operands(d, ci):
        rows = pl.ds(pl.multiple_of(ci * c, c), c)
        r = r_ref[rows, :].astype(F32)
        k = k_ref[rows, :].astype(F32)
        wa = wa_ref[rows, :].astype(F32)
        kk = kn_ref[rows, :]
        lora = jnp.where(even, jnp.tanh(wa), wa)
        pre = _dot(lora.astype(BF16), wcat_ref[d]) + wa0_ref[d:d + 1, :]
        lw = RW_NEG_DECAY_SCALE * _sigmoid(pre[:, 0:RW_WIDTH])
        a = _sigmoid(pre[:, RW_WIDTH:2 * RW_WIDTH])
        kd = k * (1.0 + (a - 1.0) * ka_ref[...])
        bv = kk * a
        gam = _dot_sel(incl16[d], lw)
        last = 0 if d == 1 else c - 1
        gtot = gam[last:last + 1, :]
        gref = gam[c // 2:c // 2 + 1, :]
        e_in = jnp.exp(gam - gref)
        e_out = jnp.exp(gref - gam)
        e_end = e_out * jnp.exp(gtot - gref)
        lhs = jnp.concatenate([-kk * jnp.exp(gam - lw - gref), r * e_in], axis=0).astype(BF16)
        return dict(rows=rows, lhs=lhs, b16=(bv * e_out).astype(BF16), k16=(kd * e_out).astype(BF16),
                    be16=(bv * e_end).astype(BF16), ke16=(kd * e_end).astype(BF16),
                    v16=v_ref[rows, :].astype(BF16), dtot=jnp.exp(gtot), eref=jnp.exp(gref))

    def main_step(i, carry):
        steps = [i * RW_UNROLL + s for s in range(RW_UNROLL)]
        prepared = [[chunk_operands(0, t), chunk_operands(1, nc - 1 - t)] for t in steps]
        for ops in prepared:
            chain_step(ops)
        return carry

    def chain_step(ops):
        n = range(len(probs))
        pick = lambda name: [ops[d][name][:, lanes[p]] for d, p in probs]
        lhs, b, k, be, ke, v = (pick(s) for s in ("lhs", "b16", "k16", "be16", "ke16", "v16"))
        st = [st_ref[d, p] for d, p in probs]
        first = lambda x: jnp.where(even, x, jnp.zeros_like(x))
        second = lambda x: jnp.where(even, jnp.zeros_like(x), x)
        v_e = [first(x) for x in v]
        v_o = [second(x) for x in v]
        aa = [jnp.where(tmask[d], _dot_nt(lhs[j], jnp.concatenate(
            [first(k[j]), second(k[j]), second(b[j]), first(b[j])], axis=0)), 0.0) for j, (d, p) in enumerate(probs)]
        sp = [_dot_nt(lhs[j], (st[j] * ops[d]["eref"][:, lanes[p]]).astype(BF16))
              for j, (d, p) in enumerate(probs)]
        aa16 = [x.astype(BF16) for x in aa]
        rhs_u = [_dot(aa16[j][0:c, 0:LANE], jnp.concatenate([v_e[j], v_o[j]], axis=0)) + sp[j][0:c] for j in n]
        x = [jnp.concatenate([jnp.where(even, aa[j][0:c, LANE:2 * LANE], rhs_u[j]),
                              jnp.where(even, rhs_u[j], aa[j][0:c, LANE:2 * LANE])], axis=0) for j in n]
        covered = 1
        while covered < c:
            x16 = [xx.astype(BF16) for xx in x]
            x = [_dot(jnp.where(p_lanes, x16[j], jnp.zeros_like(x16[j])), x16[j])
                 + jnp.where(p_lanes, 0.0, x[j]) for j in n]
            covered *= 2
        u16 = [jnp.where(even, x[j][c:2 * c], x[j][0:c]).astype(BF16) for j in n]
        for j, (d, p) in enumerate(probs):
            uv_rows = jnp.concatenate([v_e[j], v_o[j], second(u16[j]), first(u16[j])], axis=0)
            acc_ref[d, ops[d]["rows"], lanes[p]] = sp[j][c:2 * c] + _dot(aa16[j][c:2 * c], uv_rows)
            upd = _dot_tn(jnp.concatenate([u16[j], v[j]], axis=0), jnp.concatenate([be[j], ke[j]], axis=0))
            st_ref[d, p] = st[j] * ops[d]["dtot"][:, lanes[p]] + jnp.where(same_head, upd, 0.0)

    st_ref[...] = jnp.zeros_like(st_ref)
    if has_s0:
        for d in range(2):
            for h in range(RW_HEADS):
                off = (h % 2) * RW_HEAD
                st_ref[d, h // 2, off:off + RW_HEAD, off:off + RW_HEAD] = s0_ref[d, h]
    lax.fori_loop(0, seq // blk, pre_step, 0)
    lax.fori_loop(0, nc // RW_UNROLL, main_step, 0)
    for d in range(2):
        for h in range(RW_HEADS):
            off = (h % 2) * RW_HEAD
            sout_ref[d, h] = st_ref[d, h // 2, off:off + RW_HEAD, off:off + RW_HEAD]
    lax.fori_loop(0, seq // blk, post_step, 0)


def _rwkv_call(proj, s0, layer, batch, seq, wts):
    has_s0 = s0 is not None
    blk = lambda w, cb: pl.BlockSpec((seq, w), lambda b: (b, cb))
    in_specs = [blk(512, COL_RW_R // 512), blk(512, COL_RW_K // 512), blk(512, COL_RW_V // 512),
                blk(LANE, COL_RW_WA // LANE)]
    args = [proj, proj, proj, proj]
    if has_s0:
        in_specs.append(pl.BlockSpec((None, None, 2, RW_HEADS, RW_HEAD, RW_HEAD),
                                     lambda b: (b, layer, 0, 0, 0, 0)))
        args.append(s0)
    for w in wts:
        in_specs.append(pl.BlockSpec(w.shape, lambda b, n=w.ndim: (0,) * n))
        args.append(w)
    return pl.pallas_call(
        functools.partial(_rwkv_kernel, seq=seq, has_s0=has_s0),
        grid=(batch,),
        in_specs=in_specs,
        out_specs=[pl.BlockSpec((seq, RW_WIDTH), lambda b: (b, 0)),
                   pl.BlockSpec((None, 2, RW_HEADS, RW_HEAD, RW_HEAD), lambda b: (b, 0, 0, 0, 0))],
        out_shape=[jax.ShapeDtypeStruct((batch * seq, RW_WIDTH), BF16),
                   jax.ShapeDtypeStruct((batch, 2, RW_HEADS, RW_HEAD, RW_HEAD), F32)],
        scratch_shapes=[pltpu.VMEM((2, seq, RW_WIDTH), F32),
                        pltpu.VMEM((seq, RW_WIDTH), F32),
                        pltpu.VMEM((2, RW_PAIRS, LANE, LANE), F32)],
        compiler_params=pltpu.CompilerParams(vmem_limit_bytes=VMEM_LIMIT),
        name="rwkv",
    )(*args)


GLA_LEVELS = (32, 16, 8, 4, 2, 1)
GLA_GROUP = 4


def _gla_constants():
    c = CHUNK
    nl = len(GLA_LEVELS)
    mexp = np.zeros((2, (nl + 1) * c, c), np.float32)
    bmask = np.zeros((2, nl + 1, c, c), np.float32)
    for d in range(2):
        pos = np.arange(c) if d == 0 else c - 1 - np.arange(c)
        pt = pos[:, None]
        pj = pos[None, :]
        mexp[d, nl * c:] = (pj <= pt)
        for li, m in enumerate(GLA_LEVELS):
            mid = (pos // (2 * m)) * (2 * m) + m
            second = pos >= mid
            mq = (pj >= mid[:, None]) & (pj <= pt) & second[:, None]
            mk = (pj > pt) & (pj <= mid[:, None] - 1) & (~second)[:, None]
            mexp[d, li * c:(li + 1) * c] = mq | mk
            same = (pos[:, None] // (2 * m)) == (pos[None, :] // (2 * m))
            bmask[d, li] = same & second[:, None] & (~second)[None, :]
        bmask[d, nl] = np.eye(c)
    return mexp, np.concatenate([bmask, bmask], axis=-1)


def _gla_kernel(*refs, seq, has_s0):
    if has_s0:
        q_ref, k_ref, v_ref, ad_ref, s0_ref = refs[:5]
        rest = refs[5:]
    else:
        q_ref, k_ref, v_ref, ad_ref = refs[:4]
        s0_ref = None
        rest = refs[4:]
    (a2_ref, ab_ref, ng_ref, mexp_ref, bmask_ref,
     o_ref, sout_ref, acc_ref, st_ref) = rest
    c = CHUNK
    nc = seq // c
    nl = len(GLA_LEVELS)

    def chunk_operands(d, ci):
        rows = pl.ds(pl.multiple_of(ci * c, c), c)
        q = q_ref[rows, :].astype(F32) * (GLA_DK ** -0.5)
        k = k_ref[rows, :].astype(F32)
        v = v_ref[rows, :]
        x = _dot(ad_ref[rows, :].astype(BF16), a2_ref[d]) + ab_ref[d:d + 1, :]
        g = (jnp.minimum(x, 0.0) - jnp.log1p(jnp.exp(-jnp.abs(x)))) * (1.0 / GLA_LOGIT_NORM)
        g_hi, g_lo = _split(g)
        sums = _dot(mexp_ref[d], g_hi)
        ex = jnp.exp(sums[0:nl * c])
        b = sums[nl * c:(nl + 1) * c] + _dot(mexp_ref[d, nl * c:(nl + 1) * c, :], g_lo)
        last = 0 if d == 1 else c - 1
        blast = b[last:last + 1, :]
        qb = (q * jnp.exp(b)).astype(BF16)
        kdec = (k * jnp.exp(blast - b)).astype(BF16)
        dtot = jnp.exp(blast)
        qs = [(q * ex[li * c:(li + 1) * c]).astype(BF16) for li in range(nl)] + [q.astype(BF16)]
        ks = [(k * ex[li * c:(li + 1) * c]).astype(BF16) for li in range(nl)] + [k.astype(BF16)]
        return dict(rows=rows, qs=qs, ks=ks, qb=qb, kdec=kdec, dtot=dtot, v16=v.astype(BF16))

    even = lax.broadcasted_iota(jnp.int32, (1, LANE), 1) < GLA_DK
    r2 = lax.broadcasted_iota(jnp.int32, (2 * GLA_DV, LANE), 0)
    c2 = lax.broadcasted_iota(jnp.int32, (2 * GLA_DV, LANE), 1)
    same_head = (r2 < GLA_DV) == (c2 < GLA_DK)
    zv = jnp.zeros((c, GLA_DV), BF16)
    npair = GLA_HEADS // 2
    kls = [slice(p * LANE, (p + 1) * LANE) for p in range(npair)]
    vls = [slice(p * 2 * GLA_DV, (p + 1) * 2 * GLA_DV) for p in range(npair)]
    probs = [(d, p) for d in range(2) for p in range(npair)]
    first = lambda x: jnp.where(even, x, jnp.zeros_like(x))
    second = lambda x: jnp.where(even, jnp.zeros_like(x), x)

    group = min(GLA_GROUP, nc)

    def main_step(i, carry):
        ops = [[chunk_operands(0, i * group + g) for g in range(group)],
               [chunk_operands(1, nc - 1 - (i * group + g)) for g in range(group)]]
        allp = [(d, p, g) for d, p in probs for g in range(group)]
        lvl = [[_dot_nt(ops[d][g]["qs"][li][:, kls[p]],
                        jnp.concatenate([first(ops[d][g]["ks"][li][:, kls[p]]),
                                         second(ops[d][g]["ks"][li][:, kls[p]])], axis=0))
                for d, p, g in allp] for li in range(nl + 1)]
        att = [sum(bmask_ref[d, li] * lvl[li][j] for li in range(nl + 1)).astype(BF16)
               for j, (d, p, g) in enumerate(allp)]
        upd = [jnp.where(same_head, _dot_tn(ops[d][g]["v16"][:, vls[p]], ops[d][g]["kdec"][:, kls[p]]), 0.0)
               for d, p, g in allp]
        states = []
        for d, p in probs:
            st = st_ref[d, p]
            for g in range(group):
                states.append(st)
                st = st * ops[d][g]["dtot"][:, kls[p]] + upd[len(states) - 1]
            st_ref[d, p] = st
        for j, (d, p, g) in enumerate(allp):
            v_p = ops[d][g]["v16"][:, vls[p]]
            v_bd = jnp.concatenate([jnp.concatenate([v_p[:, 0:GLA_DV], zv], axis=1),
                                    jnp.concatenate([zv, v_p[:, GLA_DV:2 * GLA_DV]], axis=1)], axis=0)
            inter = _dot_nt(ops[d][g]["qb"][:, kls[p]], states[j].astype(BF16))
            acc_ref[d, ops[d][g]["rows"], vls[p]] = inter + _dot(att[j], v_bd)
        return carry

    blk = min(RW_BLOCK_ROWS, seq)

    def post_step(i, carry):
        rows = pl.ds(pl.multiple_of(i * blk, blk), blk)
        for h in range(GLA_HEADS):
            sv = slice(h * GLA_DV, (h + 1) * GLA_DV)
            o_ref[rows, sv] = (_rms(acc_ref[0, rows, sv] + acc_ref[1, rows, sv]) * ng_ref[...]).astype(o_ref.dtype)
        return carry

    st_ref[...] = jnp.zeros_like(st_ref)
    if has_s0:
        for d in range(2):
            for h in range(GLA_HEADS):
                ro, co = (h % 2) * GLA_DV, (h % 2) * GLA_DK
                st_ref[d, h // 2, ro:ro + GLA_DV, co:co + GLA_DK] = s0_ref[d, h].T
    lax.fori_loop(0, nc // group, main_step, 0)
    for d in range(2):
        for h in range(GLA_HEADS):
            ro, co = (h % 2) * GLA_DV, (h % 2) * GLA_DK
            sout_ref[d, h] = st_ref[d, h // 2, ro:ro + GLA_DV, co:co + GLA_DK].T
    lax.fori_loop(0, seq // blk, post_step, 0)


def _gla_call(proj, s0, layer, batch, seq, wts):
    has_s0 = s0 is not None
    blk = lambda w, cb: pl.BlockSpec((seq, w), lambda b: (b, cb))
    in_specs = [blk(GLA_KW, COL_GL_Q // GLA_KW), blk(GLA_KW, COL_GL_K // GLA_KW),
                blk(GLA_VW, COL_GL_V // GLA_VW), blk(LANE, COL_GL_AD // LANE)]
    args = [proj, proj, proj, proj]
    if has_s0:
        in_specs.append(pl.BlockSpec((None, None, 2, GLA_HEADS, GLA_DK, GLA_DV),
                                     lambda b: (b, layer, 0, 0, 0, 0)))
        args.append(s0)
    for w in wts:
        in_specs.append(pl.BlockSpec(w.shape, lambda b, n=w.ndim: (0,) * n))
        args.append(w)
    return pl.pallas_call(
        functools.partial(_gla_kernel, seq=seq, has_s0=has_s0),
        grid=(batch,),
        in_specs=in_specs,
        out_specs=[pl.BlockSpec((seq, GLA_VW), lambda b: (b, 0)),
                   pl.BlockSpec((None, 2, GLA_HEADS, GLA_DK, GLA_DV), lambda b: (b, 0, 0, 0, 0))],
        out_shape=[jax.ShapeDtypeStruct((batch * seq, GLA_VW), BF16),
                   jax.ShapeDtypeStruct((batch, 2, GLA_HEADS, GLA_DK, GLA_DV), F32)],
        scratch_shapes=[pltpu.VMEM((2, seq, GLA_VW), F32),
                        pltpu.VMEM((2, GLA_HEADS // 2, 2 * GLA_DV, LANE), F32)],
        compiler_params=pltpu.CompilerParams(vmem_limit_bytes=VMEM_LIMIT),
        name="gla",
    )(*args)


def _mla_kernel(*refs, seq, past, tq):
    has_ctx = past > 0
    if has_ctx:
        (qd_ref, kvd_ref, kpe_ref, kpesw_ref, cckv_ref, ckpe_ref,
         qng_ref, wq_ref, wqsw_ref, kvng_ref, wk_ref, wv_ref,
         cosq_ref, sinq_ref, cosk_ref, sink_ref, epos_ref,
         o_ref, ckv_ref, kcat_ref, vv_ref) = refs
    else:
        (qd_ref, kvd_ref, kpe_ref,
         qng_ref, wq_ref, kvng_ref, wk_ref, wv_ref,
         o_ref, ckv_ref, kcat_ref, vv_ref) = refs

    ones_hi = jnp.where(lax.broadcasted_iota(jnp.int32, (1, LANE), 1) >= MLA_V, 1.0, 0.0)

    @pl.when(pl.program_id(1) == 0)
    def _():
        ckv = _rms(kvd_ref[...].astype(F32)) * kvng_ref[...]
        ckv_ref[...] = ckv
        if has_ctx:
            kpos = kpe_ref[...] * cosk_ref[...] + kpesw_ref[...] * sink_ref[...]
        else:
            kpos = kpe_ref[...]
        segs = [(0, seq, ckv, kpos)]
        if has_ctx:
            cpos = _dot(ckpe_ref[...].astype(BF16), epos_ref[...])
            segs.append((seq, past, cckv_ref[...], cpos))
        for start, n, lat, pos in segs:
            lat16 = lat.astype(BF16)
            kc = _dot(lat16, wk_ref[...])
            vc = _dot(lat16, wv_ref[...])
            for h in range(MLA_HEADS):
                hl = slice(h * LANE, (h + 1) * LANE)
                kcat_ref[h, start:start + n, :] = (kc[:, hl] + pos).astype(BF16)
                vv_ref[h, start:start + n, :] = (vc[:, hl] + ones_hi).astype(BF16)

    qlat = (_rms(qd_ref[...].astype(F32)) * qng_ref[...]).astype(BF16)
    qc = _dot(qlat, wq_ref[...])
    if has_ctx:
        qsw = _dot(qlat, wqsw_ref[...])
    hg = MLA_HEADS if tq * (seq + past) <= MLA_GROUP_SCORES else 1
    for h0 in range(0, MLA_HEADS, hg):
        heads = range(h0, h0 + hg)
        q_raw = [qc[:, h * LANE:(h + 1) * LANE] for h in heads]
        if has_ctx:
            q_self = [(q_raw[i] * cosq_ref[...] + qsw[:, h * LANE:(h + 1) * LANE] * sinq_ref[...]).astype(BF16)
                      for i, h in enumerate(heads)]
        else:
            q_self = [x.astype(BF16) for x in q_raw]
        s1 = [_dot_nt(q_self[i], kcat_ref[h, 0:seq, :]) for i, h in enumerate(heads)]
        m = [jnp.max(s, axis=-1, keepdims=True) for s in s1]
        if has_ctx:
            s2 = [_dot_nt(q_raw[i].astype(BF16), kcat_ref[h, seq:seq + past, :]) for i, h in enumerate(heads)]
            m = [jnp.maximum(m[i], jnp.max(s2[i], axis=-1, keepdims=True)) for i in range(hg)]
        o = [_dot(jnp.exp((s1[i] - m[i]).astype(BF16)), vv_ref[h, 0:seq, :]) for i, h in enumerate(heads)]
        if has_ctx:
            o = [o[i] + _dot(jnp.exp((s2[i] - m[i]).astype(BF16)), vv_ref[h, seq:seq + past, :])
                 for i, h in enumerate(heads)]
        for i, h in enumerate(heads):
            o_h = o[i] / o[i][:, MLA_V:MLA_V + 1]
            o_ref[:, h * MLA_V:(h + 1) * MLA_V] = o_h[:, 0:MLA_V].astype(o_ref.dtype)


def _mla_call(proj, cache_ckv, cache_kpe, layer, batch, seq, wts, tables):
    has_ctx = cache_ckv is not None
    past = cache_ckv.shape[2] if has_ctx else 0
    tq = min(MLA_TQ, seq)
    nq = seq // tq
    full = lambda cb: pl.BlockSpec((seq, LANE), lambda b, i: (b, cb))
    in_specs = [pl.BlockSpec((tq, MLA_Q_RANK), lambda b, i: (b * nq + i, COL_ML_QD // MLA_Q_RANK)),
                full(COL_ML_KVD // LANE), full(COL_KPE // LANE)]
    args = [proj, proj, proj]
    if has_ctx:
        in_specs += [full(COL_KPE_SW // LANE),
                     pl.BlockSpec((None, None, past, MLA_KV_RANK), lambda b, i: (b, layer, 0, 0)),
                     pl.BlockSpec((None, None, past, MLA_ROPE), lambda b, i: (b, layer, 0, 0))]
        args += [proj, cache_ckv, cache_kpe]
    qn_g, wq_cat, wq_sw, kvn_g, wk_pad, wv = wts
    const = lambda w: pl.BlockSpec(w.shape, lambda b, i, n=w.ndim: (0,) * n)
    if has_ctx:
        cosq, sinq, cosk, sink, epos = tables
        wlist = [qn_g, wq_cat, wq_sw, kvn_g, wk_pad, wv]
        in_specs += [const(w) for w in wlist]
        in_specs += [pl.BlockSpec((tq, LANE), lambda b, i: (i, 0)), pl.BlockSpec((tq, LANE), lambda b, i: (i, 0)),
                     const(cosk), const(sink), const(epos)]
        args += wlist + [cosq, sinq, cosk, sink, epos]
    else:
        wlist = [qn_g, wq_cat, kvn_g, wk_pad, wv]
        in_specs += [const(w) for w in wlist]
        args += wlist
    return pl.pallas_call(
        functools.partial(_mla_kernel, seq=seq, past=past, tq=tq),
        grid=(batch, nq),
        in_specs=in_specs,
        out_specs=[pl.BlockSpec((tq, MLA_VW), lambda b, i: (b * nq + i, 0)),
                   pl.BlockSpec((seq, MLA_KV_RANK), lambda b, i: (b, 0))],
        out_shape=[jax.ShapeDtypeStruct((batch * seq, MLA_VW), BF16),
                   jax.ShapeDtypeStruct((batch * seq, MLA_KV_RANK), F32)],
        scratch_shapes=[pltpu.VMEM((MLA_HEADS, seq + past, LANE), BF16),
                        pltpu.VMEM((MLA_HEADS, seq + past, LANE), BF16)],
        compiler_params=pltpu.CompilerParams(vmem_limit_bytes=VMEM_LIMIT,
                                             dimension_semantics=("arbitrary", "arbitrary")),
        name="mla",
    )(*args)


def _outproj_kernel(x_ref, oa_ref, ob_ref, oc_ref, ga_ref, gb_ref, gc_ref, ma_ref, mb_ref, mc_ref,
                    mod_ref, wa_ref, wb_ref, wc_ref, wo_ref, fg_ref, o_ref, *, final):
    def branch(o_r, g_r, m_r, w_r):
        h = 0.5 * g_r[...]
        act = o_r[...] * (h * (1.0 + jnp.tanh(h)))
        y = _dot(act, w_r[...]).astype(BF16)
        return (0.5 * (1.0 + jnp.tanh(0.5 * m_r[...]))) * y

    y = (branch(oa_ref, ga_ref, ma_ref, wa_ref) + branch(ob_ref, gb_ref, mb_ref, wb_ref)
         + branch(oc_ref, gc_ref, mc_ref, wc_ref))
    y = _dot(y, wo_ref[...])
    gate = mod_ref[...][:, 2 * D_MODEL:3 * D_MODEL]
    hn = x_ref[...] + gate * y
    if final:
        hn = _rms(hn) * fg_ref[...]
    o_ref[...] = hn


def _outproj_call(x2d, o_a, o_b, o_c, proj, mod3, wts, final_g, rows_per_mod, mod_base, final):
    rows = x2d.shape[0]
    tm = OUTPROJ_TM
    tiles_per_mod = rows_per_mod // tm
    row = lambda w, cb=0: pl.BlockSpec((tm, w), lambda i: (i, cb))
    const = lambda w: pl.BlockSpec(w.shape, lambda i, n=w.ndim: (0,) * n)
    in_specs = [row(D_MODEL), row(512), row(512), row(512),
                row(512, COL_GATES // 512), row(512, COL_GATES // 512 + 1), row(512, COL_GATES // 512 + 2),
                row(D_MODEL, COL_MERGE // D_MODEL), row(D_MODEL, COL_MERGE // D_MODEL + 1),
                row(D_MODEL, COL_MERGE // D_MODEL + 2),
                pl.BlockSpec((None, 1, 3 * D_MODEL), lambda i: (mod_base + i // tiles_per_mod, 0, 0))]
    in_specs += [const(w) for w in wts] + [const(final_g)]
    return pl.pallas_call(
        functools.partial(_outproj_kernel, final=final),
        grid=(rows // tm,),
        in_specs=in_specs,
        out_specs=row(D_MODEL),
        out_shape=jax.ShapeDtypeStruct((rows, D_MODEL), F32),
        compiler_params=pltpu.CompilerParams(vmem_limit_bytes=VMEM_LIMIT),
        name="outproj",
    )(x2d, o_a, o_b, o_c, proj, proj, proj, proj, proj, proj, mod3, *wts, final_g)


def _pack_kernel(wt_ref, wa_ref, wb_ref):
    seg = lambda i: wt_ref[_IN_OFF[i]:_IN_OFF[i + 1], :]
    tr = wt_ref.shape[1]
    z = lambda n: jnp.zeros((n, tr), F32)

    def put(ref, col, xt):
        ref[:, col:col + xt.shape[0]] = xt.T.astype(BF16)

    put(wa_ref, COL_RW_R, wt_ref[_IN_OFF[0]:_IN_OFF[5], :])
    put(wa_ref, COL_GL_AD, jnp.concatenate([seg(9), z(LANE - GLA_GATE_RANK)], axis=0))
    put(wa_ref, COL_ML_QD, seg(11))
    put(wa_ref, COL_GL_Q, wt_ref[_IN_OFF[6]:_IN_OFF[9], :])
    put(wa_ref, COL_ML_KVD, seg(12))
    kpe = seg(13)
    q = MLA_ROPE // 4
    kpe_sw = jnp.concatenate([kpe[q:2 * q], kpe[0:q], kpe[3 * q:4 * q], kpe[2 * q:3 * q]], axis=0)
    tail = LANE - KPE_LANE - MLA_ROPE
    put(wa_ref, COL_KPE, jnp.concatenate([z(KPE_LANE), kpe, z(tail)], axis=0))
    put(wa_ref, COL_KPE_SW, jnp.concatenate([z(KPE_LANE), kpe_sw, z(tail)], axis=0))
    wa_ref[:, COL_KPE_SW + LANE:PACK_A] = jnp.zeros((tr, PACK_A - COL_KPE_SW - LANE), BF16)
    put(wb_ref, COL_MERGE, seg(15))
    put(wb_ref, COL_GATES, seg(5))
    put(wb_ref, COL_GATES + RW_WIDTH, seg(10))
    put(wb_ref, COL_GATES + RW_WIDTH + GLA_VW, seg(14))


def _pack_call(w_in):
    tr = 256
    wt = jnp.swapaxes(w_in, 1, 2)
    width = wt.shape[1]
    return pl.pallas_call(
        _pack_kernel,
        grid=(DEPTH, D_MODEL // tr),
        in_specs=[pl.BlockSpec((None, width, tr), lambda l, i: (l, 0, i))],
        out_specs=[pl.BlockSpec((None, tr, PACK_A), lambda l, i: (l, i, 0)),
                   pl.BlockSpec((None, tr, PACK_B), lambda l, i: (l, i, 0))],
        out_shape=[jax.ShapeDtypeStruct((DEPTH, D_MODEL, PACK_A), BF16),
                   jax.ShapeDtypeStruct((DEPTH, D_MODEL, PACK_B), BF16)],
        compiler_params=pltpu.CompilerParams(vmem_limit_bytes=VMEM_LIMIT),
        name="pack",
    )(wt)


def _rope_lane_tables(seq):
    n_freq = MLA_ROPE // 4
    t = np.arange(seq)
    inv = ROPE_THETA ** (-np.arange(n_freq, dtype=np.float64) / n_freq)
    ang = np.stack([(t // GRID_W)[:, None] * inv, (t % GRID_W)[:, None] * inv], axis=1)
    cos = np.repeat(np.cos(ang)[:, :, None, :], 2, axis=2).reshape(seq, MLA_ROPE)
    sin = np.stack([-np.sin(ang), np.sin(ang)], axis=2).reshape(seq, MLA_ROPE)
    return cos, sin


def _mla_tables(seq):
    cos, sin = _rope_lane_tables(seq)
    cosq = np.zeros((seq, LANE), np.float32)
    sinq = np.zeros((seq, LANE), np.float32)
    cosq[:, :KPE_LANE] = 1.0
    cosq[:, KPE_LANE:KPE_LANE + MLA_ROPE] = cos
    sinq[:, KPE_LANE:KPE_LANE + MLA_ROPE] = sin
    cosk = np.zeros((seq, LANE), np.float32)
    cosk[:, KPE_LANE:KPE_LANE + MLA_ROPE] = cos
    epos = np.zeros((MLA_ROPE, LANE), np.float32)
    epos[np.arange(MLA_ROPE), KPE_LANE + np.arange(MLA_ROPE)] = 1.0
    return (jnp.asarray(cosq), jnp.asarray(sinq), jnp.asarray(cosk), jnp.asarray(sinq),
            jnp.asarray(epos, dtype=BF16))


def _pack_mla_weights(qn_g, wq_up, kvn_g, wkv_up):
    scale = (MLA_NOPE + MLA_ROPE) ** -0.5
    wq = wq_up.reshape(MLA_Q_RANK, MLA_HEADS, MLA_NOPE + MLA_ROPE) * scale
    nope, rope = wq[..., :MLA_NOPE], wq[..., MLA_NOPE:]
    q = MLA_ROPE // 4
    rope_sw = jnp.concatenate([rope[..., q:2 * q], rope[..., 0:q], rope[..., 3 * q:4 * q], rope[..., 2 * q:3 * q]], -1)
    tail = jnp.zeros((MLA_Q_RANK, MLA_HEADS, LANE - KPE_LANE - MLA_ROPE), wq.dtype)
    wq_cat = jnp.concatenate([nope, rope, tail], -1).reshape(MLA_Q_RANK, MLA_HEADS * LANE).astype(BF16)
    wq_sw = jnp.concatenate([jnp.zeros_like(nope), rope_sw, tail], -1).reshape(MLA_Q_RANK, MLA_HEADS * LANE).astype(BF16)
    wkv = wkv_up.reshape(MLA_KV_RANK, MLA_HEADS, MLA_NOPE + MLA_V)
    wk = jnp.concatenate([wkv[..., :MLA_NOPE], jnp.zeros((MLA_KV_RANK, MLA_HEADS, LANE - MLA_NOPE), wkv.dtype)], -1)
    wk_pad = wk.reshape(MLA_KV_RANK, MLA_HEADS * LANE).astype(BF16)
    wv = jnp.concatenate([wkv[..., MLA_NOPE:], jnp.zeros((MLA_KV_RANK, MLA_HEADS, LANE - MLA_V), wkv.dtype)], -1)
    wv = wv.reshape(MLA_KV_RANK, MLA_HEADS * LANE).astype(BF16)
    return (qn_g.reshape(1, -1), wq_cat, wq_sw, kvn_g.reshape(1, -1), wk_pad, wv)


def _head_sum_matrix():
    lane = np.arange(LANE)
    return jnp.asarray((lane[:, None] // RW_HEAD == lane[None, :] // RW_HEAD).astype(np.float32), dtype=BF16)


def _trunk(x_prompt, x_sample, c, cache_ckv, cache_kpe, state_rwkv, state_gla, c_ctx,
           norm_g, w_mod, b_mod, w_in, rw_w0, rw_w2, rw_a0, rw_a2, rw_k_k, rw_k_a, rw_r_k,
           rw_ln_g, rw_ln_b, rw_out, gla_a2, gla_ab, gla_norm_g, gla_out,
           mla_qn_g, mla_wq_up, mla_kvn_g, mla_wkv_up, mla_out, w_out, final_g):
    bp, tp, _ = x_prompt.shape
    bs, ts, _ = x_sample.shape
    hp = x_prompt.reshape(bp * tp, D_MODEL)
    hs = x_sample.reshape(bs * ts, D_MODEL)
    cvec8 = jnp.concatenate([c, c_ctx[None, :], jnp.zeros((8 - bs - 1, D_MODEL), F32)], axis=0)
    ctx_row = bs
    hsum = _head_sum_matrix()
    mexp, bmask = _gla_constants()
    mexp = jnp.asarray(mexp, dtype=BF16)
    bmask = jnp.asarray(bmask)
    tables = _mla_tables(ts)
    w_a, w_b = _pack_call(w_in)
    fg = final_g.reshape(1, D_MODEL)
    ckv_l, kpe_l, rw_l, gla_l = [], [], [], []
    for l in range(DEPTH):
        mod3 = _mod_call(cvec8, w_mod, b_mod, l).reshape(8, 1, 3 * D_MODEL)
        ng = norm_g[l].reshape(1, D_MODEL)
        row = lambda a: a.reshape(1, -1)
        zr = jnp.zeros((2, RW_RANK, RW_WIDTH), F32)
        wcat = jnp.concatenate([jnp.concatenate([rw_w2[l], zr], axis=2),
                                jnp.concatenate([zr, rw_a2[l]], axis=2)], axis=1).astype(BF16)
        rw_wts = (wcat, jnp.concatenate([rw_w0[l], rw_a0[l]], axis=1), row(rw_k_k[l]), row(rw_k_a[l]),
                  row(rw_r_k[l]), row(rw_ln_g[l]), row(rw_ln_b[l]), hsum)
        a2p = jnp.concatenate([gla_a2[l], jnp.zeros((2, LANE - GLA_GATE_RANK, GLA_KW), F32)], axis=1).astype(BF16)
        gla_wts = (a2p, gla_ab[l], row(gla_norm_g[l]), mexp, bmask)
        mla_wts = _pack_mla_weights(mla_qn_g[l], mla_wq_up[l], mla_kvn_g[l], mla_wkv_up[l])
        out_wts = (rw_out[l].astype(BF16), gla_out[l].astype(BF16), mla_out[l].astype(BF16), w_out[l].astype(BF16))
        final = l == DEPTH - 1

        proj, proj_b = _inproj_call(hp, mod3, ng, w_a, w_b, l, bp * tp, ctx_row)
        o_a, s_rw = _rwkv_call(proj, None, l, bp, tp, rw_wts)
        o_b, s_gla = _gla_call(proj, None, l, bp, tp, gla_wts)
        o_c, ckv = _mla_call(proj, None, None, l, bp, tp, mla_wts, None)
        ckv_l.append(ckv.reshape(bp, tp, MLA_KV_RANK))
        kpe_l.append(proj[:, COL_KPE + KPE_LANE:COL_KPE + KPE_LANE + MLA_ROPE].astype(F32).reshape(bp, tp, MLA_ROPE))
        rw_l.append(s_rw)
        gla_l.append(s_gla)
        hp = _outproj_call(hp, o_a, o_b, o_c, proj_b, mod3, out_wts, fg, bp * tp, ctx_row, final)

        proj, proj_b = _inproj_call(hs, mod3, ng, w_a, w_b, l, ts, 0)
        o_a, _ = _rwkv_call(proj, state_rwkv, l, bs, ts, rw_wts)
        o_b, _ = _gla_call(proj, state_gla, l, bs, ts, gla_wts)
        o_c, _ = _mla_call(proj, cache_ckv, cache_kpe, l, bs, ts, mla_wts, tables)
        hs = _outproj_call(hs, o_a, o_b, o_c, proj_b, mod3, out_wts, fg, ts, 0, final)

    return (hp.reshape(bp, tp, D_MODEL), hs.reshape(bs, ts, D_MODEL),
            jnp.stack(ckv_l, axis=1), jnp.stack(kpe_l, axis=1),
            jnp.stack(rw_l, axis=1), jnp.stack(gla_l, axis=1))


_trunk_jit = jax.jit(_trunk)


def kernel(x_prompt, x_sample, c, cache_ckv, cache_kpe, state_rwkv, state_gla, c_ctx, norm_g, w_mod, b_mod, w_in, rw_w0, rw_w2, rw_a0, rw_a2, rw_k_k, rw_k_a, rw_r_k, rw_ln_g, rw_ln_b, rw_out, gla_a2, gla_ab, gla_norm_g, gla_out, mla_qn_g, mla_wq_up, mla_kvn_g, mla_wkv_up, mla_out, w_out, final_g):
    return _trunk_jit(x_prompt, x_sample, c, cache_ckv, cache_kpe, state_rwkv, state_gla, c_ctx, norm_g, w_mod, b_mod, w_in, rw_w0, rw_w2, rw_a0, rw_a2, rw_k_k, rw_k_a, rw_r_k, rw_ln_g, rw_ln_b, rw_out, gla_a2, gla_ab, gla_norm_g, gla_out, mla_qn_g, mla_wq_up, mla_kvn_g, mla_wkv_up, mla_out, w_out, final_g)
```

```python
import functools

import numpy as np
import jax
import jax.numpy as jnp
from jax import lax
from jax.experimental import pallas as pl
from jax.experimental.pallas import tpu as pltpu

F32 = jnp.float32
BF16 = jnp.bfloat16

D_MODEL = 1024
DEPTH = 2
GRID_W = 64
NORM_EPS = 1e-6
RW_HEADS = 8
RW_HEAD = 64
RW_WIDTH = RW_HEADS * RW_HEAD
RW_RANK = 64
RW_GN_EPS = 64e-5
GLA_HEADS = 4
GLA_DK = 64
GLA_DV = 128
GLA_KW = GLA_HEADS * GLA_DK
GLA_VW = GLA_HEADS * GLA_DV
GLA_GATE_RANK = 16
GLA_LOGIT_NORM = 16.0
MLA_HEADS = 8
MLA_NOPE = 64
MLA_ROPE = 32
MLA_V = 64
MLA_Q_RANK = 256
MLA_KV_RANK = 128
MLA_VW = MLA_HEADS * MLA_V
ROPE_THETA = 10000.0
N_BRANCH = 3

_IN_SIZES = (RW_WIDTH, RW_WIDTH, RW_WIDTH, RW_RANK, RW_RANK, RW_WIDTH,
             GLA_KW, GLA_KW, GLA_VW, GLA_GATE_RANK, GLA_VW,
             MLA_Q_RANK, MLA_KV_RANK, MLA_ROPE, MLA_VW, N_BRANCH * D_MODEL)
_IN_OFF = tuple(int(v) for v in np.concatenate([[0], np.cumsum(_IN_SIZES)]))

LANE = 128
COL_RW_R = 0
COL_RW_K = 512
COL_RW_V = 1024
COL_RW_WA = 1536
COL_GL_AD = 1664
COL_ML_QD = 1792
COL_GL_Q = 2048
COL_GL_K = 2304
COL_GL_V = 2560
COL_ML_KVD = 3072
COL_KPE = 3200
COL_KPE_SW = 3328
PACK_A = 3584
COL_MERGE = 0
COL_GATES = 3072
PACK_B = 4608
KPE_LANE = MLA_NOPE

CHUNK = 64
INPROJ_TM = 512
INPROJ_SUB = 256
INPROJ_TN = 512
OUTPROJ_TM = 512
MLA_TQ = 1024
MLA_GROUP_SCORES = 256 * 256
VMEM_LIMIT = 48 * 1024 * 1024


def _dot(a, b, prec=None):
    return jnp.dot(a, b, preferred_element_type=F32, precision=prec)


def _dot_nt(a, b, prec=None):
    return lax.dot_general(a, b, (((1,), (1,)), ((), ())), preferred_element_type=F32, precision=prec)


def _dot_tn(a, b, prec=None):
    return lax.dot_general(a, b, (((0,), (0,)), ((), ())), preferred_element_type=F32, precision=prec)


def _split(x):
    hi = x.astype(BF16)
    return hi, (x - hi.astype(F32)).astype(BF16)


def _dot_split(a, b):
    ah, al = _split(a)
    bh, bl = _split(b)
    return _dot(ah, bh) + _dot(al, bh) + _dot(ah, bl)


def _dot_sel(sel16, x):
    xh, xl = _split(x)
    return _dot(sel16, xh) + _dot(sel16, xl)


def _sigmoid(x):
    return 0.5 * jnp.tanh(0.5 * x) + 0.5


def _rms(x, eps=NORM_EPS):
    return x * lax.rsqrt(jnp.mean(x * x, axis=-1, keepdims=True) + eps)


def _mod_kernel(c_ref, w_ref, b_ref, o_ref):
    c = c_ref[...]
    o_ref[...] = _dot_split(c * _sigmoid(c), w_ref[...]) + b_ref[...]


def _mod_call(cvec8, w_mod, b_mod, layer):
    tn = 1024
    return pl.pallas_call(
        _mod_kernel,
        grid=(3 * D_MODEL // tn,),
        in_specs=[pl.BlockSpec((8, D_MODEL), lambda j: (0, 0)),
                  pl.BlockSpec((None, D_MODEL, tn), lambda j: (layer, 0, j)),
                  pl.BlockSpec((None, 1, tn), lambda j: (layer, 0, j))],
        out_specs=pl.BlockSpec((8, tn), lambda j: (0, j)),
        out_shape=jax.ShapeDtypeStruct((8, 3 * D_MODEL), F32),
        compiler_params=pltpu.CompilerParams(vmem_limit_bytes=VMEM_LIMIT),
        name="mod",
    )(cvec8, w_mod, b_mod.reshape(DEPTH, 1, 3 * D_MODEL))


def _inproj_kernel(x_ref, mod_ref, g_ref, wa_ref, wb_ref, oa_ref, ob_ref):
    m = mod_ref[...]
    shift = m[:, 0:D_MODEL]
    scale1 = 1.0 + m[:, D_MODEL:2 * D_MODEL]
    tm = x_ref.shape[0]
    for r0 in range(0, tm, INPROJ_SUB):
        rows = slice(r0, r0 + INPROJ_SUB)
        h = (_rms(x_ref[rows, :]) * g_ref[...] * scale1 + shift).astype(BF16)
        for w_ref, o_ref in ((wa_ref, oa_ref), (wb_ref, ob_ref)):
            for c0 in range(0, w_ref.shape[1], INPROJ_TN):
                cols = slice(c0, c0 + INPROJ_TN)
                o_ref[rows, cols] = _dot(h, w_ref[:, cols]).astype(BF16)


def _inproj_call(x2d, mod3, norm_g, w_a, w_b, layer, rows_per_mod, mod_base):
    rows = x2d.shape[0]
    tm = INPROJ_TM
    tiles_per_mod = rows_per_mod // tm
    resident = lambda w: pl.BlockSpec((None,) + w.shape[1:], lambda i: (layer, 0, 0), pipeline_mode=pl.Buffered(1))
    return pl.pallas_call(
        _inproj_kernel,
        grid=(rows // tm,),
        in_specs=[pl.BlockSpec((tm, D_MODEL), lambda i: (i, 0)),
                  pl.BlockSpec((None, 1, 3 * D_MODEL), lambda i: (mod_base + i // tiles_per_mod, 0, 0)),
                  pl.BlockSpec((1, D_MODEL), lambda i: (0, 0)),
                  resident(w_a), resident(w_b)],
        out_specs=[pl.BlockSpec((tm, PACK_A), lambda i: (i, 0)),
                   pl.BlockSpec((tm, PACK_B), lambda i: (i, 0))],
        out_shape=[jax.ShapeDtypeStruct((rows, PACK_A), BF16),
                   jax.ShapeDtypeStruct((rows, PACK_B), BF16)],
        compiler_params=pltpu.CompilerParams(vmem_limit_bytes=VMEM_LIMIT),
        name="inproj",
    )(x2d, mod3, norm_g, w_a, w_b)


RW_CHUNK = 64
RW_PAIRS = RW_HEADS // 2
RW_BLOCK_ROWS = 256
RW_NEG_DECAY_SCALE = -float(np.exp(-0.5))
RW_UNROLL = 4


def _rwkv_time_mask(c, reverse):
    row = lax.broadcasted_iota(jnp.int32, (2 * c, 4 * c), 0)
    col = lax.broadcasted_iota(jnp.int32, (2 * c, 4 * c), 1)
    t = jnp.where(row >= c, row - c, row)
    s = col & (c - 1)
    earlier = (s > t) if reverse else (s < t)
    return earlier | ((row >= c) & (s == t))


def _rwkv_kernel(*refs, seq, has_s0):
    if has_s0:
        r_ref, k_ref, v_ref, wa_ref, s0_ref = refs[:5]
        rest = refs[5:]
    else:
        r_ref, k_ref, v_ref, wa_ref = refs[:4]
        s0_ref = None
        rest = refs[4:]
    (wcat_ref, wa0_ref, kk_ref, ka_ref, rk_ref, lng_ref, lnb_ref, hsum_ref,
     o_ref, sout_ref, acc_ref, kn_ref, st_ref) = rest
    c = RW_CHUNK
    nc = seq // c
    blk = min(RW_BLOCK_ROWS, seq)
    hsum = hsum_ref[...]
    pair_lanes = [slice(p * LANE, (p + 1) * LANE) for p in range(RW_PAIRS)]

    def pre_step(i, carry):
        rows = pl.ds(pl.multiple_of(i * blk, blk), blk)
        kk0 = [k_ref[rows, pl_].astype(F32) * kk_ref[:, pl_] for pl_ in pair_lanes]
        ss = [_dot((x * x).astype(BF16), hsum) for x in kk0]
        for pl_, x, s in zip(pair_lanes, kk0, ss):
            kn_ref[rows, pl_] = x / jnp.maximum(jnp.sqrt(s), 1e-12)
        return carry

    def post_step(i, carry):
        rows = pl.ds(pl.multiple_of(i * blk, blk), blk)
        o = [acc_ref[0, rows, pl_] + acc_ref[1, rows, pl_] for pl_ in pair_lanes]
        rk = [r_ref[rows, pl_].astype(F32) * k_ref[rows, pl_].astype(F32) * rk_ref[:, pl_] for pl_ in pair_lanes]
        red = [_dot(jnp.concatenate([a, b], axis=0).astype(BF16), hsum) for a, b in zip(o, rk)]
        dev = [a - s[0:blk] * (1.0 / RW_HEAD) for a, s in zip(o, red)]
        var = [_dot((x * x).astype(BF16), hsum) * (1.0 / RW_HEAD) for x in dev]
        for pl_, x, vr, s in zip(pair_lanes, dev, var, red):
            normed = x * lax.rsqrt(vr + RW_GN_EPS) * lng_ref[:, pl_] + lnb_ref[:, pl_]
            o_ref[rows, pl_] = (normed + s[blk:2 * blk] * v_ref[rows, pl_]).astype(o_ref.dtype)
        return carry

    even = lax.broadcasted_iota(jnp.int32, (1, LANE), 1) < RW_HEAD
    row = lax.broadcasted_iota(jnp.int32, (c, c), 0)
    col = lax.broadcasted_iota(jnp.int32, (c, c), 1)
    incl16 = [jnp.where(col <= row, 1.0, 0.0).astype(BF16), jnp.where(col >= row, 1.0, 0.0).astype(BF16)]
    tmask = [_rwkv_time_mask(c, False), _rwkv_time_mask(c, True)]
    r2 = lax.broadcasted_iota(jnp.int32, (LANE, LANE), 0)
    c2 = lax.broadcasted_iota(jnp.int32, (LANE, LANE), 1)
    same_head = (r2 < RW_HEAD) == (c2 < RW_HEAD)
    p_lanes = ((lax.broadcasted_iota(jnp.int32, (2 * c, LANE), 0) < c)
               == (lax.broadcasted_iota(jnp.int32, (2 * c, LANE), 1) < RW_HEAD))
    lanes = [slice(p * LANE, (p + 1) * LANE) for p in range(RW_PAIRS)]
    probs = [(d, p) for d in range(2) for p in range(RW_PAIRS)]

    def chunk_operands(d, ci):
        rows = pl.ds(pl.multiple_of(ci * c, c), c)
        r = r_ref[rows, :].astype(F32)
        k = k_ref[rows, :].astype(F32)
        wa = wa_ref[rows, :].astype(F32)
        kk = kn_ref[rows, :]
        lora = jnp.where(even, jnp.tanh(wa), wa)
        pre = _dot(lora.astype(BF16), wcat_ref[d]) + wa0_ref[d:d + 1, :]
        lw = RW_NEG_DECAY_SCALE * _sigmoid(pre[:, 0:RW_WIDTH])
        a = _sigmoid(pre[:, RW_WIDTH:2 * RW_WIDTH])
        kd = k * (1.0 + (a - 1.0) * ka_ref[...])
        bv = kk * a
        gam = _dot_sel(incl16[d], lw)
        last = 0 if d == 1 else c - 1
        gtot = gam[last:last + 1, :]
        gref = gam[c // 2:c // 2 + 1, :]
        e_in = jnp.exp(gam - gref)
        e_out = jnp.exp(gref - gam)
        e_end = e_out * jnp.exp(gtot - gref)
        lhs = jnp.concatenate([-kk * jnp.exp(gam - lw - gref), r * e_in], axis=0).astype(BF16)
        return dict(rows=rows, lhs=lhs, b16=(bv * e_out).astype(BF16), k16=(kd * e_out).astype(BF16),
                    be16=(bv * e_end).astype(BF16), ke16=(kd * e_end).astype(BF16),
                    v16=v_ref[rows, :].astype(BF16), dtot=jnp.exp(gtot), eref=jnp.exp(gref))

    def main_step(i, carry):
        steps = [i * RW_UNROLL + s for s in range(RW_UNROLL)]
        prepared = [[chunk_operands(0, t), chunk_operands(1, nc - 1 - t)] for t in steps]
        for ops in prepared:
            chain_step(ops)
        return carry

    def chain_step(ops):
        n = range(len(probs))
        pick = lambda name: [ops[d][name][:, lanes[p]] for d, p in probs]
        lhs, b, k, be, ke, v = (pick(s) for s in ("lhs", "b16", "k16", "be16", "ke16", "v16"))
        st = [st_ref[d, p] for d, p in probs]
        first = lambda x: jnp.where(even, x, jnp.zeros_like(x))
        second = lambda x: jnp.where(even, jnp.zeros_like(x), x)
        v_e = [first(x) for x in v]
        v_o = [second(x) for x in v]
        aa = [jnp.where(tmask[d], _dot_nt(lhs[j], jnp.concatenate(
            [first(k[j]), second(k[j]), second(b[j]), first(b[j])], axis=0)), 0.0) for j, (d, p) in enumerate(probs)]
        sp = [_dot_nt(lhs[j], (st[j] * ops[d]["eref"][:, lanes[p]]).astype(BF16))
              for j, (d, p) in enumerate(probs)]
        aa16 = [x.astype(BF16) for x in aa]
        rhs_u = [_dot(aa16[j][0:c, 0:LANE], jnp.concatenate([v_e[j], v_o[j]], axis=0)) + sp[j][0:c] for j in n]
        x = [jnp.concatenate([jnp.where(even, aa[j][0:c, LANE:2 * LANE], rhs_u[j]),
                              jnp.where(even, rhs_u[j], aa[j][0:c, LANE:2 * LANE])], axis=0) for j in n]
        covered = 1
        while covered < c:
            x16 = [xx.astype(BF16) for xx in x]
            x = [_dot(jnp.where(p_lanes, x16[j], jnp.zeros_like(x16[j])), x16[j])
                 + jnp.where(p_lanes, 0.0, x[j]) for j in n]
            covered *= 2
        u16 = [jnp.where(even, x[j][c:2 * c], x[j][0:c]).astype(BF16) for j in n]
        for j, (d, p) in enumerate(probs):
            uv_rows = jnp.concatenate([v_e[j], v_o[j], second(u16[j]), first(u16[j])], axis=0)
            acc_ref[d, ops[d]["rows"], lanes[p]] = sp[j][c:2 * c] + _dot(aa16[j][c:2 * c], uv_rows)
            upd = _dot_tn(jnp.concatenate([u16[j], v[j]], axis=0), jnp.concatenate([be[j], ke[j]], axis=0))
            st_ref[d, p] = st[j] * ops[d]["dtot"][:, lanes[p]] + jnp.where(same_head, upd, 0.0)

    st_ref[...] = jnp.zeros_like(st_ref)
    if has_s0:
        for d in range(2):
            for h in range(RW_HEADS):
                off = (h % 2) * RW_HEAD
                st_ref[d, h // 2, off:off + RW_HEAD, off:off + RW_HEAD] = s0_ref[d, h]
    lax.fori_loop(0, seq // blk, pre_step, 0)
    lax.fori_loop(0, nc // RW_UNROLL, main_step, 0)
    for d in range(2):
        for h in range(RW_HEADS):
            off = (h % 2) * RW_HEAD
            sout_ref[d, h] = st_ref[d, h // 2, off:off + RW_HEAD, off:off + RW_HEAD]
    lax.fori_loop(0, seq // blk, post_step, 0)


def _rwkv_call(proj, s0, layer, batch, seq, wts):
    has_s0 = s0 is not None
    blk = lambda w, cb: pl.BlockSpec((seq, w), lambda b: (b, cb))
    in_specs = [blk(512, COL_RW_R // 512), blk(512, COL_RW_K // 512), blk(512, COL_RW_V // 512),
                blk(LANE, COL_RW_WA // LANE)]
    args = [proj, proj, proj, proj]
    if has_s0:
        in_specs.append(pl.BlockSpec((None, None, 2, RW_HEADS, RW_HEAD, RW_HEAD),
                                     lambda b: (b, layer, 0, 0, 0, 0)))
        args.append(s0)
    for w in wts:
        in_specs.append(pl.BlockSpec(w.shape, lambda b, n=w.ndim: (0,) * n))
        args.append(w)
    return pl.pallas_call(
        functools.partial(_rwkv_kernel, seq=seq, has_s0=has_s0),
        grid=(batch,),
        in_specs=in_specs,
        out_specs=[pl.BlockSpec((seq, RW_WIDTH), lambda b: (b, 0)),
                   pl.BlockSpec((None, 2, RW_HEADS, RW_HEAD, RW_HEAD), lambda b: (b, 0, 0, 0, 0))],
        out_shape=[jax.ShapeDtypeStruct((batch * seq, RW_WIDTH), BF16),
                   jax.ShapeDtypeStruct((batch, 2, RW_HEADS, RW_HEAD, RW_HEAD), F32)],
        scratch_shapes=[pltpu.VMEM((2, seq, RW_WIDTH), F32),
                        pltpu.VMEM((seq, RW_WIDTH), F32),
                        pltpu.VMEM((2, RW_PAIRS, LANE, LANE), F32)],
        compiler_params=pltpu.CompilerParams(vmem_limit_bytes=VMEM_LIMIT),
        name="rwkv",
    )(*args)


GLA_LEVELS = (32, 16, 8, 4, 2, 1)
GLA_GROUP = 4


def _gla_constants():
    c = CHUNK
    nl = len(GLA_LEVELS)
    mexp = np.zeros((2, (nl + 1) * c, c), np.float32)
    bmask = np.zeros((2, nl + 1, c, c), np.float32)
    for d in range(2):
        pos = np.arange(c) if d == 0 else c - 1 - np.arange(c)
        pt = pos[:, None]
        pj = pos[None, :]
        mexp[d, nl * c:] = (pj <= pt)
        for li, m in enumerate(GLA_LEVELS):
            mid = (pos // (2 * m)) * (2 * m) + m
            second = pos >= mid
            mq = (pj >= mid[:, None]) & (pj <= pt) & second[:, None]
            mk = (pj > pt) & (pj <= mid[:, None] - 1) & (~second)[:, None]
            mexp[d, li * c:(li + 1) * c] = mq | mk
            same = (pos[:, None] // (2 * m)) == (pos[None, :] // (2 * m))
            bmask[d, li] = same & second[:, None] & (~second)[None, :]
        bmask[d, nl] = np.eye(c)
    return mexp, np.concatenate([bmask, bmask], axis=-1)


def _gla_kernel(*refs, seq, has_s0):
    if has_s0:
        q_ref, k_ref, v_ref, ad_ref, s0_ref = refs[:5]
        rest = refs[5:]
    else:
        q_ref, k_ref, v_ref, ad_ref = refs[:4]
        s0_ref = None
        rest = refs[4:]
    (a2_ref, ab_ref, ng_ref, mexp_ref, bmask_ref,
     o_ref, sout_ref, acc_ref, st_ref) = rest
    c = CHUNK
    nc = seq // c
    nl = len(GLA_LEVELS)

    def chunk_operands(d, ci):
        rows = pl.ds(pl.multiple_of(ci * c, c), c)
        q = q_ref[rows, :].astype(F32) * (GLA_DK ** -0.5)
        k = k_ref[rows, :].astype(F32)
        v = v_ref[rows, :]
        x = _dot(ad_ref[rows, :].astype(BF16), a2_ref[d]) + ab_ref[d:d + 1, :]
        g = (jnp.minimum(x, 0.0) - jnp.log1p(jnp.exp(-jnp.abs(x)))) * (1.0 / GLA_LOGIT_NORM)
        g_hi, g_lo = _split(g)
        sums = _dot(mexp_ref[d], g_hi)
        ex = jnp.exp(sums[0:nl * c])
        b = sums[nl * c:(nl + 1) * c] + _dot(mexp_ref[d, nl * c:(nl + 1) * c, :], g_lo)
        last = 0 if d == 1 else c - 1
        blast = b[last:last + 1, :]
        qb = (q * jnp.exp(b)).astype(BF16)
        kdec = (k * jnp.exp(blast - b)).astype(BF16)
        dtot = jnp.exp(blast)
        qs = [(q * ex[li * c:(li + 1) * c]).astype(BF16) for li in range(nl)] + [q.astype(BF16)]
        ks = [(k * ex[li * c:(li + 1) * c]).astype(BF16) for li in range(nl)] + [k.astype(BF16)]
        return dict(rows=rows, qs=qs, ks=ks, qb=qb, kdec=kdec, dtot=dtot, v16=v.astype(BF16))

    even = lax.broadcasted_iota(jnp.int32, (1, LANE), 1) < GLA_DK
    r2 = lax.broadcasted_iota(jnp.int32, (2 * GLA_DV, LANE), 0)
    c2 = lax.broadcasted_iota(jnp.int32, (2 * GLA_DV, LANE), 1)
    same_head = (r2 < GLA_DV) == (c2 < GLA_DK)
    zv = jnp.zeros((c, GLA_DV), BF16)
    npair = GLA_HEADS // 2
    kls = [slice(p * LANE, (p + 1) * LANE) for p in range(npair)]
    vls = [slice(p * 2 * GLA_DV, (p + 1) * 2 * GLA_DV) for p in range(npair)]
    probs = [(d, p) for d in range(2) for p in range(npair)]
    first = lambda x: jnp.where(even, x, jnp.zeros_like(x))
    second = lambda x: jnp.where(even, jnp.zeros_like(x), x)

    group = min(GLA_GROUP, nc)

    def main_step(i, carry):
        ops = [[chunk_operands(0, i * group + g) for g in range(group)],
               [chunk_operands(1, nc - 1 - (i * group + g)) for g in range(group)]]
        allp = [(d, p, g) for d, p in probs for g in range(group)]
        lvl = [[_dot_nt(ops[d][g]["qs"][li][:, kls[p]],
                        jnp.concatenate([first(ops[d][g]["ks"][li][:, kls[p]]),
                                         second(ops[d][g]["ks"][li][:, kls[p]])], axis=0))
                for d, p, g in allp] for li in range(nl + 1)]
        att = [sum(bmask_ref[d, li] * lvl[li][j] for li in range(nl + 1)).astype(BF16)
               for j, (d, p, g) in enumerate(allp)]
        upd = [jnp.where(same_head, _dot_tn(ops[d][g]["v16"][:, vls[p]], ops[d][g]["kdec"][:, kls[p]]), 0.0)
               for d, p, g in allp]
        states = []
        for d, p in probs:
            st = st_ref[d, p]
            for g in range(group):
                states.append(st)
                st = st * ops[d][g]["dtot"][:, kls[p]] + upd[len(states) - 1]
            st_ref[d, p] = st
        for j, (d, p, g) in enumerate(allp):
            v_p = ops[d][g]["v16"][:, vls[p]]
            v_bd = jnp.concatenate([jnp.concatenate([v_p[:, 0:GLA_DV], zv], axis=1),
                                    jnp.concatenate([zv, v_p[:, GLA_DV:2 * GLA_DV]], axis=1)], axis=0)
            inter = _dot_nt(ops[d][g]["qb"][:, kls[p]], states[j].astype(BF16))
            acc_ref[d, ops[d][g]["rows"], vls[p]] = inter + _dot(att[j], v_bd)
        return carry

    blk = min(RW_BLOCK_ROWS, seq)

    def post_step(i, carry):
        rows = pl.ds(pl.multiple_of(i * blk, blk), blk)
        for h in range(GLA_HEADS):
            sv = slice(h * GLA_DV, (h + 1) * GLA_DV)
            o_ref[rows, sv] = (_rms(acc_ref[0, rows, sv] + acc_ref[1, rows, sv]) * ng_ref[...]).astype(o_ref.dtype)
        return carry

    st_ref[...] = jnp.zeros_like(st_ref)
    if has_s0:
        for d in range(2):
            for h in range(GLA_HEADS):
                ro, co = (h % 2) * GLA_DV, (h % 2) * GLA_DK
                st_ref[d, h // 2, ro:ro + GLA_DV, co:co + GLA_DK] = s0_ref[d, h].T
    lax.fori_loop(0, nc // group, main_step, 0)
    for d in range(2):
        for h in range(GLA_HEADS):
            ro, co = (h % 2) * GLA_DV, (h % 2) * GLA_DK
            sout_ref[d, h] = st_ref[d, h // 2, ro:ro + GLA_DV, co:co + GLA_DK].T
    lax.fori_loop(0, seq // blk, post_step, 0)


def _gla_call(proj, s0, layer, batch, seq, wts):
    has_s0 = s0 is not None
    blk = lambda w, cb: pl.BlockSpec((seq, w), lambda b: (b, cb))
    in_specs = [blk(GLA_KW, COL_GL_Q // GLA_KW), blk(GLA_KW, COL_GL_K // GLA_KW),
                blk(GLA_VW, COL_GL_V // GLA_VW), blk(LANE, COL_GL_AD // LANE)]
    args = [proj, proj, proj, proj]
    if has_s0:
        in_specs.append(pl.BlockSpec((None, None, 2, GLA_HEADS, GLA_DK, GLA_DV),
                                     lambda b: (b, layer, 0, 0, 0, 0)))
        args.append(s0)
    for w in wts:
        in_specs.append(pl.BlockSpec(w.shape, lambda b, n=w.ndim: (0,) * n))
        args.append(w)
    return pl.pallas_call(
        functools.partial(_gla_kernel, seq=seq, has_s0=has_s0),
        grid=(batch,),
        in_specs=in_specs,
        out_specs=[pl.BlockSpec((seq, GLA_VW), lambda b: (b, 0)),
                   pl.BlockSpec((None, 2, GLA_HEADS, GLA_DK, GLA_DV), lambda b: (b, 0, 0, 0, 0))],
        out_shape=[jax.ShapeDtypeStruct((batch * seq, GLA_VW), BF16),
                   jax.ShapeDtypeStruct((batch, 2, GLA_HEADS, GLA_DK, GLA_DV), F32)],
        scratch_shapes=[pltpu.VMEM((2, seq, GLA_VW), F32),
                        pltpu.VMEM((2, GLA_HEADS // 2, 2 * GLA_DV, LANE), F32)],
        compiler_params=pltpu.CompilerParams(vmem_limit_bytes=VMEM_LIMIT),
        name="gla",
    )(*args)


def _mla_kernel(*refs, seq, past, tq):
    has_ctx = past > 0
    if has_ctx:
        (qd_ref, kvd_ref, kpe_ref, kpesw_ref, cckv_ref, ckpe_ref,
         qng_ref, wq_ref, wqsw_ref, kvng_ref, wk_ref, wv_ref,
         cosq_ref, sinq_ref, cosk_ref, sink_ref, epos_ref,
         o_ref, ckv_ref, kcat_ref, vv_ref) = refs
    else:
        (qd_ref, kvd_ref, kpe_ref,
         qng_ref, wq_ref, kvng_ref, wk_ref, wv_ref,
         o_ref, ckv_ref, kcat_ref, vv_ref) = refs

    ones_hi = jnp.where(lax.broadcasted_iota(jnp.int32, (1, LANE), 1) >= MLA_V, 1.0, 0.0)

    @pl.when(pl.program_id(1) == 0)
    def _():
        ckv = _rms(kvd_ref[...].astype(F32)) * kvng_ref[...]
        ckv_ref[...] = ckv
        if has_ctx:
            kpos = kpe_ref[...] * cosk_ref[...] + kpesw_ref[...] * sink_ref[...]
        else:
            kpos = kpe_ref[...]
        segs = [(0, seq, ckv, kpos)]
        if has_ctx:
            cpos = _dot(ckpe_ref[...].astype(BF16), epos_ref[...])
            segs.append((seq, past, cckv_ref[...], cpos))
        for start, n, lat, pos in segs:
            lat16 = lat.astype(BF16)
            kc = _dot(lat16, wk_ref[...])
            vc = _dot(lat16, wv_ref[...])
            for h in range(MLA_HEADS):
                hl = slice(h * LANE, (h + 1) * LANE)
                kcat_ref[h, start:start + n, :] = (kc[:, hl] + pos).astype(BF16)
                vv_ref[h, start:start + n, :] = (vc[:, hl] + ones_hi).astype(BF16)

    qlat = (_rms(qd_ref[...].astype(F32)) * qng_ref[...]).astype(BF16)
    qc = _dot(qlat, wq_ref[...])
    if has_ctx:
        qsw = _dot(qlat, wqsw_ref[...])
    hg = MLA_HEADS if tq * (seq + past) <= MLA_GROUP_SCORES else 1
    for h0 in range(0, MLA_HEADS, hg):
        heads = range(h0, h0 + hg)
        q_raw = [qc[:, h * LANE:(h + 1) * LANE] for h in heads]
        if has_ctx:
            q_self = [(q_raw[i] * cosq_ref[...] + qsw[:, h * LANE:(h + 1) * LANE] * sinq_ref[...]).astype(BF16)
                      for i, h in enumerate(heads)]
        else:
            q_self = [x.astype(BF16) for x in q_raw]
        s1 = [_dot_nt(q_self[i], kcat_ref[h, 0:seq, :]) for i, h in enumerate(heads)]
        m = [jnp.max(s, axis=-1, keepdims=True) for s in s1]
        if has_ctx:
            s2 = [_dot_nt(q_raw[i].astype(BF16), kcat_ref[h, seq:seq + past, :]) for i, h in enumerate(heads)]
            m = [jnp.maximum(m[i], jnp.max(s2[i], axis=-1, keepdims=True)) for i in range(hg)]
        o = [_dot(jnp.exp((s1[i] - m[i]).astype(BF16)), vv_ref[h, 0:seq, :]) for i, h in enumerate(heads)]
        if has_ctx:
            o = [o[i] + _dot(jnp.exp((s2[i] - m[i]).astype(BF16)), vv_ref[h, seq:seq + past, :])
                 for i, h in enumerate(heads)]
        for i, h in enumerate(heads):
            o_h = o[i] / o[i][:, MLA_V:MLA_V + 1]
            o_ref[:, h * MLA_V:(h + 1) * MLA_V] = o_h[:, 0:MLA_V].astype(o_ref.dtype)


def _mla_call(proj, cache_ckv, cache_kpe, layer, batch, seq, wts, tables):
    has_ctx = cache_ckv is not None
    past = cache_ckv.shape[2] if has_ctx else 0
    tq = min(MLA_TQ, seq)
    nq = seq // tq
    full = lambda cb: pl.BlockSpec((seq, LANE), lambda b, i: (b, cb))
    in_specs = [pl.BlockSpec((tq, MLA_Q_RANK), lambda b, i: (b * nq + i, COL_ML_QD // MLA_Q_RANK)),
                full(COL_ML_KVD // LANE), full(COL_KPE // LANE)]
    args = [proj, proj, proj]
    if has_ctx:
        in_specs += [full(COL_KPE_SW // LANE),
                     pl.BlockSpec((None, None, past, MLA_KV_RANK), lambda b, i: (b, layer, 0, 0)),
                     pl.BlockSpec((None, None, past, MLA_ROPE), lambda b, i: (b, layer, 0, 0))]
        args += [proj, cache_ckv, cache_kpe]
    qn_g, wq_cat, wq_sw, kvn_g, wk_pad, wv = wts
    const = lambda w: pl.BlockSpec(w.shape, lambda b, i, n=w.ndim: (0,) * n)
    if has_ctx:
        cosq, sinq, cosk, sink, epos = tables
        wlist = [qn_g, wq_cat, wq_sw, kvn_g, wk_pad, wv]
        in_specs += [const(w) for w in wlist]
        in_specs += [pl.BlockSpec((tq, LANE), lambda b, i: (i, 0)), pl.BlockSpec((tq, LANE), lambda b, i: (i, 0)),
                     const(cosk), const(sink), const(epos)]
        args += wlist + [cosq, sinq, cosk, sink, epos]
    else:
        wlist = [qn_g, wq_cat, kvn_g, wk_pad, wv]
        in_specs += [const(w) for w in wlist]
        args += wlist
    return pl.pallas_call(
        functools.partial(_mla_kernel, seq=seq, past=past, tq=tq),
        grid=(batch, nq),
        in_specs=in_specs,
        out_specs=[pl.BlockSpec((tq, MLA_VW), lambda b, i: (b * nq + i, 0)),
                   pl.BlockSpec((seq, MLA_KV_RANK), lambda b, i: (b, 0))],
        out_shape=[jax.ShapeDtypeStruct((batch * seq, MLA_VW), BF16),
                   jax.ShapeDtypeStruct((batch * seq, MLA_KV_RANK), F32)],
        scratch_shapes=[pltpu.VMEM((MLA_HEADS, seq + past, LANE), BF16),
                        pltpu.VMEM((MLA_HEADS, seq + past, LANE), BF16)],
        compiler_params=pltpu.CompilerParams(vmem_limit_bytes=VMEM_LIMIT,
                                             dimension_semantics=("arbitrary", "arbitrary")),
        name="mla",
    )(*args)


def _outproj_kernel(x_ref, oa_ref, ob_ref, oc_ref, ga_ref, gb_ref, gc_ref, ma_ref, mb_ref, mc_ref,
                    mod_ref, wa_ref, wb_ref, wc_ref, wo_ref, fg_ref, o_ref, *, final):
    def branch(o_r, g_r, m_r, w_r):
        h = 0.5 * g_r[...]
        act = o_r[...] * (h * (1.0 + jnp.tanh(h)))
        y = _dot(act, w_r[...]).astype(BF16)
        return (0.5 * (1.0 + jnp.tanh(0.5 * m_r[...]))) * y

    y = (branch(oa_ref, ga_ref, ma_ref, wa_ref) + branch(ob_ref, gb_ref, mb_ref, wb_ref)
         + branch(oc_ref, gc_ref, mc_ref, wc_ref))
    y = _dot(y, wo_ref[...])
    gate = mod_ref[...][:, 2 * D_MODEL:3 * D_MODEL]
    hn = x_ref[...] + gate * y
    if final:
        hn = _rms(hn) * fg_ref[...]
    o_ref[...] = hn


def _outproj_call(x2d, o_a, o_b, o_c, proj, mod3, wts, final_g, rows_per_mod, mod_base, final):
    rows = x2d.shape[0]
    tm = OUTPROJ_TM
    tiles_per_mod = rows_per_mod // tm
    row = lambda w, cb=0: pl.BlockSpec((tm, w), lambda i: (i, cb))
    const = lambda w: pl.BlockSpec(w.shape, lambda i, n=w.ndim: (0,) * n)
    in_specs = [row(D_MODEL), row(512), row(512), row(512),
                row(512, COL_GATES // 512), row(512, COL_GATES // 512 + 1), row(512, COL_GATES // 512 + 2),
                row(D_MODEL, COL_MERGE // D_MODEL), row(D_MODEL, COL_MERGE // D_MODEL + 1),
                row(D_MODEL, COL_MERGE // D_MODEL + 2),
                pl.BlockSpec((None, 1, 3 * D_MODEL), lambda i: (mod_base + i // tiles_per_mod, 0, 0))]
    in_specs += [const(w) for w in wts] + [const(final_g)]
    return pl.pallas_call(
        functools.partial(_outproj_kernel, final=final),
        grid=(rows // tm,),
        in_specs=in_specs,
        out_specs=row(D_MODEL),
        out_shape=jax.ShapeDtypeStruct((rows, D_MODEL), F32),
        compiler_params=pltpu.CompilerParams(vmem_limit_bytes=VMEM_LIMIT),
        name="outproj",
    )(x2d, o_a, o_b, o_c, proj, proj, proj, proj, proj, proj, mod3, *wts, final_g)


def _pack_kernel(wt_ref, wa_ref, wb_ref):
    seg = lambda i: wt_ref[_IN_OFF[i]:_IN_OFF[i + 1], :]
    tr = wt_ref.shape[1]
    z = lambda n: jnp.zeros((n, tr), F32)

    def put(ref, col, xt):
        ref[:, col:col + xt.shape[0]] = xt.T.astype(BF16)

    put(wa_ref, COL_RW_R, wt_ref[_IN_OFF[0]:_IN_OFF[5], :])
    put(wa_ref, COL_GL_AD, jnp.concatenate([seg(9), z(LANE - GLA_GATE_RANK)], axis=0))
    put(wa_ref, COL_ML_QD, seg(11))
    put(wa_ref, COL_GL_Q, wt_ref[_IN_OFF[6]:_IN_OFF[9], :])
    put(wa_ref, COL_ML_KVD, seg(12))
    kpe = seg(13)
    q = MLA_ROPE // 4
    kpe_sw = jnp.concatenate([kpe[q:2 * q], kpe[0:q], kpe[3 * q:4 * q], kpe[2 * q:3 * q]], axis=0)
    tail = LANE - KPE_LANE - MLA_ROPE
    put(wa_ref, COL_KPE, jnp.concatenate([z(KPE_LANE), kpe, z(tail)], axis=0))
    put(wa_ref, COL_KPE_SW, jnp.concatenate([z(KPE_LANE), kpe_sw, z(tail)], axis=0))
    wa_ref[:, COL_KPE_SW + LANE:PACK_A] = jnp.zeros((tr, PACK_A - COL_KPE_SW - LANE), BF16)
    put(wb_ref, COL_MERGE, seg(15))
    put(wb_ref, COL_GATES, seg(5))
    put(wb_ref, COL_GATES + RW_WIDTH, seg(10))
    put(wb_ref, COL_GATES + RW_WIDTH + GLA_VW, seg(14))


def _pack_call(w_in):
    tr = 256
    wt = jnp.swapaxes(w_in, 1, 2)
    width = wt.shape[1]
    return pl.pallas_call(
        _pack_kernel,
        grid=(DEPTH, D_MODEL // tr),
        in_specs=[pl.BlockSpec((None, width, tr), lambda l, i: (l, 0, i))],
        out_specs=[pl.BlockSpec((None, tr, PACK_A), lambda l, i: (l, i, 0)),
                   pl.BlockSpec((None, tr, PACK_B), lambda l, i: (l, i, 0))],
        out_shape=[jax.ShapeDtypeStruct((DEPTH, D_MODEL, PACK_A), BF16),
                   jax.ShapeDtypeStruct((DEPTH, D_MODEL, PACK_B), BF16)],
        compiler_params=pltpu.CompilerParams(vmem_limit_bytes=VMEM_LIMIT),
        name="pack",
    )(wt)


def _rope_lane_tables(seq):
    n_freq = MLA_ROPE // 4
    t = np.arange(seq)
    inv = ROPE_THETA ** (-np.arange(n_freq, dtype=np.float64) / n_freq)
    ang = np.stack([(t // GRID_W)[:, None] * inv, (t % GRID_W)[:, None] * inv], axis=1)
    cos = np.repeat(np.cos(ang)[:, :, None, :], 2, axis=2).reshape(seq, MLA_ROPE)
    sin = np.stack([-np.sin(ang), np.sin(ang)], axis=2).reshape(seq, MLA_ROPE)
    return cos, sin


def _mla_tables(seq):
    cos, sin = _rope_lane_tables(seq)
    cosq = np.zeros((seq, LANE), np.float32)
    sinq = np.zeros((seq, LANE), np.float32)
    cosq[:, :KPE_LANE] = 1.0
    cosq[:, KPE_LANE:KPE_LANE + MLA_ROPE] = cos
    sinq[:, KPE_LANE:KPE_LANE + MLA_ROPE] = sin
    cosk = np.zeros((seq, LANE), np.float32)
    cosk[:, KPE_LANE:KPE_LANE + MLA_ROPE] = cos
    epos = np.zeros((MLA_ROPE, LANE), np.float32)
    epos[np.arange(MLA_ROPE), KPE_LANE + np.arange(MLA_ROPE)] = 1.0
    return (jnp.asarray(cosq), jnp.asarray(sinq), jnp.asarray(cosk), jnp.asarray(sinq),
            jnp.asarray(epos, dtype=BF16))


def _pack_mla_weights(qn_g, wq_up, kvn_g, wkv_up):
    scale = (MLA_NOPE + MLA_ROPE) ** -0.5
    wq = wq_up.reshape(MLA_Q_RANK, MLA_HEADS, MLA_NOPE + MLA_ROPE) * scale
    nope, rope = wq[..., :MLA_NOPE], wq[..., MLA_NOPE:]
    q = MLA_ROPE // 4
    rope_sw = jnp.concatenate([rope[..., q:2 * q], rope[..., 0:q], rope[..., 3 * q:4 * q], rope[..., 2 * q:3 * q]], -1)
    tail = jnp.zeros((MLA_Q_RANK, MLA_HEADS, LANE - KPE_LANE - MLA_ROPE), wq.dtype)
    wq_cat = jnp.concatenate([nope, rope, tail], -1).reshape(MLA_Q_RANK, MLA_HEADS * LANE).astype(BF16)
    wq_sw = jnp.concatenate([jnp.zeros_like(nope), rope_sw, tail], -1).reshape(MLA_Q_RANK, MLA_HEADS * LANE).astype(BF16)
    wkv = wkv_up.reshape(MLA_KV_RANK, MLA_HEADS, MLA_NOPE + MLA_V)
    wk = jnp.concatenate([wkv[..., :MLA_NOPE], jnp.zeros((MLA_KV_RANK, MLA_HEADS, LANE - MLA_NOPE), wkv.dtype)], -1)
    wk_pad = wk.reshape(MLA_KV_RANK, MLA_HEADS * LANE).astype(BF16)
    wv = jnp.concatenate([wkv[..., MLA_NOPE:], jnp.zeros((MLA_KV_RANK, MLA_HEADS, LANE - MLA_V), wkv.dtype)], -1)
    wv = wv.reshape(MLA_KV_RANK, MLA_HEADS * LANE).astype(BF16)
    return (qn_g.reshape(1, -1), wq_cat, wq_sw, kvn_g.reshape(1, -1), wk_pad, wv)


def _head_sum_matrix():
    lane = np.arange(LANE)
    return jnp.asarray((lane[:, None] // RW_HEAD == lane[None, :] // RW_HEAD).astype(np.float32), dtype=BF16)


def _trunk(x_prompt, x_sample, c, cache_ckv, cache_kpe, state_rwkv, state_gla, c_ctx,
           norm_g, w_mod, b_mod, w_in, rw_w0, rw_w2, rw_a0, rw_a2, rw_k_k, rw_k_a, rw_r_k,
           rw_ln_g, rw_ln_b, rw_out, gla_a2, gla_ab, gla_norm_g, gla_out,
           mla_qn_g, mla_wq_up, mla_kvn_g, mla_wkv_up, mla_out, w_out, final_g):
    bp, tp, _ = x_prompt.shape
    bs, ts, _ = x_sample.shape
    hp = x_prompt.reshape(bp * tp, D_MODEL)
    hs = x_sample.reshape(bs * ts, D_MODEL)
    cvec8 = jnp.concatenate([c, c_ctx[None, :], jnp.zeros((8 - bs - 1, D_MODEL), F32)], axis=0)
    ctx_row = bs
    hsum = _head_sum_matrix()
    mexp, bmask = _gla_constants()
    mexp = jnp.asarray(mexp, dtype=BF16)
    bmask = jnp.asarray(bmask)
    tables = _mla_tables(ts)
    w_a, w_b = _pack_call(w_in)
    fg = final_g.reshape(1, D_MODEL)
    ckv_l, kpe_l, rw_l, gla_l = [], [], [], []
    for l in range(DEPTH):
        mod3 = _mod_call(cvec8, w_mod, b_mod, l).reshape(8, 1, 3 * D_MODEL)
        ng = norm_g[l].reshape(1, D_MODEL)
        row = lambda a: a.reshape(1, -1)
        zr = jnp.zeros((2, RW_RANK, RW_WIDTH), F32)
        wcat = jnp.concatenate([jnp.concatenate([rw_w2[l], zr], axis=2),
                                jnp.concatenate([zr, rw_a2[l]], axis=2)], axis=1).astype(BF16)
        rw_wts = (wcat, jnp.concatenate([rw_w0[l], rw_a0[l]], axis=1), row(rw_k_k[l]), row(rw_k_a[l]),
                  row(rw_r_k[l]), row(rw_ln_g[l]), row(rw_ln_b[l]), hsum)
        a2p = jnp.concatenate([gla_a2[l], jnp.zeros((2, LANE - GLA_GATE_RANK, GLA_KW), F32)], axis=1).astype(BF16)
        gla_wts = (a2p, gla_ab[l], row(gla_norm_g[l]), mexp, bmask)
        mla_wts = _pack_mla_weights(mla_qn_g[l], mla_wq_up[l], mla_kvn_g[l], mla_wkv_up[l])
        out_wts = (rw_out[l].astype(BF16), gla_out[l].astype(BF16), mla_out[l].astype(BF16), w_out[l].astype(BF16))
        final = l == DEPTH - 1

        proj, proj_b = _inproj_call(hp, mod3, ng, w_a, w_b, l, bp * tp, ctx_row)
        o_a, s_rw = _rwkv_call(proj, None, l, bp, tp, rw_wts)
        o_b, s_gla = _gla_call(proj, None, l, bp, tp, gla_wts)
        o_c, ckv = _mla_call(proj, None, None, l, bp, tp, mla_wts, None)
        ckv_l.append(ckv.reshape(bp, tp, MLA_KV_RANK))
        kpe_l.append(proj[:, COL_KPE + KPE_LANE:COL_KPE + KPE_LANE + MLA_ROPE].astype(F32).reshape(bp, tp, MLA_ROPE))
        rw_l.append(s_rw)
        gla_l.append(s_gla)
        hp = _outproj_call(hp, o_a, o_b, o_c, proj_b, mod3, out_wts, fg, bp * tp, ctx_row, final)

        proj, proj_b = _inproj_call(hs, mod3, ng, w_a, w_b, l, ts, 0)
        o_a, _ = _rwkv_call(proj, state_rwkv, l, bs, ts, rw_wts)
        o_b, _ = _gla_call(proj, state_gla, l, bs, ts, gla_wts)
        o_c, _ = _mla_call(proj, cache_ckv, cache_kpe, l, bs, ts, mla_wts, tables)
        hs = _outproj_call(hs, o_a, o_b, o_c, proj_b, mod3, out_wts, fg, ts, 0, final)

    return (hp.reshape(bp, tp, D_MODEL), hs.reshape(bs, ts, D_MODEL),
            jnp.stack(ckv_l, axis=1), jnp.stack(kpe_l, axis=1),
            jnp.stack(rw_l, axis=1), jnp.stack(gla_l, axis=1))


_trunk_jit = jax.jit(_trunk)


def kernel(x_prompt, x_sample, c, cache_ckv, cache_kpe, state_rwkv, state_gla, c_ctx, norm_g, w_mod, b_mod, w_in, rw_w0, rw_w2, rw_a0, rw_a2, rw_k_k, rw_k_a, rw_r_k, rw_ln_g, rw_ln_b, rw_out, gla_a2, gla_ab, gla_norm_g, gla_out, mla_qn_g, mla_wq_up, mla_kvn_g, mla_wkv_up, mla_out, w_out, final_g):
    return _trunk_jit(x_prompt, x_sample, c, cache_ckv, cache_kpe, state_rwkv, state_gla, c_ctx, norm_g, w_mod, b_mod, w_in, rw_w0, rw_w2, rw_a0, rw_a2, rw_k_k, rw_k_a, rw_r_k, rw_ln_g, rw_ln_b, rw_out, gla_a2, gla_ab, gla_norm_g, gla_out, mla_qn_g, mla_wq_up, mla_kvn_g, mla_wkv_up, mla_out, w_out, final_g)
```

```python
import functools

import numpy as np
import jax
import jax.numpy as jnp
from jax import lax
from jax.experimental import pallas as pl
from jax.experimental.pallas import tpu as pltpu

F32 = jnp.float32
BF16 = jnp.bfloat16

D_MODEL = 1024
DEPTH = 2
GRID_W = 64
NORM_EPS = 1e-6
RW_HEADS = 8
RW_HEAD = 64
RW_WIDTH = RW_HEADS * RW_HEAD
RW_RANK = 64
RW_GN_EPS = 64e-5
GLA_HEADS = 4
GLA_DK = 64
GLA_DV = 128
GLA_KW = GLA_HEADS * GLA_DK
GLA_VW = GLA_HEADS * GLA_DV
GLA_GATE_RANK = 16
GLA_LOGIT_NORM = 16.0
MLA_HEADS = 8
MLA_NOPE = 64
MLA_ROPE = 32
MLA_V = 64
MLA_Q_RANK = 256
MLA_KV_RANK = 128
MLA_VW = MLA_HEADS * MLA_V
ROPE_THETA = 10000.0
N_BRANCH = 3

_IN_SIZES = (RW_WIDTH, RW_WIDTH, RW_WIDTH, RW_RANK, RW_RANK, RW_WIDTH,
             GLA_KW, GLA_KW, GLA_VW, GLA_GATE_RANK, GLA_VW,
             MLA_Q_RANK, MLA_KV_RANK, MLA_ROPE, MLA_VW, N_BRANCH * D_MODEL)
_IN_OFF = tuple(int(v) for v in np.concatenate([[0], np.cumsum(_IN_SIZES)]))

LANE = 128
COL_RW_R = 0
COL_RW_K = 512
COL_RW_V = 1024
COL_RW_WA = 1536
COL_GL_AD = 1664
COL_ML_QD = 1792
COL_GL_Q = 2048
COL_GL_K = 2304
COL_GL_V = 2560
COL_ML_KVD = 3072
COL_KPE = 3200
COL_KPE_SW = 3328
PACK_A = 3584
COL_MERGE = 0
COL_GATES = 3072
PACK_B = 4608
KPE_LANE = MLA_NOPE

CHUNK = 64
INPROJ_TM = 512
INPROJ_SUB = 256
INPROJ_TN = 512
OUTPROJ_TM = 512
MLA_TQ = 1024
MLA_GROUP_SCORES = 256 * 256
VMEM_LIMIT = 48 * 1024 * 1024


def _dot(a, b, prec=None):
    return jnp.dot(a, b, preferred_element_type=F32, precision=prec)


def _dot_nt(a, b, prec=None):
    return lax.dot_general(a, b, (((1,), (1,)), ((), ())), preferred_element_type=F32, precision=prec)


def _dot_tn(a, b, prec=None):
    return lax.dot_general(a, b, (((0,), (0,)), ((), ())), preferred_element_type=F32, precision=prec)


def _split(x):
    hi = x.astype(BF16)
    return hi, (x - hi.astype(F32)).astype(BF16)


def _dot_split(a, b):
    ah, al = _split(a)
    bh, bl = _split(b)
    return _dot(ah, bh) + _dot(al, bh) + _dot(ah, bl)


def _dot_sel(sel16, x):
    xh, xl = _split(x)
    return _dot(sel16, xh) + _dot(sel16, xl)


def _sigmoid(x):
    return 0.5 * jnp.tanh(0.5 * x) + 0.5


def _rms(x, eps=NORM_EPS):
    return x * lax.rsqrt(jnp.mean(x * x, axis=-1, keepdims=True) + eps)


def _mod_kernel(c_ref, w_ref, b_ref, o_ref):
    c = c_ref[...]
    o_ref[...] = _dot_split(c * _sigmoid(c), w_ref[...]) + b_ref[...]


def _mod_call(cvec8, w_mod, b_mod, layer):
    tn = 1024
    return pl.pallas_call(
        _mod_kernel,
        grid=(3 * D_MODEL // tn,),
        in_specs=[pl.BlockSpec((8, D_MODEL), lambda j: (0, 0)),
                  pl.BlockSpec((None, D_MODEL, tn), lambda j: (layer, 0, j)),
                  pl.BlockSpec((None, 1, tn), lambda j: (layer, 0, j))],
        out_specs=pl.BlockSpec((8, tn), lambda j: (0, j)),
        out_shape=jax.ShapeDtypeStruct((8, 3 * D_MODEL), F32),
        compiler_params=pltpu.CompilerParams(vmem_limit_bytes=VMEM_LIMIT),
        name="mod",
    )(cvec8, w_mod, b_mod.reshape(DEPTH, 1, 3 * D_MODEL))


def _inproj_kernel(x_ref, mod_ref, g_ref, wa_ref, wb_ref, oa_ref, ob_ref):
    m = mod_ref[...]
    shift = m[:, 0:D_MODEL]
    scale1 = 1.0 + m[:, D_MODEL:2 * D_MODEL]
    tm = x_ref.shape[0]
    for r0 in range(0, tm, INPROJ_SUB):
        rows = slice(r0, r0 + INPROJ_SUB)
        h = (_rms(x_ref[rows, :]) * g_ref[...] * scale1 + shift).astype(BF16)
        for w_ref, o_ref in ((wa_ref, oa_ref), (wb_ref, ob_ref)):
            for c0 in range(0, w_ref.shape[1], INPROJ_TN):
                cols = slice(c0, c0 + INPROJ_TN)
                o_ref[rows, cols] = _dot(h, w_ref[:, cols]).astype(BF16)


def _inproj_call(x2d, mod3, norm_g, w_a, w_b, layer, rows_per_mod, mod_base):
    rows = x2d.shape[0]
    tm = INPROJ_TM
    tiles_per_mod = rows_per_mod // tm
    resident = lambda w: pl.BlockSpec((None,) + w.shape[1:], lambda i: (layer, 0, 0), pipeline_mode=pl.Buffered(1))
    return pl.pallas_call(
        _inproj_kernel,
        grid=(rows // tm,),
        in_specs=[pl.BlockSpec((tm, D_MODEL), lambda i: (i, 0)),
                  pl.BlockSpec((None, 1, 3 * D_MODEL), lambda i: (mod_base + i // tiles_per_mod, 0, 0)),
                  pl.BlockSpec((1, D_MODEL), lambda i: (0, 0)),
                  resident(w_a), resident(w_b)],
        out_specs=[pl.BlockSpec((tm, PACK_A), lambda i: (i, 0)),
                   pl.BlockSpec((tm, PACK_B), lambda i: (i, 0))],
        out_shape=[jax.ShapeDtypeStruct((rows, PACK_A), BF16),
                   jax.ShapeDtypeStruct((rows, PACK_B), BF16)],
        compiler_params=pltpu.CompilerParams(vmem_limit_bytes=VMEM_LIMIT),
        name="inproj",
    )(x2d, mod3, norm_g, w_a, w_b)


RW_CHUNK = 64
RW_PAIRS = RW_HEADS // 2
RW_BLOCK_ROWS = 256
RW_NEG_DECAY_SCALE = -float(np.exp(-0.5))
RW_UNROLL = 4


def _rwkv_time_mask(c, reverse):
    row = lax.broadcasted_iota(jnp.int32, (2 * c, 4 * c), 0)
    col = lax.broadcasted_iota(jnp.int32, (2 * c, 4 * c), 1)
    t = jnp.where(row >= c, row - c, row)
    s = col & (c - 1)
    earlier = (s > t) if reverse else (s < t)
    return earlier | ((row >= c) & (s == t))


def _rwkv_kernel(*refs, seq, has_s0):
    if has_s0:
        r_ref, k_ref, v_ref, wa_ref, s0_ref = refs[:5]
        rest = refs[5:]
    else:
        r_ref, k_ref, v_ref, wa_ref = refs[:4]
        s0_ref = None
        rest = refs[4:]
    (wcat_ref, wa0_ref, kk_ref, ka_ref, rk_ref, lng_ref, lnb_ref, hsum_ref,
     o_ref, sout_ref, acc_ref, kn_ref, st_ref) = rest
    c = RW_CHUNK
    nc = seq // c
    blk = min(RW_BLOCK_ROWS, seq)
    hsum = hsum_ref[...]
    pair_lanes = [slice(p * LANE, (p + 1) * LANE) for p in range(RW_PAIRS)]

    def pre_step(i, carry):
        rows = pl.ds(pl.multiple_of(i * blk, blk), blk)
        kk0 = [k_ref[rows, pl_].astype(F32) * kk_ref[:, pl_] for pl_ in pair_lanes]
        ss = [_dot((x * x).astype(BF16), hsum) for x in kk0]
        for pl_, x, s in zip(pair_lanes, kk0, ss):
            kn_ref[rows, pl_] = x / jnp.maximum(jnp.sqrt(s), 1e-12)
        return carry

    def post_step(i, carry):
        rows = pl.ds(pl.multiple_of(i * blk, blk), blk)
        o = [acc_ref[0, rows, pl_] + acc_ref[1, rows, pl_] for pl_ in pair_lanes]
        rk = [r_ref[rows, pl_].astype(F32) * k_ref[rows, pl_].astype(F32) * rk_ref[:, pl_] for pl_ in pair_lanes]
        red = [_dot(jnp.concatenate([a, b], axis=0).astype(BF16), hsum) for a, b in zip(o, rk)]
        dev = [a - s[0:blk] * (1.0 / RW_HEAD) for a, s in zip(o, red)]
        var = [_dot((x * x).astype(BF16), hsum) * (1.0 / RW_HEAD) for x in dev]
        for pl_, x, vr, s in zip(pair_lanes, dev, var, red):
            normed = x * lax.rsqrt(vr + RW_GN_EPS) * lng_ref[:, pl_] + lnb_ref[:, pl_]
            o_ref[rows, pl_] = (normed + s[blk:2 * blk] * v_ref[rows, pl_]).astype(o_ref.dtype)
        return carry

    even = lax.broadcasted_iota(jnp.int32, (1, LANE), 1) < RW_HEAD
    row = lax.broadcasted_iota(jnp.int32, (c, c), 0)
    col = lax.broadcasted_iota(jnp.int32, (c, c), 1)
    incl16 = [jnp.where(col <= row, 1.0, 0.0).astype(BF16), jnp.where(col >= row, 1.0, 0.0).astype(BF16)]
    tmask = [_rwkv_time_mask(c, False), _rwkv_time_mask(c, True)]
    r2 = lax.broadcasted_iota(jnp.int32, (LANE, LANE), 0)
    c2 = lax.broadcasted_iota(jnp.int32, (LANE, LANE), 1)
    same_head = (r2 < RW_HEAD) == (c2 < RW_HEAD)
    p_lanes = ((lax.broadcasted_iota(jnp.int32, (2 * c, LANE), 0) < c)
               == (lax.broadcasted_iota(jnp.int32, (2 * c, LANE), 1) < RW_HEAD))
    lanes = [slice(p * LANE, (p + 1) * LANE) for p in range(RW_PAIRS)]
    probs = [(d, p) for d in range(2) for p in range(RW_PAIRS)]

    def chunk_operands(d, ci):
        rows = pl.ds(pl.multiple_of(ci * c, c), c)
        r = r_ref[rows, :].astype(F32)
        k = k_ref[rows, :].astype(F32)
        wa = wa_ref[rows, :].astype(F32)
        kk = kn_ref[rows, :]
        lora = jnp.where(even, jnp.tanh(wa), wa)
        pre = _dot(lora.astype(BF16), wcat_ref[d]) + wa0_ref[d:d + 1, :]
        lw = RW_NEG_DECAY_SCALE * _sigmoid(pre[:, 0:RW_WIDTH])
        a = _sigmoid(pre[:, RW_WIDTH:2 * RW_WIDTH])
        kd = k * (1.0 + (a - 1.0) * ka_ref[...])
        bv = kk * a
        gam = _dot_sel(incl16[d], lw)
        last = 0 if d == 1 else c - 1
        gtot = gam[last:last + 1, :]
        gref = gam[c // 2:c // 2 + 1, :]
        e_in = jnp.exp(gam - gref)
        e_out = jnp.exp(gref - gam)
        e_end = e_out * jnp.exp(gtot - gref)
        lhs = jnp.concatenate([-kk * jnp.exp(gam - lw - gref), r * e_in], axis=0).astype(BF16)
        return dict(rows=rows, lhs=lhs, b16=(bv * e_out).astype(BF16), k16=(kd * e_out).astype(BF16),
                    be16=(bv * e_end).astype(BF16), ke16=(kd * e_end).astype(BF16),
                    v16=v_ref[rows, :].astype(BF16), dtot=jnp.exp(gtot), eref=jnp.exp(gref))

    def main_step(i, carry):
        steps = [i * RW_UNROLL + s for s in range(RW_UNROLL)]
        prepared = [[chunk_operands(0, t), chunk_operands(1, nc - 1 - t)] for t in steps]
        for ops in prepared:
            chain_step(ops)
        return carry

    def chain_step(ops):
        n = range(len(probs))
        pick = lambda name: [ops[d][name][:, lanes[p]] for d, p in probs]
        lhs, b, k, be, ke, v = (pick(s) for s in ("lhs", "b16", "k16", "be16", "ke16", "v16"))
        st = [st_ref[d, p] for d, p in probs]
        first = lambda x: jnp.where(even, x, jnp.zeros_like(x))
        second = lambda x: jnp.where(even, jnp.zeros_like(x), x)
        v_e = [first(x) for x in v]
        v_o = [second(x) for x in v]
        aa = [jnp.where(tmask[d], _dot_nt(lhs[j], jnp.concatenate(
            [first(k[j]), second(k[j]), second(b[j]), first(b[j])], axis=0)), 0.0) for j, (d, p) in enumerate(probs)]
        sp = [_dot_nt(lhs[j], (st[j] * ops[d]["eref"][:, lanes[p]]).astype(BF16))
              for j, (d, p) in enumerate(probs)]
        aa16 = [x.astype(BF16) for x in aa]
        rhs_u = [_dot(aa16[j][0:c, 0:LANE], jnp.concatenate([v_e[j], v_o[j]], axis=0)) + sp[j][0:c] for j in n]
        x = [jnp.concatenate([jnp.where(even, aa[j][0:c, LANE:2 * LANE], rhs_u[j]),
                              jnp.where(even, rhs_u[j], aa[j][0:c, LANE:2 * LANE])], axis=0) for j in n]
        covered = 1
        while covered < c:
            x16 = [xx.astype(BF16) for xx in x]
            x = [_dot(jnp.where(p_lanes, x16[j], jnp.zeros_like(x16[j])), x16[j])
                 + jnp.where(p_lanes, 0.0, x[j]) for j in n]
            covered *= 2
        u16 = [jnp.where(even, x[j][c:2 * c], x[j][0:c]).astype(BF16) for j in n]
        for j, (d, p) in enumerate(probs):
            uv_rows = jnp.concatenate([v_e[j], v_o[j], second(u16[j]), first(u16[j])], axis=0)
            acc_ref[d, ops[d]["rows"], lanes[p]] = sp[j][c:2 * c] + _dot(aa16[j][c:2 * c], uv_rows)
            upd = _dot_tn(jnp.concatenate([u16[j], v[j]], axis=0), jnp.concatenate([be[j], ke[j]], axis=0))
            st_ref[d, p] = st[j] * ops[d]["dtot"][:, lanes[p]] + jnp.where(same_head, upd, 0.0)

    st_ref[...] = jnp.zeros_like(st_ref)
    if has_s0:
        for d in range(2):
            for h in range(RW_HEADS):
                off = (h % 2) * RW_HEAD
                st_ref[d, h // 2, off:off + RW_HEAD, off:off + RW_HEAD] = s0_ref[d, h]
    lax.fori_loop(0, seq // blk, pre_step, 0)
    lax.fori_loop(0, nc // RW_UNROLL, main_step, 0)
    for d in range(2):
        for h in range(RW_HEADS):
            off = (h % 2) * RW_HEAD
            sout_ref[d, h] = st_ref[d, h // 2, off:off + RW_HEAD, off:off + RW_HEAD]
    lax.fori_loop(0, seq // blk, post_step, 0)


def _rwkv_call(proj, s0, layer, batch, seq, wts):
    has_s0 = s0 is not None
    blk = lambda w, cb: pl.BlockSpec((seq, w), lambda b: (b, cb))
    in_specs = [blk(512, COL_RW_R // 512), blk(512, COL_RW_K // 512), blk(512, COL_RW_V // 512),
                blk(LANE, COL_RW_WA // LANE)]
    args = [proj, proj, proj, proj]
    if has_s0:
        in_specs.append(pl.BlockSpec((None, None, 2, RW_HEADS, RW_HEAD, RW_HEAD),
                                     lambda b: (b, layer, 0, 0, 0, 0)))
        args.append(s0)
    for w in wts:
        in_specs.append(pl.BlockSpec(w.shape, lambda b, n=w.ndim: (0,) * n))
        args.append(w)
    return pl.pallas_call(
        functools.partial(_rwkv_kernel, seq=seq, has_s0=has_s0),
        grid=(batch,),
        in_specs=in_specs,
        out_specs=[pl.BlockSpec((seq, RW_WIDTH), lambda b: (b, 0)),
                   pl.BlockSpec((None, 2, RW_HEADS, RW_HEAD, RW_HEAD), lambda b: (b, 0, 0, 0, 0))],
        out_shape=[jax.ShapeDtypeStruct((batch * seq, RW_WIDTH), BF16),
                   jax.ShapeDtypeStruct((batch, 2, RW_HEADS, RW_HEAD, RW_HEAD), F32)],
        scratch_shapes=[pltpu.VMEM((2, seq, RW_WIDTH), F32),
                        pltpu.VMEM((seq, RW_WIDTH), F32),
                        pltpu.VMEM((2, RW_PAIRS, LANE, LANE), F32)],
        compiler_params=pltpu.CompilerParams(vmem_limit_bytes=VMEM_LIMIT),
        name="rwkv",
    )(*args)


GLA_LEVELS = (32, 16, 8, 4, 2, 1)
GLA_GROUP = 8


def _gla_constants():
    c = CHUNK
    nl = len(GLA_LEVELS)
    mexp = np.zeros((2, (nl + 1) * c, c), np.float32)
    bmask = np.zeros((2, nl + 1, c, c), np.float32)
    for d in range(2):
        pos = np.arange(c) if d == 0 else c - 1 - np.arange(c)
        pt = pos[:, None]
        pj = pos[None, :]
        mexp[d, nl * c:] = (pj <= pt)
        for li, m in enumerate(GLA_LEVELS):
            mid = (pos // (2 * m)) * (2 * m) + m
            second = pos >= mid
            mq = (pj >= mid[:, None]) & (pj <= pt) & second[:, None]
            mk = (pj > pt) & (pj <= mid[:, None] - 1) & (~second)[:, None]
            mexp[d, li * c:(li + 1) * c] = mq | mk
            same = (pos[:, None] // (2 * m)) == (pos[None, :] // (2 * m))
            bmask[d, li] = same & second[:, None] & (~second)[None, :]
        bmask[d, nl] = np.eye(c)
    return mexp, np.concatenate([bmask, bmask], axis=-1)


def _gla_kernel(*refs, seq, has_s0):
    if has_s0:
        q_ref, k_ref, v_ref, ad_ref, s0_ref = refs[:5]
        rest = refs[5:]
    else:
        q_ref, k_ref, v_ref, ad_ref = refs[:4]
        s0_ref = None
        rest = refs[4:]
    (a2_ref, ab_ref, ng_ref, mexp_ref, bmask_ref,
     o_ref, sout_ref, acc_ref, st_ref) = rest
    c = CHUNK
    nc = seq // c
    nl = len(GLA_LEVELS)

    def chunk_operands(d, ci):
        rows = pl.ds(pl.multiple_of(ci * c, c), c)
        q = q_ref[rows, :].astype(F32) * (GLA_DK ** -0.5)
        k = k_ref[rows, :].astype(F32)
        v = v_ref[rows, :]
        x = _dot(ad_ref[rows, :].astype(BF16), a2_ref[d]) + ab_ref[d:d + 1, :]
        g = (jnp.minimum(x, 0.0) - jnp.log1p(jnp.exp(-jnp.abs(x)))) * (1.0 / GLA_LOGIT_NORM)
        g_hi, g_lo = _split(g)
        sums = _dot(mexp_ref[d], g_hi)
        ex = jnp.exp(sums[0:nl * c])
        b = sums[nl * c:(nl + 1) * c] + _dot(mexp_ref[d, nl * c:(nl + 1) * c, :], g_lo)
        last = 0 if d == 1 else c - 1
        blast = b[last:last + 1, :]
        qb = (q * jnp.exp(b)).astype(BF16)
        kdec = (k * jnp.exp(blast - b)).astype(BF16)
        dtot = jnp.exp(blast)
        qs = [(q * ex[li * c:(li + 1) * c]).astype(BF16) for li in range(nl)] + [q.astype(BF16)]
        ks = [(k * ex[li * c:(li + 1) * c]).astype(BF16) for li in range(nl)] + [k.astype(BF16)]
        return dict(rows=rows, qs=qs, ks=ks, qb=qb, kdec=kdec, dtot=dtot, v16=v.astype(BF16))

    even = lax.broadcasted_iota(jnp.int32, (1, LANE), 1) < GLA_DK
    r2 = lax.broadcasted_iota(jnp.int32, (2 * GLA_DV, LANE), 0)
    c2 = lax.broadcasted_iota(jnp.int32, (2 * GLA_DV, LANE), 1)
    same_head = (r2 < GLA_DV) == (c2 < GLA_DK)
    zv = jnp.zeros((c, GLA_DV), BF16)
    npair = GLA_HEADS // 2
    kls = [slice(p * LANE, (p + 1) * LANE) for p in range(npair)]
    vls = [slice(p * 2 * GLA_DV, (p + 1) * 2 * GLA_DV) for p in range(npair)]
    probs = [(d, p) for d in range(2) for p in range(npair)]
    first = lambda x: jnp.where(even, x, jnp.zeros_like(x))
    second = lambda x: jnp.where(even, jnp.zeros_like(x), x)

    group = min(GLA_GROUP, nc)

    def main_step(i, carry):
        ops = [[chunk_operands(0, i * group + g) for g in range(group)],
               [chunk_operands(1, nc - 1 - (i * group + g)) for g in range(group)]]
        allp = [(d, p, g) for d, p in probs for g in range(group)]
        lvl = [[_dot_nt(ops[d][g]["qs"][li][:, kls[p]],
                        jnp.concatenate([first(ops[d][g]["ks"][li][:, kls[p]]),
                                         second(ops[d][g]["ks"][li][:, kls[p]])], axis=0))
                for d, p, g in allp] for li in range(nl + 1)]
        att = [sum(bmask_ref[d, li] * lvl[li][j] for li in range(nl + 1)).astype(BF16)
               for j, (d, p, g) in enumerate(allp)]
        upd = [jnp.where(same_head, _dot_tn(ops[d][g]["v16"][:, vls[p]], ops[d][g]["kdec"][:, kls[p]]), 0.0)
               for d, p, g in allp]
        states = []
        for d, p in probs:
            st = st_ref[d, p]
            for g in range(group):
                states.append(st)
                st = st * ops[d][g]["dtot"][:, kls[p]] + upd[len(states) - 1]
            st_ref[d, p] = st
        for j, (d, p, g) in enumerate(allp):
            v_p = ops[d][g]["v16"][:, vls[p]]
            v_bd = jnp.concatenate([jnp.concatenate([v_p[:, 0:GLA_DV], zv], axis=1),
                                    jnp.concatenate([zv, v_p[:, GLA_DV:2 * GLA_DV]], axis=1)], axis=0)
            inter = _dot_nt(ops[d][g]["qb"][:, kls[p]], states[j].astype(BF16))
            acc_ref[d, ops[d][g]["rows"], vls[p]] = inter + _dot(att[j], v_bd)
        return carry

    blk = min(RW_BLOCK_ROWS, seq)

    def post_step(i, carry):
        rows = pl.ds(pl.multiple_of(i * blk, blk), blk)
        for h in range(GLA_HEADS):
            sv = slice(h * GLA_DV, (h + 1) * GLA_DV)
            o_ref[rows, sv] = (_rms(acc_ref[0, rows, sv] + acc_ref[1, rows, sv]) * ng_ref[...]).astype(o_ref.dtype)
        return carry

    st_ref[...] = jnp.zeros_like(st_ref)
    if has_s0:
        for d in range(2):
            for h in range(GLA_HEADS):
                ro, co = (h % 2) * GLA_DV, (h % 2) * GLA_DK
                st_ref[d, h // 2, ro:ro + GLA_DV, co:co + GLA_DK] = s0_ref[d, h].T
    lax.fori_loop(0, nc // group, main_step, 0)
    for d in range(2):
        for h in range(GLA_HEADS):
            ro, co = (h % 2) * GLA_DV, (h % 2) * GLA_DK
            sout_ref[d, h] = st_ref[d, h // 2, ro:ro + GLA_DV, co:co + GLA_DK].T
    lax.fori_loop(0, seq // blk, post_step, 0)


def _gla_call(proj, s0, layer, batch, seq, wts):
    has_s0 = s0 is not None
    blk = lambda w, cb: pl.BlockSpec((seq, w), lambda b: (b, cb))
    in_specs = [blk(GLA_KW, COL_GL_Q // GLA_KW), blk(GLA_KW, COL_GL_K // GLA_KW),
                blk(GLA_VW, COL_GL_V // GLA_VW), blk(LANE, COL_GL_AD // LANE)]
    args = [proj, proj, proj, proj]
    if has_s0:
        in_specs.append(pl.BlockSpec((None, None, 2, GLA_HEADS, GLA_DK, GLA_DV),
                                     lambda b: (b, layer, 0, 0, 0, 0)))
        args.append(s0)
    for w in wts:
        in_specs.append(pl.BlockSpec(w.shape, lambda b, n=w.ndim: (0,) * n))
        args.append(w)
    return pl.pallas_call(
        functools.partial(_gla_kernel, seq=seq, has_s0=has_s0),
        grid=(batch,),
        in_specs=in_specs,
        out_specs=[pl.BlockSpec((seq, GLA_VW), lambda b: (b, 0)),
                   pl.BlockSpec((None, 2, GLA_HEADS, GLA_DK, GLA_DV), lambda b: (b, 0, 0, 0, 0))],
        out_shape=[jax.ShapeDtypeStruct((batch * seq, GLA_VW), BF16),
                   jax.ShapeDtypeStruct((batch, 2, GLA_HEADS, GLA_DK, GLA_DV), F32)],
        scratch_shapes=[pltpu.VMEM((2, seq, GLA_VW), F32),
                        pltpu.VMEM((2, GLA_HEADS // 2, 2 * GLA_DV, LANE), F32)],
        compiler_params=pltpu.CompilerParams(vmem_limit_bytes=VMEM_LIMIT),
        name="gla",
    )(*args)


def _mla_kernel(*refs, seq, past, tq):
    has_ctx = past > 0
    if has_ctx:
        (qd_ref, kvd_ref, kpe_ref, kpesw_ref, cckv_ref, ckpe_ref,
         qng_ref, wq_ref, wqsw_ref, kvng_ref, wk_ref, wv_ref,
         cosq_ref, sinq_ref, cosk_ref, sink_ref, epos_ref,
         o_ref, ckv_ref, kcat_ref, vv_ref) = refs
    else:
        (qd_ref, kvd_ref, kpe_ref,
         qng_ref, wq_ref, kvng_ref, wk_ref, wv_ref,
         o_ref, ckv_ref, kcat_ref, vv_ref) = refs

    ones_hi = jnp.where(lax.broadcasted_iota(jnp.int32, (1, LANE), 1) >= MLA_V, 1.0, 0.0)

    @pl.when(pl.program_id(1) == 0)
    def _():
        ckv = _rms(kvd_ref[...].astype(F32)) * kvng_ref[...]
        ckv_ref[...] = ckv
        if has_ctx:
            kpos = kpe_ref[...] * cosk_ref[...] + kpesw_ref[...] * sink_ref[...]
        else:
            kpos = kpe_ref[...]
        segs = [(0, seq, ckv, kpos)]
        if has_ctx:
            cpos = _dot(ckpe_ref[...].astype(BF16), epos_ref[...])
            segs.append((seq, past, cckv_ref[...], cpos))
        for start, n, lat, pos in segs:
            lat16 = lat.astype(BF16)
            kc = _dot(lat16, wk_ref[...])
            vc = _dot(lat16, wv_ref[...])
            for h in range(MLA_HEADS):
                hl = slice(h * LANE, (h + 1) * LANE)
                kcat_ref[h, start:start + n, :] = (kc[:, hl] + pos).astype(BF16)
                vv_ref[h, start:start + n, :] = (vc[:, hl] + ones_hi).astype(BF16)

    qlat = (_rms(qd_ref[...].astype(F32)) * qng_ref[...]).astype(BF16)
    qc = _dot(qlat, wq_ref[...])
    if has_ctx:
        qsw = _dot(qlat, wqsw_ref[...])
    hg = MLA_HEADS if tq * (seq + past) <= MLA_GROUP_SCORES else 1
    for h0 in range(0, MLA_HEADS, hg):
        heads = range(h0, h0 + hg)
        q_raw = [qc[:, h * LANE:(h + 1) * LANE] for h in heads]
        if has_ctx:
            q_self = [(q_raw[i] * cosq_ref[...] + qsw[:, h * LANE:(h + 1) * LANE] * sinq_ref[...]).astype(BF16)
                      for i, h in enumerate(heads)]
        else:
            q_self = [x.astype(BF16) for x in q_raw]
        s1 = [_dot_nt(q_self[i], kcat_ref[h, 0:seq, :]) for i, h in enumerate(heads)]
        m = [jnp.max(s, axis=-1, keepdims=True) for s in s1]
        if has_ctx:
            s2 = [_dot_nt(q_raw[i].astype(BF16), kcat_ref[h, seq:seq + past, :]) for i, h in enumerate(heads)]
            m = [jnp.maximum(m[i], jnp.max(s2[i], axis=-1, keepdims=True)) for i in range(hg)]
        o = [_dot(jnp.exp((s1[i] - m[i]).astype(BF16)), vv_ref[h, 0:seq, :]) for i, h in enumerate(heads)]
        if has_ctx:
            o = [o[i] + _dot(jnp.exp((s2[i] - m[i]).astype(BF16)), vv_ref[h, seq:seq + past, :])
                 for i, h in enumerate(heads)]
        for i, h in enumerate(heads):
            o_h = o[i] / o[i][:, MLA_V:MLA_V + 1]
            o_ref[:, h * MLA_V:(h + 1) * MLA_V] = o_h[:, 0:MLA_V].astype(o_ref.dtype)


def _mla_call(proj, cache_ckv, cache_kpe, layer, batch, seq, wts, tables):
    has_ctx = cache_ckv is not None
    past = cache_ckv.shape[2] if has_ctx else 0
    tq = min(MLA_TQ, seq)
    nq = seq // tq
    full = lambda cb: pl.BlockSpec((seq, LANE), lambda b, i: (b, cb))
    in_specs = [pl.BlockSpec((tq, MLA_Q_RANK), lambda b, i: (b * nq + i, COL_ML_QD // MLA_Q_RANK)),
                full(COL_ML_KVD // LANE), full(COL_KPE // LANE)]
    args = [proj, proj, proj]
    if has_ctx:
        in_specs += [full(COL_KPE_SW // LANE),
                     pl.BlockSpec((None, None, past, MLA_KV_RANK), lambda b, i: (b, layer, 0, 0)),
                     pl.BlockSpec((None, None, past, MLA_ROPE), lambda b, i: (b, layer, 0, 0))]
        args += [proj, cache_ckv, cache_kpe]
    qn_g, wq_cat, wq_sw, kvn_g, wk_pad, wv = wts
    const = lambda w: pl.BlockSpec(w.shape, lambda b, i, n=w.ndim: (0,) * n)
    if has_ctx:
        cosq, sinq, cosk, sink, epos = tables
        wlist = [qn_g, wq_cat, wq_sw, kvn_g, wk_pad, wv]
        in_specs += [const(w) for w in wlist]
        in_specs += [pl.BlockSpec((tq, LANE), lambda b, i: (i, 0)), pl.BlockSpec((tq, LANE), lambda b, i: (i, 0)),
                     const(cosk), const(sink), const(epos)]
        args += wlist + [cosq, sinq, cosk, sink, epos]
    else:
        wlist = [qn_g, wq_cat, kvn_g, wk_pad, wv]
        in_specs += [const(w) for w in wlist]
        args += wlist
    return pl.pallas_call(
        functools.partial(_mla_kernel, seq=seq, past=past, tq=tq),
        grid=(batch, nq),
        in_specs=in_specs,
        out_specs=[pl.BlockSpec((tq, MLA_VW), lambda b, i: (b * nq + i, 0)),
                   pl.BlockSpec((seq, MLA_KV_RANK), lambda b, i: (b, 0))],
        out_shape=[jax.ShapeDtypeStruct((batch * seq, MLA_VW), BF16),
                   jax.ShapeDtypeStruct((batch * seq, MLA_KV_RANK), F32)],
        scratch_shapes=[pltpu.VMEM((MLA_HEADS, seq + past, LANE), BF16),
                        pltpu.VMEM((MLA_HEADS, seq + past, LANE), BF16)],
        compiler_params=pltpu.CompilerParams(vmem_limit_bytes=VMEM_LIMIT,
                                             dimension_semantics=("arbitrary", "arbitrary")),
        name="mla",
    )(*args)


def _outproj_kernel(x_ref, oa_ref, ob_ref, oc_ref, ga_ref, gb_ref, gc_ref, ma_ref, mb_ref, mc_ref,
                    mod_ref, wa_ref, wb_ref, wc_ref, wo_ref, fg_ref, o_ref, *, final):
    def branch(o_r, g_r, m_r, w_r):
        h = 0.5 * g_r[...]
        act = o_r[...] * (h * (1.0 + jnp.tanh(h)))
        y = _dot(act, w_r[...]).astype(BF16)
        return (0.5 * (1.0 + jnp.tanh(0.5 * m_r[...]))) * y

    y = (branch(oa_ref, ga_ref, ma_ref, wa_ref) + branch(ob_ref, gb_ref, mb_ref, wb_ref)
         + branch(oc_ref, gc_ref, mc_ref, wc_ref))
    y = _dot(y, wo_ref[...])
    gate = mod_ref[...][:, 2 * D_MODEL:3 * D_MODEL]
    hn = x_ref[...] + gate * y
    if final:
        hn = _rms(hn) * fg_ref[...]
    o_ref[...] = hn


def _outproj_call(x2d, o_a, o_b, o_c, proj, mod3, wts, final_g, rows_per_mod, mod_base, final):
    rows = x2d.shape[0]
    tm = OUTPROJ_TM
    tiles_per_mod = rows_per_mod // tm
    row = lambda w, cb=0: pl.BlockSpec((tm, w), lambda i: (i, cb))
    const = lambda w: pl.BlockSpec(w.shape, lambda i, n=w.ndim: (0,) * n)
    in_specs = [row(D_MODEL), row(512), row(512), row(512),
                row(512, COL_GATES // 512), row(512, COL_GATES // 512 + 1), row(512, COL_GATES // 512 + 2),
                row(D_MODEL, COL_MERGE // D_MODEL), row(D_MODEL, COL_MERGE // D_MODEL + 1),
                row(D_MODEL, COL_MERGE // D_MODEL + 2),
                pl.BlockSpec((None, 1, 3 * D_MODEL), lambda i: (mod_base + i // tiles_per_mod, 0, 0))]
    in_specs += [const(w) for w in wts] + [const(final_g)]
    return pl.pallas_call(
        functools.partial(_outproj_kernel, final=final),
        grid=(rows // tm,),
        in_specs=in_specs,
        out_specs=row(D_MODEL),
        out_shape=jax.ShapeDtypeStruct((rows, D_MODEL), F32),
        compiler_params=pltpu.CompilerParams(vmem_limit_bytes=VMEM_LIMIT),
        name="outproj",
    )(x2d, o_a, o_b, o_c, proj, proj, proj, proj, proj, proj, mod3, *wts, final_g)


def _pack_kernel(wt_ref, wa_ref, wb_ref):
    seg = lambda i: wt_ref[_IN_OFF[i]:_IN_OFF[i + 1], :]
    tr = wt_ref.shape[1]
    z = lambda n: jnp.zeros((n, tr), F32)

    def put(ref, col, xt):
        ref[:, col:col + xt.shape[0]] = xt.T.astype(BF16)

    put(wa_ref, COL_RW_R, wt_ref[_IN_OFF[0]:_IN_OFF[5], :])
    put(wa_ref, COL_GL_AD, jnp.concatenate([seg(9), z(LANE - GLA_GATE_RANK)], axis=0))
    put(wa_ref, COL_ML_QD, seg(11))
    put(wa_ref, COL_GL_Q, wt_ref[_IN_OFF[6]:_IN_OFF[9], :])
    put(wa_ref, COL_ML_KVD, seg(12))
    kpe = seg(13)
    q = MLA_ROPE // 4
    kpe_sw = jnp.concatenate([kpe[q:2 * q], kpe[0:q], kpe[3 * q:4 * q], kpe[2 * q:3 * q]], axis=0)
    tail = LANE - KPE_LANE - MLA_ROPE
    put(wa_ref, COL_KPE, jnp.concatenate([z(KPE_LANE), kpe, z(tail)], axis=0))
    put(wa_ref, COL_KPE_SW, jnp.concatenate([z(KPE_LANE), kpe_sw, z(tail)], axis=0))
    wa_ref[:, COL_KPE_SW + LANE:PACK_A] = jnp.zeros((tr, PACK_A - COL_KPE_SW - LANE), BF16)
    put(wb_ref, COL_MERGE, seg(15))
    put(wb_ref, COL_GATES, seg(5))
    put(wb_ref, COL_GATES + RW_WIDTH, seg(10))
    put(wb_ref, COL_GATES + RW_WIDTH + GLA_VW, seg(14))


def _pack_call(w_in):
    tr = 256
    wt = jnp.swapaxes(w_in, 1, 2)
    width = wt.shape[1]
    return pl.pallas_call(
        _pack_kernel,
        grid=(DEPTH, D_MODEL // tr),
        in_specs=[pl.BlockSpec((None, width, tr), lambda l, i: (l, 0, i))],
        out_specs=[pl.BlockSpec((None, tr, PACK_A), lambda l, i: (l, i, 0)),
                   pl.BlockSpec((None, tr, PACK_B), lambda l, i: (l, i, 0))],
        out_shape=[jax.ShapeDtypeStruct((DEPTH, D_MODEL, PACK_A), BF16),
                   jax.ShapeDtypeStruct((DEPTH, D_MODEL, PACK_B), BF16)],
        compiler_params=pltpu.CompilerParams(vmem_limit_bytes=VMEM_LIMIT),
        name="pack",
    )(wt)


def _rope_lane_tables(seq):
    n_freq = MLA_ROPE // 4
    t = np.arange(seq)
    inv = ROPE_THETA ** (-np.arange(n_freq, dtype=np.float64) / n_freq)
    ang = np.stack([(t // GRID_W)[:, None] * inv, (t % GRID_W)[:, None] * inv], axis=1)
    cos = np.repeat(np.cos(ang)[:, :, None, :], 2, axis=2).reshape(seq, MLA_ROPE)
    sin = np.stack([-np.sin(ang), np.sin(ang)], axis=2).reshape(seq, MLA_ROPE)
    return cos, sin


def _mla_tables(seq):
    cos, sin = _rope_lane_tables(seq)
    cosq = np.zeros((seq, LANE), np.float32)
    sinq = np.zeros((seq, LANE), np.float32)
    cosq[:, :KPE_LANE] = 1.0
    cosq[:, KPE_LANE:KPE_LANE + MLA_ROPE] = cos
    sinq[:, KPE_LANE:KPE_LANE + MLA_ROPE] = sin
    cosk = np.zeros((seq, LANE), np.float32)
    cosk[:, KPE_LANE:KPE_LANE + MLA_ROPE] = cos
    epos = np.zeros((MLA_ROPE, LANE), np.float32)
    epos[np.arange(MLA_ROPE), KPE_LANE + np.arange(MLA_ROPE)] = 1.0
    return (jnp.asarray(cosq), jnp.asarray(sinq), jnp.asarray(cosk), jnp.asarray(sinq),
            jnp.asarray(epos, dtype=BF16))


def _pack_mla_weights(qn_g, wq_up, kvn_g, wkv_up):
    scale = (MLA_NOPE + MLA_ROPE) ** -0.5
    wq = wq_up.reshape(MLA_Q_RANK, MLA_HEADS, MLA_NOPE + MLA_ROPE) * scale
    nope, rope = wq[..., :MLA_NOPE], wq[..., MLA_NOPE:]
    q = MLA_ROPE // 4
    rope_sw = jnp.concatenate([rope[..., q:2 * q], rope[..., 0:q], rope[..., 3 * q:4 * q], rope[..., 2 * q:3 * q]], -1)
    tail = jnp.zeros((MLA_Q_RANK, MLA_HEADS, LANE - KPE_LANE - MLA_ROPE), wq.dtype)
    wq_cat = jnp.concatenate([nope, rope, tail], -1).reshape(MLA_Q_RANK, MLA_HEADS * LANE).astype(BF16)
    wq_sw = jnp.concatenate([jnp.zeros_like(nope), rope_sw, tail], -1).reshape(MLA_Q_RANK, MLA_HEADS * LANE).astype(BF16)
    wkv = wkv_up.reshape(MLA_KV_RANK, MLA_HEADS, MLA_NOPE + MLA_V)
    wk = jnp.concatenate([wkv[..., :MLA_NOPE], jnp.zeros((MLA_KV_RANK, MLA_HEADS, LANE - MLA_NOPE), wkv.dtype)], -1)
    wk_pad = wk.reshape(MLA_KV_RANK, MLA_HEADS * LANE).astype(BF16)
    wv = jnp.concatenate([wkv[..., MLA_NOPE:], jnp.zeros((MLA_KV_RANK, MLA_HEADS, LANE - MLA_V), wkv.dtype)], -1)
    wv = wv.reshape(MLA_KV_RANK, MLA_HEADS * LANE).astype(BF16)
    return (qn_g.reshape(1, -1), wq_cat, wq_sw, kvn_g.reshape(1, -1), wk_pad, wv)


def _head_sum_matrix():
    lane = np.arange(LANE)
    return jnp.asarray((lane[:, None] // RW_HEAD == lane[None, :] // RW_HEAD).astype(np.float32), dtype=BF16)


def _trunk(x_prompt, x_sample, c, cache_ckv, cache_kpe, state_rwkv, state_gla, c_ctx,
           norm_g, w_mod, b_mod, w_in, rw_w0, rw_w2, rw_a0, rw_a2, rw_k_k, rw_k_a, rw_r_k,
           rw_ln_g, rw_ln_b, rw_out, gla_a2, gla_ab, gla_norm_g, gla_out,
           mla_qn_g, mla_wq_up, mla_kvn_g, mla_wkv_up, mla_out, w_out, final_g):
    bp, tp, _ = x_prompt.shape
    bs, ts, _ = x_sample.shape
    hp = x_prompt.reshape(bp * tp, D_MODEL)
    hs = x_sample.reshape(bs * ts, D_MODEL)
    cvec8 = jnp.concatenate([c, c_ctx[None, :], jnp.zeros((8 - bs - 1, D_MODEL), F32)], axis=0)
    ctx_row = bs
    hsum = _head_sum_matrix()
    mexp, bmask = _gla_constants()
    mexp = jnp.asarray(mexp, dtype=BF16)
    bmask = jnp.asarray(bmask)
    tables = _mla_tables(ts)
    w_a, w_b = _pack_call(w_in)
    fg = final_g.reshape(1, D_MODEL)
    ckv_l, kpe_l, rw_l, gla_l = [], [], [], []
    for l in range(DEPTH):
        mod3 = _mod_call(cvec8, w_mod, b_mod, l).reshape(8, 1, 3 * D_MODEL)
        ng = norm_g[l].reshape(1, D_MODEL)
        row = lambda a: a.reshape(1, -1)
        zr = jnp.zeros((2, RW_RANK, RW_WIDTH), F32)
        wcat = jnp.concatenate([jnp.concatenate([rw_w2[l], zr], axis=2),
                                jnp.concatenate([zr, rw_a2[l]], axis=2)], axis=1).astype(BF16)
        rw_wts = (wcat, jnp.concatenate([rw_w0[l], rw_a0[l]], axis=1), row(rw_k_k[l]), row(rw_k_a[l]),
                  row(rw_r_k[l]), row(rw_ln_g[l]), row(rw_ln_b[l]), hsum)
        a2p = jnp.concatenate([gla_a2[l], jnp.zeros((2, LANE - GLA_GATE_RANK, GLA_KW), F32)], axis=1).astype(BF16)
        gla_wts = (a2p, gla_ab[l], row(gla_norm_g[l]), mexp, bmask)
        mla_wts = _pack_mla_weights(mla_qn_g[l], mla_wq_up[l], mla_kvn_g[l], mla_wkv_up[l])
        out_wts = (rw_out[l].astype(BF16), gla_out[l].astype(BF16), mla_out[l].astype(BF16), w_out[l].astype(BF16))
        final = l == DEPTH - 1

        proj, proj_b = _inproj_call(hp, mod3, ng, w_a, w_b, l, bp * tp, ctx_row)
        o_a, s_rw = _rwkv_call(proj, None, l, bp, tp, rw_wts)
        o_b, s_gla = _gla_call(proj, None, l, bp, tp, gla_wts)
        o_c, ckv = _mla_call(proj, None, None, l, bp, tp, mla_wts, None)
        ckv_l.append(ckv.reshape(bp, tp, MLA_KV_RANK))
        kpe_l.append(proj[:, COL_KPE + KPE_LANE:COL_KPE + KPE_LANE + MLA_ROPE].astype(F32).reshape(bp, tp, MLA_ROPE))
        rw_l.append(s_rw)
        gla_l.append(s_gla)
        hp = _outproj_call(hp, o_a, o_b, o_c, proj_b, mod3, out_wts, fg, bp * tp, ctx_row, final)

        proj, proj_b = _inproj_call(hs, mod3, ng, w_a, w_b, l, ts, 0)
        o_a, _ = _rwkv_call(proj, state_rwkv, l, bs, ts, rw_wts)
        o_b, _ = _gla_call(proj, state_gla, l, bs, ts, gla_wts)
        o_c, _ = _mla_call(proj, cache_ckv, cache_kpe, l, bs, ts, mla_wts, tables)
        hs = _outproj_call(hs, o_a, o_b, o_c, proj_b, mod3, out_wts, fg, ts, 0, final)

    return (hp.reshape(bp, tp, D_MODEL), hs.reshape(bs, ts, D_MODEL),
            jnp.stack(ckv_l, axis=1), jnp.stack(kpe_l, axis=1),
            jnp.stack(rw_l, axis=1), jnp.stack(gla_l, axis=1))


_trunk_jit = jax.jit(_trunk)


def kernel(x_prompt, x_sample, c, cache_ckv, cache_kpe, state_rwkv, state_gla, c_ctx, norm_g, w_mod, b_mod, w_in, rw_w0, rw_w2, rw_a0, rw_a2, rw_k_k, rw_k_a, rw_r_k, rw_ln_g, rw_ln_b, rw_out, gla_a2, gla_ab, gla_norm_g, gla_out, mla_qn_g, mla_wq_up, mla_kvn_g, mla_wkv_up, mla_out, w_out, final_g):
    return _trunk_jit(x_prompt, x_sample, c, cache_ckv, cache_kpe, state_rwkv, state_gla, c_ctx, norm_g, w_mod, b_mod, w_in, rw_w0, rw_w2, rw_a0, rw_a2, rw_k_k, rw_k_a, rw_r_k, rw_ln_g, rw_ln_b, rw_out, gla_a2, gla_ab, gla_norm_g, gla_out, mla_qn_g, mla_wq_up, mla_kvn_g, mla_wkv_up, mla_out, w_out, final_g)
```
